```python
import math
import jax, jax.numpy as jnp
from jax import lax
import numpy as np

D_MODEL = 2048
BATCH = 1
SEQ = 16384
DEPTH = 1
DEC_BATCH = 32
DEC_SEQ = 16
PAST_LEN = 2048

CHUNK = 64
D_SSM = 1024
SSM_GROUP = 16
N_SSM_GROUPS = D_SSM // SSM_GROUP
SSM_STATE = 64
D_SGU = 1024
SGU_HEADS = 8
SGU_HEAD_DIM = D_SGU // SGU_HEADS
SGU_CHUNK = 128
D_IN = D_SSM + 2 * D_SGU + 2 * D_MODEL
PEER_HEADS = 8
PEER_KEYS = 128
PEER_EXPERTS = PEER_KEYS * PEER_KEYS
PEER_TOPK = 16
PEER_DKEY = 256
PEER_BLOCK = 128
EPS = 1e-6

kernel_name = 's5_sgu_peer_streaming_step'


def rms_norm(x, g):
    xf = x.astype(jnp.float32)
    y = xf * lax.rsqrt(jnp.mean(xf * xf, axis=-1, keepdims=True) + EPS)
    return (y * g.astype(jnp.float32)).astype(x.dtype)


def layer_norm(x, g, b):
    xf = x.astype(jnp.float32)
    mu = jnp.mean(xf, axis=-1, keepdims=True)
    xc = xf - mu
    y = xc * lax.rsqrt(jnp.mean(xc * xc, axis=-1, keepdims=True) + EPS)
    return (y * g.astype(jnp.float32) + b.astype(jnp.float32)).astype(x.dtype)


def _complex_affine_op(e1, e2):
    a1r, a1i, b1r, b1i = e1
    a2r, a2i, b2r, b2i = e2
    return (a2r * a1r - a2i * a1i,
            a2r * a1i + a2i * a1r,
            a2r * b1r - a2i * b1i + b2r,
            a2r * b1i + a2i * b1r + b2i)


def s5_branch(u, h0_re, h0_im, lam_re, lam_im, log_dt, b_re, b_im, c_re, c_im, d, glu_w, glu_b):
    bsz, slen, _ = u.shape
    f32 = jnp.float32
    uf = u.astype(f32)
    ug = uf.reshape(bsz, slen, N_SSM_GROUPS, SSM_GROUP)
    dt = jnp.exp(log_dt.astype(f32))[:, None]
    lr = lam_re.astype(f32)
    li = lam_im.astype(f32)
    mag = jnp.exp(lr * dt)
    ab_re = mag * jnp.cos(li * dt)
    ab_im = mag * jnp.sin(li * dt)
    den = lr * lr + li * li
    nr = ab_re - 1.0
    ni = ab_im
    coef_re = (nr * lr + ni * li) / den
    coef_im = (ni * lr - nr * li) / den
    br = b_re.astype(f32)
    bi = b_im.astype(f32)
    bb_re = coef_re[..., None] * br - coef_im[..., None] * bi
    bb_im = coef_re[..., None] * bi + coef_im[..., None] * br
    bu_re = jnp.einsum('gph,bsgh->bsgp', bb_re, ug)
    bu_im = jnp.einsum('gph,bsgh->bsgp', bb_im, ug)
    bu_re = bu_re.at[:, 0].add(ab_re * h0_re - ab_im * h0_im)
    bu_im = bu_im.at[:, 0].add(ab_re * h0_im + ab_im * h0_re)
    a_re = jnp.broadcast_to(ab_re, bu_re.shape)
    a_im = jnp.broadcast_to(ab_im, bu_im.shape)
    _, _, h_re, h_im = lax.associative_scan(_complex_affine_op, (a_re, a_im, bu_re, bu_im), axis=1)
    y = (jnp.einsum('ghp,bsgp->bsgh', c_re.astype(f32), h_re)
         - jnp.einsum('ghp,bsgp->bsgh', c_im.astype(f32), h_im)).reshape(bsz, slen, D_SSM)
    y = jax.nn.gelu(y + d.astype(f32) * uf)
    y = y * jax.nn.sigmoid(y @ glu_w.astype(f32) + glu_b.astype(f32))
    return y.astype(u.dtype), h_re[:, -1], h_im[:, -1]


def sgu_branch(u, v, ln_g, ln_b, w_s, b_s):
    bsz, slen, _ = v.shape
    L = min(SGU_CHUNK, slen)
    nc = slen // L
    vn = layer_norm(v, ln_g, ln_b)
    vc = vn.reshape(bsz, nc, L, SGU_HEADS, SGU_HEAD_DIM)
    mask = jnp.tril(jnp.ones((L, L), dtype=bool))
    w = jnp.where(mask[None], w_s[:, :L, :L], 0.0).astype(vn.dtype)
    bias = jnp.transpose(b_s[:, :L]).astype(vn.dtype)
    s = jnp.einsum('hij,bcjhd->bcihd', w, vc) + bias[None, None, :, :, None]
    return u * s.reshape(bsz, slen, D_SGU), vn


def peer_ffn(x, w_q, keys, u_tab, v_tab):
    bsz, slen, dm = x.shape
    t = bsz * slen
    nblk = -(-t // PEER_BLOCK)
    pad = nblk * PEER_BLOCK - t
    xt = jnp.pad(x.reshape(t, dm), ((0, pad), (0, 0))).reshape(nblk, PEER_BLOCK, dm)
    half = PEER_DKEY // 2

    def block(xb):
        q = (xb @ w_q).reshape(PEER_BLOCK, PEER_HEADS, PEER_DKEY)
        s1 = jnp.einsum('thd,hkd->thk', q[..., :half], keys[:, 0]).astype(jnp.float32)
        s2 = jnp.einsum('thd,hkd->thk', q[..., half:], keys[:, 1]).astype(jnp.float32)
        v1, i1 = lax.top_k(s1, PEER_TOPK)
        v2, i2 = lax.top_k(s2, PEER_TOPK)
        ncand = PEER_TOPK * PEER_TOPK
        cand = (v1[..., :, None] + v2[..., None, :]).reshape(PEER_BLOCK, PEER_HEADS, ncand)
        cidx = (i1[..., :, None] * PEER_KEYS + i2[..., None, :]).reshape(PEER_BLOCK, PEER_HEADS, ncand)
        top_s, top_j = lax.top_k(cand, PEER_TOPK)
        eidx = jnp.take_along_axis(cidx, top_j, axis=-1)
        gate = jax.nn.softmax(top_s, axis=-1)
        act = jax.nn.gelu(jnp.einsum('thkd,td->thk', u_tab[eidx], xb).astype(jnp.float32))
        wgt = (gate * act).astype(xb.dtype)
        return jnp.einsum('thk,thkd->td', wgt, v_tab[eidx])

    out = lax.map(block, xt)
    return out.reshape(nblk * PEER_BLOCK, dm)[:t].reshape(bsz, slen, dm)


def trunk_layer(x, h0_re, h0_im, norm_mix_g, w_in, lam_re, lam_im, log_dt, b_re, b_im, c_re, c_im, d,
                glu_w, glu_b, ln_g, ln_b, w_s, b_s, w_branch_a, w_branch_b, w_out, norm_ffn_g,
                w_q, keys, u_tab, v_tab):
    xn = rms_norm(x, norm_mix_g)
    z = xn @ w_in
    o1 = D_SSM
    o2 = o1 + D_SGU
    o3 = o2 + D_SGU
    o4 = o3 + D_MODEL
    u_ssm = z[..., :o1]
    sgu_u = jax.nn.gelu(z[..., o1:o2])
    sgu_v = jax.nn.gelu(z[..., o2:o3])
    gate_a = jax.nn.sigmoid(z[..., o3:o4])
    gate_b = jax.nn.sigmoid(z[..., o4:])
    y_a, h_re, h_im = s5_branch(u_ssm, h0_re, h0_im, lam_re, lam_im, log_dt, b_re, b_im,
                                c_re, c_im, d, glu_w, glu_b)
    y_b, v_rows = sgu_branch(sgu_u, sgu_v, ln_g, ln_b, w_s, b_s)
    merged = gate_a * (y_a @ w_branch_a) + gate_b * (y_b @ w_branch_b)
    x = x + merged @ w_out
    x = x + peer_ffn(rms_norm(x, norm_ffn_g), w_q, keys, u_tab, v_tab)
    return x, h_re, h_im, v_rows


def setup_inputs(seed: int = 0) -> dict:
    key = jax.random.key(seed)
    ks = jax.random.split(key, 32)
    f32 = jnp.float32
    nrm = lambda k, shape, scale: jax.random.normal(k, shape, f32) * scale
    G, P = N_SSM_GROUPS, SSM_STATE
    lam_re = -0.5 + 0.01 * jax.random.normal(ks[4], (DEPTH, G, P), f32)
    lam_im = math.pi * jnp.broadcast_to(jnp.arange(P, dtype=f32), (DEPTH, G, P)) + 0.01 * jax.random.normal(ks[5], (DEPTH, G, P), f32)
    log_dt = jax.random.uniform(ks[6], (DEPTH, G), f32, math.log(1e-3), math.log(1e-1))
    return {
        'x_prompt': nrm(ks[0], (BATCH, SEQ, D_MODEL), 1.0),
        'x_sample': nrm(ks[1], (DEC_BATCH, DEC_SEQ, D_MODEL), 1.0),
        'state_ssm_re': nrm(ks[2], (DEPTH, DEC_BATCH, G, P), 0.5),
        'state_ssm_im': nrm(ks[3], (DEPTH, DEC_BATCH, G, P), 0.5),
        'norm_mix_g': 1.0 + nrm(ks[7], (DEPTH, D_MODEL), 0.02),
        'w_in': nrm(ks[8], (DEPTH, D_MODEL, D_IN), D_MODEL ** -0.5),
        'ssm_lambda_re': lam_re,
        'ssm_lambda_im': lam_im,
        'ssm_log_dt': log_dt,
        'ssm_b_re': nrm(ks[9], (DEPTH, G, P, SSM_GROUP), (2.0 * SSM_GROUP) ** -0.5),
        'ssm_b_im': nrm(ks[10], (DEPTH, G, P, SSM_GROUP), (2.0 * SSM_GROUP) ** -0.5),
        'ssm_c_re': nrm(ks[11], (DEPTH, G, SSM_GROUP, P), (2.0 * P) ** -0.5),
        'ssm_c_im': nrm(ks[12], (DEPTH, G, SSM_GROUP, P), (2.0 * P) ** -0.5),
        'ssm_d': nrm(ks[13], (DEPTH, D_SSM), 1.0),
        'ssm_glu_w': nrm(ks[14], (DEPTH, D_SSM, D_SSM), D_SSM ** -0.5),
        'ssm_glu_b': nrm(ks[15], (DEPTH, D_SSM), 0.02),
        'sgu_ln_g': 1.0 + nrm(ks[16], (DEPTH, D_SGU), 0.02),
        'sgu_ln_b': nrm(ks[17], (DEPTH, D_SGU), 0.02),
        'sgu_w': nrm(ks[18], (DEPTH, SGU_HEADS, SGU_CHUNK, SGU_CHUNK), SGU_CHUNK ** -0.5),
        'sgu_b': 1.0 + nrm(ks[19], (DEPTH, SGU_HEADS, SGU_CHUNK), 0.1),
        'w_branch_a': nrm(ks[20], (DEPTH, D_SSM, D_MODEL), D_SSM ** -0.5),
        'w_branch_b': nrm(ks[21], (DEPTH, D_SGU, D_MODEL), D_SGU ** -0.5),
        'w_out': nrm(ks[22], (DEPTH, D_MODEL, D_MODEL), D_MODEL ** -0.5),
        'norm_ffn_g': 1.0 + nrm(ks[23], (DEPTH, D_MODEL), 0.02),
        'peer_w_q': nrm(ks[24], (DEPTH, D_MODEL, PEER_HEADS * PEER_DKEY), D_MODEL ** -0.5),
        'peer_keys': nrm(ks[25], (DEPTH, PEER_HEADS, 2, PEER_KEYS, PEER_DKEY // 2), (PEER_DKEY // 2) ** -0.5),
        'peer_u': nrm(ks[26], (DEPTH, PEER_EXPERTS, D_MODEL), D_MODEL ** -0.5),
        'peer_v': nrm(ks[27], (DEPTH, PEER_EXPERTS, D_MODEL), PEER_HEADS ** -0.5),
        'norm_final_g': 1.0 + nrm(ks[28], (D_MODEL,), 0.02),
    }


def reference(x_prompt, x_sample, state_ssm_re, state_ssm_im, norm_mix_g, w_in, ssm_lambda_re,
              ssm_lambda_im, ssm_log_dt, ssm_b_re, ssm_b_im, ssm_c_re, ssm_c_im, ssm_d, ssm_glu_w,
              ssm_glu_b, sgu_ln_g, sgu_ln_b, sgu_w, sgu_b, w_branch_a, w_branch_b, w_out, norm_ffn_g,
              peer_w_q, peer_keys, peer_u, peer_v, norm_final_g):
    xp = x_prompt
    xs = x_sample
    p_re, p_im, s_re, s_im, s_v = [], [], [], [], []
    for l in range(DEPTH):
        lp = (norm_mix_g[l], w_in[l], ssm_lambda_re[l], ssm_lambda_im[l], ssm_log_dt[l],
              ssm_b_re[l], ssm_b_im[l], ssm_c_re[l], ssm_c_im[l], ssm_d[l], ssm_glu_w[l], ssm_glu_b[l],
              sgu_ln_g[l], sgu_ln_b[l], sgu_w[l], sgu_b[l], w_branch_a[l], w_branch_b[l], w_out[l],
              norm_ffn_g[l], peer_w_q[l], peer_keys[l], peer_u[l], peer_v[l])
        h0 = jnp.zeros((xp.shape[0], N_SSM_GROUPS, SSM_STATE), jnp.float32)
        xp, hr_p, hi_p, _ = trunk_layer(xp, h0, h0, *lp)
        xs, hr_s, hi_s, v_s = trunk_layer(xs, state_ssm_re[l].astype(jnp.float32),
                                          state_ssm_im[l].astype(jnp.float32), *lp)
        p_re.append(hr_p.astype(x_prompt.dtype))
        p_im.append(hi_p.astype(x_prompt.dtype))
        s_re.append(hr_s.astype(x_sample.dtype))
        s_im.append(hi_s.astype(x_sample.dtype))
        s_v.append(v_s)
    y_prompt = rms_norm(xp, norm_final_g)
    y_sample = rms_norm(xs, norm_final_g)
    new_ssm_re_prompt = jnp.stack(p_re)
    new_ssm_im_prompt = jnp.stack(p_im)
    new_ssm_re_sample = jnp.stack(s_re)
    new_ssm_im_sample = jnp.stack(s_im)
    new_sgu_v_sample = jnp.stack(s_v)
    return (y_prompt, y_sample, new_ssm_re_prompt, new_ssm_im_prompt, new_ssm_re_sample, new_ssm_im_sample, new_sgu_v_sample)
```

```python
import functools
import math

import jax
import jax.numpy as jnp
from jax import lax
from jax.experimental import pallas as pl
from jax.experimental.pallas import tpu as pltpu

F32 = jnp.float32
BF16 = jnp.bfloat16
I32 = jnp.int32
U32 = jnp.uint32
EPS = 1e-6

LANES = 128
SUBLANES = 8
VMEM_LIMIT = 56 * 1024 * 1024

SSM_SLABS = 4
SCAN_LANES = 512
PEER_TOPK = 16
SGU_CHUNK = 128


def _pick(n, candidates):
    for c in candidates:
        if n % c == 0:
            return c
    raise ValueError(f"no block size for {n} in {candidates}")


def _const_spec(shape):
    nd = len(shape)
    return pl.BlockSpec(shape, lambda *_: (0,) * nd, pipeline_mode=pl.Buffered(1))


def _rms(x, g):
    return x * lax.rsqrt(jnp.mean(x * x, axis=-1, keepdims=True) + EPS) * g


def _inproj_kernel(x_ref, g_ref, w_ref, lng_ref, lnb_ref, u_ref, su_ref, vn_ref, gates_ref, xn_sc):
    j = pl.program_id(1)

    @pl.when(j == 0)
    def _():
        xn_sc[...] = _rms(x_ref[...], g_ref[...]).astype(BF16)

    z = jnp.dot(xn_sc[...], w_ref[...], preferred_element_type=F32)

    @pl.when(j == 0)
    def _():
        u_ref[...] = z

    @pl.when(j == 1)
    def _():
        su_ref[...] = jax.nn.gelu(z).astype(BF16)

    @pl.when(j == 2)
    def _():
        v = jax.nn.gelu(z)
        mu = jnp.mean(v, axis=-1, keepdims=True)
        vc = v - mu
        y = vc * lax.rsqrt(jnp.mean(vc * vc, axis=-1, keepdims=True) + EPS)
        vn_ref[...] = y * lng_ref[...] + lnb_ref[...]

    @pl.when(j >= 3)
    def _():
        gates_ref[...] = jax.nn.sigmoid(z).astype(BF16)


def _inproj(x, g, w_in_bf, ln_g, ln_b, d_ssm, d_sgu):
    rows, dm = x.shape
    d_in = w_in_bf.shape[1]
    tn = d_ssm
    assert d_sgu == tn and (d_in - 3 * tn) % tn == 0
    nj = d_in // tn
    tm = _pick(rows, (512, 256, 128))
    return pl.pallas_call(
        _inproj_kernel,
        grid=(rows // tm, nj),
        in_specs=[
            pl.BlockSpec((tm, dm), lambda i, j: (i, 0)),
            pl.BlockSpec((1, dm), lambda i, j: (0, 0)),
            pl.BlockSpec((dm, tn), lambda i, j: (0, j)),
            pl.BlockSpec((1, tn), lambda i, j: (0, 0)),
            pl.BlockSpec((1, tn), lambda i, j: (0, 0)),
        ],
        out_specs=[
            pl.BlockSpec((tm, tn), lambda i, j: (i, 0)),
            pl.BlockSpec((tm, tn), lambda i, j: (i, 0)),
            pl.BlockSpec((tm, tn), lambda i, j: (i, 0)),
            pl.BlockSpec((tm, tn), lambda i, j: (i, jnp.maximum(j - 3, 0))),
        ],
        out_shape=[
            jax.ShapeDtypeStruct((rows, tn), F32),
            jax.ShapeDtypeStruct((rows, tn), BF16),
            jax.ShapeDtypeStruct((rows, tn), F32),
            jax.ShapeDtypeStruct((rows, d_in - 3 * tn), BF16),
        ],
        scratch_shapes=[pltpu.VMEM((tm, dm), BF16)],
        compiler_params=pltpu.CompilerParams(
            dimension_semantics=("arbitrary", "arbitrary"), vmem_limit_bytes=VMEM_LIMIT),
        name="inproj",
    )(x, g, w_in_bf, ln_g, ln_b)


def _s5_kernel(u_ref, h0r_ref, h0i_ref, bs_ref, cs_ref, d_ref, gluw_ref, glub_ref, m_ref, pw_ref,
               ya_ref, hlr_ref, hli_ref, hr_sc, hi_sc, cr_sc, ci_sc, *, seq_len):
    i = pl.program_id(0)
    tl, d_ssm = u_ref.shape
    n_state = hr_sc.shape[1]
    kin = d_ssm // SSM_SLABS
    kst = n_state // SSM_SLABS

    @pl.when(i == 0)
    def _():
        cr_sc[...] = jnp.zeros_like(cr_sc)
        ci_sc[...] = jnp.zeros_like(ci_sc)

    ub = u_ref[...].astype(BF16)
    for s in range(SSM_SLABS):
        r = jnp.dot(ub[:, kin * s:kin * (s + 1)], bs_ref[s], preferred_element_type=F32)
        hr_sc[:, kst * s:kst * (s + 1)] = r[:, :kst]
        hi_sc[:, kst * s:kst * (s + 1)] = r[:, kst:]

    n_tiles = tl // SUBLANES
    for lc in range(n_state // SCAN_LANES):
        ls = slice(lc * SCAN_LANES, (lc + 1) * SCAN_LANES)

        def tile_body(t, carry, ls=ls):
            cr, ci = carry
            row0 = i * tl + t * SUBLANES
            b = row0 // seq_len
            is_start = (row0 % seq_len) == 0
            h0r = jnp.broadcast_to(h0r_ref[pl.ds(b, 1), ls], (SUBLANES, SCAN_LANES))
            h0i = jnp.broadcast_to(h0i_ref[pl.ds(b, 1), ls], (SUBLANES, SCAN_LANES))
            cr = jnp.where(is_start, h0r, cr)
            ci = jnp.where(is_start, h0i, ci)
            r0 = pl.multiple_of(t * SUBLANES, SUBLANES)
            br = hr_sc[pl.ds(r0, SUBLANES), ls]
            bi = hi_sc[pl.ds(r0, SUBLANES), ls]
            for k, shift in enumerate((1, 2, 4)):
                mr = m_ref[2 * k, :, ls]
                mi = m_ref[2 * k + 1, :, ls]
                sr = pltpu.roll(br, shift, 0)
                si = pltpu.roll(bi, shift, 0)
                br, bi = br + (mr * sr - mi * si), bi + (mr * si + mi * sr)
            pr = pw_ref[0, :, ls]
            pi_ = pw_ref[1, :, ls]
            hr = br + (pr * cr - pi_ * ci)
            hi = bi + (pr * ci + pi_ * cr)
            hr_sc[pl.ds(r0, SUBLANES), ls] = hr
            hi_sc[pl.ds(r0, SUBLANES), ls] = hi
            last_r = hr[SUBLANES - 1:SUBLANES, :]
            last_i = hi[SUBLANES - 1:SUBLANES, :]
            hlr_ref[pl.ds(b, 1), ls] = last_r
            hli_ref[pl.ds(b, 1), ls] = last_i
            return (jnp.broadcast_to(last_r, (SUBLANES, SCAN_LANES)),
                    jnp.broadcast_to(last_i, (SUBLANES, SCAN_LANES)))

        cr, ci = lax.fori_loop(0, n_tiles, tile_body, (cr_sc[:, ls], ci_sc[:, ls]))
        cr_sc[:, ls] = cr
        ci_sc[:, ls] = ci

    ys = []
    for s in range(SSM_SLABS):
        hre = hr_sc[:, kst * s:kst * (s + 1)].astype(BF16)
        him = hi_sc[:, kst * s:kst * (s + 1)].astype(BF16)
        ys.append(jnp.dot(hre, cs_ref[s, :kst, :], preferred_element_type=F32)
                  + jnp.dot(him, cs_ref[s, kst:, :], preferred_element_type=F32))
    y = jnp.concatenate(ys, axis=1)
    y = jax.nn.gelu(y + d_ref[...] * u_ref[...])
    gt = jnp.dot(y.astype(BF16), gluw_ref[...], preferred_element_type=F32) + glub_ref[...]
    ya_ref[...] = (y * jax.nn.sigmoid(gt)).astype(BF16)


def _s5(u, h0r, h0i, tabs, seq_len):
    rows, d_ssm = u.shape
    nb, n_state = h0r.shape
    tl = _pick(rows, (256, 128, 64, 32, 16, 8)) if seq_len >= 256 else _pick(rows, (512, 256, 128, 64, 32, 16))
    assert (seq_len % tl == 0 or tl % seq_len == 0) and seq_len % SUBLANES == 0
    bs, cs, d, gluw, glub, mtab, ptab = tabs
    return pl.pallas_call(
        functools.partial(_s5_kernel, seq_len=seq_len),
        grid=(rows // tl,),
        in_specs=[
            pl.BlockSpec((tl, d_ssm), lambda i: (i, 0)),
            _const_spec((nb, n_state)),
            _const_spec((nb, n_state)),
            _const_spec(bs.shape),
            _const_spec(cs.shape),
            _const_spec(d.shape),
            _const_spec(gluw.shape),
            _const_spec(glub.shape),
            _const_spec(mtab.shape),
            _const_spec(ptab.shape),
        ],
        out_specs=[
            pl.BlockSpec((tl, d_ssm), lambda i: (i, 0)),
            pl.BlockSpec((nb, n_state), lambda i: (0, 0)),
            pl.BlockSpec((nb, n_state), lambda i: (0, 0)),
        ],
        out_shape=[
            jax.ShapeDtypeStruct((rows, d_ssm), BF16),
            jax.ShapeDtypeStruct((nb, n_state), F32),
            jax.ShapeDtypeStruct((nb, n_state), F32),
        ],
        scratch_shapes=[
            pltpu.VMEM((tl, n_state), F32),
            pltpu.VMEM((tl, n_state), F32),
            pltpu.VMEM((SUBLANES, n_state), F32),
            pltpu.VMEM((SUBLANES, n_state), F32),
        ],
        compiler_params=pltpu.CompilerParams(
            dimension_semantics=("arbitrary",), vmem_limit_bytes=VMEM_LIMIT),
        name="s5",
    )(u, h0r, h0i, bs, cs, d, gluw, glub, mtab, ptab)


def _mix_kernel(x_ref, ya_ref, su_ref, vn_ref, gates_ref, ws_ref, bias_ref, wa_ref, wb_ref, wo_ref, x1_ref):
    tm, dm = x_ref.shape
    n_heads = ws_ref.shape[0]
    hd = vn_ref.shape[1] // n_heads
    vb = vn_ref[...].astype(BF16)
    chunks = []
    for c in range(tm // SGU_CHUNK):
        heads = [jnp.dot(ws_ref[h], vb[c * SGU_CHUNK:(c + 1) * SGU_CHUNK, h * hd:(h + 1) * hd],
                         preferred_element_type=F32) for h in range(n_heads)]
        chunks.append(jnp.concatenate(heads, axis=1) + bias_ref[...])
    s_all = jnp.concatenate(chunks, axis=0)
    yb = (su_ref[...].astype(F32) * s_all).astype(BF16)
    pa = jnp.dot(ya_ref[...], wa_ref[...], preferred_element_type=F32)
    pb = jnp.dot(yb, wb_ref[...], preferred_element_type=F32)
    merged = gates_ref[:, :dm].astype(F32) * pa + gates_ref[:, dm:].astype(F32) * pb
    x1_ref[...] = x_ref[...] + jnp.dot(merged.astype(BF16), wo_ref[...], preferred_element_type=F32)


def _mix(x, ya, su, vn, gates, ws, bias, wa, wb, wo):
    rows, dm = x.shape
    d_ssm = ya.shape[1]
    d_sgu = su.shape[1]
    tm = _pick(rows, (256, 128))
    row = lambda w: pl.BlockSpec((tm, w), lambda i: (i, 0))
    return pl.pallas_call(
        _mix_kernel,
        grid=(rows // tm,),
        in_specs=[row(dm), row(d_ssm), row(d_sgu), row(d_sgu), row(2 * dm),
                  _const_spec(ws.shape), _const_spec(bias.shape),
                  _const_spec(wa.shape), _const_spec(wb.shape), _const_spec(wo.shape)],
        out_specs=row(dm),
        out_shape=jax.ShapeDtypeStruct((rows, dm), F32),
        compiler_params=pltpu.CompilerParams(
            dimension_semantics=("arbitrary",), vmem_limit_bytes=VMEM_LIMIT),
        name="mix",
    )(x, ya, su, vn, gates, ws, bias, wa, wb, wo)


def _topk_rows(s, k):
    rows = s.shape[0]
    iota = lax.broadcasted_iota(I32, s.shape, 0)
    vals, idxs = [], []
    for _ in range(k):
        m = jnp.max(s, axis=0, keepdims=True)
        j = jnp.min(jnp.where(s == m, iota, rows), axis=0, keepdims=True)
        vals.append(m)
        idxs.append(j)
        s = jnp.where(iota == j, -jnp.inf, s)
    return jnp.concatenate(vals, axis=0), jnp.concatenate(idxs, axis=0)


def _route_kernel(x1_ref, g_ref, wq_ref, keys_ref, xn_ref, eidx_ref, gate_ref, q_sc, e_sc, p_sc):
    tb = x1_ref.shape[0]
    n_heads, _, n_keys, half = keys_ref.shape
    dkey = 2 * half
    xn = _rms(x1_ref[...], g_ref[...])
    xn_ref[...] = xn
    q = jnp.dot(xn.astype(BF16), wq_ref[...], preferred_element_type=F32).astype(BF16)
    for h in range(n_heads):
        q_sc[h] = q[:, h * dkey:(h + 1) * dkey]

    nt = (((1,), (1,)), ((), ()))

    def head_body(h, _):
        qh = q_sc[h]
        s1 = lax.dot_general(keys_ref[h, 0], qh[:, :half], nt, preferred_element_type=F32)
        s2 = lax.dot_general(keys_ref[h, 1], qh[:, half:], nt, preferred_element_type=F32)
        v1, i1 = _topk_rows(s1, PEER_TOPK)
        v2, i2 = _topk_rows(s2, PEER_TOPK)
        cv, ci = [], []
        for a in range(PEER_TOPK):
            nb = PEER_TOPK // (a + 1)
            cv.append(v1[a:a + 1, :] + v2[:nb, :])
            ci.append(i1[a:a + 1, :] * n_keys + i2[:nb, :])
        cand = jnp.concatenate(cv, axis=0)
        cidx = jnp.concatenate(ci, axis=0)
        ts, tj = _topk_rows(cand, PEER_TOPK)
        iota = lax.broadcasted_iota(I32, cand.shape, 0)
        es = [jnp.sum(jnp.where(iota == tj[r:r + 1, :], cidx, 0), axis=0, keepdims=True)
              for r in range(PEER_TOPK)]
        ex = jnp.exp(ts - ts[0:1, :])
        e_sc[h] = jnp.concatenate(es, axis=0).astype(F32)
        p_sc[h] = ex / jnp.sum(ex, axis=0, keepdims=True)
        return 0

    lax.fori_loop(0, n_heads, head_body, 0)
    e_all = jnp.concatenate([e_sc[h] for h in range(n_heads)], axis=0)
    p_all = jnp.concatenate([p_sc[h] for h in range(n_heads)], axis=0)
    eidx_ref[...] = e_all.T.astype(I32)
    gate_ref[...] = p_all.T


def _route(x1, g, wq, keys_bf):
    rows, dm = x1.shape
    n_heads, _, n_keys, half = keys_bf.shape
    nsel = n_heads * PEER_TOPK
    tb = _pick(rows, (256, 128))
    row = lambda w: pl.BlockSpec((tb, w), lambda i: (i, 0))
    return pl.pallas_call(
        _route_kernel,
        grid=(rows // tb,),
        in_specs=[row(dm), _const_spec(g.shape), _const_spec(wq.shape), _const_spec(keys_bf.shape)],
        out_specs=[row(dm), row(nsel), row(nsel)],
        out_shape=[
            jax.ShapeDtypeStruct((rows, dm), F32),
            jax.ShapeDtypeStruct((rows, nsel), I32),
            jax.ShapeDtypeStruct((rows, nsel), F32),
        ],
        scratch_shapes=[
            pltpu.VMEM((n_heads, tb, 2 * half), BF16),
            pltpu.VMEM((n_heads, PEER_TOPK, tb), F32),
            pltpu.VMEM((n_heads, PEER_TOPK, tb), F32),
        ],
        compiler_params=pltpu.CompilerParams(
            dimension_semantics=("arbitrary",), vmem_limit_bytes=VMEM_LIMIT),
        name="route",
    )(x1, g, wq, keys_bf)


PEER_BUFS = 4


def _peer_kernel(eidx_hbm, gate_ref, xn_ref, x1_ref, gf_ref, tab_hbm, y_ref,
                 idx_sm, idx_sem, buf, sem, peer_sc, *, final_norm):
    i = pl.program_id(0)
    tbk, nsel = gate_ref.shape
    half = tab_hbm.shape[1] // 2
    nlt = half // LANES

    icp = pltpu.make_async_copy(eidx_hbm.at[i], idx_sm, idx_sem)
    icp.start()
    icp.wait()

    def issue(t, slot):
        base = t * nsel
        for k in range(nsel):
            e = idx_sm[base + k]
            pltpu.make_async_copy(tab_hbm.at[pl.ds(e, 1), :], buf.at[slot, pl.ds(k, 1), :],
                                  sem.at[slot]).start()

    def wait(slot):
        pltpu.make_async_copy(tab_hbm.at[pl.ds(0, nsel), :], buf.at[slot], sem.at[slot]).wait()

    for t in range(PEER_BUFS - 1):
        issue(t, t)

    hi_mask = jnp.uint32(0xFFFF0000)
    eye = (lax.broadcasted_iota(I32, (nsel, nsel), 0) == lax.broadcasted_iota(I32, (nsel, nsel), 1))

    def unpack(w):
        return (lax.bitcast_convert_type(w & hi_mask, F32),
                lax.bitcast_convert_type(w << 16, F32))

    def token_body(t, _):
        slot = t % PEER_BUFS
        nxt = t + PEER_BUFS - 1

        @pl.when(nxt < tbk)
        def _():
            issue(nxt, nxt % PEER_BUFS)

        wait(slot)
        xrow = xn_ref[pl.ds(t, 1), :]
        acc = jnp.zeros((nsel, LANES), F32)
        for j in range(nlt):
            uh, ul = unpack(buf[slot, :, j * LANES:(j + 1) * LANES])
            acc = acc + uh * xrow[:, j * LANES:(j + 1) * LANES] \
                      + ul * xrow[:, half + j * LANES:half + (j + 1) * LANES]
        act = jnp.sum(acc, axis=1, keepdims=True)
        grow = jnp.broadcast_to(gate_ref[pl.ds(t, 1), :], (nsel, nsel))
        gcol = jnp.sum(jnp.where(eye, grow, 0.0), axis=1, keepdims=True)
        wgt = jnp.broadcast_to(gcol * jax.nn.gelu(act), (nsel, LANES))
        t8 = pl.multiple_of((t // SUBLANES) * SUBLANES, SUBLANES)
        rsel = lax.broadcasted_iota(I32, (SUBLANES, LANES), 0) == (t % SUBLANES)

        def put(lane0, row):
            cur = peer_sc[pl.ds(t8, SUBLANES), lane0:lane0 + LANES]
            peer_sc[pl.ds(t8, SUBLANES), lane0:lane0 + LANES] = jnp.where(
                rsel, jnp.broadcast_to(row, (SUBLANES, LANES)), cur)

        for j in range(nlt):
            vh, vl = unpack(buf[slot, :, half + j * LANES:half + (j + 1) * LANES])
            put(j * LANES, jnp.sum(vh * wgt, axis=0, keepdims=True))
            put(half + j * LANES, jnp.sum(vl * wgt, axis=0, keepdims=True))
        return 0

    peer_sc[...] = jnp.zeros_like(peer_sc)
    lax.fori_loop(0, tbk, token_body, 0)
    xo = x1_ref[...] + peer_sc[...]
    y_ref[...] = _rms(xo, gf_ref[...]) if final_norm else xo


def _peer(eidx, gate, xn, x1, gf, table, final_norm):
    rows, dm = x1.shape
    nsel = gate.shape[1]
    tbk = _pick(rows, (64, 32, 16, 8))
    nblk = rows // tbk
    eidx_blk = eidx.reshape(nblk, tbk * nsel)
    row = lambda w: pl.BlockSpec((tbk, w), lambda i: (i, 0))
    return pl.pallas_call(
        functools.partial(_peer_kernel, final_norm=final_norm),
        grid=(nblk,),
        in_specs=[pl.BlockSpec(memory_space=pl.ANY), row(nsel), row(dm), row(dm),
                  _const_spec(gf.shape), pl.BlockSpec(memory_space=pl.ANY)],
        out_specs=row(dm),
        out_shape=jax.ShapeDtypeStruct((rows, dm), F32),
        scratch_shapes=[
            pltpu.SMEM((tbk * nsel,), I32),
            pltpu.SemaphoreType.DMA(()),
            pltpu.VMEM((PEER_BUFS, nsel, table.shape[1]), U32),
            pltpu.SemaphoreType.DMA((PEER_BUFS,)),
            pltpu.VMEM((tbk, dm), F32),
        ],
        compiler_params=pltpu.CompilerParams(
            dimension_semantics=("arbitrary",), vmem_limit_bytes=VMEM_LIMIT),
        name="peer",
    )(eidx_blk, gate, xn, x1, gf, table)


def _cmul(ar, ai, br, bi):
    return ar * br - ai * bi, ar * bi + ai * br


def _s5_tables(lam_re, lam_im, log_dt, b_re, b_im, c_re, c_im, d, glu_w, glu_b):
    g, p = lam_re.shape
    hg = b_re.shape[-1]
    gs = g // SSM_SLABS
    dt = jnp.exp(log_dt.astype(F32))[:, None]
    lr = lam_re.astype(F32)
    li = lam_im.astype(F32)
    mag = jnp.exp(lr * dt)
    ab_re = mag * jnp.cos(li * dt)
    ab_im = mag * jnp.sin(li * dt)
    den = lr * lr + li * li
    nr = ab_re - 1.0
    ni = ab_im
    coef_re = (nr * lr + ni * li) / den
    coef_im = (ni * lr - nr * li) / den
    br = b_re.astype(F32)
    bi = b_im.astype(F32)
    bb_re = coef_re[..., None] * br - coef_im[..., None] * bi
    bb_im = coef_re[..., None] * bi + coef_im[..., None] * br
    eye = jnp.eye(gs, dtype=F32)

    def b_slab(bb):
        t = bb.reshape(SSM_SLABS, gs, p, hg).transpose(0, 1, 3, 2)
        return jnp.einsum('sihp,ij->sihjp', t, eye).reshape(SSM_SLABS, gs * hg, gs * p)

    bs = jnp.concatenate([b_slab(bb_re), b_slab(bb_im)], axis=2).astype(BF16)

    def c_slab(c):
        t = c.astype(F32).reshape(SSM_SLABS, gs, hg, p).transpose(0, 1, 3, 2)
        return jnp.einsum('siph,ij->sipjh', t, eye).reshape(SSM_SLABS, gs * p, gs * hg)

    cs = jnp.concatenate([c_slab(c_re), -c_slab(c_im)], axis=1).astype(BF16)

    a1 = (ab_re.reshape(1, g * p), ab_im.reshape(1, g * p))
    a2 = _cmul(*a1, *a1)
    a4 = _cmul(*a2, *a2)
    rowid = jnp.arange(SUBLANES)[:, None]
    mtab = jnp.stack([jnp.where(rowid >= s, comp, 0.0)
                      for s, a in ((1, a1), (2, a2), (4, a4)) for comp in a])
    pows = [a1]
    for _ in range(SUBLANES - 1):
        pows.append(_cmul(*pows[-1], *a1))
    ptab = jnp.stack([jnp.concatenate([q[0] for q in pows], axis=0),
                      jnp.concatenate([q[1] for q in pows], axis=0)])
    return (bs, cs, d.astype(F32).reshape(1, -1), glu_w.astype(BF16),
            glu_b.astype(F32).reshape(1, -1), mtab, ptab)


def _sgu_tables(w_s, b_s, seq_len):
    n_heads = w_s.shape[0]
    ln = min(SGU_CHUNK, seq_len)
    assert SGU_CHUNK % ln == 0 and seq_len % ln == 0
    rep = SGU_CHUNK // ln
    mask = jnp.tril(jnp.ones((ln, ln), dtype=bool))
    w = jnp.where(mask[None], w_s[:, :ln, :ln], 0.0).astype(F32)
    ws = jnp.einsum('hij,ab->haibj', w, jnp.eye(rep, dtype=F32)).reshape(n_heads, SGU_CHUNK, SGU_CHUNK)
    bias = jnp.tile(jnp.transpose(b_s[:, :ln]).astype(F32), (rep, 1))
    return ws.astype(BF16), bias


def _pack_rows(t):
    n, dd = t.shape
    b = lax.bitcast_convert_type(t.astype(BF16), jnp.uint16).astype(U32)
    return (b[:, :dd // 2] << 16) | b[:, dd // 2:]


def _trunk_layer(x, h0r, h0i, seq_len, lp, final_g):
    (norm_mix_g, w_in_bf, s5_tabs, ln_g, ln_b, sgu_w, sgu_b, wa, wb, wo, norm_ffn_g, wq, keys_bf, table,
     d_ssm, d_sgu) = lp
    u, su, vn, gates = _inproj(x, norm_mix_g, w_in_bf, ln_g, ln_b, d_ssm, d_sgu)
    ya, hlr, hli = _s5(u, h0r, h0i, s5_tabs, seq_len)
    ws, bias = _sgu_tables(sgu_w, sgu_b, seq_len)
    hd = d_sgu // ws.shape[0]
    bias_tile = jnp.repeat(bias, hd, axis=1)
    x1 = _mix(x, ya, su, vn, gates, ws, bias_tile, wa, wb, wo)
    xn, eidx, gate = _route(x1, norm_ffn_g, wq, keys_bf)
    y = _peer(eidx, gate, xn, x1, final_g, table, final_norm=True)
    return y, hlr, hli, vn


def kernel(x_prompt, x_sample, state_ssm_re, state_ssm_im, norm_mix_g, w_in, ssm_lambda_re, ssm_lambda_im, ssm_log_dt, ssm_b_re, ssm_b_im, ssm_c_re, ssm_c_im, ssm_d, ssm_glu_w, ssm_glu_b, sgu_ln_g, sgu_ln_b, sgu_w, sgu_b, w_branch_a, w_branch_b, w_out, norm_ffn_g, peer_w_q, peer_keys, peer_u, peer_v, norm_final_g):
    depth = w_in.shape[0]
    assert depth == 1, "the final norm is fused into the last layer's PEER kernel"
    bp, sp, dm = x_prompt.shape
    bs_, ss, _ = x_sample.shape
    g, p = ssm_lambda_re.shape[1:]
    d_ssm = ssm_d.shape[1]
    d_sgu = sgu_ln_g.shape[1]
    n_state = g * p
    l = 0
    lp = (norm_mix_g[l].reshape(1, dm), w_in[l].astype(BF16),
          _s5_tables(ssm_lambda_re[l], ssm_lambda_im[l], ssm_log_dt[l], ssm_b_re[l], ssm_b_im[l],
                     ssm_c_re[l], ssm_c_im[l], ssm_d[l], ssm_glu_w[l], ssm_glu_b[l]),
          sgu_ln_g[l].reshape(1, d_sgu), sgu_ln_b[l].reshape(1, d_sgu), sgu_w[l], sgu_b[l],
          w_branch_a[l].astype(BF16), w_branch_b[l].astype(BF16), w_out[l].astype(BF16),
          norm_ffn_g[l].reshape(1, dm), peer_w_q[l].astype(BF16), peer_keys[l].astype(BF16),
          jnp.concatenate([_pack_rows(peer_u[l]), _pack_rows(peer_v[l])], axis=1),
          d_ssm, d_sgu)
    gf = norm_final_g.reshape(1, dm)

    zeros = jnp.zeros((bp, n_state), F32)
    yp, hr_p, hi_p, _ = _trunk_layer(x_prompt.reshape(bp * sp, dm), zeros, zeros, sp, lp, gf)
    ys, hr_s, hi_s, v_s = _trunk_layer(x_sample.reshape(bs_ * ss, dm),
                                       state_ssm_re[l].astype(F32).reshape(bs_, n_state),
                                       state_ssm_im[l].astype(F32).reshape(bs_, n_state), ss, lp, gf)
    return (yp.reshape(bp, sp, dm), ys.reshape(bs_, ss, dm),
            hr_p.reshape(1, bp, g, p), hi_p.reshape(1, bp, g, p),
            hr_s.reshape(1, bs_, g, p), hi_s.reshape(1, bs_, g, p),
            v_s.reshape(1, bs_, ss, d_sgu))
```

```python
import functools
import math

import jax
import jax.numpy as jnp
from jax import lax
from jax.experimental import pallas as pl
from jax.experimental.pallas import tpu as pltpu

F32 = jnp.float32
BF16 = jnp.bfloat16
I32 = jnp.int32
U32 = jnp.uint32
EPS = 1e-6

LANES = 128
SUBLANES = 8
VMEM_LIMIT = 56 * 1024 * 1024

SSM_SLABS = 4
SCAN_LANES = 512
PEER_TOPK = 16
SGU_CHUNK = 128


def _pick(n, candidates):
    for c in candidates:
        if n % c == 0:
            return c
    raise ValueError(f"no block size for {n} in {candidates}")


def _const_spec(shape):
    nd = len(shape)
    return pl.BlockSpec(shape, lambda *_: (0,) * nd, pipeline_mode=pl.Buffered(1))


def _rms(x, g):
    return x * lax.rsqrt(jnp.mean(x * x, axis=-1, keepdims=True) + EPS) * g


def _inproj_kernel(x_ref, g_ref, w_ref, lng_ref, lnb_ref, u_ref, su_ref, vn_ref, gates_ref, xn_sc):
    j = pl.program_id(1)

    @pl.when(j == 0)
    def _():
        xn_sc[...] = _rms(x_ref[...], g_ref[...]).astype(BF16)

    z = jnp.dot(xn_sc[...], w_ref[...], preferred_element_type=F32)

    @pl.when(j == 0)
    def _():
        u_ref[...] = z

    @pl.when(j == 1)
    def _():
        su_ref[...] = jax.nn.gelu(z).astype(BF16)

    @pl.when(j == 2)
    def _():
        v = jax.nn.gelu(z)
        mu = jnp.mean(v, axis=-1, keepdims=True)
        vc = v - mu
        y = vc * lax.rsqrt(jnp.mean(vc * vc, axis=-1, keepdims=True) + EPS)
        vn_ref[...] = y * lng_ref[...] + lnb_ref[...]

    @pl.when(j >= 3)
    def _():
        gates_ref[...] = jax.nn.sigmoid(z).astype(BF16)


def _inproj(x, g, w_in_bf, ln_g, ln_b, d_ssm, d_sgu):
    rows, dm = x.shape
    d_in = w_in_bf.shape[1]
    tn = d_ssm
    assert d_sgu == tn and (d_in - 3 * tn) % tn == 0
    nj = d_in // tn
    tm = _pick(rows, (512, 256, 128))
    return pl.pallas_call(
        _inproj_kernel,
        grid=(rows // tm, nj),
        in_specs=[
            pl.BlockSpec((tm, dm), lambda i, j: (i, 0)),
            pl.BlockSpec((1, dm), lambda i, j: (0, 0)),
            pl.BlockSpec((dm, tn), lambda i, j: (0, j)),
            pl.BlockSpec((1, tn), lambda i, j: (0, 0)),
            pl.BlockSpec((1, tn), lambda i, j: (0, 0)),
        ],
        out_specs=[
            pl.BlockSpec((tm, tn), lambda i, j: (i, 0)),
            pl.BlockSpec((tm, tn), lambda i, j: (i, 0)),
            pl.BlockSpec((tm, tn), lambda i, j: (i, 0)),
            pl.BlockSpec((tm, tn), lambda i, j: (i, jnp.maximum(j - 3, 0))),
        ],
        out_shape=[
            jax.ShapeDtypeStruct((rows, tn), F32),
            jax.ShapeDtypeStruct((rows, tn), BF16),
            jax.ShapeDtypeStruct((rows, tn), F32),
            jax.ShapeDtypeStruct((rows, d_in - 3 * tn), BF16),
        ],
        scratch_shapes=[pltpu.VMEM((tm, dm), BF16)],
        compiler_params=pltpu.CompilerParams(
            dimension_semantics=("arbitrary", "arbitrary"), vmem_limit_bytes=VMEM_LIMIT),
        name="inproj",
    )(x, g, w_in_bf, ln_g, ln_b)


def _s5_kernel(u_ref, h0r_ref, h0i_ref, bs_ref, cs_ref, d_ref, gluw_ref, glub_ref, m_ref, pw_ref,
               ya_ref, hlr_ref, hli_ref, hr_sc, hi_sc, cr_sc, ci_sc, *, seq_len):
    i = pl.program_id(0)
    tl, d_ssm = u_ref.shape
    n_state = hr_sc.shape[1]
    kin = d_ssm // SSM_SLABS
    kst = n_state // SSM_SLABS

    @pl.when(i == 0)
    def _():
        cr_sc[...] = jnp.zeros_like(cr_sc)
        ci_sc[...] = jnp.zeros_like(ci_sc)

    ub = u_ref[...].astype(BF16)
    for s in range(SSM_SLABS):
        r = jnp.dot(ub[:, kin * s:kin * (s + 1)], bs_ref[s], preferred_element_type=F32)
        hr_sc[:, kst * s:kst * (s + 1)] = r[:, :kst]
        hi_sc[:, kst * s:kst * (s + 1)] = r[:, kst:]

    n_tiles = tl // SUBLANES
    for lc in range(n_state // SCAN_LANES):
        ls = slice(lc * SCAN_LANES, (lc + 1) * SCAN_LANES)

        def tile_body(t, carry, ls=ls):
            cr, ci = carry
            row0 = i * tl + t * SUBLANES
            b = row0 // seq_len
            is_start = (row0 % seq_len) == 0
            h0r = jnp.broadcast_to(h0r_ref[pl.ds(b, 1), ls], (SUBLANES, SCAN_LANES))
            h0i = jnp.broadcast_to(h0i_ref[pl.ds(b, 1), ls], (SUBLANES, SCAN_LANES))
            cr = jnp.where(is_start, h0r, cr)
            ci = jnp.where(is_start, h0i, ci)
            r0 = pl.multiple_of(t * SUBLANES, SUBLANES)
            br = hr_sc[pl.ds(r0, SUBLANES), ls]
            bi = hi_sc[pl.ds(r0, SUBLANES), ls]
            for k, shift in enumerate((1, 2, 4)):
                mr = m_ref[2 * k, :, ls]
                mi = m_ref[2 * k + 1, :, ls]
                sr = pltpu.roll(br, shift, 0)
                si = pltpu.roll(bi, shift, 0)
                br, bi = br + (mr * sr - mi * si), bi + (mr * si + mi * sr)
            pr = pw_ref[0, :, ls]
            pi_ = pw_ref[1, :, ls]
            hr = br + (pr * cr - pi_ * ci)
            hi = bi + (pr * ci + pi_ * cr)
            hr_sc[pl.ds(r0, SUBLANES), ls] = hr
            hi_sc[pl.ds(r0, SUBLANES), ls] = hi
            last_r = hr[SUBLANES - 1:SUBLANES, :]
            last_i = hi[SUBLANES - 1:SUBLANES, :]
            hlr_ref[pl.ds(b, 1), ls] = last_r
            hli_ref[pl.ds(b, 1), ls] = last_i
            return (jnp.broadcast_to(last_r, (SUBLANES, SCAN_LANES)),
                    jnp.broadcast_to(last_i, (SUBLANES, SCAN_LANES)))

        cr, ci = lax.fori_loop(0, n_tiles, tile_body, (cr_sc[:, ls], ci_sc[:, ls]))
        cr_sc[:, ls] = cr
        ci_sc[:, ls] = ci

    ys = []
    for s in range(SSM_SLABS):
        hre = hr_sc[:, kst * s:kst * (s + 1)].astype(BF16)
        him = hi_sc[:, kst * s:kst * (s + 1)].astype(BF16)
        ys.append(jnp.dot(hre, cs_ref[s, :kst, :], preferred_element_type=F32)
                  + jnp.dot(him, cs_ref[s, kst:, :], preferred_element_type=F32))
    y = jnp.concatenate(ys, axis=1)
    y = jax.nn.gelu(y + d_ref[...] * u_ref[...])
    gt = jnp.dot(y.astype(BF16), gluw_ref[...], preferred_element_type=F32) + glub_ref[...]
    ya_ref[...] = (y * jax.nn.sigmoid(gt)).astype(BF16)


def _s5(u, h0r, h0i, tabs, seq_len):
    rows, d_ssm = u.shape
    nb, n_state = h0r.shape
    tl = _pick(rows, (256, 128, 64, 32, 16, 8)) if seq_len >= 256 else _pick(rows, (512, 256, 128, 64, 32, 16))
    assert (seq_len % tl == 0 or tl % seq_len == 0) and seq_len % SUBLANES == 0
    bs, cs, d, gluw, glub, mtab, ptab = tabs
    return pl.pallas_call(
        functools.partial(_s5_kernel, seq_len=seq_len),
        grid=(rows // tl,),
        in_specs=[
            pl.BlockSpec((tl, d_ssm), lambda i: (i, 0)),
            _const_spec((nb, n_state)),
            _const_spec((nb, n_state)),
            _const_spec(bs.shape),
            _const_spec(cs.shape),
            _const_spec(d.shape),
            _const_spec(gluw.shape),
            _const_spec(glub.shape),
            _const_spec(mtab.shape),
            _const_spec(ptab.shape),
        ],
        out_specs=[
            pl.BlockSpec((tl, d_ssm), lambda i: (i, 0)),
            pl.BlockSpec((nb, n_state), lambda i: (0, 0)),
            pl.BlockSpec((nb, n_state), lambda i: (0, 0)),
        ],
        out_shape=[
            jax.ShapeDtypeStruct((rows, d_ssm), BF16),
            jax.ShapeDtypeStruct((nb, n_state), F32),
            jax.ShapeDtypeStruct((nb, n_state), F32),
        ],
        scratch_shapes=[
            pltpu.VMEM((tl, n_state), F32),
            pltpu.VMEM((tl, n_state), F32),
            pltpu.VMEM((SUBLANES, n_state), F32),
            pltpu.VMEM((SUBLANES, n_state), F32),
        ],
        compiler_params=pltpu.CompilerParams(
            dimension_semantics=("arbitrary",), vmem_limit_bytes=VMEM_LIMIT),
        name="s5",
    )(u, h0r, h0i, bs, cs, d, gluw, glub, mtab, ptab)


def _mix_kernel(x_ref, ya_ref, su_ref, vn_ref, gates_ref, ws_ref, bias_ref, wa_ref, wb_ref, wo_ref, x1_ref):
    tm, dm = x_ref.shape
    n_heads = ws_ref.shape[0]
    hd = vn_ref.shape[1] // n_heads
    vb = vn_ref[...].astype(BF16)
    chunks = []
    for c in range(tm // SGU_CHUNK):
        heads = [jnp.dot(ws_ref[h], vb[c * SGU_CHUNK:(c + 1) * SGU_CHUNK, h * hd:(h + 1) * hd],
                         preferred_element_type=F32) for h in range(n_heads)]
        chunks.append(jnp.concatenate(heads, axis=1) + bias_ref[...])
    s_all = jnp.concatenate(chunks, axis=0)
    yb = (su_ref[...].astype(F32) * s_all).astype(BF16)
    pa = jnp.dot(ya_ref[...], wa_ref[...], preferred_element_type=F32)
    pb = jnp.dot(yb, wb_ref[...], preferred_element_type=F32)
    merged = gates_ref[:, :dm].astype(F32) * pa + gates_ref[:, dm:].astype(F32) * pb
    x1_ref[...] = x_ref[...] + jnp.dot(merged.astype(BF16), wo_ref[...], preferred_element_type=F32)


def _mix(x, ya, su, vn, gates, ws, bias, wa, wb, wo):
    rows, dm = x.shape
    d_ssm = ya.shape[1]
    d_sgu = su.shape[1]
    tm = _pick(rows, (256, 128))
    row = lambda w: pl.BlockSpec((tm, w), lambda i: (i, 0))
    return pl.pallas_call(
        _mix_kernel,
        grid=(rows // tm,),
        in_specs=[row(dm), row(d_ssm), row(d_sgu), row(d_sgu), row(2 * dm),
                  _const_spec(ws.shape), _const_spec(bias.shape),
                  _const_spec(wa.shape), _const_spec(wb.shape), _const_spec(wo.shape)],
        out_specs=row(dm),
        out_shape=jax.ShapeDtypeStruct((rows, dm), F32),
        compiler_params=pltpu.CompilerParams(
            dimension_semantics=("arbitrary",), vmem_limit_bytes=VMEM_LIMIT),
        name="mix",
    )(x, ya, su, vn, gates, ws, bias, wa, wb, wo)


def _topk_rows(s, k):
    rows = s.shape[0]
    iota = lax.broadcasted_iota(I32, s.shape, 0)
    vals, idxs = [], []
    for _ in range(k):
        m = jnp.max(s, axis=0, keepdims=True)
        j = jnp.min(jnp.where(s == m, iota, rows), axis=0, keepdims=True)
        vals.append(m)
        idxs.append(j)
        s = jnp.where(iota == j, -jnp.inf, s)
    return jnp.concatenate(vals, axis=0), jnp.concatenate(idxs, axis=0)


def _route_kernel(x1_ref, g_ref, wq_ref, keys_ref, xn_ref, eidx_ref, gate_ref, q_sc, e_sc, p_sc):
    tb = x1_ref.shape[0]
    n_heads, _, n_keys, half = keys_ref.shape
    dkey = 2 * half
    xn = _rms(x1_ref[...], g_ref[...])
    xn_ref[...] = xn
    q = jnp.dot(xn.astype(BF16), wq_ref[...], preferred_element_type=F32).astype(BF16)
    for h in range(n_heads):
        q_sc[h] = q[:, h * dkey:(h + 1) * dkey]

    nt = (((1,), (1,)), ((), ()))

    def head_body(h, _):
        qh = q_sc[h]
        s1 = lax.dot_general(keys_ref[h, 0], qh[:, :half], nt, preferred_element_type=F32)
        s2 = lax.dot_general(keys_ref[h, 1], qh[:, half:], nt, preferred_element_type=F32)
        v1, i1 = _topk_rows(s1, PEER_TOPK)
        v2, i2 = _topk_rows(s2, PEER_TOPK)
        cv, ci = [], []
        for a in range(PEER_TOPK):
            nb = PEER_TOPK // (a + 1)
            cv.append(v1[a:a + 1, :] + v2[:nb, :])
            ci.append(i1[a:a + 1, :] * n_keys + i2[:nb, :])
        cand = jnp.concatenate(cv, axis=0)
        cidx = jnp.concatenate(ci, axis=0)
        ts, tj = _topk_rows(cand, PEER_TOPK)
        iota = lax.broadcasted_iota(I32, cand.shape, 0)
        es = [jnp.sum(jnp.where(iota == tj[r:r + 1, :], cidx, 0), axis=0, keepdims=True)
              for r in range(PEER_TOPK)]
        ex = jnp.exp(ts - ts[0:1, :])
        e_sc[h] = jnp.concatenate(es, axis=0).astype(F32)
        p_sc[h] = ex / jnp.sum(ex, axis=0, keepdims=True)
        return 0

    lax.fori_loop(0, n_heads, head_body, 0)
    e_all = jnp.concatenate([e_sc[h] for h in range(n_heads)], axis=0)
    p_all = jnp.concatenate([p_sc[h] for h in range(n_heads)], axis=0)
    eidx_ref[...] = e_all.T.astype(I32)
    gate_ref[...] = p_all.T


def _route(x1, g, wq, keys_bf):
    rows, dm = x1.shape
    n_heads, _, n_keys, half = keys_bf.shape
    nsel = n_heads * PEER_TOPK
    tb = _pick(rows, (256, 128))
    row = lambda w: pl.BlockSpec((tb, w), lambda i: (i, 0))
    return pl.pallas_call(
        _route_kernel,
        grid=(rows // tb,),
        in_specs=[row(dm), _const_spec(g.shape), _const_spec(wq.shape), _const_spec(keys_bf.shape)],
        out_specs=[row(dm), row(nsel), row(nsel)],
        out_shape=[
            jax.ShapeDtypeStruct((rows, dm), F32),
            jax.ShapeDtypeStruct((rows, nsel), I32),
            jax.ShapeDtypeStruct((rows, nsel), F32),
        ],
        scratch_shapes=[
            pltpu.VMEM((n_heads, tb, 2 * half), BF16),
            pltpu.VMEM((n_heads, PEER_TOPK, tb), F32),
            pltpu.VMEM((n_heads, PEER_TOPK, tb), F32),
        ],
        compiler_params=pltpu.CompilerParams(
            dimension_semantics=("arbitrary",), vmem_limit_bytes=VMEM_LIMIT),
        name="route",
    )(x1, g, wq, keys_bf)


PEER_BUFS = 4


def _peer_kernel(eidx_hbm, gate_ref, xn_ref, x1_ref, gf_ref, tab_hbm, y_ref,
                 idx_sm, idx_sem, b0, b1, b2, b3, sem, peer_sc, *, final_norm):
    i = pl.program_id(0)
    bufs = (b0, b1, b2, b3)
    tbk, nsel = gate_ref.shape
    half = tab_hbm.shape[1] // 2
    nlt = half // LANES

    icp = pltpu.make_async_copy(eidx_hbm.at[i], idx_sm, idx_sem)
    icp.start()
    icp.wait()

    def issue(t, s):
        base = t * nsel
        for k in range(nsel):
            e = idx_sm[base + k]
            pltpu.make_async_copy(tab_hbm.at[pl.ds(e, 1), :], bufs[s].at[pl.ds(k, 1), :],
                                  sem.at[s]).start(priority=k % 2)

    def wait(s):
        pltpu.make_async_copy(tab_hbm.at[pl.ds(0, nsel), :], bufs[s], sem.at[s]).wait()

    hi_mask = jnp.uint32(0xFFFF0000)
    eye = (lax.broadcasted_iota(I32, (nsel, nsel), 0) == lax.broadcasted_iota(I32, (nsel, nsel), 1))

    def unpack(w):
        return (lax.bitcast_convert_type(w & hi_mask, F32),
                lax.bitcast_convert_type(w << 16, F32))

    def compute(t, s):
        buf = bufs[s]
        xrow = xn_ref[pl.ds(t, 1), :]
        acc = jnp.zeros((nsel, LANES), F32)
        for j in range(nlt):
            uh, ul = unpack(buf[:, j * LANES:(j + 1) * LANES])
            acc = acc + uh * xrow[:, j * LANES:(j + 1) * LANES] \
                      + ul * xrow[:, half + j * LANES:half + (j + 1) * LANES]
        act = jnp.sum(acc, axis=1, keepdims=True)
        grow = jnp.broadcast_to(gate_ref[pl.ds(t, 1), :], (nsel, nsel))
        gcol = jnp.sum(jnp.where(eye, grow, 0.0), axis=1, keepdims=True)
        wgt = jnp.broadcast_to(gcol * jax.nn.gelu(act), (nsel, LANES))
        t8 = pl.multiple_of((t // SUBLANES) * SUBLANES, SUBLANES)
        rsel = lax.broadcasted_iota(I32, (SUBLANES, LANES), 0) == (t % SUBLANES)

        def put(lane0, row):
            cur = peer_sc[pl.ds(t8, SUBLANES), lane0:lane0 + LANES]
            peer_sc[pl.ds(t8, SUBLANES), lane0:lane0 + LANES] = jnp.where(
                rsel, jnp.broadcast_to(row, (SUBLANES, LANES)), cur)

        for j in range(nlt):
            vh, vl = unpack(buf[:, half + j * LANES:half + (j + 1) * LANES])
            put(j * LANES, jnp.sum(vh * wgt, axis=0, keepdims=True))
            put(half + j * LANES, jnp.sum(vl * wgt, axis=0, keepdims=True))

    def step(t, s, prefetch):
        wait(s)
        if prefetch:
            issue(t + PEER_BUFS - 1, (s + PEER_BUFS - 1) % PEER_BUFS)
        compute(t, s)

    peer_sc[...] = jnp.zeros_like(peer_sc)
    for t in range(PEER_BUFS - 1):
        issue(t, t)

    def group(gi, _):
        for s in range(PEER_BUFS):
            step(gi * PEER_BUFS + s, s, True)
        return 0

    lax.fori_loop(0, tbk // PEER_BUFS - 1, group, 0)
    for s in range(PEER_BUFS):
        t = tbk - PEER_BUFS + s
        step(t, s, t + PEER_BUFS - 1 < tbk)

    xo = x1_ref[...] + peer_sc[...]
    y_ref[...] = _rms(xo, gf_ref[...]) if final_norm else xo


def _peer(eidx, gate, xn, x1, gf, table, final_norm):
    rows, dm = x1.shape
    nsel = gate.shape[1]
    tbk = _pick(rows, (256, 128, 64, 32, 16, 8))
    assert tbk % PEER_BUFS == 0 and tbk >= 2 * PEER_BUFS
    nblk = rows // tbk
    eidx_blk = eidx.reshape(nblk, tbk * nsel)
    row = lambda w: pl.BlockSpec((tbk, w), lambda i: (i, 0))
    return pl.pallas_call(
        functools.partial(_peer_kernel, final_norm=final_norm),
        grid=(nblk,),
        in_specs=[pl.BlockSpec(memory_space=pl.ANY), row(nsel), row(dm), row(dm),
                  _const_spec(gf.shape), pl.BlockSpec(memory_space=pl.ANY)],
        out_specs=row(dm),
        out_shape=jax.ShapeDtypeStruct((rows, dm), F32),
        scratch_shapes=[
            pltpu.SMEM((tbk * nsel,), I32),
            pltpu.SemaphoreType.DMA(()),
        ] + [pltpu.VMEM((nsel, table.shape[1]), U32) for _ in range(PEER_BUFS)] + [
            pltpu.SemaphoreType.DMA((PEER_BUFS,)),
            pltpu.VMEM((tbk, dm), F32),
        ],
        compiler_params=pltpu.CompilerParams(
            dimension_semantics=("arbitrary",), vmem_limit_bytes=VMEM_LIMIT),
        name="peer",
    )(eidx_blk, gate, xn, x1, gf, table)


def _cmul(ar, ai, br, bi):
    return ar * br - ai * bi, ar * bi + ai * br


def _s5_tables(lam_re, lam_im, log_dt, b_re, b_im, c_re, c_im, d, glu_w, glu_b):
    g, p = lam_re.shape
    hg = b_re.shape[-1]
    gs = g // SSM_SLABS
    dt = jnp.exp(log_dt.astype(F32))[:, None]
    lr = lam_re.astype(F32)
    li = lam_im.astype(F32)
    mag = jnp.exp(lr * dt)
    ab_re = mag * jnp.cos(li * dt)
    ab_im = mag * jnp.sin(li * dt)
    den = lr * lr + li * li
    nr = ab_re - 1.0
    ni = ab_im
    coef_re = (nr * lr + ni * li) / den
    coef_im = (ni * lr - nr * li) / den
    br = b_re.astype(F32)
    bi = b_im.astype(F32)
    bb_re = coef_re[..., None] * br - coef_im[..., None] * bi
    bb_im = coef_re[..., None] * bi + coef_im[..., None] * br
    eye = jnp.eye(gs, dtype=F32)

    def b_slab(bb):
        t = bb.reshape(SSM_SLABS, gs, p, hg).transpose(0, 1, 3, 2)
        return jnp.einsum('sihp,ij->sihjp', t, eye).reshape(SSM_SLABS, gs * hg, gs * p)

    bs = jnp.concatenate([b_slab(bb_re), b_slab(bb_im)], axis=2).astype(BF16)

    def c_slab(c):
        t = c.astype(F32).reshape(SSM_SLABS, gs, hg, p).transpose(0, 1, 3, 2)
        return jnp.einsum('siph,ij->sipjh', t, eye).reshape(SSM_SLABS, gs * p, gs * hg)

    cs = jnp.concatenate([c_slab(c_re), -c_slab(c_im)], axis=1).astype(BF16)

    a1 = (ab_re.reshape(1, g * p), ab_im.reshape(1, g * p))
    a2 = _cmul(*a1, *a1)
    a4 = _cmul(*a2, *a2)
    rowid = jnp.arange(SUBLANES)[:, None]
    mtab = jnp.stack([jnp.where(rowid >= s, comp, 0.0)
                      for s, a in ((1, a1), (2, a2), (4, a4)) for comp in a])
    pows = [a1]
    for _ in range(SUBLANES - 1):
        pows.append(_cmul(*pows[-1], *a1))
    ptab = jnp.stack([jnp.concatenate([q[0] for q in pows], axis=0),
                      jnp.concatenate([q[1] for q in pows], axis=0)])
    return (bs, cs, d.astype(F32).reshape(1, -1), glu_w.astype(BF16),
            glu_b.astype(F32).reshape(1, -1), mtab, ptab)


def _sgu_tables(w_s, b_s, seq_len):
    n_heads = w_s.shape[0]
    ln = min(SGU_CHUNK, seq_len)
    assert SGU_CHUNK % ln == 0 and seq_len % ln == 0
    rep = SGU_CHUNK // ln
    mask = jnp.tril(jnp.ones((ln, ln), dtype=bool))
    w = jnp.where(mask[None], w_s[:, :ln, :ln], 0.0).astype(F32)
    ws = jnp.einsum('hij,ab->haibj', w, jnp.eye(rep, dtype=F32)).reshape(n_heads, SGU_CHUNK, SGU_CHUNK)
    bias = jnp.tile(jnp.transpose(b_s[:, :ln]).astype(F32), (rep, 1))
    return ws.astype(BF16), bias


def _pack_rows(t):
    n, dd = t.shape
    b = lax.bitcast_convert_type(t.astype(BF16), jnp.uint16).astype(U32)
    return (b[:, :dd // 2] << 16) | b[:, dd // 2:]


def _trunk_layer(x, h0r, h0i, seq_len, lp, final_g):
    (norm_mix_g, w_in_bf, s5_tabs, ln_g, ln_b, sgu_w, sgu_b, wa, wb, wo, norm_ffn_g, wq, keys_bf, table,
     d_ssm, d_sgu) = lp
    u, su, vn, gates = _inproj(x, norm_mix_g, w_in_bf, ln_g, ln_b, d_ssm, d_sgu)
    ya, hlr, hli = _s5(u, h0r, h0i, s5_tabs, seq_len)
    ws, bias = _sgu_tables(sgu_w, sgu_b, seq_len)
    hd = d_sgu // ws.shape[0]
    bias_tile = jnp.repeat(bias, hd, axis=1)
    x1 = _mix(x, ya, su, vn, gates, ws, bias_tile, wa, wb, wo)
    xn, eidx, gate = _route(x1, norm_ffn_g, wq, keys_bf)
    y = _peer(eidx, gate, xn, x1, final_g, table, final_norm=True)
    return y, hlr, hli, vn


def kernel(x_prompt, x_sample, state_ssm_re, state_ssm_im, norm_mix_g, w_in, ssm_lambda_re, ssm_lambda_im, ssm_log_dt, ssm_b_re, ssm_b_im, ssm_c_re, ssm_c_im, ssm_d, ssm_glu_w, ssm_glu_b, sgu_ln_g, sgu_ln_b, sgu_w, sgu_b, w_branch_a, w_branch_b, w_out, norm_ffn_g, peer_w_q, peer_keys, peer_u, peer_v, norm_final_g):
    depth = w_in.shape[0]
    assert depth == 1, "the final norm is fused into the last layer's PEER kernel"
    bp, sp, dm = x_prompt.shape
    bs_, ss, _ = x_sample.shape
    g, p = ssm_lambda_re.shape[1:]
    d_ssm = ssm_d.shape[1]
    d_sgu = sgu_ln_g.shape[1]
    n_state = g * p
    l = 0
    lp = (norm_mix_g[l].reshape(1, dm), w_in[l].astype(BF16),
          _s5_tables(ssm_lambda_re[l], ssm_lambda_im[l], ssm_log_dt[l], ssm_b_re[l], ssm_b_im[l],
                     ssm_c_re[l], ssm_c_im[l], ssm_d[l], ssm_glu_w[l], ssm_glu_b[l]),
          sgu_ln_g[l].reshape(1, d_sgu), sgu_ln_b[l].reshape(1, d_sgu), sgu_w[l], sgu_b[l],
          w_branch_a[l].astype(BF16), w_branch_b[l].astype(BF16), w_out[l].astype(BF16),
          norm_ffn_g[l].reshape(1, dm), peer_w_q[l].astype(BF16), peer_keys[l].astype(BF16),
          jnp.concatenate([_pack_rows(peer_u[l]), _pack_rows(peer_v[l])], axis=1),
          d_ssm, d_sgu)
    gf = norm_final_g.reshape(1, dm)

    zeros = jnp.zeros((bp, n_state), F32)
    yp, hr_p, hi_p, _ = _trunk_layer(x_prompt.reshape(bp * sp, dm), zeros, zeros, sp, lp, gf)
    ys, hr_s, hi_s, v_s = _trunk_layer(x_sample.reshape(bs_ * ss, dm),
                                       state_ssm_re[l].astype(F32).reshape(bs_, n_state),
                                       state_ssm_im[l].astype(F32).reshape(bs_, n_state), ss, lp, gf)
    return (yp.reshape(bp, sp, dm), ys.reshape(bs_, ss, dm),
            hr_p.reshape(1, bp, g, p), hi_p.reshape(1, bp, g, p),
            hr_s.reshape(1, bs_, g, p), hi_s.reshape(1, bs_, g, p),
            v_s.reshape(1, bs_, ss, d_sgu))
```

```python
import functools
import math

import jax
import jax.numpy as jnp
from jax import lax
from jax.experimental import pallas as pl
from jax.experimental.pallas import tpu as pltpu

F32 = jnp.float32
BF16 = jnp.bfloat16
I32 = jnp.int32
U32 = jnp.uint32
EPS = 1e-6

LANES = 128
SUBLANES = 8
VMEM_LIMIT = 56 * 1024 * 1024

SSM_SLABS = 4
SCAN_LANES = 512
PEER_TOPK = 16
SGU_CHUNK = 128
ROUTE_LANES = 256
INPROJ_SUB = 128
PACK_ROWS = 512


def _pick(n, candidates):
    for c in candidates:
        if n % c == 0:
            return c
    raise ValueError(f"no block size for {n} in {candidates}")


def _const_spec(shape):
    nd = len(shape)
    return pl.BlockSpec(shape, lambda *_: (0,) * nd, pipeline_mode=pl.Buffered(1))


def _rms(x, g):
    return x * lax.rsqrt(jnp.mean(x * x, axis=-1, keepdims=True) + EPS) * g


def _inproj_kernel(x_ref, g_ref, w_ref, lng_ref, lnb_ref, u_ref, su_ref, vn_ref, gates_ref, xn_sc):
    j = pl.program_id(1)
    tm = x_ref.shape[0]
    sub = min(tm, INPROJ_SUB)

    @pl.when(j == 0)
    def _():
        xn_sc[...] = _rms(x_ref[...], g_ref[...]).astype(BF16)

    def section(out_ref, epilogue):
        for r in range(tm // sub):
            rows = slice(r * sub, (r + 1) * sub)
            z = jnp.dot(xn_sc[rows, :], w_ref[...], preferred_element_type=F32)
            out_ref[rows, :] = epilogue(z).astype(out_ref.dtype)

    def layer_norm_gelu(z):
        v = jax.nn.gelu(z)
        mu = jnp.mean(v, axis=-1, keepdims=True)
        vc = v - mu
        y = vc * lax.rsqrt(jnp.mean(vc * vc, axis=-1, keepdims=True) + EPS)
        return y * lng_ref[...] + lnb_ref[...]

    @pl.when(j == 0)
    def _():
        section(u_ref, lambda z: z)

    @pl.when(j == 1)
    def _():
        section(su_ref, jax.nn.gelu)

    @pl.when(j == 2)
    def _():
        section(vn_ref, layer_norm_gelu)

    @pl.when(j >= 3)
    def _():
        section(gates_ref, jax.nn.sigmoid)


def _inproj(x, g, w_in_bf, ln_g, ln_b, d_ssm, d_sgu):
    rows, dm = x.shape
    d_in = w_in_bf.shape[1]
    tn = d_ssm
    assert d_sgu == tn and (d_in - 3 * tn) % tn == 0
    nj = d_in // tn
    tm = _pick(rows, (512, 256, 128))
    return pl.pallas_call(
        _inproj_kernel,
        grid=(rows // tm, nj),
        in_specs=[
            pl.BlockSpec((tm, dm), lambda i, j: (i, 0)),
            pl.BlockSpec((1, dm), lambda i, j: (0, 0)),
            pl.BlockSpec((dm, tn), lambda i, j: (0, j)),
            pl.BlockSpec((1, tn), lambda i, j: (0, 0)),
            pl.BlockSpec((1, tn), lambda i, j: (0, 0)),
        ],
        out_specs=[
            pl.BlockSpec((tm, tn), lambda i, j: (i, 0)),
            pl.BlockSpec((tm, tn), lambda i, j: (i, 0)),
            pl.BlockSpec((tm, tn), lambda i, j: (i, 0)),
            pl.BlockSpec((tm, tn), lambda i, j: (i, jnp.maximum(j - 3, 0))),
        ],
        out_shape=[
            jax.ShapeDtypeStruct((rows, tn), F32),
            jax.ShapeDtypeStruct((rows, tn), BF16),
            jax.ShapeDtypeStruct((rows, tn), F32),
            jax.ShapeDtypeStruct((rows, d_in - 3 * tn), BF16),
        ],
        scratch_shapes=[pltpu.VMEM((tm, dm), BF16)],
        compiler_params=pltpu.CompilerParams(
            dimension_semantics=("arbitrary", "arbitrary"), vmem_limit_bytes=VMEM_LIMIT),
        name="inproj",
    )(x, g, w_in_bf, ln_g, ln_b)


def _s5_kernel(u_ref, h0r_ref, h0i_ref, bs_ref, cs_ref, d_ref, gluw_ref, glub_ref, m_ref, pw_ref,
               ya_ref, hlr_ref, hli_ref, hr_sc, hi_sc, cr_sc, ci_sc, *, seq_len):
    i = pl.program_id(0)
    tl, d_ssm = u_ref.shape
    n_state = hr_sc.shape[1]
    kin = d_ssm // SSM_SLABS
    kst = n_state // SSM_SLABS

    @pl.when(i == 0)
    def _():
        cr_sc[...] = jnp.zeros_like(cr_sc)
        ci_sc[...] = jnp.zeros_like(ci_sc)

    ub = u_ref[...].astype(BF16)
    for s in range(SSM_SLABS):
        r = jnp.dot(ub[:, kin * s:kin * (s + 1)], bs_ref[s], preferred_element_type=F32)
        hr_sc[:, kst * s:kst * (s + 1)] = r[:, :kst]
        hi_sc[:, kst * s:kst * (s + 1)] = r[:, kst:]

    n_tiles = tl // SUBLANES
    for lc in range(n_state // SCAN_LANES):
        ls = slice(lc * SCAN_LANES, (lc + 1) * SCAN_LANES)

        def tile_body(t, carry, ls=ls):
            cr, ci = carry
            row0 = i * tl + t * SUBLANES
            b = row0 // seq_len
            is_start = (row0 % seq_len) == 0
            h0r = jnp.broadcast_to(h0r_ref[pl.ds(b, 1), ls], (SUBLANES, SCAN_LANES))
            h0i = jnp.broadcast_to(h0i_ref[pl.ds(b, 1), ls], (SUBLANES, SCAN_LANES))
            cr = jnp.where(is_start, h0r, cr)
            ci = jnp.where(is_start, h0i, ci)
            r0 = pl.multiple_of(t * SUBLANES, SUBLANES)
            br = hr_sc[pl.ds(r0, SUBLANES), ls]
            bi = hi_sc[pl.ds(r0, SUBLANES), ls]
            for k, shift in enumerate((1, 2, 4)):
                mr = m_ref[2 * k, :, ls]
                mi = m_ref[2 * k + 1, :, ls]
                sr = pltpu.roll(br, shift, 0)
                si = pltpu.roll(bi, shift, 0)
                br, bi = br + (mr * sr - mi * si), bi + (mr * si + mi * sr)
            pr = pw_ref[0, :, ls]
            pi_ = pw_ref[1, :, ls]
            hr = br + (pr * cr - pi_ * ci)
            hi = bi + (pr * ci + pi_ * cr)
            hr_sc[pl.ds(r0, SUBLANES), ls] = hr
            hi_sc[pl.ds(r0, SUBLANES), ls] = hi
            last_r = hr[SUBLANES - 1:SUBLANES, :]
            last_i = hi[SUBLANES - 1:SUBLANES, :]
            hlr_ref[pl.ds(b, 1), ls] = last_r
            hli_ref[pl.ds(b, 1), ls] = last_i
            return (jnp.broadcast_to(last_r, (SUBLANES, SCAN_LANES)),
                    jnp.broadcast_to(last_i, (SUBLANES, SCAN_LANES)))

        cr, ci = lax.fori_loop(0, n_tiles, tile_body, (cr_sc[:, ls], ci_sc[:, ls]))
        cr_sc[:, ls] = cr
        ci_sc[:, ls] = ci

    ys = []
    for s in range(SSM_SLABS):
        hre = hr_sc[:, kst * s:kst * (s + 1)].astype(BF16)
        him = hi_sc[:, kst * s:kst * (s + 1)].astype(BF16)
        ys.append(jnp.dot(hre, cs_ref[s, :kst, :], preferred_element_type=F32)
                  + jnp.dot(him, cs_ref[s, kst:, :], preferred_element_type=F32))
    y = jnp.concatenate(ys, axis=1)
    y = jax.nn.gelu(y + d_ref[...] * u_ref[...])
    gt = jnp.dot(y.astype(BF16), gluw_ref[...], preferred_element_type=F32) + glub_ref[...]
    ya_ref[...] = (y * jax.nn.sigmoid(gt)).astype(BF16)


def _s5(u, h0r, h0i, tabs, seq_len):
    rows, d_ssm = u.shape
    nb, n_state = h0r.shape
    tl = _pick(rows, (256, 128, 64, 32, 16, 8)) if seq_len >= 256 else _pick(rows, (512, 256, 128, 64, 32, 16))
    assert (seq_len % tl == 0 or tl % seq_len == 0) and seq_len % SUBLANES == 0
    bs, cs, d, gluw, glub, mtab, ptab = tabs
    return pl.pallas_call(
        functools.partial(_s5_kernel, seq_len=seq_len),
        grid=(rows // tl,),
        in_specs=[
            pl.BlockSpec((tl, d_ssm), lambda i: (i, 0)),
            _const_spec((nb, n_state)),
            _const_spec((nb, n_state)),
            _const_spec(bs.shape),
            _const_spec(cs.shape),
            _const_spec(d.shape),
            _const_spec(gluw.shape),
            _const_spec(glub.shape),
            _const_spec(mtab.shape),
            _const_spec(ptab.shape),
        ],
        out_specs=[
            pl.BlockSpec((tl, d_ssm), lambda i: (i, 0)),
            pl.BlockSpec((nb, n_state), lambda i: (0, 0)),
            pl.BlockSpec((nb, n_state), lambda i: (0, 0)),
        ],
        out_shape=[
            jax.ShapeDtypeStruct((rows, d_ssm), BF16),
            jax.ShapeDtypeStruct((nb, n_state), F32),
            jax.ShapeDtypeStruct((nb, n_state), F32),
        ],
        scratch_shapes=[
            pltpu.VMEM((tl, n_state), F32),
            pltpu.VMEM((tl, n_state), F32),
            pltpu.VMEM((SUBLANES, n_state), F32),
            pltpu.VMEM((SUBLANES, n_state), F32),
        ],
        compiler_params=pltpu.CompilerParams(
            dimension_semantics=("arbitrary",), vmem_limit_bytes=VMEM_LIMIT),
        name="s5",
    )(u, h0r, h0i, bs, cs, d, gluw, glub, mtab, ptab)


def _mix_kernel(x_ref, ya_ref, su_ref, vn_ref, gates_ref, ws_ref, bias_ref, wa_ref, wb_ref, wo_ref, x1_ref):
    tm, dm = x_ref.shape
    n_heads = ws_ref.shape[0]
    hd = vn_ref.shape[1] // n_heads
    vb = vn_ref[...].astype(BF16)
    chunks = []
    for c in range(tm // SGU_CHUNK):
        heads = [jnp.dot(ws_ref[h], vb[c * SGU_CHUNK:(c + 1) * SGU_CHUNK, h * hd:(h + 1) * hd],
                         preferred_element_type=F32) for h in range(n_heads)]
        chunks.append(jnp.concatenate(heads, axis=1) + bias_ref[...])
    s_all = jnp.concatenate(chunks, axis=0)
    yb = (su_ref[...].astype(F32) * s_all).astype(BF16)
    pa = jnp.dot(ya_ref[...], wa_ref[...], preferred_element_type=F32)
    pb = jnp.dot(yb, wb_ref[...], preferred_element_type=F32)
    merged = gates_ref[:, :dm].astype(F32) * pa + gates_ref[:, dm:].astype(F32) * pb
    x1_ref[...] = x_ref[...] + jnp.dot(merged.astype(BF16), wo_ref[...], preferred_element_type=F32)


def _mix(x, ya, su, vn, gates, ws, bias, wa, wb, wo):
    rows, dm = x.shape
    d_ssm = ya.shape[1]
    d_sgu = su.shape[1]
    tm = _pick(rows, (256, 128))
    row = lambda w: pl.BlockSpec((tm, w), lambda i: (i, 0))
    return pl.pallas_call(
        _mix_kernel,
        grid=(rows // tm,),
        in_specs=[row(dm), row(d_ssm), row(d_sgu), row(d_sgu), row(2 * dm),
                  _const_spec(ws.shape), _const_spec(bias.shape),
                  _const_spec(wa.shape), _const_spec(wb.shape), _const_spec(wo.shape)],
        out_specs=row(dm),
        out_shape=jax.ShapeDtypeStruct((rows, dm), F32),
        compiler_params=pltpu.CompilerParams(
            dimension_semantics=("arbitrary",), vmem_limit_bytes=VMEM_LIMIT),
        name="mix",
    )(x, ya, su, vn, gates, ws, bias, wa, wb, wo)


def _topk_rows(s, k, payload=None):
    rows = s.shape[0]
    iota = lax.broadcasted_iota(I32, s.shape, 0).astype(F32)
    vals, outs = [], []
    for _ in range(k):
        m = jnp.max(s, axis=0, keepdims=True)
        j = jnp.min(jnp.where(s == m, iota, float(rows)), axis=0, keepdims=True)
        sel = iota == j
        vals.append(m)
        outs.append(j if payload is None else jnp.max(jnp.where(sel, payload, -1.0), axis=0, keepdims=True))
        s = jnp.where(sel, -jnp.inf, s)
    return jnp.concatenate(vals, axis=0), jnp.concatenate(outs, axis=0)


def _route_kernel(x1_ref, g_ref, wq_ref, keys_ref, xn_ref, eidx_ref, gate_ref, q_sc, e_sc, p_sc):
    tb = x1_ref.shape[0]
    tw = e_sc.shape[2]
    nsub = tb // tw
    n_heads, _, n_keys, half = keys_ref.shape
    dkey = 2 * half
    xn = _rms(x1_ref[...], g_ref[...])
    xn_ref[...] = xn
    q = jnp.dot(xn.astype(BF16), wq_ref[...], preferred_element_type=F32).astype(BF16)
    for h in range(n_heads):
        q_sc[h] = q[:, h * dkey:(h + 1) * dkey]

    nt = (((1,), (1,)), ((), ()))

    def body(n, _):
        h = n // nsub
        r0 = pl.multiple_of((n % nsub) * tw, tw)
        qh = q_sc[h, pl.ds(r0, tw), :]
        s1 = lax.dot_general(keys_ref[h, 0], qh[:, :half], nt, preferred_element_type=F32)
        s2 = lax.dot_general(keys_ref[h, 1], qh[:, half:], nt, preferred_element_type=F32)
        v1, i1 = _topk_rows(s1, PEER_TOPK)
        v2, i2 = _topk_rows(s2, PEER_TOPK)
        cv, ci = [], []
        for a in range(PEER_TOPK):
            nb = PEER_TOPK // (a + 1)
            cv.append(v1[a:a + 1, :] + v2[:nb, :])
            ci.append(i1[a:a + 1, :] * float(n_keys) + i2[:nb, :])
        ts, te = _topk_rows(jnp.concatenate(cv, axis=0), PEER_TOPK, payload=jnp.concatenate(ci, axis=0))
        ex = jnp.exp(ts - ts[0:1, :])
        e_sc[n] = te
        p_sc[n] = ex / jnp.sum(ex, axis=0, keepdims=True)
        return 0

    lax.fori_loop(0, n_heads * nsub, body, 0)
    for c in range(nsub):
        rows = slice(c * tw, (c + 1) * tw)
        e_all = jnp.concatenate([e_sc[h * nsub + c] for h in range(n_heads)], axis=0)
        p_all = jnp.concatenate([p_sc[h * nsub + c] for h in range(n_heads)], axis=0)
        eidx_ref[rows, :] = e_all.T.astype(I32)
        gate_ref[rows, :] = p_all.T


def _route(x1, g, wq, keys_bf):
    rows, dm = x1.shape
    n_heads, _, n_keys, half = keys_bf.shape
    nsel = n_heads * PEER_TOPK
    tb = _pick(rows, (256, 128))
    tw = min(tb, ROUTE_LANES)
    row = lambda w: pl.BlockSpec((tb, w), lambda i: (i, 0))
    return pl.pallas_call(
        _route_kernel,
        grid=(rows // tb,),
        in_specs=[row(dm), _const_spec(g.shape), _const_spec(wq.shape), _const_spec(keys_bf.shape)],
        out_specs=[row(dm), row(nsel), row(nsel)],
        out_shape=[
            jax.ShapeDtypeStruct((rows, dm), F32),
            jax.ShapeDtypeStruct((rows, nsel), I32),
            jax.ShapeDtypeStruct((rows, nsel), F32),
        ],
        scratch_shapes=[
            pltpu.VMEM((n_heads, tb, 2 * half), BF16),
            pltpu.VMEM((n_heads * (tb // tw), PEER_TOPK, tw), F32),
            pltpu.VMEM((n_heads * (tb // tw), PEER_TOPK, tw), F32),
        ],
        compiler_params=pltpu.CompilerParams(
            dimension_semantics=("arbitrary",), vmem_limit_bytes=VMEM_LIMIT),
        name="route",
    )(x1, g, wq, keys_bf)


PEER_BUFS = 4


def _peer_kernel(eidx_hbm, gate_ref, xn_ref, x1_ref, gf_ref, tab_hbm, y_ref,
                 idx_sm, idx_sem, b0, b1, b2, b3, sem, peer_sc, *, final_norm):
    i = pl.program_id(0)
    bufs = (b0, b1, b2, b3)
    tbk, nsel = gate_ref.shape
    half = tab_hbm.shape[1] // 2
    nlt = half // LANES

    icp = pltpu.make_async_copy(eidx_hbm.at[i], idx_sm, idx_sem)
    icp.start()
    icp.wait()

    def issue(t, s):
        base = t * nsel
        for k in range(nsel):
            e = idx_sm[base + k]
            pltpu.make_async_copy(tab_hbm.at[pl.ds(e, 1), :], bufs[s].at[pl.ds(k, 1), :],
                                  sem.at[s]).start(priority=k % 2)

    def wait(s):
        pltpu.make_async_copy(tab_hbm.at[pl.ds(0, nsel), :], bufs[s], sem.at[s]).wait()

    hi_mask = jnp.uint32(0xFFFF0000)
    eye = (lax.broadcasted_iota(I32, (nsel, nsel), 0) == lax.broadcasted_iota(I32, (nsel, nsel), 1))

    def unpack(w):
        return (lax.bitcast_convert_type(w & hi_mask, F32),
                lax.bitcast_convert_type(w << 16, F32))

    def compute(t, s):
        buf = bufs[s]
        xrow = xn_ref[pl.ds(t, 1), :]
        acc = jnp.zeros((nsel, LANES), F32)
        for j in range(nlt):
            uh, ul = unpack(buf[:, j * LANES:(j + 1) * LANES])
            acc = acc + uh * xrow[:, j * LANES:(j + 1) * LANES] \
                      + ul * xrow[:, half + j * LANES:half + (j + 1) * LANES]
        act = jnp.sum(acc, axis=1, keepdims=True)
        grow = jnp.broadcast_to(gate_ref[pl.ds(t, 1), :], (nsel, nsel))
        gcol = jnp.sum(jnp.where(eye, grow, 0.0), axis=1, keepdims=True)
        wgt = jnp.broadcast_to(gcol * jax.nn.gelu(act), (nsel, LANES))
        t8 = pl.multiple_of((t // SUBLANES) * SUBLANES, SUBLANES)
        rsel = lax.broadcasted_iota(I32, (SUBLANES, LANES), 0) == (t % SUBLANES)

        def put(lane0, row):
            cur = peer_sc[pl.ds(t8, SUBLANES), lane0:lane0 + LANES]
            peer_sc[pl.ds(t8, SUBLANES), lane0:lane0 + LANES] = jnp.where(
                rsel, jnp.broadcast_to(row, (SUBLANES, LANES)), cur)

        for j in range(nlt):
            vh, vl = unpack(buf[:, half + j * LANES:half + (j + 1) * LANES])
            put(j * LANES, jnp.sum(vh * wgt, axis=0, keepdims=True))
            put(half + j * LANES, jnp.sum(vl * wgt, axis=0, keepdims=True))

    def step(t, s, prefetch):
        wait(s)
        if prefetch:
            issue(t + PEER_BUFS - 1, (s + PEER_BUFS - 1) % PEER_BUFS)
        compute(t, s)

    peer_sc[...] = jnp.zeros_like(peer_sc)
    for t in range(PEER_BUFS - 1):
        issue(t, t)

    def group(gi, _):
        for s in range(PEER_BUFS):
            step(gi * PEER_BUFS + s, s, True)
        return 0

    lax.fori_loop(0, tbk // PEER_BUFS - 1, group, 0)
    for s in range(PEER_BUFS):
        t = tbk - PEER_BUFS + s
        step(t, s, t + PEER_BUFS - 1 < tbk)

    xo = x1_ref[...] + peer_sc[...]
    y_ref[...] = _rms(xo, gf_ref[...]) if final_norm else xo


def _peer(eidx, gate, xn, x1, gf, table, final_norm):
    rows, dm = x1.shape
    nsel = gate.shape[1]
    tbk = _pick(rows, (256, 128, 64, 32, 16, 8))
    assert tbk % PEER_BUFS == 0 and tbk >= 2 * PEER_BUFS
    nblk = rows // tbk
    eidx_blk = eidx.reshape(nblk, tbk * nsel)
    row = lambda w: pl.BlockSpec((tbk, w), lambda i: (i, 0))
    return pl.pallas_call(
        functools.partial(_peer_kernel, final_norm=final_norm),
        grid=(nblk,),
        in_specs=[pl.BlockSpec(memory_space=pl.ANY), row(nsel), row(dm), row(dm),
                  _const_spec(gf.shape), pl.BlockSpec(memory_space=pl.ANY)],
        out_specs=row(dm),
        out_shape=jax.ShapeDtypeStruct((rows, dm), F32),
        scratch_shapes=[
            pltpu.SMEM((tbk * nsel,), I32),
            pltpu.SemaphoreType.DMA(()),
        ] + [pltpu.VMEM((nsel, table.shape[1]), U32) for _ in range(PEER_BUFS)] + [
            pltpu.SemaphoreType.DMA((PEER_BUFS,)),
            pltpu.VMEM((tbk, dm), F32),
        ],
        compiler_params=pltpu.CompilerParams(
            dimension_semantics=("arbitrary",), vmem_limit_bytes=VMEM_LIMIT),
        name="peer",
    )(eidx_blk, gate, xn, x1, gf, table)


def _cmul(ar, ai, br, bi):
    return ar * br - ai * bi, ar * bi + ai * br


def _s5_tables(lam_re, lam_im, log_dt, b_re, b_im, c_re, c_im, d, glu_w, glu_b):
    g, p = lam_re.shape
    hg = b_re.shape[-1]
    gs = g // SSM_SLABS
    dt = jnp.exp(log_dt.astype(F32))[:, None]
    lr = lam_re.astype(F32)
    li = lam_im.astype(F32)
    mag = jnp.exp(lr * dt)
    ab_re = mag * jnp.cos(li * dt)
    ab_im = mag * jnp.sin(li * dt)
    den = lr * lr + li * li
    nr = ab_re - 1.0
    ni = ab_im
    coef_re = (nr * lr + ni * li) / den
    coef_im = (ni * lr - nr * li) / den
    br = b_re.astype(F32)
    bi = b_im.astype(F32)
    bb_re = coef_re[..., None] * br - coef_im[..., None] * bi
    bb_im = coef_re[..., None] * bi + coef_im[..., None] * br
    eye = jnp.eye(gs, dtype=F32)

    def b_slab(bb):
        t = bb.reshape(SSM_SLABS, gs, p, hg).transpose(0, 1, 3, 2)
        return jnp.einsum('sihp,ij->sihjp', t, eye).reshape(SSM_SLABS, gs * hg, gs * p)

    bs = jnp.concatenate([b_slab(bb_re), b_slab(bb_im)], axis=2).astype(BF16)

    def c_slab(c):
        t = c.astype(F32).reshape(SSM_SLABS, gs, hg, p).transpose(0, 1, 3, 2)
        return jnp.einsum('siph,ij->sipjh', t, eye).reshape(SSM_SLABS, gs * p, gs * hg)

    cs = jnp.concatenate([c_slab(c_re), -c_slab(c_im)], axis=1).astype(BF16)

    a1 = (ab_re.reshape(1, g * p), ab_im.reshape(1, g * p))
    a2 = _cmul(*a1, *a1)
    a4 = _cmul(*a2, *a2)
    rowid = jnp.arange(SUBLANES)[:, None]
    mtab = jnp.stack([jnp.where(rowid >= s, comp, 0.0)
                      for s, a in ((1, a1), (2, a2), (4, a4)) for comp in a])
    pows = [a1]
    for _ in range(SUBLANES - 1):
        pows.append(_cmul(*pows[-1], *a1))
    ptab = jnp.stack([jnp.concatenate([q[0] for q in pows], axis=0),
                      jnp.concatenate([q[1] for q in pows], axis=0)])
    return (bs, cs, d.astype(F32).reshape(1, -1), glu_w.astype(BF16),
            glu_b.astype(F32).reshape(1, -1), mtab, ptab)


def _sgu_tables(w_s, b_s, seq_len):
    n_heads = w_s.shape[0]
    ln = min(SGU_CHUNK, seq_len)
    assert SGU_CHUNK % ln == 0 and seq_len % ln == 0
    rep = SGU_CHUNK // ln
    mask = jnp.tril(jnp.ones((ln, ln), dtype=bool))
    w = jnp.where(mask[None], w_s[:, :ln, :ln], 0.0).astype(F32)
    ws = jnp.einsum('hij,ab->haibj', w, jnp.eye(rep, dtype=F32)).reshape(n_heads, SGU_CHUNK, SGU_CHUNK)
    bias = jnp.tile(jnp.transpose(b_s[:, :ln]).astype(F32), (rep, 1))
    return ws.astype(BF16), bias


def _pack_kernel(u_ref, v_ref, o_ref):
    half = u_ref.shape[1] // 2

    def bf16_bits(x):
        return lax.bitcast_convert_type(x.astype(BF16).astype(F32), U32)

    def pack(t_ref):
        return bf16_bits(t_ref[:, :half]) | (bf16_bits(t_ref[:, half:]) >> 16)

    o_ref[:, :half] = pack(u_ref)
    o_ref[:, half:] = pack(v_ref)


def _pack_tables(u, v):
    n, dd = u.shape
    tr = _pick(n, (PACK_ROWS, 256, 128, 8))
    spec = pl.BlockSpec((tr, dd), lambda i: (i, 0))
    return pl.pallas_call(
        _pack_kernel,
        grid=(n // tr,),
        in_specs=[spec, spec],
        out_specs=spec,
        out_shape=jax.ShapeDtypeStruct((n, dd), U32),
        compiler_params=pltpu.CompilerParams(
            dimension_semantics=("arbitrary",), vmem_limit_bytes=VMEM_LIMIT),
        name="pack",
    )(u, v)


def _trunk_layer(x, h0r, h0i, seq_len, lp, final_g):
    (norm_mix_g, w_in_bf, s5_tabs, ln_g, ln_b, sgu_w, sgu_b, wa, wb, wo, norm_ffn_g, wq, keys_bf, table,
     d_ssm, d_sgu) = lp
    u, su, vn, gates = _inproj(x, norm_mix_g, w_in_bf, ln_g, ln_b, d_ssm, d_sgu)
    ya, hlr, hli = _s5(u, h0r, h0i, s5_tabs, seq_len)
    ws, bias = _sgu_tables(sgu_w, sgu_b, seq_len)
    hd = d_sgu // ws.shape[0]
    bias_tile = jnp.repeat(bias, hd, axis=1)
    x1 = _mix(x, ya, su, vn, gates, ws, bias_tile, wa, wb, wo)
    xn, eidx, gate = _route(x1, norm_ffn_g, wq, keys_bf)
    y = _peer(eidx, gate, xn, x1, final_g, table, final_norm=True)
    return y, hlr, hli, vn


def kernel(x_prompt, x_sample, state_ssm_re, state_ssm_im, norm_mix_g, w_in, ssm_lambda_re, ssm_lambda_im, ssm_log_dt, ssm_b_re, ssm_b_im, ssm_c_re, ssm_c_im, ssm_d, ssm_glu_w, ssm_glu_b, sgu_ln_g, sgu_ln_b, sgu_w, sgu_b, w_branch_a, w_branch_b, w_out, norm_ffn_g, peer_w_q, peer_keys, peer_u, peer_v, norm_final_g):
    depth = w_in.shape[0]
    assert depth == 1, "the final norm is fused into the last layer's PEER kernel"
    bp, sp, dm = x_prompt.shape
    bs_, ss, _ = x_sample.shape
    g, p = ssm_lambda_re.shape[1:]
    d_ssm = ssm_d.shape[1]
    d_sgu = sgu_ln_g.shape[1]
    n_state = g * p
    l = 0
    lp = (norm_mix_g[l].reshape(1, dm), w_in[l].astype(BF16),
          _s5_tables(ssm_lambda_re[l], ssm_lambda_im[l], ssm_log_dt[l], ssm_b_re[l], ssm_b_im[l],
                     ssm_c_re[l], ssm_c_im[l], ssm_d[l], ssm_glu_w[l], ssm_glu_b[l]),
          sgu_ln_g[l].reshape(1, d_sgu), sgu_ln_b[l].reshape(1, d_sgu), sgu_w[l], sgu_b[l],
          w_branch_a[l].astype(BF16), w_branch_b[l].astype(BF16), w_out[l].astype(BF16),
          norm_ffn_g[l].reshape(1, dm), peer_w_q[l].astype(BF16), peer_keys[l].astype(BF16),
          _pack_tables(peer_u[l].astype(F32), peer_v[l].astype(F32)),
          d_ssm, d_sgu)
    gf = norm_final_g.reshape(1, dm)

    zeros = jnp.zeros((bp, n_state), F32)
    yp, hr_p, hi_p, _ = _trunk_layer(x_prompt.reshape(bp * sp, dm), zeros, zeros, sp, lp, gf)
    ys, hr_s, hi_s, v_s = _trunk_layer(x_sample.reshape(bs_ * ss, dm),
                                       state_ssm_re[l].astype(F32).reshape(bs_, n_state),
                                       state_ssm_im[l].astype(F32).reshape(bs_, n_state), ss, lp, gf)
    return (yp.reshape(bp, sp, dm), ys.reshape(bs_, ss, dm),
            hr_p.reshape(1, bp, g, p), hi_p.reshape(1, bp, g, p),
            hr_s.reshape(1, bs_, g, p), hi_s.reshape(1, bs_, g, p),
            v_s.reshape(1, bs_, ss, d_sgu))
```

```python
import functools
import math

import jax
import jax.numpy as jnp
from jax import lax
from jax.experimental import pallas as pl
from jax.experimental.pallas import tpu as pltpu
from jax.experimental.pallas import tpu_sc as plsc

F32 = jnp.float32
BF16 = jnp.bfloat16
I32 = jnp.int32
U32 = jnp.uint32
EPS = 1e-6

LANES = 128
SUBLANES = 8
VMEM_LIMIT = 56 * 1024 * 1024

SSM_SLABS = 4
SCAN_LANES = 512
PEER_TOPK = 16
SGU_CHUNK = 128
ROUTE_LANES = 256
INPROJ_SUB = 128
PACK_ROWS = 512


def _pick(n, candidates):
    for c in candidates:
        if n % c == 0:
            return c
    raise ValueError(f"no block size for {n} in {candidates}")


def _const_spec(shape):
    nd = len(shape)
    return pl.BlockSpec(shape, lambda *_: (0,) * nd, pipeline_mode=pl.Buffered(1))


def _rms(x, g):
    return x * lax.rsqrt(jnp.mean(x * x, axis=-1, keepdims=True) + EPS) * g


def _inproj_kernel(x_ref, g_ref, w_ref, lng_ref, lnb_ref, u_ref, su_ref, vn_ref, gates_ref, xn_sc):
    j = pl.program_id(1)
    tm = x_ref.shape[0]
    sub = min(tm, INPROJ_SUB)

    @pl.when(j == 0)
    def _():
        xn_sc[...] = _rms(x_ref[...], g_ref[...]).astype(BF16)

    def section(out_ref, epilogue):
        for r in range(tm // sub):
            rows = slice(r * sub, (r + 1) * sub)
            z = jnp.dot(xn_sc[rows, :], w_ref[...], preferred_element_type=F32)
            out_ref[rows, :] = epilogue(z).astype(out_ref.dtype)

    def layer_norm_gelu(z):
        v = jax.nn.gelu(z)
        mu = jnp.mean(v, axis=-1, keepdims=True)
        vc = v - mu
        y = vc * lax.rsqrt(jnp.mean(vc * vc, axis=-1, keepdims=True) + EPS)
        return y * lng_ref[...] + lnb_ref[...]

    @pl.when(j == 0)
    def _():
        section(u_ref, lambda z: z)

    @pl.when(j == 1)
    def _():
        section(su_ref, jax.nn.gelu)

    @pl.when(j == 2)
    def _():
        section(vn_ref, layer_norm_gelu)

    @pl.when(j >= 3)
    def _():
        section(gates_ref, jax.nn.sigmoid)


def _inproj(x, g, w_in_bf, ln_g, ln_b, d_ssm, d_sgu):
    rows, dm = x.shape
    d_in = w_in_bf.shape[1]
    tn = d_ssm
    assert d_sgu == tn and (d_in - 3 * tn) % tn == 0
    nj = d_in // tn
    tm = _pick(rows, (512, 256, 128))
    return pl.pallas_call(
        _inproj_kernel,
        grid=(rows // tm, nj),
        in_specs=[
            pl.BlockSpec((tm, dm), lambda i, j: (i, 0)),
            pl.BlockSpec((1, dm), lambda i, j: (0, 0)),
            pl.BlockSpec((dm, tn), lambda i, j: (0, j)),
            pl.BlockSpec((1, tn), lambda i, j: (0, 0)),
            pl.BlockSpec((1, tn), lambda i, j: (0, 0)),
        ],
        out_specs=[
            pl.BlockSpec((tm, tn), lambda i, j: (i, 0)),
            pl.BlockSpec((tm, tn), lambda i, j: (i, 0)),
            pl.BlockSpec((tm, tn), lambda i, j: (i, 0)),
            pl.BlockSpec((tm, tn), lambda i, j: (i, jnp.maximum(j - 3, 0))),
        ],
        out_shape=[
            jax.ShapeDtypeStruct((rows, tn), F32),
            jax.ShapeDtypeStruct((rows, tn), BF16),
            jax.ShapeDtypeStruct((rows, tn), F32),
            jax.ShapeDtypeStruct((rows, d_in - 3 * tn), BF16),
        ],
        scratch_shapes=[pltpu.VMEM((tm, dm), BF16)],
        compiler_params=pltpu.CompilerParams(
            dimension_semantics=("arbitrary", "arbitrary"), vmem_limit_bytes=VMEM_LIMIT),
        name="inproj",
    )(x, g, w_in_bf, ln_g, ln_b)


def _s5_kernel(u_ref, h0r_ref, h0i_ref, bs_ref, cs_ref, d_ref, gluw_ref, glub_ref, m_ref, pw_ref,
               ya_ref, hlr_ref, hli_ref, hr_sc, hi_sc, cr_sc, ci_sc, *, seq_len):
    i = pl.program_id(0)
    tl, d_ssm = u_ref.shape
    n_state = hr_sc.shape[1]
    kin = d_ssm // SSM_SLABS
    kst = n_state // SSM_SLABS

    @pl.when(i == 0)
    def _():
        cr_sc[...] = jnp.zeros_like(cr_sc)
        ci_sc[...] = jnp.zeros_like(ci_sc)

    ub = u_ref[...].astype(BF16)
    for s in range(SSM_SLABS):
        r = jnp.dot(ub[:, kin * s:kin * (s + 1)], bs_ref[s], preferred_element_type=F32)
        hr_sc[:, kst * s:kst * (s + 1)] = r[:, :kst]
        hi_sc[:, kst * s:kst * (s + 1)] = r[:, kst:]

    n_tiles = tl // SUBLANES
    for lc in range(n_state // SCAN_LANES):
        ls = slice(lc * SCAN_LANES, (lc + 1) * SCAN_LANES)

        def tile_body(t, carry, ls=ls):
            cr, ci = carry
            row0 = i * tl + t * SUBLANES
            b = row0 // seq_len
            is_start = (row0 % seq_len) == 0
            h0r = jnp.broadcast_to(h0r_ref[pl.ds(b, 1), ls], (SUBLANES, SCAN_LANES))
            h0i = jnp.broadcast_to(h0i_ref[pl.ds(b, 1), ls], (SUBLANES, SCAN_LANES))
            cr = jnp.where(is_start, h0r, cr)
            ci = jnp.where(is_start, h0i, ci)
            r0 = pl.multiple_of(t * SUBLANES, SUBLANES)
            br = hr_sc[pl.ds(r0, SUBLANES), ls]
            bi = hi_sc[pl.ds(r0, SUBLANES), ls]
            for k, shift in enumerate((1, 2, 4)):
                mr = m_ref[2 * k, :, ls]
                mi = m_ref[2 * k + 1, :, ls]
                sr = pltpu.roll(br, shift, 0)
                si = pltpu.roll(bi, shift, 0)
                br, bi = br + (mr * sr - mi * si), bi + (mr * si + mi * sr)
            pr = pw_ref[0, :, ls]
            pi_ = pw_ref[1, :, ls]
            hr = br + (pr * cr - pi_ * ci)
            hi = bi + (pr * ci + pi_ * cr)
            hr_sc[pl.ds(r0, SUBLANES), ls] = hr
            hi_sc[pl.ds(r0, SUBLANES), ls] = hi
            last_r = hr[SUBLANES - 1:SUBLANES, :]
            last_i = hi[SUBLANES - 1:SUBLANES, :]
            hlr_ref[pl.ds(b, 1), ls] = last_r
            hli_ref[pl.ds(b, 1), ls] = last_i
            return (jnp.broadcast_to(last_r, (SUBLANES, SCAN_LANES)),
                    jnp.broadcast_to(last_i, (SUBLANES, SCAN_LANES)))

        cr, ci = lax.fori_loop(0, n_tiles, tile_body, (cr_sc[:, ls], ci_sc[:, ls]))
        cr_sc[:, ls] = cr
        ci_sc[:, ls] = ci

    ys = []
    for s in range(SSM_SLABS):
        hre = hr_sc[:, kst * s:kst * (s + 1)].astype(BF16)
        him = hi_sc[:, kst * s:kst * (s + 1)].astype(BF16)
        ys.append(jnp.dot(hre, cs_ref[s, :kst, :], preferred_element_type=F32)
                  + jnp.dot(him, cs_ref[s, kst:, :], preferred_element_type=F32))
    y = jnp.concatenate(ys, axis=1)
    y = jax.nn.gelu(y + d_ref[...] * u_ref[...])
    gt = jnp.dot(y.astype(BF16), gluw_ref[...], preferred_element_type=F32) + glub_ref[...]
    ya_ref[...] = (y * jax.nn.sigmoid(gt)).astype(BF16)


def _s5(u, h0r, h0i, tabs, seq_len):
    rows, d_ssm = u.shape
    nb, n_state = h0r.shape
    tl = _pick(rows, (256, 128, 64, 32, 16, 8)) if seq_len >= 256 else _pick(rows, (512, 256, 128, 64, 32, 16))
    assert (seq_len % tl == 0 or tl % seq_len == 0) and seq_len % SUBLANES == 0
    bs, cs, d, gluw, glub, mtab, ptab = tabs
    return pl.pallas_call(
        functools.partial(_s5_kernel, seq_len=seq_len),
        grid=(rows // tl,),
        in_specs=[
            pl.BlockSpec((tl, d_ssm), lambda i: (i, 0)),
            _const_spec((nb, n_state)),
            _const_spec((nb, n_state)),
            _const_spec(bs.shape),
            _const_spec(cs.shape),
            _const_spec(d.shape),
            _const_spec(gluw.shape),
            _const_spec(glub.shape),
            _const_spec(mtab.shape),
            _const_spec(ptab.shape),
        ],
        out_specs=[
            pl.BlockSpec((tl, d_ssm), lambda i: (i, 0)),
            pl.BlockSpec((nb, n_state), lambda i: (0, 0)),
            pl.BlockSpec((nb, n_state), lambda i: (0, 0)),
        ],
        out_shape=[
            jax.ShapeDtypeStruct((rows, d_ssm), BF16),
            jax.ShapeDtypeStruct((nb, n_state), F32),
            jax.ShapeDtypeStruct((nb, n_state), F32),
        ],
        scratch_shapes=[
            pltpu.VMEM((tl, n_state), F32),
            pltpu.VMEM((tl, n_state), F32),
            pltpu.VMEM((SUBLANES, n_state), F32),
            pltpu.VMEM((SUBLANES, n_state), F32),
        ],
        compiler_params=pltpu.CompilerParams(
            dimension_semantics=("arbitrary",), vmem_limit_bytes=VMEM_LIMIT),
        name="s5",
    )(u, h0r, h0i, bs, cs, d, gluw, glub, mtab, ptab)


def _mix_kernel(x_ref, ya_ref, su_ref, vn_ref, gates_ref, ws_ref, bias_ref, wa_ref, wb_ref, wo_ref, x1_ref):
    tm, dm = x_ref.shape
    n_heads = ws_ref.shape[0]
    hd = vn_ref.shape[1] // n_heads
    vb = vn_ref[...].astype(BF16)
    chunks = []
    for c in range(tm // SGU_CHUNK):
        heads = [jnp.dot(ws_ref[h], vb[c * SGU_CHUNK:(c + 1) * SGU_CHUNK, h * hd:(h + 1) * hd],
                         preferred_element_type=F32) for h in range(n_heads)]
        chunks.append(jnp.concatenate(heads, axis=1) + bias_ref[...])
    s_all = jnp.concatenate(chunks, axis=0)
    yb = (su_ref[...].astype(F32) * s_all).astype(BF16)
    pa = jnp.dot(ya_ref[...], wa_ref[...], preferred_element_type=F32)
    pb = jnp.dot(yb, wb_ref[...], preferred_element_type=F32)
    merged = gates_ref[:, :dm].astype(F32) * pa + gates_ref[:, dm:].astype(F32) * pb
    x1_ref[...] = x_ref[...] + jnp.dot(merged.astype(BF16), wo_ref[...], preferred_element_type=F32)


def _mix(x, ya, su, vn, gates, ws, bias, wa, wb, wo):
    rows, dm = x.shape
    d_ssm = ya.shape[1]
    d_sgu = su.shape[1]
    tm = _pick(rows, (256, 128))
    row = lambda w: pl.BlockSpec((tm, w), lambda i: (i, 0))
    return pl.pallas_call(
        _mix_kernel,
        grid=(rows // tm,),
        in_specs=[row(dm), row(d_ssm), row(d_sgu), row(d_sgu), row(2 * dm),
                  _const_spec(ws.shape), _const_spec(bias.shape),
                  _const_spec(wa.shape), _const_spec(wb.shape), _const_spec(wo.shape)],
        out_specs=row(dm),
        out_shape=jax.ShapeDtypeStruct((rows, dm), F32),
        compiler_params=pltpu.CompilerParams(
            dimension_semantics=("arbitrary",), vmem_limit_bytes=VMEM_LIMIT),
        name="mix",
    )(x, ya, su, vn, gates, ws, bias, wa, wb, wo)


def _topk_rows(s, k, payload=None):
    rows = s.shape[0]
    iota = lax.broadcasted_iota(I32, s.shape, 0).astype(F32)
    vals, outs = [], []
    for _ in range(k):
        m = jnp.max(s, axis=0, keepdims=True)
        j = jnp.min(jnp.where(s == m, iota, float(rows)), axis=0, keepdims=True)
        sel = iota == j
        vals.append(m)
        outs.append(j if payload is None else jnp.max(jnp.where(sel, payload, -1.0), axis=0, keepdims=True))
        s = jnp.where(sel, -jnp.inf, s)
    return jnp.concatenate(vals, axis=0), jnp.concatenate(outs, axis=0)


def _route_kernel(x1_ref, g_ref, wq_ref, keys_ref, xn_ref, eidx_ref, gate_ref, q_sc, e_sc, p_sc):
    tb = x1_ref.shape[0]
    tw = e_sc.shape[2]
    nsub = tb // tw
    n_heads, _, n_keys, half = keys_ref.shape
    dkey = 2 * half
    xn = _rms(x1_ref[...], g_ref[...])
    xn_ref[...] = xn
    q = jnp.dot(xn.astype(BF16), wq_ref[...], preferred_element_type=F32).astype(BF16)
    for h in range(n_heads):
        q_sc[h] = q[:, h * dkey:(h + 1) * dkey]

    nt = (((1,), (1,)), ((), ()))

    def body(n, _):
        h = n // nsub
        r0 = pl.multiple_of((n % nsub) * tw, tw)
        qh = q_sc[h, pl.ds(r0, tw), :]
        s1 = lax.dot_general(keys_ref[h, 0], qh[:, :half], nt, preferred_element_type=F32)
        s2 = lax.dot_general(keys_ref[h, 1], qh[:, half:], nt, preferred_element_type=F32)
        v1, i1 = _topk_rows(s1, PEER_TOPK)
        v2, i2 = _topk_rows(s2, PEER_TOPK)
        cv, ci = [], []
        for a in range(PEER_TOPK):
            nb = PEER_TOPK // (a + 1)
            cv.append(v1[a:a + 1, :] + v2[:nb, :])
            ci.append(i1[a:a + 1, :] * float(n_keys) + i2[:nb, :])
        ts, te = _topk_rows(jnp.concatenate(cv, axis=0), PEER_TOPK, payload=jnp.concatenate(ci, axis=0))
        ex = jnp.exp(ts - ts[0:1, :])
        e_sc[n] = te
        p_sc[n] = ex / jnp.sum(ex, axis=0, keepdims=True)
        return 0

    lax.fori_loop(0, n_heads * nsub, body, 0)
    for c in range(nsub):
        rows = slice(c * tw, (c + 1) * tw)
        e_all = jnp.concatenate([e_sc[h * nsub + c] for h in range(n_heads)], axis=0)
        p_all = jnp.concatenate([p_sc[h * nsub + c] for h in range(n_heads)], axis=0)
        eidx_ref[rows, :] = e_all.T.astype(I32)
        gate_ref[rows, :] = p_all.T


def _route(x1, g, wq, keys_bf):
    rows, dm = x1.shape
    n_heads, _, n_keys, half = keys_bf.shape
    nsel = n_heads * PEER_TOPK
    tb = _pick(rows, (256, 128))
    tw = min(tb, ROUTE_LANES)
    row = lambda w: pl.BlockSpec((tb, w), lambda i: (i, 0))
    return pl.pallas_call(
        _route_kernel,
        grid=(rows // tb,),
        in_specs=[row(dm), _const_spec(g.shape), _const_spec(wq.shape), _const_spec(keys_bf.shape)],
        out_specs=[row(dm), row(nsel), row(nsel)],
        out_shape=[
            jax.ShapeDtypeStruct((rows, dm), F32),
            jax.ShapeDtypeStruct((rows, nsel), I32),
            jax.ShapeDtypeStruct((rows, nsel), F32),
        ],
        scratch_shapes=[
            pltpu.VMEM((n_heads, tb, 2 * half), BF16),
            pltpu.VMEM((n_heads * (tb // tw), PEER_TOPK, tw), F32),
            pltpu.VMEM((n_heads * (tb // tw), PEER_TOPK, tw), F32),
        ],
        compiler_params=pltpu.CompilerParams(
            dimension_semantics=("arbitrary",), vmem_limit_bytes=VMEM_LIMIT),
        name="route",
    )(x1, g, wq, keys_bf)


PEER_BUFS = 4


def _peer_kernel(eidx_hbm, gate_ref, xn_ref, x1_ref, gf_ref, tab_hbm, y_ref,
                 idx_sm, idx_sem, b0, b1, b2, b3, sem, peer_sc, *, final_norm, staged):
    i = pl.program_id(0)
    bufs = (b0, b1, b2, b3)
    tbk, nsel = gate_ref.shape
    half = tab_hbm.shape[1] // 2
    nlt = half // LANES

    if staged:
        def token_rows(t):
            return tab_hbm.at[pl.ds(pl.multiple_of((i * tbk + t) * nsel, nsel), nsel), :]

        def issue(t, s):
            pltpu.make_async_copy(token_rows(t), bufs[s], sem.at[s]).start()
    else:
        icp = pltpu.make_async_copy(eidx_hbm.at[i], idx_sm, idx_sem)
        icp.start()
        icp.wait()

        def issue(t, s):
            base = t * nsel
            for k in range(nsel):
                e = idx_sm[base + k]
                pltpu.make_async_copy(tab_hbm.at[pl.ds(e, 1), :], bufs[s].at[pl.ds(k, 1), :],
                                      sem.at[s]).start(priority=k % 2)

    def wait(s):
        pltpu.make_async_copy(tab_hbm.at[pl.ds(0, nsel), :], bufs[s], sem.at[s]).wait()

    hi_mask = jnp.int32(-65536)
    eye = (lax.broadcasted_iota(I32, (nsel, nsel), 0) == lax.broadcasted_iota(I32, (nsel, nsel), 1))

    def unpack(w):
        return (lax.bitcast_convert_type(w & hi_mask, F32),
                lax.bitcast_convert_type(w << 16, F32))

    def compute(t, s):
        buf = bufs[s]
        xrow = xn_ref[pl.ds(t, 1), :]
        acc = jnp.zeros((nsel, LANES), F32)
        for j in range(nlt):
            uh, ul = unpack(buf[:, j * LANES:(j + 1) * LANES])
            acc = acc + uh * xrow[:, j * LANES:(j + 1) * LANES] \
                      + ul * xrow[:, half + j * LANES:half + (j + 1) * LANES]
        act = jnp.sum(acc, axis=1, keepdims=True)
        grow = jnp.broadcast_to(gate_ref[pl.ds(t, 1), :], (nsel, nsel))
        gcol = jnp.sum(jnp.where(eye, grow, 0.0), axis=1, keepdims=True)
        wgt = jnp.broadcast_to(gcol * jax.nn.gelu(act), (nsel, LANES))
        t8 = pl.multiple_of((t // SUBLANES) * SUBLANES, SUBLANES)
        rsel = lax.broadcasted_iota(I32, (SUBLANES, LANES), 0) == (t % SUBLANES)

        def put(lane0, row):
            cur = peer_sc[pl.ds(t8, SUBLANES), lane0:lane0 + LANES]
            peer_sc[pl.ds(t8, SUBLANES), lane0:lane0 + LANES] = jnp.where(
                rsel, jnp.broadcast_to(row, (SUBLANES, LANES)), cur)

        for j in range(nlt):
            vh, vl = unpack(buf[:, half + j * LANES:half + (j + 1) * LANES])
            put(j * LANES, jnp.sum(vh * wgt, axis=0, keepdims=True))
            put(half + j * LANES, jnp.sum(vl * wgt, axis=0, keepdims=True))

    def step(t, s, prefetch):
        wait(s)
        if prefetch:
            issue(t + PEER_BUFS - 1, (s + PEER_BUFS - 1) % PEER_BUFS)
        compute(t, s)

    peer_sc[...] = jnp.zeros_like(peer_sc)
    for t in range(PEER_BUFS - 1):
        issue(t, t)

    def group(gi, _):
        for s in range(PEER_BUFS):
            step(gi * PEER_BUFS + s, s, True)
        return 0

    lax.fori_loop(0, tbk // PEER_BUFS - 1, group, 0)
    for s in range(PEER_BUFS):
        t = tbk - PEER_BUFS + s
        step(t, s, t + PEER_BUFS - 1 < tbk)

    xo = x1_ref[...] + peer_sc[...]
    y_ref[...] = _rms(xo, gf_ref[...]) if final_norm else xo


def _peer(eidx, gate, xn, x1, gf, table, final_norm, staged=False):
    rows, dm = x1.shape
    nsel = gate.shape[1]
    tbk = _pick(rows, (256, 128, 64, 32, 16, 8))
    assert tbk % PEER_BUFS == 0 and tbk >= 2 * PEER_BUFS
    nblk = rows // tbk
    eidx_blk = eidx.reshape(nblk, tbk * nsel)
    row = lambda w: pl.BlockSpec((tbk, w), lambda i: (i, 0))
    return pl.pallas_call(
        functools.partial(_peer_kernel, final_norm=final_norm, staged=staged),
        grid=(nblk,),
        in_specs=[pl.BlockSpec(memory_space=pl.ANY), row(nsel), row(dm), row(dm),
                  _const_spec(gf.shape), pl.BlockSpec(memory_space=pl.ANY)],
        out_specs=row(dm),
        out_shape=jax.ShapeDtypeStruct((rows, dm), F32),
        scratch_shapes=[
            pltpu.SMEM((tbk * nsel,), I32),
            pltpu.SemaphoreType.DMA(()),
        ] + [pltpu.VMEM((nsel, table.shape[1]), table.dtype) for _ in range(PEER_BUFS)] + [
            pltpu.SemaphoreType.DMA((PEER_BUFS,)),
            pltpu.VMEM((tbk, dm), F32),
        ],
        compiler_params=pltpu.CompilerParams(
            dimension_semantics=("arbitrary",), vmem_limit_bytes=VMEM_LIMIT),
        name="peer_staged" if staged else "peer",
    )(eidx_blk, gate, xn, x1, gf, table)


SC_WINDOW = 16
SC_SHARE = 0.375
SC_SPLIT_ALIGN = 512


def _sc_gather(table, idx_flat):
    n = idx_flat.shape[0]
    d = table.shape[1]
    info = plsc.get_sparse_core_info()
    n_workers = info.num_cores * info.num_subcores
    per_tok = 128
    n_win = per_tok // SC_WINDOW
    tpw = n // per_tok // n_workers
    assert tpw * n_workers * per_tok == n
    mesh = plsc.VectorSubcoreMesh(core_axis_name="core", subcore_axis_name="subcore")

    @functools.partial(
        pl.kernel, out_type=jax.ShapeDtypeStruct((n, d), table.dtype), mesh=mesh,
        scratch_types=[pltpu.VMEM((per_tok,), I32), pltpu.VMEM((2, SC_WINDOW, d), table.dtype),
                       pltpu.SemaphoreType.DMA((2,)), pltpu.SemaphoreType.DMA((2,))],
        name="sc_gather")
    def k(x_hbm, i_hbm, o_hbm, idx_v, rows_v, gsem, ssem):
        wid = lax.axis_index("subcore") * info.num_cores + lax.axis_index("core")

        @pl.loop(0, tpw)
        def _(tt):
            row0 = (wid * tpw + tt) * per_tok
            pltpu.sync_copy(i_hbm.at[pl.ds(row0, per_tok)], idx_v)

            def gather(j):
                return pltpu.make_async_copy(x_hbm.at[idx_v.at[pl.ds(j * SC_WINDOW, SC_WINDOW)]],
                                             rows_v.at[j % 2], gsem.at[j % 2])

            def store(j):
                return pltpu.make_async_copy(rows_v.at[j % 2],
                                             o_hbm.at[pl.ds(row0 + j * SC_WINDOW, SC_WINDOW)], ssem.at[j % 2])

            gather(0).start()
            for j in range(n_win):
                gather(j).wait()
                if j + 1 < n_win:
                    if j >= 1:
                        store(j - 1).wait()
                    gather(j + 1).start()
                store(j).start()
            store(n_win - 2).wait()
            store(n_win - 1).wait()

    return k(table, idx_flat)


def _cmul(ar, ai, br, bi):
    return ar * br - ai * bi, ar * bi + ai * br


def _s5_tables(lam_re, lam_im, log_dt, b_re, b_im, c_re, c_im, d, glu_w, glu_b):
    g, p = lam_re.shape
    hg = b_re.shape[-1]
    gs = g // SSM_SLABS
    dt = jnp.exp(log_dt.astype(F32))[:, None]
    lr = lam_re.astype(F32)
    li = lam_im.astype(F32)
    mag = jnp.exp(lr * dt)
    ab_re = mag * jnp.cos(li * dt)
    ab_im = mag * jnp.sin(li * dt)
    den = lr * lr + li * li
    nr = ab_re - 1.0
    ni = ab_im
    coef_re = (nr * lr + ni * li) / den
    coef_im = (ni * lr - nr * li) / den
    br = b_re.astype(F32)
    bi = b_im.astype(F32)
    bb_re = coef_re[..., None] * br - coef_im[..., None] * bi
    bb_im = coef_re[..., None] * bi + coef_im[..., None] * br
    eye = jnp.eye(gs, dtype=F32)

    def b_slab(bb):
        t = bb.reshape(SSM_SLABS, gs, p, hg).transpose(0, 1, 3, 2)
        return jnp.einsum('sihp,ij->sihjp', t, eye).reshape(SSM_SLABS, gs * hg, gs * p)

    bs = jnp.concatenate([b_slab(bb_re), b_slab(bb_im)], axis=2).astype(BF16)

    def c_slab(c):
        t = c.astype(F32).reshape(SSM_SLABS, gs, hg, p).transpose(0, 1, 3, 2)
        return jnp.einsum('siph,ij->sipjh', t, eye).reshape(SSM_SLABS, gs * p, gs * hg)

    cs = jnp.concatenate([c_slab(c_re), -c_slab(c_im)], axis=1).astype(BF16)

    a1 = (ab_re.reshape(1, g * p), ab_im.reshape(1, g * p))
    a2 = _cmul(*a1, *a1)
    a4 = _cmul(*a2, *a2)
    rowid = jnp.arange(SUBLANES)[:, None]
    mtab = jnp.stack([jnp.where(rowid >= s, comp, 0.0)
                      for s, a in ((1, a1), (2, a2), (4, a4)) for comp in a])
    pows = [a1]
    for _ in range(SUBLANES - 1):
        pows.append(_cmul(*pows[-1], *a1))
    ptab = jnp.stack([jnp.concatenate([q[0] for q in pows], axis=0),
                      jnp.concatenate([q[1] for q in pows], axis=0)])
    return (bs, cs, d.astype(F32).reshape(1, -1), glu_w.astype(BF16),
            glu_b.astype(F32).reshape(1, -1), mtab, ptab)


def _sgu_tables(w_s, b_s, seq_len):
    n_heads = w_s.shape[0]
    ln = min(SGU_CHUNK, seq_len)
    assert SGU_CHUNK % ln == 0 and seq_len % ln == 0
    rep = SGU_CHUNK // ln
    mask = jnp.tril(jnp.ones((ln, ln), dtype=bool))
    w = jnp.where(mask[None], w_s[:, :ln, :ln], 0.0).astype(F32)
    ws = jnp.einsum('hij,ab->haibj', w, jnp.eye(rep, dtype=F32)).reshape(n_heads, SGU_CHUNK, SGU_CHUNK)
    bias = jnp.tile(jnp.transpose(b_s[:, :ln]).astype(F32), (rep, 1))
    return ws.astype(BF16), bias


def _pack_kernel(u_ref, v_ref, o_ref):
    half = u_ref.shape[1] // 2

    def bf16_bits(x):
        return lax.bitcast_convert_type(x.astype(BF16).astype(F32), U32)

    def pack(t_ref):
        w = bf16_bits(t_ref[:, :half]) | (bf16_bits(t_ref[:, half:]) >> 16)
        return lax.bitcast_convert_type(w, I32)

    o_ref[:, :half] = pack(u_ref)
    o_ref[:, half:] = pack(v_ref)


def _pack_tables(u, v):
    n, dd = u.shape
    tr = _pick(n, (PACK_ROWS, 256, 128, 8))
    spec = pl.BlockSpec((tr, dd), lambda i: (i, 0))
    return pl.pallas_call(
        _pack_kernel,
        grid=(n // tr,),
        in_specs=[spec, spec],
        out_specs=spec,
        out_shape=jax.ShapeDtypeStruct((n, dd), I32),
        compiler_params=pltpu.CompilerParams(
            dimension_semantics=("arbitrary",), vmem_limit_bytes=VMEM_LIMIT),
        name="pack",
    )(u, v)


def _trunk_to_route(x, h0r, h0i, seq_len, lp):
    (norm_mix_g, w_in_bf, s5_tabs, ln_g, ln_b, sgu_w, sgu_b, wa, wb, wo, norm_ffn_g, wq, keys_bf, _,
     d_ssm, d_sgu) = lp
    u, su, vn, gates = _inproj(x, norm_mix_g, w_in_bf, ln_g, ln_b, d_ssm, d_sgu)
    ya, hlr, hli = _s5(u, h0r, h0i, s5_tabs, seq_len)
    ws, bias = _sgu_tables(sgu_w, sgu_b, seq_len)
    hd = d_sgu // ws.shape[0]
    bias_tile = jnp.repeat(bias, hd, axis=1)
    x1 = _mix(x, ya, su, vn, gates, ws, bias_tile, wa, wb, wo)
    xn, eidx, gate = _route(x1, norm_ffn_g, wq, keys_bf)
    return (x1, xn, eidx, gate), hlr, hli, vn


def _trunk_layer(x, h0r, h0i, seq_len, lp, final_g):
    (x1, xn, eidx, gate), hlr, hli, vn = _trunk_to_route(x, h0r, h0i, seq_len, lp)
    y = _peer(eidx, gate, xn, x1, final_g, lp[13], final_norm=True)
    return y, hlr, hli, vn


def _prompt_split(x, seq_len, lp, final_g, n_state):
    rows = x.shape[0]
    nb = (int(rows * SC_SHARE) // SC_SPLIT_ALIGN) * SC_SPLIT_ALIGN
    zeros = jnp.zeros((1, n_state), F32)
    (x1b, xnb, eb, gb), hr_b, hi_b, _ = _trunk_to_route(x[:nb], zeros, zeros, seq_len, lp)
    (x1a, xna, ea, ga), hr, hi, _ = _trunk_to_route(x[nb:], hr_b, hi_b, seq_len, lp)
    ya = _peer(ea, ga, xna, x1a, final_g, lp[13], final_norm=True)
    return (x1b, xnb, eb, gb), ya, hr, hi


def _prompt_finish(first, ya, lp, final_g):
    x1b, xnb, eb, gb = first
    staged = _sc_gather(lp[13], eb.reshape(-1))
    yb = _peer(eb, gb, xnb, x1b, final_g, staged, final_norm=True, staged=True)
    return jnp.concatenate([yb, ya], axis=0)


def kernel(x_prompt, x_sample, state_ssm_re, state_ssm_im, norm_mix_g, w_in, ssm_lambda_re, ssm_lambda_im, ssm_log_dt, ssm_b_re, ssm_b_im, ssm_c_re, ssm_c_im, ssm_d, ssm_glu_w, ssm_glu_b, sgu_ln_g, sgu_ln_b, sgu_w, sgu_b, w_branch_a, w_branch_b, w_out, norm_ffn_g, peer_w_q, peer_keys, peer_u, peer_v, norm_final_g):
    depth = w_in.shape[0]
    assert depth == 1, "the final norm is fused into the last layer's PEER kernel"
    bp, sp, dm = x_prompt.shape
    bs_, ss, _ = x_sample.shape
    g, p = ssm_lambda_re.shape[1:]
    d_ssm = ssm_d.shape[1]
    d_sgu = sgu_ln_g.shape[1]
    n_state = g * p
    l = 0
    lp = (norm_mix_g[l].reshape(1, dm), w_in[l].astype(BF16),
          _s5_tables(ssm_lambda_re[l], ssm_lambda_im[l], ssm_log_dt[l], ssm_b_re[l], ssm_b_im[l],
                     ssm_c_re[l], ssm_c_im[l], ssm_d[l], ssm_glu_w[l], ssm_glu_b[l]),
          sgu_ln_g[l].reshape(1, d_sgu), sgu_ln_b[l].reshape(1, d_sgu), sgu_w[l], sgu_b[l],
          w_branch_a[l].astype(BF16), w_branch_b[l].astype(BF16), w_out[l].astype(BF16),
          norm_ffn_g[l].reshape(1, dm), peer_w_q[l].astype(BF16), peer_keys[l].astype(BF16),
          _pack_tables(peer_u[l].astype(F32), peer_v[l].astype(F32)),
          d_ssm, d_sgu)
    gf = norm_final_g.reshape(1, dm)

    xp2 = x_prompt.reshape(bp * sp, dm)
    split = bp == 1 and sp >= 4 * SC_SPLIT_ALIGN
    if split:
        first, ya, hr_p, hi_p = _prompt_split(xp2, sp, lp, gf, n_state)
    else:
        zeros = jnp.zeros((bp, n_state), F32)
        yp, hr_p, hi_p, _ = _trunk_layer(xp2, zeros, zeros, sp, lp, gf)
    ys, hr_s, hi_s, v_s = _trunk_layer(x_sample.reshape(bs_ * ss, dm),
                                       state_ssm_re[l].astype(F32).reshape(bs_, n_state),
                                       state_ssm_im[l].astype(F32).reshape(bs_, n_state), ss, lp, gf)
    if split:
        yp = _prompt_finish(first, ya, lp, gf)
    return (yp.reshape(bp, sp, dm), ys.reshape(bs_, ss, dm),
            hr_p.reshape(1, bp, g, p), hi_p.reshape(1, bp, g, p),
            hr_s.reshape(1, bs_, g, p), hi_s.reshape(1, bs_, g, p),
            v_s.reshape(1, bs_, ss, d_sgu))
```

```python
import functools
import math

import jax
import jax.numpy as jnp
from jax import lax
from jax.experimental import pallas as pl
from jax.experimental.pallas import tpu as pltpu
from jax.experimental.pallas import tpu_sc as plsc

F32 = jnp.float32
BF16 = jnp.bfloat16
I32 = jnp.int32
U32 = jnp.uint32
EPS = 1e-6

LANES = 128
SUBLANES = 8
VMEM_LIMIT = 56 * 1024 * 1024

SSM_SLABS = 4
SCAN_LANES = 512
PEER_TOPK = 16
SGU_CHUNK = 128
ROUTE_LANES = 256
INPROJ_SUB = 128
PACK_ROWS = 512


def _pick(n, candidates):
    for c in candidates:
        if n % c == 0:
            return c
    raise ValueError(f"no block size for {n} in {candidates}")


def _const_spec(shape):
    nd = len(shape)
    return pl.BlockSpec(shape, lambda *_: (0,) * nd, pipeline_mode=pl.Buffered(1))


def _rms(x, g):
    return x * lax.rsqrt(jnp.mean(x * x, axis=-1, keepdims=True) + EPS) * g


def _inproj_kernel(x_ref, g_ref, w_ref, lng_ref, lnb_ref, u_ref, su_ref, vn_ref, gates_ref, xn_sc):
    j = pl.program_id(1)
    tm = x_ref.shape[0]
    sub = min(tm, INPROJ_SUB)

    @pl.when(j == 0)
    def _():
        xn_sc[...] = _rms(x_ref[...], g_ref[...]).astype(BF16)

    def section(out_ref, epilogue):
        for r in range(tm // sub):
            rows = slice(r * sub, (r + 1) * sub)
            z = jnp.dot(xn_sc[rows, :], w_ref[...], preferred_element_type=F32)
            out_ref[rows, :] = epilogue(z).astype(out_ref.dtype)

    def layer_norm_gelu(z):
        v = jax.nn.gelu(z)
        mu = jnp.mean(v, axis=-1, keepdims=True)
        vc = v - mu
        y = vc * lax.rsqrt(jnp.mean(vc * vc, axis=-1, keepdims=True) + EPS)
        return y * lng_ref[...] + lnb_ref[...]

    @pl.when(j == 0)
    def _():
        section(u_ref, lambda z: z)

    @pl.when(j == 1)
    def _():
        section(su_ref, jax.nn.gelu)

    @pl.when(j == 2)
    def _():
        section(vn_ref, layer_norm_gelu)

    @pl.when(j >= 3)
    def _():
        section(gates_ref, jax.nn.sigmoid)


def _inproj(x, g, w_in_bf, ln_g, ln_b, d_ssm, d_sgu):
    rows, dm = x.shape
    d_in = w_in_bf.shape[1]
    tn = d_ssm
    assert d_sgu == tn and (d_in - 3 * tn) % tn == 0
    nj = d_in // tn
    tm = _pick(rows, (512, 256, 128))
    return pl.pallas_call(
        _inproj_kernel,
        grid=(rows // tm, nj),
        in_specs=[
            pl.BlockSpec((tm, dm), lambda i, j: (i, 0)),
            pl.BlockSpec((1, dm), lambda i, j: (0, 0)),
            pl.BlockSpec((dm, tn), lambda i, j: (0, j)),
            pl.BlockSpec((1, tn), lambda i, j: (0, 0)),
            pl.BlockSpec((1, tn), lambda i, j: (0, 0)),
        ],
        out_specs=[
            pl.BlockSpec((tm, tn), lambda i, j: (i, 0)),
            pl.BlockSpec((tm, tn), lambda i, j: (i, 0)),
            pl.BlockSpec((tm, tn), lambda i, j: (i, 0)),
            pl.BlockSpec((tm, tn), lambda i, j: (i, jnp.maximum(j - 3, 0))),
        ],
        out_shape=[
            jax.ShapeDtypeStruct((rows, tn), F32),
            jax.ShapeDtypeStruct((rows, tn), BF16),
            jax.ShapeDtypeStruct((rows, tn), F32),
            jax.ShapeDtypeStruct((rows, d_in - 3 * tn), BF16),
        ],
        scratch_shapes=[pltpu.VMEM((tm, dm), BF16)],
        compiler_params=pltpu.CompilerParams(
            dimension_semantics=("arbitrary", "arbitrary"), vmem_limit_bytes=VMEM_LIMIT),
        name="inproj",
    )(x, g, w_in_bf, ln_g, ln_b)


def _s5_kernel(u_ref, h0r_ref, h0i_ref, bs_ref, cs_ref, d_ref, gluw_ref, glub_ref, m_ref, pw_ref,
               ya_ref, hlr_ref, hli_ref, hr_sc, hi_sc, cr_sc, ci_sc, *, seq_len):
    i = pl.program_id(0)
    tl, d_ssm = u_ref.shape
    n_state = hr_sc.shape[1]
    kin = d_ssm // SSM_SLABS
    kst = n_state // SSM_SLABS

    @pl.when(i == 0)
    def _():
        cr_sc[...] = jnp.zeros_like(cr_sc)
        ci_sc[...] = jnp.zeros_like(ci_sc)

    ub = u_ref[...].astype(BF16)
    for s in range(SSM_SLABS):
        r = jnp.dot(ub[:, kin * s:kin * (s + 1)], bs_ref[s], preferred_element_type=F32)
        hr_sc[:, kst * s:kst * (s + 1)] = r[:, :kst]
        hi_sc[:, kst * s:kst * (s + 1)] = r[:, kst:]

    n_tiles = tl // SUBLANES
    for lc in range(n_state // SCAN_LANES):
        ls = slice(lc * SCAN_LANES, (lc + 1) * SCAN_LANES)

        def tile_body(t, carry, ls=ls):
            cr, ci = carry
            row0 = i * tl + t * SUBLANES
            b = row0 // seq_len
            is_start = (row0 % seq_len) == 0
            h0r = jnp.broadcast_to(h0r_ref[pl.ds(b, 1), ls], (SUBLANES, SCAN_LANES))
            h0i = jnp.broadcast_to(h0i_ref[pl.ds(b, 1), ls], (SUBLANES, SCAN_LANES))
            cr = jnp.where(is_start, h0r, cr)
            ci = jnp.where(is_start, h0i, ci)
            r0 = pl.multiple_of(t * SUBLANES, SUBLANES)
            br = hr_sc[pl.ds(r0, SUBLANES), ls]
            bi = hi_sc[pl.ds(r0, SUBLANES), ls]
            for k, shift in enumerate((1, 2, 4)):
                mr = m_ref[2 * k, :, ls]
                mi = m_ref[2 * k + 1, :, ls]
                sr = pltpu.roll(br, shift, 0)
                si = pltpu.roll(bi, shift, 0)
                br, bi = br + (mr * sr - mi * si), bi + (mr * si + mi * sr)
            pr = pw_ref[0, :, ls]
            pi_ = pw_ref[1, :, ls]
            hr = br + (pr * cr - pi_ * ci)
            hi = bi + (pr * ci + pi_ * cr)
            hr_sc[pl.ds(r0, SUBLANES), ls] = hr
            hi_sc[pl.ds(r0, SUBLANES), ls] = hi
            last_r = hr[SUBLANES - 1:SUBLANES, :]
            last_i = hi[SUBLANES - 1:SUBLANES, :]
            hlr_ref[pl.ds(b, 1), ls] = last_r
            hli_ref[pl.ds(b, 1), ls] = last_i
            return (jnp.broadcast_to(last_r, (SUBLANES, SCAN_LANES)),
                    jnp.broadcast_to(last_i, (SUBLANES, SCAN_LANES)))

        cr, ci = lax.fori_loop(0, n_tiles, tile_body, (cr_sc[:, ls], ci_sc[:, ls]))
        cr_sc[:, ls] = cr
        ci_sc[:, ls] = ci

    ys = []
    for s in range(SSM_SLABS):
        hre = hr_sc[:, kst * s:kst * (s + 1)].astype(BF16)
        him = hi_sc[:, kst * s:kst * (s + 1)].astype(BF16)
        ys.append(jnp.dot(hre, cs_ref[s, :kst, :], preferred_element_type=F32)
                  + jnp.dot(him, cs_ref[s, kst:, :], preferred_element_type=F32))
    y = jnp.concatenate(ys, axis=1)
    y = jax.nn.gelu(y + d_ref[...] * u_ref[...])
    gt = jnp.dot(y.astype(BF16), gluw_ref[...], preferred_element_type=F32) + glub_ref[...]
    ya_ref[...] = (y * jax.nn.sigmoid(gt)).astype(BF16)


def _s5(u, h0r, h0i, tabs, seq_len):
    rows, d_ssm = u.shape
    nb, n_state = h0r.shape
    tl = _pick(rows, (256, 128, 64, 32, 16, 8)) if seq_len >= 256 else _pick(rows, (512, 256, 128, 64, 32, 16))
    assert (seq_len % tl == 0 or tl % seq_len == 0) and seq_len % SUBLANES == 0
    bs, cs, d, gluw, glub, mtab, ptab = tabs
    return pl.pallas_call(
        functools.partial(_s5_kernel, seq_len=seq_len),
        grid=(rows // tl,),
        in_specs=[
            pl.BlockSpec((tl, d_ssm), lambda i: (i, 0)),
            _const_spec((nb, n_state)),
            _const_spec((nb, n_state)),
            _const_spec(bs.shape),
            _const_spec(cs.shape),
            _const_spec(d.shape),
            _const_spec(gluw.shape),
            _const_spec(glub.shape),
            _const_spec(mtab.shape),
            _const_spec(ptab.shape),
        ],
        out_specs=[
            pl.BlockSpec((tl, d_ssm), lambda i: (i, 0)),
            pl.BlockSpec((nb, n_state), lambda i: (0, 0)),
            pl.BlockSpec((nb, n_state), lambda i: (0, 0)),
        ],
        out_shape=[
            jax.ShapeDtypeStruct((rows, d_ssm), BF16),
            jax.ShapeDtypeStruct((nb, n_state), F32),
            jax.ShapeDtypeStruct((nb, n_state), F32),
        ],
        scratch_shapes=[
            pltpu.VMEM((tl, n_state), F32),
            pltpu.VMEM((tl, n_state), F32),
            pltpu.VMEM((SUBLANES, n_state), F32),
            pltpu.VMEM((SUBLANES, n_state), F32),
        ],
        compiler_params=pltpu.CompilerParams(
            dimension_semantics=("arbitrary",), vmem_limit_bytes=VMEM_LIMIT),
        name="s5",
    )(u, h0r, h0i, bs, cs, d, gluw, glub, mtab, ptab)


def _mix_kernel(x_ref, ya_ref, su_ref, vn_ref, gates_ref, ws_ref, bias_ref, wa_ref, wb_ref, wo_ref, x1_ref):
    tm, dm = x_ref.shape
    n_heads = ws_ref.shape[0]
    hd = vn_ref.shape[1] // n_heads
    vb = vn_ref[...].astype(BF16)
    chunks = []
    for c in range(tm // SGU_CHUNK):
        heads = [jnp.dot(ws_ref[h], vb[c * SGU_CHUNK:(c + 1) * SGU_CHUNK, h * hd:(h + 1) * hd],
                         preferred_element_type=F32) for h in range(n_heads)]
        chunks.append(jnp.concatenate(heads, axis=1) + bias_ref[...])
    s_all = jnp.concatenate(chunks, axis=0)
    yb = (su_ref[...].astype(F32) * s_all).astype(BF16)
    pa = jnp.dot(ya_ref[...], wa_ref[...], preferred_element_type=F32)
    pb = jnp.dot(yb, wb_ref[...], preferred_element_type=F32)
    merged = gates_ref[:, :dm].astype(F32) * pa + gates_ref[:, dm:].astype(F32) * pb
    x1_ref[...] = x_ref[...] + jnp.dot(merged.astype(BF16), wo_ref[...], preferred_element_type=F32)


def _mix(x, ya, su, vn, gates, ws, bias, wa, wb, wo):
    rows, dm = x.shape
    d_ssm = ya.shape[1]
    d_sgu = su.shape[1]
    tm = _pick(rows, (256, 128))
    row = lambda w: pl.BlockSpec((tm, w), lambda i: (i, 0))
    return pl.pallas_call(
        _mix_kernel,
        grid=(rows // tm,),
        in_specs=[row(dm), row(d_ssm), row(d_sgu), row(d_sgu), row(2 * dm),
                  _const_spec(ws.shape), _const_spec(bias.shape),
                  _const_spec(wa.shape), _const_spec(wb.shape), _const_spec(wo.shape)],
        out_specs=row(dm),
        out_shape=jax.ShapeDtypeStruct((rows, dm), F32),
        compiler_params=pltpu.CompilerParams(
            dimension_semantics=("arbitrary",), vmem_limit_bytes=VMEM_LIMIT),
        name="mix",
    )(x, ya, su, vn, gates, ws, bias, wa, wb, wo)


def _topk_rows(s, k, payload=None):
    rows = s.shape[0]
    iota = lax.broadcasted_iota(I32, s.shape, 0).astype(F32)
    vals, outs = [], []
    for _ in range(k):
        m = jnp.max(s, axis=0, keepdims=True)
        j = jnp.min(jnp.where(s == m, iota, float(rows)), axis=0, keepdims=True)
        sel = iota == j
        vals.append(m)
        outs.append(j if payload is None else jnp.max(jnp.where(sel, payload, -1.0), axis=0, keepdims=True))
        s = jnp.where(sel, -jnp.inf, s)
    return jnp.concatenate(vals, axis=0), jnp.concatenate(outs, axis=0)


def _route_kernel(x1_ref, g_ref, wq_ref, keys_ref, xn_ref, eidx_ref, gate_ref, q_sc, e_sc, p_sc):
    tb = x1_ref.shape[0]
    tw = e_sc.shape[2]
    nsub = tb // tw
    n_heads, _, n_keys, half = keys_ref.shape
    dkey = 2 * half
    xn = _rms(x1_ref[...], g_ref[...])
    xn_ref[...] = xn
    q = jnp.dot(xn.astype(BF16), wq_ref[...], preferred_element_type=F32).astype(BF16)
    for h in range(n_heads):
        q_sc[h] = q[:, h * dkey:(h + 1) * dkey]

    nt = (((1,), (1,)), ((), ()))

    def body(n, _):
        h = n // nsub
        r0 = pl.multiple_of((n % nsub) * tw, tw)
        qh = q_sc[h, pl.ds(r0, tw), :]
        s1 = lax.dot_general(keys_ref[h, 0], qh[:, :half], nt, preferred_element_type=F32)
        s2 = lax.dot_general(keys_ref[h, 1], qh[:, half:], nt, preferred_element_type=F32)
        v1, i1 = _topk_rows(s1, PEER_TOPK)
        v2, i2 = _topk_rows(s2, PEER_TOPK)
        cv, ci = [], []
        for a in range(PEER_TOPK):
            nb = PEER_TOPK // (a + 1)
            cv.append(v1[a:a + 1, :] + v2[:nb, :])
            ci.append(i1[a:a + 1, :] * float(n_keys) + i2[:nb, :])
        ts, te = _topk_rows(jnp.concatenate(cv, axis=0), PEER_TOPK, payload=jnp.concatenate(ci, axis=0))
        ex = jnp.exp(ts - ts[0:1, :])
        e_sc[n] = te
        p_sc[n] = ex / jnp.sum(ex, axis=0, keepdims=True)
        return 0

    lax.fori_loop(0, n_heads * nsub, body, 0)
    for c in range(nsub):
        rows = slice(c * tw, (c + 1) * tw)
        e_all = jnp.concatenate([e_sc[h * nsub + c] for h in range(n_heads)], axis=0)
        p_all = jnp.concatenate([p_sc[h * nsub + c] for h in range(n_heads)], axis=0)
        eidx_ref[rows, :] = e_all.T.astype(I32)
        gate_ref[rows, :] = p_all.T


def _route(x1, g, wq, keys_bf):
    rows, dm = x1.shape
    n_heads, _, n_keys, half = keys_bf.shape
    nsel = n_heads * PEER_TOPK
    tb = _pick(rows, (256, 128))
    tw = min(tb, ROUTE_LANES)
    row = lambda w: pl.BlockSpec((tb, w), lambda i: (i, 0))
    return pl.pallas_call(
        _route_kernel,
        grid=(rows // tb,),
        in_specs=[row(dm), _const_spec(g.shape), _const_spec(wq.shape), _const_spec(keys_bf.shape)],
        out_specs=[row(dm), row(nsel), row(nsel)],
        out_shape=[
            jax.ShapeDtypeStruct((rows, dm), F32),
            jax.ShapeDtypeStruct((rows, nsel), I32),
            jax.ShapeDtypeStruct((rows, nsel), F32),
        ],
        scratch_shapes=[
            pltpu.VMEM((n_heads, tb, 2 * half), BF16),
            pltpu.VMEM((n_heads * (tb // tw), PEER_TOPK, tw), F32),
            pltpu.VMEM((n_heads * (tb // tw), PEER_TOPK, tw), F32),
        ],
        compiler_params=pltpu.CompilerParams(
            dimension_semantics=("arbitrary",), vmem_limit_bytes=VMEM_LIMIT),
        name="route",
    )(x1, g, wq, keys_bf)


PEER_BUFS = 4


def _peer_kernel(eidx_hbm, gate_ref, xn_ref, x1_ref, gf_ref, tab_hbm, y_ref,
                 idx_sm, idx_sem, b0, b1, b2, b3, sem, peer_sc, *, final_norm, staged):
    i = pl.program_id(0)
    bufs = (b0, b1, b2, b3)
    tbk, nsel = gate_ref.shape
    half = tab_hbm.shape[1] // 2
    nlt = half // LANES

    if staged:
        def token_rows(t):
            return tab_hbm.at[pl.ds(pl.multiple_of((i * tbk + t) * nsel, nsel), nsel), :]

        def issue(t, s):
            pltpu.make_async_copy(token_rows(t), bufs[s], sem.at[s]).start()
    else:
        icp = pltpu.make_async_copy(eidx_hbm.at[i], idx_sm, idx_sem)
        icp.start()
        icp.wait()

        def issue(t, s):
            base = t * nsel
            for k in range(nsel):
                e = idx_sm[base + k]
                pltpu.make_async_copy(tab_hbm.at[pl.ds(e, 1), :], bufs[s].at[pl.ds(k, 1), :],
                                      sem.at[s]).start(priority=k % 2)

    def wait(s):
        pltpu.make_async_copy(tab_hbm.at[pl.ds(0, nsel), :], bufs[s], sem.at[s]).wait()

    hi_mask = jnp.int32(-65536)
    eye = (lax.broadcasted_iota(I32, (nsel, nsel), 0) == lax.broadcasted_iota(I32, (nsel, nsel), 1))

    def unpack(w):
        return (lax.bitcast_convert_type(w & hi_mask, F32),
                lax.bitcast_convert_type(w << 16, F32))

    def compute(t, s):
        buf = bufs[s]
        xrow = xn_ref[pl.ds(t, 1), :]
        acc = jnp.zeros((nsel, LANES), F32)
        for j in range(nlt):
            uh, ul = unpack(buf[:, j * LANES:(j + 1) * LANES])
            acc = acc + uh * xrow[:, j * LANES:(j + 1) * LANES] \
                      + ul * xrow[:, half + j * LANES:half + (j + 1) * LANES]
        act = jnp.sum(acc, axis=1, keepdims=True)
        grow = jnp.broadcast_to(gate_ref[pl.ds(t, 1), :], (nsel, nsel))
        gcol = jnp.sum(jnp.where(eye, grow, 0.0), axis=1, keepdims=True)
        wgt = jnp.broadcast_to(gcol * jax.nn.gelu(act), (nsel, LANES))
        t8 = pl.multiple_of((t // SUBLANES) * SUBLANES, SUBLANES)
        rsel = lax.broadcasted_iota(I32, (SUBLANES, LANES), 0) == (t % SUBLANES)

        def put(lane0, row):
            cur = peer_sc[pl.ds(t8, SUBLANES), lane0:lane0 + LANES]
            peer_sc[pl.ds(t8, SUBLANES), lane0:lane0 + LANES] = jnp.where(
                rsel, jnp.broadcast_to(row, (SUBLANES, LANES)), cur)

        for j in range(nlt):
            vh, vl = unpack(buf[:, half + j * LANES:half + (j + 1) * LANES])
            put(j * LANES, jnp.sum(vh * wgt, axis=0, keepdims=True))
            put(half + j * LANES, jnp.sum(vl * wgt, axis=0, keepdims=True))

    def step(t, s, prefetch):
        wait(s)
        if prefetch:
            issue(t + PEER_BUFS - 1, (s + PEER_BUFS - 1) % PEER_BUFS)
        compute(t, s)

    peer_sc[...] = jnp.zeros_like(peer_sc)
    for t in range(PEER_BUFS - 1):
        issue(t, t)

    def group(gi, _):
        for s in range(PEER_BUFS):
            step(gi * PEER_BUFS + s, s, True)
        return 0

    lax.fori_loop(0, tbk // PEER_BUFS - 1, group, 0)
    for s in range(PEER_BUFS):
        t = tbk - PEER_BUFS + s
        step(t, s, t + PEER_BUFS - 1 < tbk)

    xo = x1_ref[...] + peer_sc[...]
    y_ref[...] = _rms(xo, gf_ref[...]) if final_norm else xo


def _peer(eidx, gate, xn, x1, gf, table, final_norm, staged=False):
    rows, dm = x1.shape
    nsel = gate.shape[1]
    tbk = _pick(rows, (256, 128, 64, 32, 16, 8))
    assert tbk % PEER_BUFS == 0 and tbk >= 2 * PEER_BUFS
    nblk = rows // tbk
    eidx_blk = eidx.reshape(nblk, tbk * nsel)
    row = lambda w: pl.BlockSpec((tbk, w), lambda i: (i, 0))
    return pl.pallas_call(
        functools.partial(_peer_kernel, final_norm=final_norm, staged=staged),
        grid=(nblk,),
        in_specs=[pl.BlockSpec(memory_space=pl.ANY), row(nsel), row(dm), row(dm),
                  _const_spec(gf.shape), pl.BlockSpec(memory_space=pl.ANY)],
        out_specs=row(dm),
        out_shape=jax.ShapeDtypeStruct((rows, dm), F32),
        scratch_shapes=[
            pltpu.SMEM((tbk * nsel,), I32),
            pltpu.SemaphoreType.DMA(()),
        ] + [pltpu.VMEM((nsel, table.shape[1]), table.dtype) for _ in range(PEER_BUFS)] + [
            pltpu.SemaphoreType.DMA((PEER_BUFS,)),
            pltpu.VMEM((tbk, dm), F32),
        ],
        compiler_params=pltpu.CompilerParams(
            dimension_semantics=("arbitrary",), vmem_limit_bytes=VMEM_LIMIT),
        name="peer_staged" if staged else "peer",
    )(eidx_blk, gate, xn, x1, gf, table)


SC_WINDOW = 16
SC_SHARE = 0.375
SC_SPLIT_ALIGN = 512


def _sc_gather(table, idx_flat):
    n = idx_flat.shape[0]
    d = table.shape[1]
    info = plsc.get_sparse_core_info()
    n_workers = info.num_cores * info.num_subcores
    per_tok = 128
    n_win = per_tok // SC_WINDOW
    tpw = n // per_tok // n_workers
    assert tpw * n_workers * per_tok == n
    mesh = plsc.VectorSubcoreMesh(core_axis_name="core", subcore_axis_name="subcore")

    @functools.partial(
        pl.kernel, out_type=jax.ShapeDtypeStruct((n, d), table.dtype), mesh=mesh,
        scratch_types=[pltpu.VMEM((per_tok,), I32), pltpu.VMEM((2, SC_WINDOW, d), table.dtype),
                       pltpu.SemaphoreType.DMA((2,)), pltpu.SemaphoreType.DMA((2,))],
        name="sc_gather")
    def k(x_hbm, i_hbm, o_hbm, idx_v, rows_v, gsem, ssem):
        wid = lax.axis_index("subcore") * info.num_cores + lax.axis_index("core")

        @pl.loop(0, tpw)
        def _(tt):
            row0 = (wid * tpw + tt) * per_tok
            pltpu.sync_copy(i_hbm.at[pl.ds(row0, per_tok)], idx_v)

            def gather(j):
                return pltpu.make_async_copy(x_hbm.at[idx_v.at[pl.ds(j * SC_WINDOW, SC_WINDOW)]],
                                             rows_v.at[j % 2], gsem.at[j % 2])

            def store(j):
                return pltpu.make_async_copy(rows_v.at[j % 2],
                                             o_hbm.at[pl.ds(row0 + j * SC_WINDOW, SC_WINDOW)], ssem.at[j % 2])

            gather(0).start()
            for j in range(n_win):
                gather(j).wait()
                if j + 1 < n_win:
                    if j >= 1:
                        store(j - 1).wait()
                    gather(j + 1).start()
                store(j).start()
            store(n_win - 2).wait()
            store(n_win - 1).wait()

    return k(table, idx_flat)


SC_L = 16


def _sc_peer(table, eidx, gate, xn):
    ntok, nsel = eidx.shape
    d = table.shape[1]
    half = d // 2
    nch = half // SC_L
    info = plsc.get_sparse_core_info()
    n_workers = info.num_cores * info.num_subcores
    n_win = nsel // SC_WINDOW
    tpw = ntok // n_workers
    assert tpw * n_workers == ntok and SC_WINDOW == SC_L and info.num_lanes == SC_L
    mesh = plsc.VectorSubcoreMesh(core_axis_name="core", subcore_axis_name="subcore")
    c0 = math.sqrt(2.0 / math.pi)

    @functools.partial(
        pl.kernel, out_type=jax.ShapeDtypeStruct((ntok * d,), F32), mesh=mesh,
        scratch_types=[pltpu.VMEM((nsel,), I32), pltpu.VMEM((nsel,), F32), pltpu.VMEM((d,), F32),
                       pltpu.VMEM((2, SC_WINDOW, d), I32), pltpu.VMEM((d,), F32),
                       pltpu.SemaphoreType.DMA((2,))],
        compiler_params=pltpu.CompilerParams(needs_layout_passes=False),
        name="sc_peer")
    def k(tab_hbm, idx_hbm, gate_hbm, x_hbm, o_hbm, idx_v, gate_v, x_v, rows_v, out_v, gsem):
        wid = lax.axis_index("subcore") * info.num_cores + lax.axis_index("core")
        lane = lax.iota(I32, SC_L)
        zero = jnp.zeros((SC_L,), F32)

        def unpack(w):
            return (lax.bitcast_convert_type(w & jnp.int32(-65536), F32),
                    lax.bitcast_convert_type(w << 16, F32))

        @pl.loop(0, tpw)
        def _(tt):
            tok = wid * tpw + tt
            pltpu.sync_copy(idx_hbm.at[pl.ds(tok * nsel, nsel)], idx_v)
            pltpu.sync_copy(gate_hbm.at[pl.ds(tok * nsel, nsel)], gate_v)
            pltpu.sync_copy(x_hbm.at[pl.ds(tok * d, d)], x_v)

            @pl.loop(0, d // SC_L)
            def _(c):
                out_v[pl.ds(c * SC_L, SC_L)] = zero

            def gather(w):
                return pltpu.make_async_copy(tab_hbm.at[idx_v.at[pl.ds(w * SC_WINDOW, SC_WINDOW)]],
                                             rows_v.at[w % 2], gsem.at[w % 2])

            gather(0).start()
            for w in range(n_win):
                gather(w).wait()
                if w + 1 < n_win:
                    gather(w + 1).start()
                slot = w % 2

                def dot_chunk(j, accs, slot=slot):
                    xa = x_v[pl.ds(j * SC_L, SC_L)]
                    xb = x_v[pl.ds(half + j * SC_L, SC_L)]
                    out = []
                    for r in range(SC_WINDOW):
                        uh, ul = unpack(rows_v[slot, r, pl.ds(j * SC_L, SC_L)])
                        out.append(accs[r] + (uh * xa + ul * xb))
                    return tuple(out)

                accs = lax.fori_loop(0, nch, dot_chunk, (zero,) * SC_WINDOW)
                act = zero
                for r in range(SC_WINDOW):
                    act = jnp.where(lane == r, jnp.sum(accs[r]), act)
                z = c0 * (act + 0.044715 * (act * act * act))
                tanh_z = 1.0 - 2.0 / (jnp.exp(2.0 * z) + 1.0)
                wgt = gate_v[pl.ds(w * SC_WINDOW, SC_WINDOW)] * (0.5 * act * (1.0 + tanh_z))
                splat = [jnp.full((SC_L,), jnp.sum(jnp.where(lane == r, wgt, 0.0)), F32)
                         for r in range(SC_WINDOW)]

                @pl.loop(0, nch)
                def _(j, slot=slot, splat=splat):
                    oh = zero
                    ol = zero
                    for r in range(SC_WINDOW):
                        vh, vl = unpack(rows_v[slot, r, pl.ds(half + j * SC_L, SC_L)])
                        oh = oh + splat[r] * vh
                        ol = ol + splat[r] * vl
                    out_v[pl.ds(j * SC_L, SC_L)] += oh
                    out_v[pl.ds(half + j * SC_L, SC_L)] += ol

            pltpu.sync_copy(out_v, o_hbm.at[pl.ds(tok * d, d)])

    return k(table, eidx.reshape(-1), gate.reshape(-1), xn.reshape(-1)).reshape(ntok, d)


def _finish_kernel(x1_ref, p_ref, gf_ref, y_ref):
    y_ref[...] = _rms(x1_ref[...] + p_ref[...], gf_ref[...])


def _finish(x1, p, gf):
    rows, dm = x1.shape
    tm = _pick(rows, (512, 256, 128))
    row = pl.BlockSpec((tm, dm), lambda i: (i, 0))
    return pl.pallas_call(
        _finish_kernel, grid=(rows // tm,), in_specs=[row, row, _const_spec(gf.shape)], out_specs=row,
        out_shape=jax.ShapeDtypeStruct((rows, dm), F32),
        compiler_params=pltpu.CompilerParams(dimension_semantics=("arbitrary",), vmem_limit_bytes=VMEM_LIMIT),
        name="finish",
    )(x1, p, gf)


def _cmul(ar, ai, br, bi):
    return ar * br - ai * bi, ar * bi + ai * br


def _s5_tables(lam_re, lam_im, log_dt, b_re, b_im, c_re, c_im, d, glu_w, glu_b):
    g, p = lam_re.shape
    hg = b_re.shape[-1]
    gs = g // SSM_SLABS
    dt = jnp.exp(log_dt.astype(F32))[:, None]
    lr = lam_re.astype(F32)
    li = lam_im.astype(F32)
    mag = jnp.exp(lr * dt)
    ab_re = mag * jnp.cos(li * dt)
    ab_im = mag * jnp.sin(li * dt)
    den = lr * lr + li * li
    nr = ab_re - 1.0
    ni = ab_im
    coef_re = (nr * lr + ni * li) / den
    coef_im = (ni * lr - nr * li) / den
    br = b_re.astype(F32)
    bi = b_im.astype(F32)
    bb_re = coef_re[..., None] * br - coef_im[..., None] * bi
    bb_im = coef_re[..., None] * bi + coef_im[..., None] * br
    eye = jnp.eye(gs, dtype=F32)

    def b_slab(bb):
        t = bb.reshape(SSM_SLABS, gs, p, hg).transpose(0, 1, 3, 2)
        return jnp.einsum('sihp,ij->sihjp', t, eye).reshape(SSM_SLABS, gs * hg, gs * p)

    bs = jnp.concatenate([b_slab(bb_re), b_slab(bb_im)], axis=2).astype(BF16)

    def c_slab(c):
        t = c.astype(F32).reshape(SSM_SLABS, gs, hg, p).transpose(0, 1, 3, 2)
        return jnp.einsum('siph,ij->sipjh', t, eye).reshape(SSM_SLABS, gs * p, gs * hg)

    cs = jnp.concatenate([c_slab(c_re), -c_slab(c_im)], axis=1).astype(BF16)

    a1 = (ab_re.reshape(1, g * p), ab_im.reshape(1, g * p))
    a2 = _cmul(*a1, *a1)
    a4 = _cmul(*a2, *a2)
    rowid = jnp.arange(SUBLANES)[:, None]
    mtab = jnp.stack([jnp.where(rowid >= s, comp, 0.0)
                      for s, a in ((1, a1), (2, a2), (4, a4)) for comp in a])
    pows = [a1]
    for _ in range(SUBLANES - 1):
        pows.append(_cmul(*pows[-1], *a1))
    ptab = jnp.stack([jnp.concatenate([q[0] for q in pows], axis=0),
                      jnp.concatenate([q[1] for q in pows], axis=0)])
    return (bs, cs, d.astype(F32).reshape(1, -1), glu_w.astype(BF16),
            glu_b.astype(F32).reshape(1, -1), mtab, ptab)


def _sgu_tables(w_s, b_s, seq_len):
    n_heads = w_s.shape[0]
    ln = min(SGU_CHUNK, seq_len)
    assert SGU_CHUNK % ln == 0 and seq_len % ln == 0
    rep = SGU_CHUNK // ln
    mask = jnp.tril(jnp.ones((ln, ln), dtype=bool))
    w = jnp.where(mask[None], w_s[:, :ln, :ln], 0.0).astype(F32)
    ws = jnp.einsum('hij,ab->haibj', w, jnp.eye(rep, dtype=F32)).reshape(n_heads, SGU_CHUNK, SGU_CHUNK)
    bias = jnp.tile(jnp.transpose(b_s[:, :ln]).astype(F32), (rep, 1))
    return ws.astype(BF16), bias


def _pack_kernel(u_ref, v_ref, o_ref):
    half = u_ref.shape[1] // 2

    def bf16_bits(x):
        return lax.bitcast_convert_type(x.astype(BF16).astype(F32), U32)

    def pack(t_ref):
        w = bf16_bits(t_ref[:, :half]) | (bf16_bits(t_ref[:, half:]) >> 16)
        return lax.bitcast_convert_type(w, I32)

    o_ref[:, :half] = pack(u_ref)
    o_ref[:, half:] = pack(v_ref)


def _pack_tables(u, v):
    n, dd = u.shape
    tr = _pick(n, (PACK_ROWS, 256, 128, 8))
    spec = pl.BlockSpec((tr, dd), lambda i: (i, 0))
    return pl.pallas_call(
        _pack_kernel,
        grid=(n // tr,),
        in_specs=[spec, spec],
        out_specs=spec,
        out_shape=jax.ShapeDtypeStruct((n, dd), I32),
        compiler_params=pltpu.CompilerParams(
            dimension_semantics=("arbitrary",), vmem_limit_bytes=VMEM_LIMIT),
        name="pack",
    )(u, v)


def _trunk_to_route(x, h0r, h0i, seq_len, lp):
    (norm_mix_g, w_in_bf, s5_tabs, ln_g, ln_b, sgu_w, sgu_b, wa, wb, wo, norm_ffn_g, wq, keys_bf, _,
     d_ssm, d_sgu) = lp
    u, su, vn, gates = _inproj(x, norm_mix_g, w_in_bf, ln_g, ln_b, d_ssm, d_sgu)
    ya, hlr, hli = _s5(u, h0r, h0i, s5_tabs, seq_len)
    ws, bias = _sgu_tables(sgu_w, sgu_b, seq_len)
    hd = d_sgu // ws.shape[0]
    bias_tile = jnp.repeat(bias, hd, axis=1)
    x1 = _mix(x, ya, su, vn, gates, ws, bias_tile, wa, wb, wo)
    xn, eidx, gate = _route(x1, norm_ffn_g, wq, keys_bf)
    return (x1, xn, eidx, gate), hlr, hli, vn


def _trunk_layer(x, h0r, h0i, seq_len, lp, final_g):
    (x1, xn, eidx, gate), hlr, hli, vn = _trunk_to_route(x, h0r, h0i, seq_len, lp)
    y = _peer(eidx, gate, xn, x1, final_g, lp[13], final_norm=True)
    return y, hlr, hli, vn


def _prompt_split(x, seq_len, lp, final_g, n_state):
    rows = x.shape[0]
    nb = (int(rows * SC_SHARE) // SC_SPLIT_ALIGN) * SC_SPLIT_ALIGN
    zeros = jnp.zeros((1, n_state), F32)
    (x1b, xnb, eb, gb), hr_b, hi_b, _ = _trunk_to_route(x[:nb], zeros, zeros, seq_len, lp)
    (x1a, xna, ea, ga), hr, hi, _ = _trunk_to_route(x[nb:], hr_b, hi_b, seq_len, lp)
    ya = _peer(ea, ga, xna, x1a, final_g, lp[13], final_norm=True)
    return (x1b, xnb, eb, gb), ya, hr, hi


def _prompt_finish(first, ya, lp, final_g):
    x1b, xnb, eb, gb = first
    yb = _finish(x1b, _sc_peer(lp[13], eb, gb, xnb), final_g)
    return jnp.concatenate([yb, ya], axis=0)


def kernel(x_prompt, x_sample, state_ssm_re, state_ssm_im, norm_mix_g, w_in, ssm_lambda_re, ssm_lambda_im, ssm_log_dt, ssm_b_re, ssm_b_im, ssm_c_re, ssm_c_im, ssm_d, ssm_glu_w, ssm_glu_b, sgu_ln_g, sgu_ln_b, sgu_w, sgu_b, w_branch_a, w_branch_b, w_out, norm_ffn_g, peer_w_q, peer_keys, peer_u, peer_v, norm_final_g):
    depth = w_in.shape[0]
    assert depth == 1, "the final norm is fused into the last layer's PEER kernel"
    bp, sp, dm = x_prompt.shape
    bs_, ss, _ = x_sample.shape
    g, p = ssm_lambda_re.shape[1:]
    d_ssm = ssm_d.shape[1]
    d_sgu = sgu_ln_g.shape[1]
    n_state = g * p
    l = 0
    lp = (norm_mix_g[l].reshape(1, dm), w_in[l].astype(BF16),
          _s5_tables(ssm_lambda_re[l], ssm_lambda_im[l], ssm_log_dt[l], ssm_b_re[l], ssm_b_im[l],
                     ssm_c_re[l], ssm_c_im[l], ssm_d[l], ssm_glu_w[l], ssm_glu_b[l]),
          sgu_ln_g[l].reshape(1, d_sgu), sgu_ln_b[l].reshape(1, d_sgu), sgu_w[l], sgu_b[l],
          w_branch_a[l].astype(BF16), w_branch_b[l].astype(BF16), w_out[l].astype(BF16),
          norm_ffn_g[l].reshape(1, dm), peer_w_q[l].astype(BF16), peer_keys[l].astype(BF16),
          _pack_tables(peer_u[l].astype(F32), peer_v[l].astype(F32)),
          d_ssm, d_sgu)
    gf = norm_final_g.reshape(1, dm)

    xp2 = x_prompt.reshape(bp * sp, dm)
    split = bp == 1 and sp >= 4 * SC_SPLIT_ALIGN
    if split:
        first, ya, hr_p, hi_p = _prompt_split(xp2, sp, lp, gf, n_state)
    else:
        zeros = jnp.zeros((bp, n_state), F32)
        yp, hr_p, hi_p, _ = _trunk_layer(xp2, zeros, zeros, sp, lp, gf)
    ys, hr_s, hi_s, v_s = _trunk_layer(x_sample.reshape(bs_ * ss, dm),
                                       state_ssm_re[l].astype(F32).reshape(bs_, n_state),
                                       state_ssm_im[l].astype(F32).reshape(bs_, n_state), ss, lp, gf)
    if split:
        yp = _prompt_finish(first, ya, lp, gf)
    return (yp.reshape(bp, sp, dm), ys.reshape(bs_, ss, dm),
            hr_p.reshape(1, bp, g, p), hi_p.reshape(1, bp, g, p),
            hr_s.reshape(1, bs_, g, p), hi_s.reshape(1, bs_, g, p),
            v_s.reshape(1, bs_, ss, d_sgu))
```

```python
import functools
import math

import jax
import jax.numpy as jnp
from jax import lax
from jax.experimental import pallas as pl
from jax.experimental.pallas import tpu as pltpu
from jax.experimental.pallas import tpu_sc as plsc

F32 = jnp.float32
BF16 = jnp.bfloat16
I32 = jnp.int32
U32 = jnp.uint32
EPS = 1e-6

LANES = 128
SUBLANES = 8
VMEM_LIMIT = 56 * 1024 * 1024

SSM_SLABS = 4
SCAN_LANES = 512
PEER_TOPK = 16
SGU_CHUNK = 128
ROUTE_LANES = 256
INPROJ_SUB = 128
PACK_ROWS = 512


def _pick(n, candidates):
    for c in candidates:
        if n % c == 0:
            return c
    raise ValueError(f"no block size for {n} in {candidates}")


def _const_spec(shape):
    nd = len(shape)
    return pl.BlockSpec(shape, lambda *_: (0,) * nd, pipeline_mode=pl.Buffered(1))


def _rms(x, g):
    return x * lax.rsqrt(jnp.mean(x * x, axis=-1, keepdims=True) + EPS) * g


def _inproj_kernel(x_ref, g_ref, w_ref, lng_ref, lnb_ref, u_ref, su_ref, vn_ref, gates_ref, xn_sc):
    j = pl.program_id(1)
    tm = x_ref.shape[0]
    sub = min(tm, INPROJ_SUB)

    @pl.when(j == 0)
    def _():
        xn_sc[...] = _rms(x_ref[...], g_ref[...]).astype(BF16)

    def section(out_ref, epilogue):
        for r in range(tm // sub):
            rows = slice(r * sub, (r + 1) * sub)
            z = jnp.dot(xn_sc[rows, :], w_ref[...], preferred_element_type=F32)
            out_ref[rows, :] = epilogue(z).astype(out_ref.dtype)

    def layer_norm_gelu(z):
        v = jax.nn.gelu(z)
        mu = jnp.mean(v, axis=-1, keepdims=True)
        vc = v - mu
        y = vc * lax.rsqrt(jnp.mean(vc * vc, axis=-1, keepdims=True) + EPS)
        return y * lng_ref[...] + lnb_ref[...]

    @pl.when(j == 0)
    def _():
        section(u_ref, lambda z: z)

    @pl.when(j == 1)
    def _():
        section(su_ref, jax.nn.gelu)

    @pl.when(j == 2)
    def _():
        section(vn_ref, layer_norm_gelu)

    @pl.when(j >= 3)
    def _():
        section(gates_ref, jax.nn.sigmoid)


def _inproj(x, g, w_in_bf, ln_g, ln_b, d_ssm, d_sgu):
    rows, dm = x.shape
    d_in = w_in_bf.shape[1]
    tn = d_ssm
    assert d_sgu == tn and (d_in - 3 * tn) % tn == 0
    nj = d_in // tn
    tm = _pick(rows, (512, 256, 128))
    return pl.pallas_call(
        _inproj_kernel,
        grid=(rows // tm, nj),
        in_specs=[
            pl.BlockSpec((tm, dm), lambda i, j: (i, 0)),
            pl.BlockSpec((1, dm), lambda i, j: (0, 0)),
            pl.BlockSpec((dm, tn), lambda i, j: (0, j)),
            pl.BlockSpec((1, tn), lambda i, j: (0, 0)),
            pl.BlockSpec((1, tn), lambda i, j: (0, 0)),
        ],
        out_specs=[
            pl.BlockSpec((tm, tn), lambda i, j: (i, 0)),
            pl.BlockSpec((tm, tn), lambda i, j: (i, 0)),
            pl.BlockSpec((tm, tn), lambda i, j: (i, 0)),
            pl.BlockSpec((tm, tn), lambda i, j: (i, jnp.maximum(j - 3, 0))),
        ],
        out_shape=[
            jax.ShapeDtypeStruct((rows, tn), F32),
            jax.ShapeDtypeStruct((rows, tn), BF16),
            jax.ShapeDtypeStruct((rows, tn), F32),
            jax.ShapeDtypeStruct((rows, d_in - 3 * tn), BF16),
        ],
        scratch_shapes=[pltpu.VMEM((tm, dm), BF16)],
        compiler_params=pltpu.CompilerParams(
            dimension_semantics=("arbitrary", "arbitrary"), vmem_limit_bytes=VMEM_LIMIT),
        name="inproj",
    )(x, g, w_in_bf, ln_g, ln_b)


def _s5_kernel(u_ref, h0r_ref, h0i_ref, bs_ref, cs_ref, d_ref, gluw_ref, glub_ref, m_ref, pw_ref,
               ya_ref, hlr_ref, hli_ref, hr_sc, hi_sc, cr_sc, ci_sc, *, seq_len):
    i = pl.program_id(0)
    tl, d_ssm = u_ref.shape
    n_state = hr_sc.shape[1]
    kin = d_ssm // SSM_SLABS
    kst = n_state // SSM_SLABS

    @pl.when(i == 0)
    def _():
        cr_sc[...] = jnp.zeros_like(cr_sc)
        ci_sc[...] = jnp.zeros_like(ci_sc)

    ub = u_ref[...].astype(BF16)
    for s in range(SSM_SLABS):
        r = jnp.dot(ub[:, kin * s:kin * (s + 1)], bs_ref[s], preferred_element_type=F32)
        hr_sc[:, kst * s:kst * (s + 1)] = r[:, :kst]
        hi_sc[:, kst * s:kst * (s + 1)] = r[:, kst:]

    n_tiles = tl // SUBLANES
    for lc in range(n_state // SCAN_LANES):
        ls = slice(lc * SCAN_LANES, (lc + 1) * SCAN_LANES)

        def tile_body(t, carry, ls=ls):
            cr, ci = carry
            row0 = i * tl + t * SUBLANES
            b = row0 // seq_len
            is_start = (row0 % seq_len) == 0
            h0r = jnp.broadcast_to(h0r_ref[pl.ds(b, 1), ls], (SUBLANES, SCAN_LANES))
            h0i = jnp.broadcast_to(h0i_ref[pl.ds(b, 1), ls], (SUBLANES, SCAN_LANES))
            cr = jnp.where(is_start, h0r, cr)
            ci = jnp.where(is_start, h0i, ci)
            r0 = pl.multiple_of(t * SUBLANES, SUBLANES)
            br = hr_sc[pl.ds(r0, SUBLANES), ls]
            bi = hi_sc[pl.ds(r0, SUBLANES), ls]
            for k, shift in enumerate((1, 2, 4)):
                mr = m_ref[2 * k, :, ls]
                mi = m_ref[2 * k + 1, :, ls]
                sr = pltpu.roll(br, shift, 0)
                si = pltpu.roll(bi, shift, 0)
                br, bi = br + (mr * sr - mi * si), bi + (mr * si + mi * sr)
            pr = pw_ref[0, :, ls]
            pi_ = pw_ref[1, :, ls]
            hr = br + (pr * cr - pi_ * ci)
            hi = bi + (pr * ci + pi_ * cr)
            hr_sc[pl.ds(r0, SUBLANES), ls] = hr
            hi_sc[pl.ds(r0, SUBLANES), ls] = hi
            last_r = hr[SUBLANES - 1:SUBLANES, :]
            last_i = hi[SUBLANES - 1:SUBLANES, :]
            hlr_ref[pl.ds(b, 1), ls] = last_r
            hli_ref[pl.ds(b, 1), ls] = last_i
            return (jnp.broadcast_to(last_r, (SUBLANES, SCAN_LANES)),
                    jnp.broadcast_to(last_i, (SUBLANES, SCAN_LANES)))

        cr, ci = lax.fori_loop(0, n_tiles, tile_body, (cr_sc[:, ls], ci_sc[:, ls]))
        cr_sc[:, ls] = cr
        ci_sc[:, ls] = ci

    ys = []
    for s in range(SSM_SLABS):
        hre = hr_sc[:, kst * s:kst * (s + 1)].astype(BF16)
        him = hi_sc[:, kst * s:kst * (s + 1)].astype(BF16)
        ys.append(jnp.dot(hre, cs_ref[s, :kst, :], preferred_element_type=F32)
                  + jnp.dot(him, cs_ref[s, kst:, :], preferred_element_type=F32))
    y = jnp.concatenate(ys, axis=1)
    y = jax.nn.gelu(y + d_ref[...] * u_ref[...])
    gt = jnp.dot(y.astype(BF16), gluw_ref[...], preferred_element_type=F32) + glub_ref[...]
    ya_ref[...] = (y * jax.nn.sigmoid(gt)).astype(BF16)


def _s5(u, h0r, h0i, tabs, seq_len):
    rows, d_ssm = u.shape
    nb, n_state = h0r.shape
    tl = _pick(rows, (256, 128, 64, 32, 16, 8)) if seq_len >= 256 else _pick(rows, (512, 256, 128, 64, 32, 16))
    assert (seq_len % tl == 0 or tl % seq_len == 0) and seq_len % SUBLANES == 0
    bs, cs, d, gluw, glub, mtab, ptab = tabs
    return pl.pallas_call(
        functools.partial(_s5_kernel, seq_len=seq_len),
        grid=(rows // tl,),
        in_specs=[
            pl.BlockSpec((tl, d_ssm), lambda i: (i, 0)),
            _const_spec((nb, n_state)),
            _const_spec((nb, n_state)),
            _const_spec(bs.shape),
            _const_spec(cs.shape),
            _const_spec(d.shape),
            _const_spec(gluw.shape),
            _const_spec(glub.shape),
            _const_spec(mtab.shape),
            _const_spec(ptab.shape),
        ],
        out_specs=[
            pl.BlockSpec((tl, d_ssm), lambda i: (i, 0)),
            pl.BlockSpec((nb, n_state), lambda i: (0, 0)),
            pl.BlockSpec((nb, n_state), lambda i: (0, 0)),
        ],
        out_shape=[
            jax.ShapeDtypeStruct((rows, d_ssm), BF16),
            jax.ShapeDtypeStruct((nb, n_state), F32),
            jax.ShapeDtypeStruct((nb, n_state), F32),
        ],
        scratch_shapes=[
            pltpu.VMEM((tl, n_state), F32),
            pltpu.VMEM((tl, n_state), F32),
            pltpu.VMEM((SUBLANES, n_state), F32),
            pltpu.VMEM((SUBLANES, n_state), F32),
        ],
        compiler_params=pltpu.CompilerParams(
            dimension_semantics=("arbitrary",), vmem_limit_bytes=VMEM_LIMIT),
        name="s5",
    )(u, h0r, h0i, bs, cs, d, gluw, glub, mtab, ptab)


def _mix_kernel(x_ref, ya_ref, su_ref, vn_ref, gates_ref, ws_ref, bias_ref, wa_ref, wb_ref, wo_ref, x1_ref):
    tm, dm = x_ref.shape
    n_heads = ws_ref.shape[0]
    hd = vn_ref.shape[1] // n_heads
    vb = vn_ref[...].astype(BF16)
    chunks = []
    for c in range(tm // SGU_CHUNK):
        heads = [jnp.dot(ws_ref[h], vb[c * SGU_CHUNK:(c + 1) * SGU_CHUNK, h * hd:(h + 1) * hd],
                         preferred_element_type=F32) for h in range(n_heads)]
        chunks.append(jnp.concatenate(heads, axis=1) + bias_ref[...])
    s_all = jnp.concatenate(chunks, axis=0)
    yb = (su_ref[...].astype(F32) * s_all).astype(BF16)
    pa = jnp.dot(ya_ref[...], wa_ref[...], preferred_element_type=F32)
    pb = jnp.dot(yb, wb_ref[...], preferred_element_type=F32)
    merged = gates_ref[:, :dm].astype(F32) * pa + gates_ref[:, dm:].astype(F32) * pb
    x1_ref[...] = x_ref[...] + jnp.dot(merged.astype(BF16), wo_ref[...], preferred_element_type=F32)


def _mix(x, ya, su, vn, gates, ws, bias, wa, wb, wo):
    rows, dm = x.shape
    d_ssm = ya.shape[1]
    d_sgu = su.shape[1]
    tm = _pick(rows, (256, 128))
    row = lambda w: pl.BlockSpec((tm, w), lambda i: (i, 0))
    return pl.pallas_call(
        _mix_kernel,
        grid=(rows // tm,),
        in_specs=[row(dm), row(d_ssm), row(d_sgu), row(d_sgu), row(2 * dm),
                  _const_spec(ws.shape), _const_spec(bias.shape),
                  _const_spec(wa.shape), _const_spec(wb.shape), _const_spec(wo.shape)],
        out_specs=row(dm),
        out_shape=jax.ShapeDtypeStruct((rows, dm), F32),
        compiler_params=pltpu.CompilerParams(
            dimension_semantics=("arbitrary",), vmem_limit_bytes=VMEM_LIMIT),
        name="mix",
    )(x, ya, su, vn, gates, ws, bias, wa, wb, wo)


def _topk_rows(s, k, payload=None):
    rows = s.shape[0]
    iota = lax.broadcasted_iota(I32, s.shape, 0).astype(F32)
    vals, outs = [], []
    for _ in range(k):
        m = jnp.max(s, axis=0, keepdims=True)
        j = jnp.min(jnp.where(s == m, iota, float(rows)), axis=0, keepdims=True)
        sel = iota == j
        vals.append(m)
        outs.append(j if payload is None else jnp.max(jnp.where(sel, payload, -1.0), axis=0, keepdims=True))
        s = jnp.where(sel, -jnp.inf, s)
    return jnp.concatenate(vals, axis=0), jnp.concatenate(outs, axis=0)


def _route_kernel(x1_ref, g_ref, wq_ref, keys_ref, xn_ref, eidx_ref, gate_ref, q_sc, e_sc, p_sc):
    tb = x1_ref.shape[0]
    tw = e_sc.shape[2]
    nsub = tb // tw
    n_heads, _, n_keys, half = keys_ref.shape
    dkey = 2 * half
    xn = _rms(x1_ref[...], g_ref[...])
    xn_ref[...] = xn
    q = jnp.dot(xn.astype(BF16), wq_ref[...], preferred_element_type=F32).astype(BF16)
    for h in range(n_heads):
        q_sc[h] = q[:, h * dkey:(h + 1) * dkey]

    nt = (((1,), (1,)), ((), ()))

    def body(n, _):
        h = n // nsub
        r0 = pl.multiple_of((n % nsub) * tw, tw)
        qh = q_sc[h, pl.ds(r0, tw), :]
        s1 = lax.dot_general(keys_ref[h, 0], qh[:, :half], nt, preferred_element_type=F32)
        s2 = lax.dot_general(keys_ref[h, 1], qh[:, half:], nt, preferred_element_type=F32)
        v1, i1 = _topk_rows(s1, PEER_TOPK)
        v2, i2 = _topk_rows(s2, PEER_TOPK)
        cv, ci = [], []
        for a in range(PEER_TOPK):
            nb = PEER_TOPK // (a + 1)
            cv.append(v1[a:a + 1, :] + v2[:nb, :])
            ci.append(i1[a:a + 1, :] * float(n_keys) + i2[:nb, :])
        ts, te = _topk_rows(jnp.concatenate(cv, axis=0), PEER_TOPK, payload=jnp.concatenate(ci, axis=0))
        ex = jnp.exp(ts - ts[0:1, :])
        e_sc[n] = te
        p_sc[n] = ex / jnp.sum(ex, axis=0, keepdims=True)
        return 0

    lax.fori_loop(0, n_heads * nsub, body, 0)
    for c in range(nsub):
        rows = slice(c * tw, (c + 1) * tw)
        e_all = jnp.concatenate([e_sc[h * nsub + c] for h in range(n_heads)], axis=0)
        p_all = jnp.concatenate([p_sc[h * nsub + c] for h in range(n_heads)], axis=0)
        eidx_ref[rows, :] = e_all.T.astype(I32)
        gate_ref[rows, :] = p_all.T


def _route(x1, g, wq, keys_bf):
    rows, dm = x1.shape
    n_heads, _, n_keys, half = keys_bf.shape
    nsel = n_heads * PEER_TOPK
    tb = _pick(rows, (256, 128))
    tw = min(tb, ROUTE_LANES)
    row = lambda w: pl.BlockSpec((tb, w), lambda i: (i, 0))
    return pl.pallas_call(
        _route_kernel,
        grid=(rows // tb,),
        in_specs=[row(dm), _const_spec(g.shape), _const_spec(wq.shape), _const_spec(keys_bf.shape)],
        out_specs=[row(dm), row(nsel), row(nsel)],
        out_shape=[
            jax.ShapeDtypeStruct((rows, dm), F32),
            jax.ShapeDtypeStruct((rows, nsel), I32),
            jax.ShapeDtypeStruct((rows, nsel), F32),
        ],
        scratch_shapes=[
            pltpu.VMEM((n_heads, tb, 2 * half), BF16),
            pltpu.VMEM((n_heads * (tb // tw), PEER_TOPK, tw), F32),
            pltpu.VMEM((n_heads * (tb // tw), PEER_TOPK, tw), F32),
        ],
        compiler_params=pltpu.CompilerParams(
            dimension_semantics=("arbitrary",), vmem_limit_bytes=VMEM_LIMIT),
        name="route",
    )(x1, g, wq, keys_bf)


PEER_BUFS = 4


def _peer_kernel(eidx_hbm, gate_ref, xn_ref, x1_ref, gf_ref, tab_hbm, y_ref,
                 idx_sm, idx_sem, b0, b1, b2, b3, sem, peer_sc, *, final_norm):
    i = pl.program_id(0)
    bufs = (b0, b1, b2, b3)
    tbk, nsel = gate_ref.shape
    half = tab_hbm.shape[1] // 2
    nlt = half // LANES

    icp = pltpu.make_async_copy(eidx_hbm.at[i], idx_sm, idx_sem)
    icp.start()
    icp.wait()

    def issue(t, s):
        base = t * nsel
        for k in range(nsel):
            e = idx_sm[base + k]
            pltpu.make_async_copy(tab_hbm.at[pl.ds(e, 1), :], bufs[s].at[pl.ds(k, 1), :],
                                  sem.at[s]).start(priority=k % 2)

    def wait(s):
        pltpu.make_async_copy(tab_hbm.at[pl.ds(0, nsel), :], bufs[s], sem.at[s]).wait()

    hi_mask = jnp.int32(-65536)
    eye = (lax.broadcasted_iota(I32, (nsel, nsel), 0) == lax.broadcasted_iota(I32, (nsel, nsel), 1))

    def unpack(w):
        return (lax.bitcast_convert_type(w & hi_mask, F32),
                lax.bitcast_convert_type(w << 16, F32))

    def compute(t, s):
        buf = bufs[s]
        xrow = xn_ref[pl.ds(t, 1), :]
        acc = jnp.zeros((nsel, LANES), F32)
        for j in range(nlt):
            uh, ul = unpack(buf[:, j * LANES:(j + 1) * LANES])
            acc = acc + uh * xrow[:, j * LANES:(j + 1) * LANES] \
                      + ul * xrow[:, half + j * LANES:half + (j + 1) * LANES]
        act = jnp.sum(acc, axis=1, keepdims=True)
        grow = jnp.broadcast_to(gate_ref[pl.ds(t, 1), :], (nsel, nsel))
        gcol = jnp.sum(jnp.where(eye, grow, 0.0), axis=1, keepdims=True)
        wgt = jnp.broadcast_to(gcol * jax.nn.gelu(act), (nsel, LANES))
        t8 = pl.multiple_of((t // SUBLANES) * SUBLANES, SUBLANES)
        rsel = lax.broadcasted_iota(I32, (SUBLANES, LANES), 0) == (t % SUBLANES)

        def put(lane0, row):
            cur = peer_sc[pl.ds(t8, SUBLANES), lane0:lane0 + LANES]
            peer_sc[pl.ds(t8, SUBLANES), lane0:lane0 + LANES] = jnp.where(
                rsel, jnp.broadcast_to(row, (SUBLANES, LANES)), cur)

        for j in range(nlt):
            vh, vl = unpack(buf[:, half + j * LANES:half + (j + 1) * LANES])
            put(j * LANES, jnp.sum(vh * wgt, axis=0, keepdims=True))
            put(half + j * LANES, jnp.sum(vl * wgt, axis=0, keepdims=True))

    def step(t, s, prefetch):
        wait(s)
        if prefetch:
            issue(t + PEER_BUFS - 1, (s + PEER_BUFS - 1) % PEER_BUFS)
        compute(t, s)

    peer_sc[...] = jnp.zeros_like(peer_sc)
    for t in range(PEER_BUFS - 1):
        issue(t, t)

    def group(gi, _):
        for s in range(PEER_BUFS):
            step(gi * PEER_BUFS + s, s, True)
        return 0

    lax.fori_loop(0, tbk // PEER_BUFS - 1, group, 0)
    for s in range(PEER_BUFS):
        t = tbk - PEER_BUFS + s
        step(t, s, t + PEER_BUFS - 1 < tbk)

    xo = x1_ref[...] + peer_sc[...]
    y_ref[...] = _rms(xo, gf_ref[...]) if final_norm else xo


def _peer(eidx, gate, xn, x1, gf, table, final_norm):
    rows, dm = x1.shape
    nsel = gate.shape[1]
    tbk = _pick(rows, (256, 128, 64, 32, 16, 8))
    assert tbk % PEER_BUFS == 0 and tbk >= 2 * PEER_BUFS
    nblk = rows // tbk
    eidx_blk = eidx.reshape(nblk, tbk * nsel)
    row = lambda w: pl.BlockSpec((tbk, w), lambda i: (i, 0))
    return pl.pallas_call(
        functools.partial(_peer_kernel, final_norm=final_norm),
        grid=(nblk,),
        in_specs=[pl.BlockSpec(memory_space=pl.ANY), row(nsel), row(dm), row(dm),
                  _const_spec(gf.shape), pl.BlockSpec(memory_space=pl.ANY)],
        out_specs=row(dm),
        out_shape=jax.ShapeDtypeStruct((rows, dm), F32),
        scratch_shapes=[
            pltpu.SMEM((tbk * nsel,), I32),
            pltpu.SemaphoreType.DMA(()),
        ] + [pltpu.VMEM((nsel, table.shape[1]), table.dtype) for _ in range(PEER_BUFS)] + [
            pltpu.SemaphoreType.DMA((PEER_BUFS,)),
            pltpu.VMEM((tbk, dm), F32),
        ],
        compiler_params=pltpu.CompilerParams(
            dimension_semantics=("arbitrary",), vmem_limit_bytes=VMEM_LIMIT),
        name="peer",
    )(eidx_blk, gate, xn, x1, gf, table)


SC_L = 16
SC_WINDOW = SC_L
SC_SHARE = 0.4375
SC_SPLIT_ALIGN = 512


def _sc_peer(table, eidx, gate, xn):
    ntok, nsel = eidx.shape
    d = table.shape[1]
    half = d // 2
    nch = half // SC_L
    info = plsc.get_sparse_core_info()
    n_workers = info.num_cores * info.num_subcores
    n_win = nsel // SC_WINDOW
    tpw = ntok // n_workers
    assert tpw * n_workers == ntok and SC_WINDOW == SC_L and info.num_lanes == SC_L
    mesh = plsc.VectorSubcoreMesh(core_axis_name="core", subcore_axis_name="subcore")
    c0 = math.sqrt(2.0 / math.pi)

    @functools.partial(
        pl.kernel, out_type=jax.ShapeDtypeStruct((ntok * d,), F32), mesh=mesh,
        scratch_types=[pltpu.VMEM((nsel,), I32), pltpu.VMEM((nsel,), F32), pltpu.VMEM((d,), F32),
                       pltpu.VMEM((2, SC_WINDOW, d), I32), pltpu.VMEM((d,), F32),
                       pltpu.SemaphoreType.DMA((2,))],
        compiler_params=pltpu.CompilerParams(needs_layout_passes=False),
        name="sc_peer")
    def k(tab_hbm, idx_hbm, gate_hbm, x_hbm, o_hbm, idx_v, gate_v, x_v, rows_v, out_v, gsem):
        wid = lax.axis_index("subcore") * info.num_cores + lax.axis_index("core")
        lane = lax.iota(I32, SC_L)
        zero = jnp.zeros((SC_L,), F32)

        def unpack(w):
            return (lax.bitcast_convert_type(w & jnp.int32(-65536), F32),
                    lax.bitcast_convert_type(w << 16, F32))

        @pl.loop(0, tpw)
        def _(tt):
            tok = wid * tpw + tt
            pltpu.sync_copy(idx_hbm.at[pl.ds(tok * nsel, nsel)], idx_v)
            pltpu.sync_copy(gate_hbm.at[pl.ds(tok * nsel, nsel)], gate_v)
            pltpu.sync_copy(x_hbm.at[pl.ds(tok * d, d)], x_v)

            @pl.loop(0, d // SC_L)
            def _(c):
                out_v[pl.ds(c * SC_L, SC_L)] = zero

            def gather(w):
                return pltpu.make_async_copy(tab_hbm.at[idx_v.at[pl.ds(w * SC_WINDOW, SC_WINDOW)]],
                                             rows_v.at[w % 2], gsem.at[w % 2])

            gather(0).start()
            for w in range(n_win):
                gather(w).wait()
                if w + 1 < n_win:
                    gather(w + 1).start()
                slot = w % 2

                def dot_chunk(j, accs, slot=slot):
                    xa = x_v[pl.ds(j * SC_L, SC_L)]
                    xb = x_v[pl.ds(half + j * SC_L, SC_L)]
                    out = []
                    for r in range(SC_WINDOW):
                        uh, ul = unpack(rows_v[slot, r, pl.ds(j * SC_L, SC_L)])
                        out.append(accs[r] + (uh * xa + ul * xb))
                    return tuple(out)

                accs = lax.fori_loop(0, nch, dot_chunk, (zero,) * SC_WINDOW)
                act = zero
                for r in range(SC_WINDOW):
                    act = jnp.where(lane == r, jnp.sum(accs[r]), act)
                z = c0 * (act + 0.044715 * (act * act * act))
                tanh_z = 1.0 - 2.0 / (jnp.exp(2.0 * z) + 1.0)
                wgt = gate_v[pl.ds(w * SC_WINDOW, SC_WINDOW)] * (0.5 * act * (1.0 + tanh_z))
                splat = [jnp.full((SC_L,), jnp.sum(jnp.where(lane == r, wgt, 0.0)), F32)
                         for r in range(SC_WINDOW)]

                @pl.loop(0, nch)
                def _(j, slot=slot, splat=splat):
                    oh = zero
                    ol = zero
                    for r in range(SC_WINDOW):
                        vh, vl = unpack(rows_v[slot, r, pl.ds(half + j * SC_L, SC_L)])
                        oh = oh + splat[r] * vh
                        ol = ol + splat[r] * vl
                    out_v[pl.ds(j * SC_L, SC_L)] += oh
                    out_v[pl.ds(half + j * SC_L, SC_L)] += ol

            pltpu.sync_copy(out_v, o_hbm.at[pl.ds(tok * d, d)])

    return k(table, eidx.reshape(-1), gate.reshape(-1), xn.reshape(-1)).reshape(ntok, d)


def _finish_kernel(x1_ref, p_ref, gf_ref, y_ref):
    y_ref[...] = _rms(x1_ref[...] + p_ref[...], gf_ref[...])


def _finish(x1, p, gf):
    rows, dm = x1.shape
    tm = _pick(rows, (512, 256, 128))
    row = pl.BlockSpec((tm, dm), lambda i: (i, 0))
    return pl.pallas_call(
        _finish_kernel, grid=(rows // tm,), in_specs=[row, row, _const_spec(gf.shape)], out_specs=row,
        out_shape=jax.ShapeDtypeStruct((rows, dm), F32),
        compiler_params=pltpu.CompilerParams(dimension_semantics=("arbitrary",), vmem_limit_bytes=VMEM_LIMIT),
        name="finish",
    )(x1, p, gf)


def _cmul(ar, ai, br, bi):
    return ar * br - ai * bi, ar * bi + ai * br


def _s5_tables(lam_re, lam_im, log_dt, b_re, b_im, c_re, c_im, d, glu_w, glu_b):
    g, p = lam_re.shape
    hg = b_re.shape[-1]
    gs = g // SSM_SLABS
    dt = jnp.exp(log_dt.astype(F32))[:, None]
    lr = lam_re.astype(F32)
    li = lam_im.astype(F32)
    mag = jnp.exp(lr * dt)
    ab_re = mag * jnp.cos(li * dt)
    ab_im = mag * jnp.sin(li * dt)
    den = lr * lr + li * li
    nr = ab_re - 1.0
    ni = ab_im
    coef_re = (nr * lr + ni * li) / den
    coef_im = (ni * lr - nr * li) / den
    br = b_re.astype(F32)
    bi = b_im.astype(F32)
    bb_re = coef_re[..., None] * br - coef_im[..., None] * bi
    bb_im = coef_re[..., None] * bi + coef_im[..., None] * br
    eye = jnp.eye(gs, dtype=F32)

    def b_slab(bb):
        t = bb.reshape(SSM_SLABS, gs, p, hg).transpose(0, 1, 3, 2)
        return jnp.einsum('sihp,ij->sihjp', t, eye).reshape(SSM_SLABS, gs * hg, gs * p)

    bs = jnp.concatenate([b_slab(bb_re), b_slab(bb_im)], axis=2).astype(BF16)

    def c_slab(c):
        t = c.astype(F32).reshape(SSM_SLABS, gs, hg, p).transpose(0, 1, 3, 2)
        return jnp.einsum('siph,ij->sipjh', t, eye).reshape(SSM_SLABS, gs * p, gs * hg)

    cs = jnp.concatenate([c_slab(c_re), -c_slab(c_im)], axis=1).astype(BF16)

    a1 = (ab_re.reshape(1, g * p), ab_im.reshape(1, g * p))
    a2 = _cmul(*a1, *a1)
    a4 = _cmul(*a2, *a2)
    rowid = jnp.arange(SUBLANES)[:, None]
    mtab = jnp.stack([jnp.where(rowid >= s, comp, 0.0)
                      for s, a in ((1, a1), (2, a2), (4, a4)) for comp in a])
    pows = [a1]
    for _ in range(SUBLANES - 1):
        pows.append(_cmul(*pows[-1], *a1))
    ptab = jnp.stack([jnp.concatenate([q[0] for q in pows], axis=0),
                      jnp.concatenate([q[1] for q in pows], axis=0)])
    return (bs, cs, d.astype(F32).reshape(1, -1), glu_w.astype(BF16),
            glu_b.astype(F32).reshape(1, -1), mtab, ptab)


def _sgu_tables(w_s, b_s, seq_len):
    n_heads = w_s.shape[0]
    ln = min(SGU_CHUNK, seq_len)
    assert SGU_CHUNK % ln == 0 and seq_len % ln == 0
    rep = SGU_CHUNK // ln
    mask = jnp.tril(jnp.ones((ln, ln), dtype=bool))
    w = jnp.where(mask[None], w_s[:, :ln, :ln], 0.0).astype(F32)
    ws = jnp.einsum('hij,ab->haibj', w, jnp.eye(rep, dtype=F32)).reshape(n_heads, SGU_CHUNK, SGU_CHUNK)
    bias = jnp.tile(jnp.transpose(b_s[:, :ln]).astype(F32), (rep, 1))
    return ws.astype(BF16), bias


def _pack_kernel(u_ref, v_ref, o_ref):
    half = u_ref.shape[1] // 2

    def bf16_bits(x):
        return lax.bitcast_convert_type(x.astype(BF16).astype(F32), U32)

    def pack(t_ref):
        w = bf16_bits(t_ref[:, :half]) | (bf16_bits(t_ref[:, half:]) >> 16)
        return lax.bitcast_convert_type(w, I32)

    o_ref[:, :half] = pack(u_ref)
    o_ref[:, half:] = pack(v_ref)


def _pack_tables(u, v):
    n, dd = u.shape
    tr = _pick(n, (PACK_ROWS, 256, 128, 8))
    spec = pl.BlockSpec((tr, dd), lambda i: (i, 0))
    return pl.pallas_call(
        _pack_kernel,
        grid=(n // tr,),
        in_specs=[spec, spec],
        out_specs=spec,
        out_shape=jax.ShapeDtypeStruct((n, dd), I32),
        compiler_params=pltpu.CompilerParams(
            dimension_semantics=("arbitrary",), vmem_limit_bytes=VMEM_LIMIT),
        name="pack",
    )(u, v)


def _trunk_to_route(x, h0r, h0i, seq_len, lp):
    (norm_mix_g, w_in_bf, s5_tabs, ln_g, ln_b, sgu_w, sgu_b, wa, wb, wo, norm_ffn_g, wq, keys_bf, _,
     d_ssm, d_sgu) = lp
    u, su, vn, gates = _inproj(x, norm_mix_g, w_in_bf, ln_g, ln_b, d_ssm, d_sgu)
    ya, hlr, hli = _s5(u, h0r, h0i, s5_tabs, seq_len)
    ws, bias = _sgu_tables(sgu_w, sgu_b, seq_len)
    hd = d_sgu // ws.shape[0]
    bias_tile = jnp.repeat(bias, hd, axis=1)
    x1 = _mix(x, ya, su, vn, gates, ws, bias_tile, wa, wb, wo)
    xn, eidx, gate = _route(x1, norm_ffn_g, wq, keys_bf)
    return (x1, xn, eidx, gate), hlr, hli, vn


def _trunk_layer(x, h0r, h0i, seq_len, lp, final_g):
    (x1, xn, eidx, gate), hlr, hli, vn = _trunk_to_route(x, h0r, h0i, seq_len, lp)
    y = _peer(eidx, gate, xn, x1, final_g, lp[13], final_norm=True)
    return y, hlr, hli, vn


def _prompt_split(x, seq_len, lp, final_g, n_state):
    rows = x.shape[0]
    nb = (int(rows * SC_SHARE) // SC_SPLIT_ALIGN) * SC_SPLIT_ALIGN
    zeros = jnp.zeros((1, n_state), F32)
    (x1b, xnb, eb, gb), hr_b, hi_b, _ = _trunk_to_route(x[:nb], zeros, zeros, seq_len, lp)
    (x1a, xna, ea, ga), hr, hi, _ = _trunk_to_route(x[nb:], hr_b, hi_b, seq_len, lp)
    ya = _peer(ea, ga, xna, x1a, final_g, lp[13], final_norm=True)
    return (x1b, xnb, eb, gb), ya, hr, hi


def _prompt_finish(first, ya, lp, final_g):
    x1b, xnb, eb, gb = first
    yb = _finish(x1b, _sc_peer(lp[13], eb, gb, xnb), final_g)
    return jnp.concatenate([yb, ya], axis=0)


def kernel(x_prompt, x_sample, state_ssm_re, state_ssm_im, norm_mix_g, w_in, ssm_lambda_re, ssm_lambda_im, ssm_log_dt, ssm_b_re, ssm_b_im, ssm_c_re, ssm_c_im, ssm_d, ssm_glu_w, ssm_glu_b, sgu_ln_g, sgu_ln_b, sgu_w, sgu_b, w_branch_a, w_branch_b, w_out, norm_ffn_g, peer_w_q, peer_keys, peer_u, peer_v, norm_final_g):
    depth = w_in.shape[0]
    assert depth == 1, "the final norm is fused into the last layer's PEER kernel"
    bp, sp, dm = x_prompt.shape
    bs_, ss, _ = x_sample.shape
    g, p = ssm_lambda_re.shape[1:]
    d_ssm = ssm_d.shape[1]
    d_sgu = sgu_ln_g.shape[1]
    n_state = g * p
    l = 0
    lp = (norm_mix_g[l].reshape(1, dm), w_in[l].astype(BF16),
          _s5_tables(ssm_lambda_re[l], ssm_lambda_im[l], ssm_log_dt[l], ssm_b_re[l], ssm_b_im[l],
                     ssm_c_re[l], ssm_c_im[l], ssm_d[l], ssm_glu_w[l], ssm_glu_b[l]),
          sgu_ln_g[l].reshape(1, d_sgu), sgu_ln_b[l].reshape(1, d_sgu), sgu_w[l], sgu_b[l],
          w_branch_a[l].astype(BF16), w_branch_b[l].astype(BF16), w_out[l].astype(BF16),
          norm_ffn_g[l].reshape(1, dm), peer_w_q[l].astype(BF16), peer_keys[l].astype(BF16),
          _pack_tables(peer_u[l].astype(F32), peer_v[l].astype(F32)),
          d_ssm, d_sgu)
    gf = norm_final_g.reshape(1, dm)

    xp2 = x_prompt.reshape(bp * sp, dm)
    split = bp == 1 and sp >= 4 * SC_SPLIT_ALIGN
    if split:
        first, ya, hr_p, hi_p = _prompt_split(xp2, sp, lp, gf, n_state)
    else:
        zeros = jnp.zeros((bp, n_state), F32)
        yp, hr_p, hi_p, _ = _trunk_layer(xp2, zeros, zeros, sp, lp, gf)
    ys, hr_s, hi_s, v_s = _trunk_layer(x_sample.reshape(bs_ * ss, dm),
                                       state_ssm_re[l].astype(F32).reshape(bs_, n_state),
                                       state_ssm_im[l].astype(F32).reshape(bs_, n_state), ss, lp, gf)
    if split:
        yp = _prompt_finish(first, ya, lp, gf)
    return (yp.reshape(bp, sp, dm), ys.reshape(bs_, ss, dm),
            hr_p.reshape(1, bp, g, p), hi_p.reshape(1, bp, g, p),
            hr_s.reshape(1, bs_, g, p), hi_s.reshape(1, bs_, g, p),
            v_s.reshape(1, bs_, ss, d_sgu))
```

```python
import functools
import math

import jax
import jax.numpy as jnp
from jax import lax
from jax.experimental import pallas as pl
from jax.experimental.pallas import tpu as pltpu
from jax.experimental.pallas import tpu_sc as plsc

F32 = jnp.float32
BF16 = jnp.bfloat16
I32 = jnp.int32
U32 = jnp.uint32
EPS = 1e-6

LANES = 128
SUBLANES = 8
VMEM_LIMIT = 56 * 1024 * 1024

SSM_SLABS = 4
SCAN_LANES = 512
PEER_TOPK = 16
SGU_CHUNK = 128
ROUTE_LANES = 256
INPROJ_SUB = 128
PACK_ROWS = 512


def _pick(n, candidates):
    for c in candidates:
        if n % c == 0:
            return c
    raise ValueError(f"no block size for {n} in {candidates}")


def _const_spec(shape):
    nd = len(shape)
    return pl.BlockSpec(shape, lambda *_: (0,) * nd, pipeline_mode=pl.Buffered(1))


def _rms(x, g):
    return x * lax.rsqrt(jnp.mean(x * x, axis=-1, keepdims=True) + EPS) * g


def _inproj_kernel(x_ref, g_ref, w_ref, lng_ref, lnb_ref, u_ref, su_ref, vn_ref, gates_ref, xn_sc):
    j = pl.program_id(1)
    tm = x_ref.shape[0]
    sub = min(tm, INPROJ_SUB)

    @pl.when(j == 0)
    def _():
        xn_sc[...] = _rms(x_ref[...], g_ref[...]).astype(BF16)

    def section(out_ref, epilogue):
        for r in range(tm // sub):
            rows = slice(r * sub, (r + 1) * sub)
            z = jnp.dot(xn_sc[rows, :], w_ref[...], preferred_element_type=F32)
            out_ref[rows, :] = epilogue(z).astype(out_ref.dtype)

    def layer_norm_gelu(z):
        v = jax.nn.gelu(z)
        mu = jnp.mean(v, axis=-1, keepdims=True)
        vc = v - mu
        y = vc * lax.rsqrt(jnp.mean(vc * vc, axis=-1, keepdims=True) + EPS)
        return y * lng_ref[...] + lnb_ref[...]

    @pl.when(j == 0)
    def _():
        section(u_ref, lambda z: z)

    @pl.when(j == 1)
    def _():
        section(su_ref, jax.nn.gelu)

    @pl.when(j == 2)
    def _():
        section(vn_ref, layer_norm_gelu)

    @pl.when(j >= 3)
    def _():
        section(gates_ref, jax.nn.sigmoid)


def _inproj(x, g, w_in_bf, ln_g, ln_b, d_ssm, d_sgu):
    rows, dm = x.shape
    d_in = w_in_bf.shape[1]
    tn = d_ssm
    assert d_sgu == tn and (d_in - 3 * tn) % tn == 0
    nj = d_in // tn
    tm = _pick(rows, (512, 256, 128))
    return pl.pallas_call(
        _inproj_kernel,
        grid=(rows // tm, nj),
        in_specs=[
            pl.BlockSpec((tm, dm), lambda i, j: (i, 0)),
            pl.BlockSpec((1, dm), lambda i, j: (0, 0)),
            pl.BlockSpec((dm, tn), lambda i, j: (0, j)),
            pl.BlockSpec((1, tn), lambda i, j: (0, 0)),
            pl.BlockSpec((1, tn), lambda i, j: (0, 0)),
        ],
        out_specs=[
            pl.BlockSpec((tm, tn), lambda i, j: (i, 0)),
            pl.BlockSpec((tm, tn), lambda i, j: (i, 0)),
            pl.BlockSpec((tm, tn), lambda i, j: (i, 0)),
            pl.BlockSpec((tm, tn), lambda i, j: (i, jnp.maximum(j - 3, 0))),
        ],
        out_shape=[
            jax.ShapeDtypeStruct((rows, tn), F32),
            jax.ShapeDtypeStruct((rows, tn), BF16),
            jax.ShapeDtypeStruct((rows, tn), F32),
            jax.ShapeDtypeStruct((rows, d_in - 3 * tn), BF16),
        ],
        scratch_shapes=[pltpu.VMEM((tm, dm), BF16)],
        compiler_params=pltpu.CompilerParams(
            dimension_semantics=("arbitrary", "arbitrary"), vmem_limit_bytes=VMEM_LIMIT),
        name="inproj",
    )(x, g, w_in_bf, ln_g, ln_b)


def _s5_kernel(u_ref, h0r_ref, h0i_ref, bs_ref, cs_ref, d_ref, gluw_ref, glub_ref, m_ref, pw_ref,
               ya_ref, hlr_ref, hli_ref, hr_sc, hi_sc, cr_sc, ci_sc, *, seq_len):
    i = pl.program_id(0)
    tl, d_ssm = u_ref.shape
    n_state = hr_sc.shape[1]
    kin = d_ssm // SSM_SLABS
    kst = n_state // SSM_SLABS

    @pl.when(i == 0)
    def _():
        cr_sc[...] = jnp.zeros_like(cr_sc)
        ci_sc[...] = jnp.zeros_like(ci_sc)

    ub = u_ref[...].astype(BF16)
    for s in range(SSM_SLABS):
        r = jnp.dot(ub[:, kin * s:kin * (s + 1)], bs_ref[s], preferred_element_type=F32)
        hr_sc[:, kst * s:kst * (s + 1)] = r[:, :kst]
        hi_sc[:, kst * s:kst * (s + 1)] = r[:, kst:]

    n_tiles = tl // SUBLANES
    for lc in range(n_state // SCAN_LANES):
        ls = slice(lc * SCAN_LANES, (lc + 1) * SCAN_LANES)

        def tile_body(t, carry, ls=ls):
            cr, ci = carry
            row0 = i * tl + t * SUBLANES
            b = row0 // seq_len
            is_start = (row0 % seq_len) == 0
            h0r = jnp.broadcast_to(h0r_ref[pl.ds(b, 1), ls], (SUBLANES, SCAN_LANES))
            h0i = jnp.broadcast_to(h0i_ref[pl.ds(b, 1), ls], (SUBLANES, SCAN_LANES))
            cr = jnp.where(is_start, h0r, cr)
            ci = jnp.where(is_start, h0i, ci)
            r0 = pl.multiple_of(t * SUBLANES, SUBLANES)
            br = hr_sc[pl.ds(r0, SUBLANES), ls]
            bi = hi_sc[pl.ds(r0, SUBLANES), ls]
            for k, shift in enumerate((1, 2, 4)):
                mr = m_ref[2 * k, :, ls]
                mi = m_ref[2 * k + 1, :, ls]
                sr = pltpu.roll(br, shift, 0)
                si = pltpu.roll(bi, shift, 0)
                br, bi = br + (mr * sr - mi * si), bi + (mr * si + mi * sr)
            pr = pw_ref[0, :, ls]
            pi_ = pw_ref[1, :, ls]
            hr = br + (pr * cr - pi_ * ci)
            hi = bi + (pr * ci + pi_ * cr)
            hr_sc[pl.ds(r0, SUBLANES), ls] = hr
            hi_sc[pl.ds(r0, SUBLANES), ls] = hi
            last_r = hr[SUBLANES - 1:SUBLANES, :]
            last_i = hi[SUBLANES - 1:SUBLANES, :]
            hlr_ref[pl.ds(b, 1), ls] = last_r
            hli_ref[pl.ds(b, 1), ls] = last_i
            return (jnp.broadcast_to(last_r, (SUBLANES, SCAN_LANES)),
                    jnp.broadcast_to(last_i, (SUBLANES, SCAN_LANES)))

        cr, ci = lax.fori_loop(0, n_tiles, tile_body, (cr_sc[:, ls], ci_sc[:, ls]))
        cr_sc[:, ls] = cr
        ci_sc[:, ls] = ci

    ys = []
    for s in range(SSM_SLABS):
        hre = hr_sc[:, kst * s:kst * (s + 1)].astype(BF16)
        him = hi_sc[:, kst * s:kst * (s + 1)].astype(BF16)
        ys.append(jnp.dot(hre, cs_ref[s, :kst, :], preferred_element_type=F32)
                  + jnp.dot(him, cs_ref[s, kst:, :], preferred_element_type=F32))
    y = jnp.concatenate(ys, axis=1)
    y = jax.nn.gelu(y + d_ref[...] * u_ref[...])
    gt = jnp.dot(y.astype(BF16), gluw_ref[...], preferred_element_type=F32) + glub_ref[...]
    ya_ref[...] = (y * jax.nn.sigmoid(gt)).astype(BF16)


def _s5(u, h0r, h0i, tabs, seq_len):
    rows, d_ssm = u.shape
    nb, n_state = h0r.shape
    tl = _pick(rows, (256, 128, 64, 32, 16, 8)) if seq_len >= 256 else _pick(rows, (512, 256, 128, 64, 32, 16))
    assert (seq_len % tl == 0 or tl % seq_len == 0) and seq_len % SUBLANES == 0
    bs, cs, d, gluw, glub, mtab, ptab = tabs
    return pl.pallas_call(
        functools.partial(_s5_kernel, seq_len=seq_len),
        grid=(rows // tl,),
        in_specs=[
            pl.BlockSpec((tl, d_ssm), lambda i: (i, 0)),
            _const_spec((nb, n_state)),
            _const_spec((nb, n_state)),
            _const_spec(bs.shape),
            _const_spec(cs.shape),
            _const_spec(d.shape),
            _const_spec(gluw.shape),
            _const_spec(glub.shape),
            _const_spec(mtab.shape),
            _const_spec(ptab.shape),
        ],
        out_specs=[
            pl.BlockSpec((tl, d_ssm), lambda i: (i, 0)),
            pl.BlockSpec((nb, n_state), lambda i: (0, 0)),
            pl.BlockSpec((nb, n_state), lambda i: (0, 0)),
        ],
        out_shape=[
            jax.ShapeDtypeStruct((rows, d_ssm), BF16),
            jax.ShapeDtypeStruct((nb, n_state), F32),
            jax.ShapeDtypeStruct((nb, n_state), F32),
        ],
        scratch_shapes=[
            pltpu.VMEM((tl, n_state), F32),
            pltpu.VMEM((tl, n_state), F32),
            pltpu.VMEM((SUBLANES, n_state), F32),
            pltpu.VMEM((SUBLANES, n_state), F32),
        ],
        compiler_params=pltpu.CompilerParams(
            dimension_semantics=("arbitrary",), vmem_limit_bytes=VMEM_LIMIT),
        name="s5",
    )(u, h0r, h0i, bs, cs, d, gluw, glub, mtab, ptab)


def _mix_kernel(x_ref, ya_ref, su_ref, vn_ref, gates_ref, ws_ref, bias_ref, wa_ref, wb_ref, wo_ref, x1_ref):
    tm, dm = x_ref.shape
    n_heads = ws_ref.shape[0]
    hd = vn_ref.shape[1] // n_heads
    vb = vn_ref[...].astype(BF16)
    chunks = []
    for c in range(tm // SGU_CHUNK):
        heads = [jnp.dot(ws_ref[h], vb[c * SGU_CHUNK:(c + 1) * SGU_CHUNK, h * hd:(h + 1) * hd],
                         preferred_element_type=F32) for h in range(n_heads)]
        chunks.append(jnp.concatenate(heads, axis=1) + bias_ref[...])
    s_all = jnp.concatenate(chunks, axis=0)
    yb = (su_ref[...].astype(F32) * s_all).astype(BF16)
    pa = jnp.dot(ya_ref[...], wa_ref[...], preferred_element_type=F32)
    pb = jnp.dot(yb, wb_ref[...], preferred_element_type=F32)
    merged = gates_ref[:, :dm].astype(F32) * pa + gates_ref[:, dm:].astype(F32) * pb
    x1_ref[...] = x_ref[...] + jnp.dot(merged.astype(BF16), wo_ref[...], preferred_element_type=F32)


def _mix(x, ya, su, vn, gates, ws, bias, wa, wb, wo):
    rows, dm = x.shape
    d_ssm = ya.shape[1]
    d_sgu = su.shape[1]
    tm = _pick(rows, (256, 128))
    row = lambda w: pl.BlockSpec((tm, w), lambda i: (i, 0))
    return pl.pallas_call(
        _mix_kernel,
        grid=(rows // tm,),
        in_specs=[row(dm), row(d_ssm), row(d_sgu), row(d_sgu), row(2 * dm),
                  _const_spec(ws.shape), _const_spec(bias.shape),
                  _const_spec(wa.shape), _const_spec(wb.shape), _const_spec(wo.shape)],
        out_specs=row(dm),
        out_shape=jax.ShapeDtypeStruct((rows, dm), F32),
        compiler_params=pltpu.CompilerParams(
            dimension_semantics=("arbitrary",), vmem_limit_bytes=VMEM_LIMIT),
        name="mix",
    )(x, ya, su, vn, gates, ws, bias, wa, wb, wo)


def _topk_rows(s, k, payload=None):
    rows = s.shape[0]
    iota = lax.broadcasted_iota(I32, s.shape, 0).astype(F32)
    vals, outs = [], []
    for _ in range(k):
        m = jnp.max(s, axis=0, keepdims=True)
        j = jnp.min(jnp.where(s == m, iota, float(rows)), axis=0, keepdims=True)
        sel = iota == j
        vals.append(m)
        outs.append(j if payload is None else jnp.max(jnp.where(sel, payload, -1.0), axis=0, keepdims=True))
        s = jnp.where(sel, -jnp.inf, s)
    return jnp.concatenate(vals, axis=0), jnp.concatenate(outs, axis=0)


def _route_kernel(x1_ref, g_ref, wq_ref, keys_ref, xn_ref, eidx_ref, gate_ref, q_sc, e_sc, p_sc):
    tb = x1_ref.shape[0]
    tw = e_sc.shape[2]
    nsub = tb // tw
    n_heads, _, n_keys, half = keys_ref.shape
    dkey = 2 * half
    xn = _rms(x1_ref[...], g_ref[...])
    xn_ref[...] = xn
    q = jnp.dot(xn.astype(BF16), wq_ref[...], preferred_element_type=F32).astype(BF16)
    for h in range(n_heads):
        q_sc[h] = q[:, h * dkey:(h + 1) * dkey]

    nt = (((1,), (1,)), ((), ()))

    def body(n, _):
        h = n // nsub
        r0 = pl.multiple_of((n % nsub) * tw, tw)
        qh = q_sc[h, pl.ds(r0, tw), :]
        s1 = lax.dot_general(keys_ref[h, 0], qh[:, :half], nt, preferred_element_type=F32)
        s2 = lax.dot_general(keys_ref[h, 1], qh[:, half:], nt, preferred_element_type=F32)
        v1, i1 = _topk_rows(s1, PEER_TOPK)
        v2, i2 = _topk_rows(s2, PEER_TOPK)
        cv, ci = [], []
        for a in range(PEER_TOPK):
            nb = PEER_TOPK // (a + 1)
            cv.append(v1[a:a + 1, :] + v2[:nb, :])
            ci.append(i1[a:a + 1, :] * float(n_keys) + i2[:nb, :])
        ts, te = _topk_rows(jnp.concatenate(cv, axis=0), PEER_TOPK, payload=jnp.concatenate(ci, axis=0))
        ex = jnp.exp(ts - ts[0:1, :])
        e_sc[n] = te
        p_sc[n] = ex / jnp.sum(ex, axis=0, keepdims=True)
        return 0

    lax.fori_loop(0, n_heads * nsub, body, 0)
    for c in range(nsub):
        rows = slice(c * tw, (c + 1) * tw)
        e_all = jnp.concatenate([e_sc[h * nsub + c] for h in range(n_heads)], axis=0)
        p_all = jnp.concatenate([p_sc[h * nsub + c] for h in range(n_heads)], axis=0)
        eidx_ref[rows, :] = e_all.T.astype(I32)
        gate_ref[rows, :] = p_all.T


def _route(x1, g, wq, keys_bf):
    rows, dm = x1.shape
    n_heads, _, n_keys, half = keys_bf.shape
    nsel = n_heads * PEER_TOPK
    tb = _pick(rows, (256, 128))
    tw = min(tb, ROUTE_LANES)
    row = lambda w: pl.BlockSpec((tb, w), lambda i: (i, 0))
    return pl.pallas_call(
        _route_kernel,
        grid=(rows // tb,),
        in_specs=[row(dm), _const_spec(g.shape), _const_spec(wq.shape), _const_spec(keys_bf.shape)],
        out_specs=[row(dm), row(nsel), row(nsel)],
        out_shape=[
            jax.ShapeDtypeStruct((rows, dm), F32),
            jax.ShapeDtypeStruct((rows, nsel), I32),
            jax.ShapeDtypeStruct((rows, nsel), F32),
        ],
        scratch_shapes=[
            pltpu.VMEM((n_heads, tb, 2 * half), BF16),
            pltpu.VMEM((n_heads * (tb // tw), PEER_TOPK, tw), F32),
            pltpu.VMEM((n_heads * (tb // tw), PEER_TOPK, tw), F32),
        ],
        compiler_params=pltpu.CompilerParams(
            dimension_semantics=("arbitrary",), vmem_limit_bytes=VMEM_LIMIT),
        name="route",
    )(x1, g, wq, keys_bf)


PEER_BUFS = 4


def _peer_kernel(eidx_hbm, gate_ref, xn_ref, x1_ref, gf_ref, tab_hbm, y_ref,
                 idx_sm, idx_sem, b0, b1, b2, b3, sem, peer_sc, *, final_norm):
    i = pl.program_id(0)
    bufs = (b0, b1, b2, b3)
    tbk, nsel = gate_ref.shape
    half = tab_hbm.shape[1] // 2
    nlt = half // LANES

    icp = pltpu.make_async_copy(eidx_hbm.at[i], idx_sm, idx_sem)
    icp.start()
    icp.wait()

    def issue(t, s):
        base = t * nsel
        for k in range(nsel):
            e = idx_sm[base + k]
            pltpu.make_async_copy(tab_hbm.at[pl.ds(e, 1), :], bufs[s].at[pl.ds(k, 1), :],
                                  sem.at[s]).start(priority=k % 2)

    def wait(s):
        pltpu.make_async_copy(tab_hbm.at[pl.ds(0, nsel), :], bufs[s], sem.at[s]).wait()

    hi_mask = jnp.int32(-65536)
    eye = (lax.broadcasted_iota(I32, (nsel, nsel), 0) == lax.broadcasted_iota(I32, (nsel, nsel), 1))

    def unpack(w):
        return (lax.bitcast_convert_type(w & hi_mask, F32),
                lax.bitcast_convert_type(w << 16, F32))

    def compute(t, s):
        buf = bufs[s]
        xrow = xn_ref[pl.ds(t, 1), :]
        acc = jnp.zeros((nsel, LANES), F32)
        for j in range(nlt):
            uh, ul = unpack(buf[:, j * LANES:(j + 1) * LANES])
            acc = acc + uh * xrow[:, j * LANES:(j + 1) * LANES] \
                      + ul * xrow[:, half + j * LANES:half + (j + 1) * LANES]
        act = jnp.sum(acc, axis=1, keepdims=True)
        grow = jnp.broadcast_to(gate_ref[pl.ds(t, 1), :], (nsel, nsel))
        gcol = jnp.sum(jnp.where(eye, grow, 0.0), axis=1, keepdims=True)
        wgt = jnp.broadcast_to(gcol * jax.nn.gelu(act), (nsel, LANES))
        t8 = pl.multiple_of((t // SUBLANES) * SUBLANES, SUBLANES)
        rsel = lax.broadcasted_iota(I32, (SUBLANES, LANES), 0) == (t % SUBLANES)

        def put(lane0, row):
            cur = peer_sc[pl.ds(t8, SUBLANES), lane0:lane0 + LANES]
            peer_sc[pl.ds(t8, SUBLANES), lane0:lane0 + LANES] = jnp.where(
                rsel, jnp.broadcast_to(row, (SUBLANES, LANES)), cur)

        for j in range(nlt):
            vh, vl = unpack(buf[:, half + j * LANES:half + (j + 1) * LANES])
            put(j * LANES, jnp.sum(vh * wgt, axis=0, keepdims=True))
            put(half + j * LANES, jnp.sum(vl * wgt, axis=0, keepdims=True))

    def step(t, s, prefetch):
        wait(s)
        if prefetch:
            issue(t + PEER_BUFS - 1, (s + PEER_BUFS - 1) % PEER_BUFS)
        compute(t, s)

    peer_sc[...] = jnp.zeros_like(peer_sc)
    for t in range(PEER_BUFS - 1):
        issue(t, t)

    def group(gi, _):
        for s in range(PEER_BUFS):
            step(gi * PEER_BUFS + s, s, True)
        return 0

    lax.fori_loop(0, tbk // PEER_BUFS - 1, group, 0)
    for s in range(PEER_BUFS):
        t = tbk - PEER_BUFS + s
        step(t, s, t + PEER_BUFS - 1 < tbk)

    xo = x1_ref[...] + peer_sc[...]
    y_ref[...] = _rms(xo, gf_ref[...]) if final_norm else xo


def _peer(eidx, gate, xn, x1, gf, table, final_norm):
    rows, dm = x1.shape
    nsel = gate.shape[1]
    tbk = _pick(rows, (256, 128, 64, 32, 16, 8))
    assert tbk % PEER_BUFS == 0 and tbk >= 2 * PEER_BUFS
    nblk = rows // tbk
    eidx_blk = eidx.reshape(nblk, tbk * nsel)
    row = lambda w: pl.BlockSpec((tbk, w), lambda i: (i, 0))
    return pl.pallas_call(
        functools.partial(_peer_kernel, final_norm=final_norm),
        grid=(nblk,),
        in_specs=[pl.BlockSpec(memory_space=pl.ANY), row(nsel), row(dm), row(dm),
                  _const_spec(gf.shape), pl.BlockSpec(memory_space=pl.ANY)],
        out_specs=row(dm),
        out_shape=jax.ShapeDtypeStruct((rows, dm), F32),
        scratch_shapes=[
            pltpu.SMEM((tbk * nsel,), I32),
            pltpu.SemaphoreType.DMA(()),
        ] + [pltpu.VMEM((nsel, table.shape[1]), table.dtype) for _ in range(PEER_BUFS)] + [
            pltpu.SemaphoreType.DMA((PEER_BUFS,)),
            pltpu.VMEM((tbk, dm), F32),
        ],
        compiler_params=pltpu.CompilerParams(
            dimension_semantics=("arbitrary",), vmem_limit_bytes=VMEM_LIMIT),
        name="peer",
    )(eidx_blk, gate, xn, x1, gf, table)


SC_L = 16
SC_WINDOW = SC_L
SC_SHARE = 0.46875
SC_SPLIT_ALIGN = 512


def _sc_peer(table, eidx, gate, xn):
    ntok, nsel = eidx.shape
    d = table.shape[1]
    half = d // 2
    nch = half // SC_L
    info = plsc.get_sparse_core_info()
    n_workers = info.num_cores * info.num_subcores
    n_win = nsel // SC_WINDOW
    tpw = ntok // n_workers
    assert tpw * n_workers == ntok and SC_WINDOW == SC_L and info.num_lanes == SC_L
    mesh = plsc.VectorSubcoreMesh(core_axis_name="core", subcore_axis_name="subcore")
    c0 = math.sqrt(2.0 / math.pi)

    @functools.partial(
        pl.kernel, out_type=jax.ShapeDtypeStruct((ntok * d,), F32), mesh=mesh,
        scratch_types=[pltpu.VMEM((nsel,), I32), pltpu.VMEM((nsel,), F32), pltpu.VMEM((d,), F32),
                       pltpu.VMEM((2, SC_WINDOW, d), I32), pltpu.VMEM((d,), F32),
                       pltpu.SemaphoreType.DMA((2,))],
        compiler_params=pltpu.CompilerParams(needs_layout_passes=False),
        name="sc_peer")
    def k(tab_hbm, idx_hbm, gate_hbm, x_hbm, o_hbm, idx_v, gate_v, x_v, rows_v, out_v, gsem):
        wid = lax.axis_index("subcore") * info.num_cores + lax.axis_index("core")
        lane = lax.iota(I32, SC_L)
        zero = jnp.zeros((SC_L,), F32)

        def unpack(w):
            return (lax.bitcast_convert_type(w & jnp.int32(-65536), F32),
                    lax.bitcast_convert_type(w << 16, F32))

        @pl.loop(0, tpw)
        def _(tt):
            tok = wid * tpw + tt
            pltpu.sync_copy(idx_hbm.at[pl.ds(tok * nsel, nsel)], idx_v)
            pltpu.sync_copy(gate_hbm.at[pl.ds(tok * nsel, nsel)], gate_v)
            pltpu.sync_copy(x_hbm.at[pl.ds(tok * d, d)], x_v)

            @pl.loop(0, d // SC_L)
            def _(c):
                out_v[pl.ds(c * SC_L, SC_L)] = zero

            def gather(w):
                return pltpu.make_async_copy(tab_hbm.at[idx_v.at[pl.ds(w * SC_WINDOW, SC_WINDOW)]],
                                             rows_v.at[w % 2], gsem.at[w % 2])

            gather(0).start()
            for w in range(n_win):
                gather(w).wait()
                if w + 1 < n_win:
                    gather(w + 1).start()
                slot = w % 2

                def dot_chunk(j, accs, slot=slot):
                    xa = x_v[pl.ds(j * SC_L, SC_L)]
                    xb = x_v[pl.ds(half + j * SC_L, SC_L)]
                    out = []
                    for r in range(SC_WINDOW):
                        uh, ul = unpack(rows_v[slot, r, pl.ds(j * SC_L, SC_L)])
                        out.append(accs[r] + (uh * xa + ul * xb))
                    return tuple(out)

                accs = lax.fori_loop(0, nch, dot_chunk, (zero,) * SC_WINDOW)
                act = zero
                for r in range(SC_WINDOW):
                    act = jnp.where(lane == r, jnp.sum(accs[r]), act)
                z = c0 * (act + 0.044715 * (act * act * act))
                tanh_z = 1.0 - 2.0 / (jnp.exp(2.0 * z) + 1.0)
                wgt = gate_v[pl.ds(w * SC_WINDOW, SC_WINDOW)] * (0.5 * act * (1.0 + tanh_z))
                splat = [jnp.full((SC_L,), jnp.sum(jnp.where(lane == r, wgt, 0.0)), F32)
                         for r in range(SC_WINDOW)]

                @pl.loop(0, nch)
                def _(j, slot=slot, splat=splat):
                    oh = zero
                    ol = zero
                    for r in range(SC_WINDOW):
                        vh, vl = unpack(rows_v[slot, r, pl.ds(half + j * SC_L, SC_L)])
                        oh = oh + splat[r] * vh
                        ol = ol + splat[r] * vl
                    out_v[pl.ds(j * SC_L, SC_L)] += oh
                    out_v[pl.ds(half + j * SC_L, SC_L)] += ol

            pltpu.sync_copy(out_v, o_hbm.at[pl.ds(tok * d, d)])

    return k(table, eidx.reshape(-1), gate.reshape(-1), xn.reshape(-1)).reshape(ntok, d)


def _finish_kernel(x1_ref, p_ref, gf_ref, y_ref):
    y_ref[...] = _rms(x1_ref[...] + p_ref[...], gf_ref[...])


def _finish(x1, p, gf):
    rows, dm = x1.shape
    tm = _pick(rows, (512, 256, 128))
    row = pl.BlockSpec((tm, dm), lambda i: (i, 0))
    return pl.pallas_call(
        _finish_kernel, grid=(rows // tm,), in_specs=[row, row, _const_spec(gf.shape)], out_specs=row,
        out_shape=jax.ShapeDtypeStruct((rows, dm), F32),
        compiler_params=pltpu.CompilerParams(dimension_semantics=("arbitrary",), vmem_limit_bytes=VMEM_LIMIT),
        name="finish",
    )(x1, p, gf)


def _cmul(ar, ai, br, bi):
    return ar * br - ai * bi, ar * bi + ai * br


def _s5_tables(lam_re, lam_im, log_dt, b_re, b_im, c_re, c_im, d, glu_w, glu_b):
    g, p = lam_re.shape
    hg = b_re.shape[-1]
    gs = g // SSM_SLABS
    dt = jnp.exp(log_dt.astype(F32))[:, None]
    lr = lam_re.astype(F32)
    li = lam_im.astype(F32)
    mag = jnp.exp(lr * dt)
    ab_re = mag * jnp.cos(li * dt)
    ab_im = mag * jnp.sin(li * dt)
    den = lr * lr + li * li
    nr = ab_re - 1.0
    ni = ab_im
    coef_re = (nr * lr + ni * li) / den
    coef_im = (ni * lr - nr * li) / den
    br = b_re.astype(F32)
    bi = b_im.astype(F32)
    bb_re = coef_re[..., None] * br - coef_im[..., None] * bi
    bb_im = coef_re[..., None] * bi + coef_im[..., None] * br
    eye = jnp.eye(gs, dtype=F32)

    def b_slab(bb):
        t = bb.reshape(SSM_SLABS, gs, p, hg).transpose(0, 1, 3, 2)
        return jnp.einsum('sihp,ij->sihjp', t, eye).reshape(SSM_SLABS, gs * hg, gs * p)

    bs = jnp.concatenate([b_slab(bb_re), b_slab(bb_im)], axis=2).astype(BF16)

    def c_slab(c):
        t = c.astype(F32).reshape(SSM_SLABS, gs, hg, p).transpose(0, 1, 3, 2)
        return jnp.einsum('siph,ij->sipjh', t, eye).reshape(SSM_SLABS, gs * p, gs * hg)

    cs = jnp.concatenate([c_slab(c_re), -c_slab(c_im)], axis=1).astype(BF16)

    a1 = (ab_re.reshape(1, g * p), ab_im.reshape(1, g * p))
    a2 = _cmul(*a1, *a1)
    a4 = _cmul(*a2, *a2)
    rowid = jnp.arange(SUBLANES)[:, None]
    mtab = jnp.stack([jnp.where(rowid >= s, comp, 0.0)
                      for s, a in ((1, a1), (2, a2), (4, a4)) for comp in a])
    pows = [a1]
    for _ in range(SUBLANES - 1):
        pows.append(_cmul(*pows[-1], *a1))
    ptab = jnp.stack([jnp.concatenate([q[0] for q in pows], axis=0),
                      jnp.concatenate([q[1] for q in pows], axis=0)])
    return (bs, cs, d.astype(F32).reshape(1, -1), glu_w.astype(BF16),
            glu_b.astype(F32).reshape(1, -1), mtab, ptab)


def _sgu_tables(w_s, b_s, seq_len):
    n_heads = w_s.shape[0]
    ln = min(SGU_CHUNK, seq_len)
    assert SGU_CHUNK % ln == 0 and seq_len % ln == 0
    rep = SGU_CHUNK // ln
    mask = jnp.tril(jnp.ones((ln, ln), dtype=bool))
    w = jnp.where(mask[None], w_s[:, :ln, :ln], 0.0).astype(F32)
    ws = jnp.einsum('hij,ab->haibj', w, jnp.eye(rep, dtype=F32)).reshape(n_heads, SGU_CHUNK, SGU_CHUNK)
    bias = jnp.tile(jnp.transpose(b_s[:, :ln]).astype(F32), (rep, 1))
    return ws.astype(BF16), bias


def _pack_kernel(u_ref, v_ref, o_ref):
    half = u_ref.shape[1] // 2

    def bf16_bits(x):
        return lax.bitcast_convert_type(x.astype(BF16).astype(F32), U32)

    def pack(t_ref):
        w = bf16_bits(t_ref[:, :half]) | (bf16_bits(t_ref[:, half:]) >> 16)
        return lax.bitcast_convert_type(w, I32)

    o_ref[:, :half] = pack(u_ref)
    o_ref[:, half:] = pack(v_ref)


def _pack_tables(u, v):
    n, dd = u.shape
    tr = _pick(n, (PACK_ROWS, 256, 128, 8))
    spec = pl.BlockSpec((tr, dd), lambda i: (i, 0))
    return pl.pallas_call(
        _pack_kernel,
        grid=(n // tr,),
        in_specs=[spec, spec],
        out_specs=spec,
        out_shape=jax.ShapeDtypeStruct((n, dd), I32),
        compiler_params=pltpu.CompilerParams(
            dimension_semantics=("arbitrary",), vmem_limit_bytes=VMEM_LIMIT),
        name="pack",
    )(u, v)


def _trunk_to_route(x, h0r, h0i, seq_len, lp):
    (norm_mix_g, w_in_bf, s5_tabs, ln_g, ln_b, sgu_w, sgu_b, wa, wb, wo, norm_ffn_g, wq, keys_bf, _,
     d_ssm, d_sgu) = lp
    u, su, vn, gates = _inproj(x, norm_mix_g, w_in_bf, ln_g, ln_b, d_ssm, d_sgu)
    ya, hlr, hli = _s5(u, h0r, h0i, s5_tabs, seq_len)
    ws, bias = _sgu_tables(sgu_w, sgu_b, seq_len)
    hd = d_sgu // ws.shape[0]
    bias_tile = jnp.repeat(bias, hd, axis=1)
    x1 = _mix(x, ya, su, vn, gates, ws, bias_tile, wa, wb, wo)
    xn, eidx, gate = _route(x1, norm_ffn_g, wq, keys_bf)
    return (x1, xn, eidx, gate), hlr, hli, vn


def _trunk_layer(x, h0r, h0i, seq_len, lp, final_g):
    (x1, xn, eidx, gate), hlr, hli, vn = _trunk_to_route(x, h0r, h0i, seq_len, lp)
    y = _peer(eidx, gate, xn, x1, final_g, lp[13], final_norm=True)
    return y, hlr, hli, vn


def _prompt_split(x, seq_len, lp, final_g, n_state):
    rows = x.shape[0]
    nb = (int(rows * SC_SHARE) // SC_SPLIT_ALIGN) * SC_SPLIT_ALIGN
    zeros = jnp.zeros((1, n_state), F32)
    (x1b, xnb, eb, gb), hr_b, hi_b, _ = _trunk_to_route(x[:nb], zeros, zeros, seq_len, lp)
    (x1a, xna, ea, ga), hr, hi, _ = _trunk_to_route(x[nb:], hr_b, hi_b, seq_len, lp)
    ya = _peer(ea, ga, xna, x1a, final_g, lp[13], final_norm=True)
    return (x1b, xnb, eb, gb), ya, hr, hi


def _prompt_finish(first, ya, lp, final_g):
    x1b, xnb, eb, gb = first
    yb = _finish(x1b, _sc_peer(lp[13], eb, gb, xnb), final_g)
    return jnp.concatenate([yb, ya], axis=0)


def kernel(x_prompt, x_sample, state_ssm_re, state_ssm_im, norm_mix_g, w_in, ssm_lambda_re, ssm_lambda_im, ssm_log_dt, ssm_b_re, ssm_b_im, ssm_c_re, ssm_c_im, ssm_d, ssm_glu_w, ssm_glu_b, sgu_ln_g, sgu_ln_b, sgu_w, sgu_b, w_branch_a, w_branch_b, w_out, norm_ffn_g, peer_w_q, peer_keys, peer_u, peer_v, norm_final_g):
    depth = w_in.shape[0]
    assert depth == 1, "the final norm is fused into the last layer's PEER kernel"
    bp, sp, dm = x_prompt.shape
    bs_, ss, _ = x_sample.shape
    g, p = ssm_lambda_re.shape[1:]
    d_ssm = ssm_d.shape[1]
    d_sgu = sgu_ln_g.shape[1]
    n_state = g * p
    l = 0
    lp = (norm_mix_g[l].reshape(1, dm), w_in[l].astype(BF16),
          _s5_tables(ssm_lambda_re[l], ssm_lambda_im[l], ssm_log_dt[l], ssm_b_re[l], ssm_b_im[l],
                     ssm_c_re[l], ssm_c_im[l], ssm_d[l], ssm_glu_w[l], ssm_glu_b[l]),
          sgu_ln_g[l].reshape(1, d_sgu), sgu_ln_b[l].reshape(1, d_sgu), sgu_w[l], sgu_b[l],
          w_branch_a[l].astype(BF16), w_branch_b[l].astype(BF16), w_out[l].astype(BF16),
          norm_ffn_g[l].reshape(1, dm), peer_w_q[l].astype(BF16), peer_keys[l].astype(BF16),
          _pack_tables(peer_u[l].astype(F32), peer_v[l].astype(F32)),
          d_ssm, d_sgu)
    gf = norm_final_g.reshape(1, dm)

    xp2 = x_prompt.reshape(bp * sp, dm)
    split = bp == 1 and sp >= 4 * SC_SPLIT_ALIGN
    if split:
        first, ya, hr_p, hi_p = _prompt_split(xp2, sp, lp, gf, n_state)
    else:
        zeros = jnp.zeros((bp, n_state), F32)
        yp, hr_p, hi_p, _ = _trunk_layer(xp2, zeros, zeros, sp, lp, gf)
    ys, hr_s, hi_s, v_s = _trunk_layer(x_sample.reshape(bs_ * ss, dm),
                                       state_ssm_re[l].astype(F32).reshape(bs_, n_state),
                                       state_ssm_im[l].astype(F32).reshape(bs_, n_state), ss, lp, gf)
    if split:
        yp = _prompt_finish(first, ya, lp, gf)
    return (yp.reshape(bp, sp, dm), ys.reshape(bs_, ss, dm),
            hr_p.reshape(1, bp, g, p), hi_p.reshape(1, bp, g, p),
            hr_s.reshape(1, bs_, g, p), hi_s.reshape(1, bs_, g, p),
            v_s.reshape(1, bs_, ss, d_sgu))
```

```python
import functools
import math

import jax
import jax.numpy as jnp
from jax import lax
from jax.experimental import pallas as pl
from jax.experimental.pallas import tpu as pltpu
from jax.experimental.pallas import tpu_sc as plsc

F32 = jnp.float32
BF16 = jnp.bfloat16
I32 = jnp.int32
U32 = jnp.uint32
EPS = 1e-6

LANES = 128
SUBLANES = 8
VMEM_LIMIT = 56 * 1024 * 1024

SSM_SLABS = 4
SCAN_LANES = 512
PEER_TOPK = 16
SGU_CHUNK = 128
ROUTE_LANES = 256
INPROJ_SUB = 128
PACK_ROWS = 512


def _pick(n, candidates):
    for c in candidates:
        if n % c == 0:
            return c
    raise ValueError(f"no block size for {n} in {candidates}")


def _const_spec(shape):
    nd = len(shape)
    return pl.BlockSpec(shape, lambda *_: (0,) * nd, pipeline_mode=pl.Buffered(1))


def _rms(x, g):
    return x * lax.rsqrt(jnp.mean(x * x, axis=-1, keepdims=True) + EPS) * g


def _inproj_kernel(x_ref, g_ref, w_ref, lng_ref, lnb_ref, u_ref, su_ref, vn_ref, gates_ref, xn_sc):
    j = pl.program_id(1)
    tm = x_ref.shape[0]
    sub = min(tm, INPROJ_SUB)

    @pl.when(j == 0)
    def _():
        xn_sc[...] = _rms(x_ref[...], g_ref[...]).astype(BF16)

    def section(out_ref, epilogue):
        for r in range(tm // sub):
            rows = slice(r * sub, (r + 1) * sub)
            z = jnp.dot(xn_sc[rows, :], w_ref[...], preferred_element_type=F32)
            out_ref[rows, :] = epilogue(z).astype(out_ref.dtype)

    def layer_norm_gelu(z):
        v = jax.nn.gelu(z)
        mu = jnp.mean(v, axis=-1, keepdims=True)
        vc = v - mu
        y = vc * lax.rsqrt(jnp.mean(vc * vc, axis=-1, keepdims=True) + EPS)
        return y * lng_ref[...] + lnb_ref[...]

    @pl.when(j == 0)
    def _():
        section(u_ref, lambda z: z)

    @pl.when(j == 1)
    def _():
        section(su_ref, jax.nn.gelu)

    @pl.when(j == 2)
    def _():
        section(vn_ref, layer_norm_gelu)

    @pl.when(j >= 3)
    def _():
        section(gates_ref, jax.nn.sigmoid)


def _inproj(x, g, w_in_bf, ln_g, ln_b, d_ssm, d_sgu):
    rows, dm = x.shape
    d_in = w_in_bf.shape[1]
    tn = d_ssm
    assert d_sgu == tn and (d_in - 3 * tn) % tn == 0
    nj = d_in // tn
    tm = _pick(rows, (512, 256, 128))
    return pl.pallas_call(
        _inproj_kernel,
        grid=(rows // tm, nj),
        in_specs=[
            pl.BlockSpec((tm, dm), lambda i, j: (i, 0)),
            pl.BlockSpec((1, dm), lambda i, j: (0, 0)),
            pl.BlockSpec((dm, tn), lambda i, j: (0, j)),
            pl.BlockSpec((1, tn), lambda i, j: (0, 0)),
            pl.BlockSpec((1, tn), lambda i, j: (0, 0)),
        ],
        out_specs=[
            pl.BlockSpec((tm, tn), lambda i, j: (i, 0)),
            pl.BlockSpec((tm, tn), lambda i, j: (i, 0)),
            pl.BlockSpec((tm, tn), lambda i, j: (i, 0)),
            pl.BlockSpec((tm, tn), lambda i, j: (i, jnp.maximum(j - 3, 0))),
        ],
        out_shape=[
            jax.ShapeDtypeStruct((rows, tn), F32),
            jax.ShapeDtypeStruct((rows, tn), BF16),
            jax.ShapeDtypeStruct((rows, tn), F32),
            jax.ShapeDtypeStruct((rows, d_in - 3 * tn), BF16),
        ],
        scratch_shapes=[pltpu.VMEM((tm, dm), BF16)],
        compiler_params=pltpu.CompilerParams(
            dimension_semantics=("arbitrary", "arbitrary"), vmem_limit_bytes=VMEM_LIMIT),
        name="inproj",
    )(x, g, w_in_bf, ln_g, ln_b)


def _s5_kernel(u_ref, h0r_ref, h0i_ref, bs_ref, cs_ref, d_ref, gluw_ref, glub_ref, m_ref, pw_ref,
               ya_ref, hlr_ref, hli_ref, hr_sc, hi_sc, cr_sc, ci_sc, *, seq_len):
    i = pl.program_id(0)
    tl, d_ssm = u_ref.shape
    n_state = hr_sc.shape[1]
    kin = d_ssm // SSM_SLABS
    kst = n_state // SSM_SLABS

    @pl.when(i == 0)
    def _():
        cr_sc[...] = jnp.zeros_like(cr_sc)
        ci_sc[...] = jnp.zeros_like(ci_sc)

    ub = u_ref[...].astype(BF16)
    for s in range(SSM_SLABS):
        r = jnp.dot(ub[:, kin * s:kin * (s + 1)], bs_ref[s], preferred_element_type=F32)
        hr_sc[:, kst * s:kst * (s + 1)] = r[:, :kst]
        hi_sc[:, kst * s:kst * (s + 1)] = r[:, kst:]

    n_tiles = tl // SUBLANES
    for lc in range(n_state // SCAN_LANES):
        ls = slice(lc * SCAN_LANES, (lc + 1) * SCAN_LANES)

        def tile_body(t, carry, ls=ls):
            cr, ci = carry
            row0 = i * tl + t * SUBLANES
            b = row0 // seq_len
            is_start = (row0 % seq_len) == 0
            h0r = jnp.broadcast_to(h0r_ref[pl.ds(b, 1), ls], (SUBLANES, SCAN_LANES))
            h0i = jnp.broadcast_to(h0i_ref[pl.ds(b, 1), ls], (SUBLANES, SCAN_LANES))
            cr = jnp.where(is_start, h0r, cr)
            ci = jnp.where(is_start, h0i, ci)
            r0 = pl.multiple_of(t * SUBLANES, SUBLANES)
            br = hr_sc[pl.ds(r0, SUBLANES), ls]
            bi = hi_sc[pl.ds(r0, SUBLANES), ls]
            for k, shift in enumerate((1, 2, 4)):
                mr = m_ref[2 * k, :, ls]
                mi = m_ref[2 * k + 1, :, ls]
                sr = pltpu.roll(br, shift, 0)
                si = pltpu.roll(bi, shift, 0)
                br, bi = br + (mr * sr - mi * si), bi + (mr * si + mi * sr)
            pr = pw_ref[0, :, ls]
            pi_ = pw_ref[1, :, ls]
            hr = br + (pr * cr - pi_ * ci)
            hi = bi + (pr * ci + pi_ * cr)
            hr_sc[pl.ds(r0, SUBLANES), ls] = hr
            hi_sc[pl.ds(r0, SUBLANES), ls] = hi
            last_r = hr[SUBLANES - 1:SUBLANES, :]
            last_i = hi[SUBLANES - 1:SUBLANES, :]
            hlr_ref[pl.ds(b, 1), ls] = last_r
            hli_ref[pl.ds(b, 1), ls] = last_i
            return (jnp.broadcast_to(last_r, (SUBLANES, SCAN_LANES)),
                    jnp.broadcast_to(last_i, (SUBLANES, SCAN_LANES)))

        cr, ci = lax.fori_loop(0, n_tiles, tile_body, (cr_sc[:, ls], ci_sc[:, ls]))
        cr_sc[:, ls] = cr
        ci_sc[:, ls] = ci

    ys = []
    for s in range(SSM_SLABS):
        hre = hr_sc[:, kst * s:kst * (s + 1)].astype(BF16)
        him = hi_sc[:, kst * s:kst * (s + 1)].astype(BF16)
        ys.append(jnp.dot(hre, cs_ref[s, :kst, :], preferred_element_type=F32)
                  + jnp.dot(him, cs_ref[s, kst:, :], preferred_element_type=F32))
    y = jnp.concatenate(ys, axis=1)
    y = jax.nn.gelu(y + d_ref[...] * u_ref[...])
    gt = jnp.dot(y.astype(BF16), gluw_ref[...], preferred_element_type=F32) + glub_ref[...]
    ya_ref[...] = (y * jax.nn.sigmoid(gt)).astype(BF16)


def _s5(u, h0r, h0i, tabs, seq_len):
    rows, d_ssm = u.shape
    nb, n_state = h0r.shape
    tl = _pick(rows, (256, 128, 64, 32, 16, 8)) if seq_len >= 256 else _pick(rows, (512, 256, 128, 64, 32, 16))
    assert (seq_len % tl == 0 or tl % seq_len == 0) and seq_len % SUBLANES == 0
    bs, cs, d, gluw, glub, mtab, ptab = tabs
    return pl.pallas_call(
        functools.partial(_s5_kernel, seq_len=seq_len),
        grid=(rows // tl,),
        in_specs=[
            pl.BlockSpec((tl, d_ssm), lambda i: (i, 0)),
            _const_spec((nb, n_state)),
            _const_spec((nb, n_state)),
            _const_spec(bs.shape),
            _const_spec(cs.shape),
            _const_spec(d.shape),
            _const_spec(gluw.shape),
            _const_spec(glub.shape),
            _const_spec(mtab.shape),
            _const_spec(ptab.shape),
        ],
        out_specs=[
            pl.BlockSpec((tl, d_ssm), lambda i: (i, 0)),
            pl.BlockSpec((nb, n_state), lambda i: (0, 0)),
            pl.BlockSpec((nb, n_state), lambda i: (0, 0)),
        ],
        out_shape=[
            jax.ShapeDtypeStruct((rows, d_ssm), BF16),
            jax.ShapeDtypeStruct((nb, n_state), F32),
            jax.ShapeDtypeStruct((nb, n_state), F32),
        ],
        scratch_shapes=[
            pltpu.VMEM((tl, n_state), F32),
            pltpu.VMEM((tl, n_state), F32),
            pltpu.VMEM((SUBLANES, n_state), F32),
            pltpu.VMEM((SUBLANES, n_state), F32),
        ],
        compiler_params=pltpu.CompilerParams(
            dimension_semantics=("arbitrary",), vmem_limit_bytes=VMEM_LIMIT),
        name="s5",
    )(u, h0r, h0i, bs, cs, d, gluw, glub, mtab, ptab)


def _mix_kernel(x_ref, ya_ref, su_ref, vn_ref, gates_ref, ws_ref, bias_ref, wa_ref, wb_ref, wo_ref, x1_ref):
    tm, dm = x_ref.shape
    n_heads = ws_ref.shape[0]
    hd = vn_ref.shape[1] // n_heads
    vb = vn_ref[...].astype(BF16)
    chunks = []
    for c in range(tm // SGU_CHUNK):
        heads = [jnp.dot(ws_ref[h], vb[c * SGU_CHUNK:(c + 1) * SGU_CHUNK, h * hd:(h + 1) * hd],
                         preferred_element_type=F32) for h in range(n_heads)]
        chunks.append(jnp.concatenate(heads, axis=1) + bias_ref[...])
    s_all = jnp.concatenate(chunks, axis=0)
    yb = (su_ref[...].astype(F32) * s_all).astype(BF16)
    pa = jnp.dot(ya_ref[...], wa_ref[...], preferred_element_type=F32)
    pb = jnp.dot(yb, wb_ref[...], preferred_element_type=F32)
    merged = gates_ref[:, :dm].astype(F32) * pa + gates_ref[:, dm:].astype(F32) * pb
    x1_ref[...] = x_ref[...] + jnp.dot(merged.astype(BF16), wo_ref[...], preferred_element_type=F32)


def _mix(x, ya, su, vn, gates, ws, bias, wa, wb, wo):
    rows, dm = x.shape
    d_ssm = ya.shape[1]
    d_sgu = su.shape[1]
    tm = _pick(rows, (256, 128))
    row = lambda w: pl.BlockSpec((tm, w), lambda i: (i, 0))
    return pl.pallas_call(
        _mix_kernel,
        grid=(rows // tm,),
        in_specs=[row(dm), row(d_ssm), row(d_sgu), row(d_sgu), row(2 * dm),
                  _const_spec(ws.shape), _const_spec(bias.shape),
                  _const_spec(wa.shape), _const_spec(wb.shape), _const_spec(wo.shape)],
        out_specs=row(dm),
        out_shape=jax.ShapeDtypeStruct((rows, dm), F32),
        compiler_params=pltpu.CompilerParams(
            dimension_semantics=("arbitrary",), vmem_limit_bytes=VMEM_LIMIT),
        name="mix",
    )(x, ya, su, vn, gates, ws, bias, wa, wb, wo)


def _topk_rows(s, k, payload=None):
    rows = s.shape[0]
    iota = lax.broadcasted_iota(I32, s.shape, 0).astype(F32)
    vals, outs = [], []
    for _ in range(k):
        m = jnp.max(s, axis=0, keepdims=True)
        j = jnp.min(jnp.where(s == m, iota, float(rows)), axis=0, keepdims=True)
        sel = iota == j
        vals.append(m)
        outs.append(j if payload is None else jnp.max(jnp.where(sel, payload, -1.0), axis=0, keepdims=True))
        s = jnp.where(sel, -jnp.inf, s)
    return jnp.concatenate(vals, axis=0), jnp.concatenate(outs, axis=0)


def _route_kernel(x1_ref, g_ref, wq_ref, keys_ref, xn_ref, eidx_ref, gate_ref, q_sc, e_sc, p_sc):
    tb = x1_ref.shape[0]
    tw = e_sc.shape[2]
    nsub = tb // tw
    n_heads, _, n_keys, half = keys_ref.shape
    dkey = 2 * half
    xn = _rms(x1_ref[...], g_ref[...])
    xn_ref[...] = xn
    q = jnp.dot(xn.astype(BF16), wq_ref[...], preferred_element_type=F32).astype(BF16)
    for h in range(n_heads):
        q_sc[h] = q[:, h * dkey:(h + 1) * dkey]

    nt = (((1,), (1,)), ((), ()))

    def body(n, _):
        h = n // nsub
        r0 = pl.multiple_of((n % nsub) * tw, tw)
        qh = q_sc[h, pl.ds(r0, tw), :]
        s1 = lax.dot_general(keys_ref[h, 0], qh[:, :half], nt, preferred_element_type=F32)
        s2 = lax.dot_general(keys_ref[h, 1], qh[:, half:], nt, preferred_element_type=F32)
        v1, i1 = _topk_rows(s1, PEER_TOPK)
        v2, i2 = _topk_rows(s2, PEER_TOPK)
        cv, ci = [], []
        for a in range(PEER_TOPK):
            nb = PEER_TOPK // (a + 1)
            cv.append(v1[a:a + 1, :] + v2[:nb, :])
            ci.append(i1[a:a + 1, :] * float(n_keys) + i2[:nb, :])
        ts, te = _topk_rows(jnp.concatenate(cv, axis=0), PEER_TOPK, payload=jnp.concatenate(ci, axis=0))
        ex = jnp.exp(ts - ts[0:1, :])
        e_sc[n] = te
        p_sc[n] = ex / jnp.sum(ex, axis=0, keepdims=True)
        return 0

    lax.fori_loop(0, n_heads * nsub, body, 0)
    for c in range(nsub):
        rows = slice(c * tw, (c + 1) * tw)
        e_all = jnp.concatenate([e_sc[h * nsub + c] for h in range(n_heads)], axis=0)
        p_all = jnp.concatenate([p_sc[h * nsub + c] for h in range(n_heads)], axis=0)
        eidx_ref[rows, :] = e_all.T.astype(I32)
        gate_ref[rows, :] = p_all.T


def _route(x1, g, wq, keys_bf):
    rows, dm = x1.shape
    n_heads, _, n_keys, half = keys_bf.shape
    nsel = n_heads * PEER_TOPK
    tb = _pick(rows, (256, 128))
    tw = min(tb, ROUTE_LANES)
    row = lambda w: pl.BlockSpec((tb, w), lambda i: (i, 0))
    return pl.pallas_call(
        _route_kernel,
        grid=(rows // tb,),
        in_specs=[row(dm), _const_spec(g.shape), _const_spec(wq.shape), _const_spec(keys_bf.shape)],
        out_specs=[row(dm), row(nsel), row(nsel)],
        out_shape=[
            jax.ShapeDtypeStruct((rows, dm), F32),
            jax.ShapeDtypeStruct((rows, nsel), I32),
            jax.ShapeDtypeStruct((rows, nsel), F32),
        ],
        scratch_shapes=[
            pltpu.VMEM((n_heads, tb, 2 * half), BF16),
            pltpu.VMEM((n_heads * (tb // tw), PEER_TOPK, tw), F32),
            pltpu.VMEM((n_heads * (tb // tw), PEER_TOPK, tw), F32),
        ],
        compiler_params=pltpu.CompilerParams(
            dimension_semantics=("arbitrary",), vmem_limit_bytes=VMEM_LIMIT),
        name="route",
    )(x1, g, wq, keys_bf)


PEER_BUFS = 4


def _peer_kernel(eidx_hbm, gate_ref, xn_ref, x1_ref, gf_ref, tab_hbm, y_ref,
                 idx_sm, idx_sem, b0, b1, b2, b3, sem, peer_sc, *, final_norm):
    i = pl.program_id(0)
    bufs = (b0, b1, b2, b3)
    tbk, nsel = gate_ref.shape
    half = tab_hbm.shape[1] // 2
    nlt = half // LANES

    icp = pltpu.make_async_copy(eidx_hbm.at[i], idx_sm, idx_sem)
    icp.start()
    icp.wait()

    def issue(t, s):
        base = t * nsel
        for k in range(nsel):
            e = idx_sm[base + k]
            pltpu.make_async_copy(tab_hbm.at[pl.ds(e, 1), :], bufs[s].at[pl.ds(k, 1), :],
                                  sem.at[s]).start(priority=k % 2)

    def wait(s):
        pltpu.make_async_copy(tab_hbm.at[pl.ds(0, nsel), :], bufs[s], sem.at[s]).wait()

    hi_mask = jnp.int32(-65536)
    eye = (lax.broadcasted_iota(I32, (nsel, nsel), 0) == lax.broadcasted_iota(I32, (nsel, nsel), 1))

    def unpack(w):
        return (lax.bitcast_convert_type(w & hi_mask, F32),
                lax.bitcast_convert_type(w << 16, F32))

    def compute(t, s):
        buf = bufs[s]
        xrow = xn_ref[pl.ds(t, 1), :]
        acc = jnp.zeros((nsel, LANES), F32)
        for j in range(nlt):
            uh, ul = unpack(buf[:, j * LANES:(j + 1) * LANES])
            acc = acc + uh * xrow[:, j * LANES:(j + 1) * LANES] \
                      + ul * xrow[:, half + j * LANES:half + (j + 1) * LANES]
        act = jnp.sum(acc, axis=1, keepdims=True)
        grow = jnp.broadcast_to(gate_ref[pl.ds(t, 1), :], (nsel, nsel))
        gcol = jnp.sum(jnp.where(eye, grow, 0.0), axis=1, keepdims=True)
        wgt = jnp.broadcast_to(gcol * jax.nn.gelu(act), (nsel, LANES))
        t8 = pl.multiple_of((t // SUBLANES) * SUBLANES, SUBLANES)
        rsel = lax.broadcasted_iota(I32, (SUBLANES, LANES), 0) == (t % SUBLANES)

        def put(lane0, row):
            cur = peer_sc[pl.ds(t8, SUBLANES), lane0:lane0 + LANES]
            peer_sc[pl.ds(t8, SUBLANES), lane0:lane0 + LANES] = jnp.where(
                rsel, jnp.broadcast_to(row, (SUBLANES, LANES)), cur)

        for j in range(nlt):
            vh, vl = unpack(buf[:, half + j * LANES:half + (j + 1) * LANES])
            put(j * LANES, jnp.sum(vh * wgt, axis=0, keepdims=True))
            put(half + j * LANES, jnp.sum(vl * wgt, axis=0, keepdims=True))

    def step(t, s, prefetch):
        wait(s)
        if prefetch:
            issue(t + PEER_BUFS - 1, (s + PEER_BUFS - 1) % PEER_BUFS)
        compute(t, s)

    peer_sc[...] = jnp.zeros_like(peer_sc)
    for t in range(PEER_BUFS - 1):
        issue(t, t)

    def group(gi, _):
        for s in range(PEER_BUFS):
            step(gi * PEER_BUFS + s, s, True)
        return 0

    lax.fori_loop(0, tbk // PEER_BUFS - 1, group, 0)
    for s in range(PEER_BUFS):
        t = tbk - PEER_BUFS + s
        step(t, s, t + PEER_BUFS - 1 < tbk)

    xo = x1_ref[...] + peer_sc[...]
    y_ref[...] = _rms(xo, gf_ref[...]) if final_norm else xo


def _peer(eidx, gate, xn, x1, gf, table, final_norm):
    rows, dm = x1.shape
    nsel = gate.shape[1]
    tbk = _pick(rows, (256, 128, 64, 32, 16, 8))
    assert tbk % PEER_BUFS == 0 and tbk >= 2 * PEER_BUFS
    nblk = rows // tbk
    eidx_blk = eidx.reshape(nblk, tbk * nsel)
    row = lambda w: pl.BlockSpec((tbk, w), lambda i: (i, 0))
    return pl.pallas_call(
        functools.partial(_peer_kernel, final_norm=final_norm),
        grid=(nblk,),
        in_specs=[pl.BlockSpec(memory_space=pl.ANY), row(nsel), row(dm), row(dm),
                  _const_spec(gf.shape), pl.BlockSpec(memory_space=pl.ANY)],
        out_specs=row(dm),
        out_shape=jax.ShapeDtypeStruct((rows, dm), F32),
        scratch_shapes=[
            pltpu.SMEM((tbk * nsel,), I32),
            pltpu.SemaphoreType.DMA(()),
        ] + [pltpu.VMEM((nsel, table.shape[1]), table.dtype) for _ in range(PEER_BUFS)] + [
            pltpu.SemaphoreType.DMA((PEER_BUFS,)),
            pltpu.VMEM((tbk, dm), F32),
        ],
        compiler_params=pltpu.CompilerParams(
            dimension_semantics=("arbitrary",), vmem_limit_bytes=VMEM_LIMIT),
        name="peer",
    )(eidx_blk, gate, xn, x1, gf, table)


SC_L = 16
SC_WINDOW = SC_L
SC_CUTS = (0.125, 0.53125)
SC_SPLIT_ALIGN = 512


def _sc_peer(table, eidx, gate, xn):
    ntok, nsel = eidx.shape
    d = table.shape[1]
    half = d // 2
    nch = half // SC_L
    info = plsc.get_sparse_core_info()
    n_workers = info.num_cores * info.num_subcores
    n_win = nsel // SC_WINDOW
    tpw = ntok // n_workers
    assert tpw * n_workers == ntok and SC_WINDOW == SC_L and info.num_lanes == SC_L
    mesh = plsc.VectorSubcoreMesh(core_axis_name="core", subcore_axis_name="subcore")
    c0 = math.sqrt(2.0 / math.pi)

    @functools.partial(
        pl.kernel, out_type=jax.ShapeDtypeStruct((ntok, d), F32), mesh=mesh,
        scratch_types=[pltpu.VMEM((nsel,), I32), pltpu.VMEM((nsel,), F32), pltpu.VMEM((d,), F32),
                       pltpu.VMEM((2, SC_WINDOW, d), I32), pltpu.VMEM((d,), F32),
                       pltpu.SemaphoreType.DMA((2,))],
        compiler_params=pltpu.CompilerParams(needs_layout_passes=False),
        name="sc_peer")
    def k(tab_hbm, idx_hbm, gate_hbm, x_hbm, o_hbm, idx_v, gate_v, x_v, rows_v, out_v, gsem):
        wid = lax.axis_index("subcore") * info.num_cores + lax.axis_index("core")
        lane = lax.iota(I32, SC_L)
        zero = jnp.zeros((SC_L,), F32)

        def unpack(w):
            return (lax.bitcast_convert_type(w & jnp.int32(-65536), F32),
                    lax.bitcast_convert_type(w << 16, F32))

        @pl.loop(0, tpw)
        def _(tt):
            tok = wid * tpw + tt
            pltpu.sync_copy(idx_hbm.at[pl.ds(tok * nsel, nsel)], idx_v)
            pltpu.sync_copy(gate_hbm.at[pl.ds(tok * nsel, nsel)], gate_v)
            pltpu.sync_copy(x_hbm.at[tok], x_v)

            @pl.loop(0, d // SC_L)
            def _(c):
                out_v[pl.ds(c * SC_L, SC_L)] = zero

            def gather(w):
                return pltpu.make_async_copy(tab_hbm.at[idx_v.at[pl.ds(w * SC_WINDOW, SC_WINDOW)]],
                                             rows_v.at[w % 2], gsem.at[w % 2])

            gather(0).start()
            for w in range(n_win):
                gather(w).wait()
                if w + 1 < n_win:
                    gather(w + 1).start()
                slot = w % 2

                def dot_chunk(j, accs, slot=slot):
                    xa = x_v[pl.ds(j * SC_L, SC_L)]
                    xb = x_v[pl.ds(half + j * SC_L, SC_L)]
                    out = []
                    for r in range(SC_WINDOW):
                        uh, ul = unpack(rows_v[slot, r, pl.ds(j * SC_L, SC_L)])
                        out.append(accs[r] + (uh * xa + ul * xb))
                    return tuple(out)

                accs = lax.fori_loop(0, nch, dot_chunk, (zero,) * SC_WINDOW)
                act = zero
                for r in range(SC_WINDOW):
                    act = jnp.where(lane == r, jnp.sum(accs[r]), act)
                z = c0 * (act + 0.044715 * (act * act * act))
                tanh_z = 1.0 - 2.0 / (jnp.exp(2.0 * z) + 1.0)
                wgt = gate_v[pl.ds(w * SC_WINDOW, SC_WINDOW)] * (0.5 * act * (1.0 + tanh_z))
                splat = [jnp.full((SC_L,), jnp.sum(jnp.where(lane == r, wgt, 0.0)), F32)
                         for r in range(SC_WINDOW)]

                @pl.loop(0, nch)
                def _(j, slot=slot, splat=splat):
                    oh = zero
                    ol = zero
                    for r in range(SC_WINDOW):
                        vh, vl = unpack(rows_v[slot, r, pl.ds(half + j * SC_L, SC_L)])
                        oh = oh + splat[r] * vh
                        ol = ol + splat[r] * vl
                    out_v[pl.ds(j * SC_L, SC_L)] += oh
                    out_v[pl.ds(half + j * SC_L, SC_L)] += ol

            pltpu.sync_copy(out_v, o_hbm.at[tok])

    return k(table, eidx.reshape(-1), gate.reshape(-1), xn)


def _finish_kernel(x1_ref, p_ref, gf_ref, y_ref):
    y_ref[...] = _rms(x1_ref[...] + p_ref[...], gf_ref[...])


def _finish(x1, p, gf):
    rows, dm = x1.shape
    tm = _pick(rows, (512, 256, 128))
    row = pl.BlockSpec((tm, dm), lambda i: (i, 0))
    return pl.pallas_call(
        _finish_kernel, grid=(rows // tm,), in_specs=[row, row, _const_spec(gf.shape)], out_specs=row,
        out_shape=jax.ShapeDtypeStruct((rows, dm), F32),
        compiler_params=pltpu.CompilerParams(dimension_semantics=("arbitrary",), vmem_limit_bytes=VMEM_LIMIT),
        name="finish",
    )(x1, p, gf)


def _cmul(ar, ai, br, bi):
    return ar * br - ai * bi, ar * bi + ai * br


def _s5_tables(lam_re, lam_im, log_dt, b_re, b_im, c_re, c_im, d, glu_w, glu_b):
    g, p = lam_re.shape
    hg = b_re.shape[-1]
    gs = g // SSM_SLABS
    dt = jnp.exp(log_dt.astype(F32))[:, None]
    lr = lam_re.astype(F32)
    li = lam_im.astype(F32)
    mag = jnp.exp(lr * dt)
    ab_re = mag * jnp.cos(li * dt)
    ab_im = mag * jnp.sin(li * dt)
    den = lr * lr + li * li
    nr = ab_re - 1.0
    ni = ab_im
    coef_re = (nr * lr + ni * li) / den
    coef_im = (ni * lr - nr * li) / den
    br = b_re.astype(F32)
    bi = b_im.astype(F32)
    bb_re = coef_re[..., None] * br - coef_im[..., None] * bi
    bb_im = coef_re[..., None] * bi + coef_im[..., None] * br
    eye = jnp.eye(gs, dtype=F32)

    def b_slab(bb):
        t = bb.reshape(SSM_SLABS, gs, p, hg).transpose(0, 1, 3, 2)
        return jnp.einsum('sihp,ij->sihjp', t, eye).reshape(SSM_SLABS, gs * hg, gs * p)

    bs = jnp.concatenate([b_slab(bb_re), b_slab(bb_im)], axis=2).astype(BF16)

    def c_slab(c):
        t = c.astype(F32).reshape(SSM_SLABS, gs, hg, p).transpose(0, 1, 3, 2)
        return jnp.einsum('siph,ij->sipjh', t, eye).reshape(SSM_SLABS, gs * p, gs * hg)

    cs = jnp.concatenate([c_slab(c_re), -c_slab(c_im)], axis=1).astype(BF16)

    a1 = (ab_re.reshape(1, g * p), ab_im.reshape(1, g * p))
    a2 = _cmul(*a1, *a1)
    a4 = _cmul(*a2, *a2)
    rowid = jnp.arange(SUBLANES)[:, None]
    mtab = jnp.stack([jnp.where(rowid >= s, comp, 0.0)
                      for s, a in ((1, a1), (2, a2), (4, a4)) for comp in a])
    pows = [a1]
    for _ in range(SUBLANES - 1):
        pows.append(_cmul(*pows[-1], *a1))
    ptab = jnp.stack([jnp.concatenate([q[0] for q in pows], axis=0),
                      jnp.concatenate([q[1] for q in pows], axis=0)])
    return (bs, cs, d.astype(F32).reshape(1, -1), glu_w.astype(BF16),
            glu_b.astype(F32).reshape(1, -1), mtab, ptab)


def _sgu_tables(w_s, b_s, seq_len):
    n_heads = w_s.shape[0]
    ln = min(SGU_CHUNK, seq_len)
    assert SGU_CHUNK % ln == 0 and seq_len % ln == 0
    rep = SGU_CHUNK // ln
    mask = jnp.tril(jnp.ones((ln, ln), dtype=bool))
    w = jnp.where(mask[None], w_s[:, :ln, :ln], 0.0).astype(F32)
    ws = jnp.einsum('hij,ab->haibj', w, jnp.eye(rep, dtype=F32)).reshape(n_heads, SGU_CHUNK, SGU_CHUNK)
    bias = jnp.tile(jnp.transpose(b_s[:, :ln]).astype(F32), (rep, 1))
    return ws.astype(BF16), bias


def _pack_kernel(u_ref, v_ref, o_ref):
    half = u_ref.shape[1] // 2

    def bf16_bits(x):
        return lax.bitcast_convert_type(x.astype(BF16).astype(F32), U32)

    def pack(t_ref):
        w = bf16_bits(t_ref[:, :half]) | (bf16_bits(t_ref[:, half:]) >> 16)
        return lax.bitcast_convert_type(w, I32)

    o_ref[:, :half] = pack(u_ref)
    o_ref[:, half:] = pack(v_ref)


def _pack_tables(u, v):
    n, dd = u.shape
    tr = _pick(n, (PACK_ROWS, 256, 128, 8))
    spec = pl.BlockSpec((tr, dd), lambda i: (i, 0))
    return pl.pallas_call(
        _pack_kernel,
        grid=(n // tr,),
        in_specs=[spec, spec],
        out_specs=spec,
        out_shape=jax.ShapeDtypeStruct((n, dd), I32),
        compiler_params=pltpu.CompilerParams(
            dimension_semantics=("arbitrary",), vmem_limit_bytes=VMEM_LIMIT),
        name="pack",
    )(u, v)


def _trunk_to_route(x, h0r, h0i, seq_len, lp):
    (norm_mix_g, w_in_bf, s5_tabs, ln_g, ln_b, sgu_w, sgu_b, wa, wb, wo, norm_ffn_g, wq, keys_bf, _,
     d_ssm, d_sgu) = lp
    u, su, vn, gates = _inproj(x, norm_mix_g, w_in_bf, ln_g, ln_b, d_ssm, d_sgu)
    ya, hlr, hli = _s5(u, h0r, h0i, s5_tabs, seq_len)
    ws, bias = _sgu_tables(sgu_w, sgu_b, seq_len)
    hd = d_sgu // ws.shape[0]
    bias_tile = jnp.repeat(bias, hd, axis=1)
    x1 = _mix(x, ya, su, vn, gates, ws, bias_tile, wa, wb, wo)
    xn, eidx, gate = _route(x1, norm_ffn_g, wq, keys_bf)
    return (x1, xn, eidx, gate), hlr, hli, vn


def _trunk_layer(x, h0r, h0i, seq_len, lp, final_g):
    (x1, xn, eidx, gate), hlr, hli, vn = _trunk_to_route(x, h0r, h0i, seq_len, lp)
    y = _peer(eidx, gate, xn, x1, final_g, lp[13], final_norm=True)
    return y, hlr, hli, vn


def _prompt_split(x, seq_len, lp, final_g, n_state):
    rows = x.shape[0]
    cuts = [(int(rows * c) // SC_SPLIT_ALIGN) * SC_SPLIT_ALIGN for c in SC_CUTS]
    bounds = [0] + cuts + [rows]
    hr = hi = jnp.zeros((1, n_state), F32)
    parts = []
    for a, b in zip(bounds[:-1], bounds[1:]):
        part, hr, hi, _ = _trunk_to_route(x[a:b], hr, hi, seq_len, lp)
        parts.append(part)
    x1a, xna, ea, ga = parts[-1]
    ya = _peer(ea, ga, xna, x1a, final_g, lp[13], final_norm=True)
    return parts[:-1], ya, hr, hi


def _prompt_finish(firsts, ya, lp, final_g):
    ys = [_finish(x1, _sc_peer(lp[13], e, g, xn), final_g) for (x1, xn, e, g) in firsts]
    return jnp.concatenate(ys + [ya], axis=0)


def kernel(x_prompt, x_sample, state_ssm_re, state_ssm_im, norm_mix_g, w_in, ssm_lambda_re, ssm_lambda_im, ssm_log_dt, ssm_b_re, ssm_b_im, ssm_c_re, ssm_c_im, ssm_d, ssm_glu_w, ssm_glu_b, sgu_ln_g, sgu_ln_b, sgu_w, sgu_b, w_branch_a, w_branch_b, w_out, norm_ffn_g, peer_w_q, peer_keys, peer_u, peer_v, norm_final_g):
    depth = w_in.shape[0]
    assert depth == 1, "the final norm is fused into the last layer's PEER kernel"
    bp, sp, dm = x_prompt.shape
    bs_, ss, _ = x_sample.shape
    g, p = ssm_lambda_re.shape[1:]
    d_ssm = ssm_d.shape[1]
    d_sgu = sgu_ln_g.shape[1]
    n_state = g * p
    l = 0
    lp = (norm_mix_g[l].reshape(1, dm), w_in[l].astype(BF16),
          _s5_tables(ssm_lambda_re[l], ssm_lambda_im[l], ssm_log_dt[l], ssm_b_re[l], ssm_b_im[l],
                     ssm_c_re[l], ssm_c_im[l], ssm_d[l], ssm_glu_w[l], ssm_glu_b[l]),
          sgu_ln_g[l].reshape(1, d_sgu), sgu_ln_b[l].reshape(1, d_sgu), sgu_w[l], sgu_b[l],
          w_branch_a[l].astype(BF16), w_branch_b[l].astype(BF16), w_out[l].astype(BF16),
          norm_ffn_g[l].reshape(1, dm), peer_w_q[l].astype(BF16), peer_keys[l].astype(BF16),
          _pack_tables(peer_u[l].astype(F32), peer_v[l].astype(F32)),
          d_ssm, d_sgu)
    gf = norm_final_g.reshape(1, dm)

    xp2 = x_prompt.reshape(bp * sp, dm)
    split = bp == 1 and sp >= 4 * SC_SPLIT_ALIGN
    if split:
        first, ya, hr_p, hi_p = _prompt_split(xp2, sp, lp, gf, n_state)
    else:
        zeros = jnp.zeros((bp, n_state), F32)
        yp, hr_p, hi_p, _ = _trunk_layer(xp2, zeros, zeros, sp, lp, gf)
    ys, hr_s, hi_s, v_s = _trunk_layer(x_sample.reshape(bs_ * ss, dm),
                                       state_ssm_re[l].astype(F32).reshape(bs_, n_state),
                                       state_ssm_im[l].astype(F32).reshape(bs_, n_state), ss, lp, gf)
    if split:
        yp = _prompt_finish(first, ya, lp, gf)
    return (yp.reshape(bp, sp, dm), ys.reshape(bs_, ss, dm),
            hr_p.reshape(1, bp, g, p), hi_p.reshape(1, bp, g, p),
            hr_s.reshape(1, bs_, g, p), hi_s.reshape(1, bs_, g, p),
            v_s.reshape(1, bs_, ss, d_sgu))
```

```python
import functools
import math

import jax
import jax.numpy as jnp
from jax import lax
from jax.experimental import pallas as pl
from jax.experimental.pallas import tpu as pltpu
from jax.experimental.pallas import tpu_sc as plsc

F32 = jnp.float32
BF16 = jnp.bfloat16
I32 = jnp.int32
U32 = jnp.uint32
EPS = 1e-6

LANES = 128
SUBLANES = 8
VMEM_LIMIT = 56 * 1024 * 1024

SSM_SLABS = 4
SCAN_LANES = 512
PEER_TOPK = 16
SGU_CHUNK = 128
ROUTE_LANES = 256
INPROJ_SUB = 128
PACK_ROWS = 512


def _pick(n, candidates):
    for c in candidates:
        if n % c == 0:
            return c
    raise ValueError(f"no block size for {n} in {candidates}")


def _const_spec(shape):
    nd = len(shape)
    return pl.BlockSpec(shape, lambda *_: (0,) * nd, pipeline_mode=pl.Buffered(1))


def _rms(x, g):
    return x * lax.rsqrt(jnp.mean(x * x, axis=-1, keepdims=True) + EPS) * g


def _inproj_kernel(x_ref, g_ref, w_ref, lng_ref, lnb_ref, u_ref, su_ref, vn_ref, gates_ref, xn_sc):
    j = pl.program_id(1)
    tm = x_ref.shape[0]
    sub = min(tm, INPROJ_SUB)

    @pl.when(j == 0)
    def _():
        xn_sc[...] = _rms(x_ref[...], g_ref[...]).astype(BF16)

    def section(out_ref, epilogue):
        for r in range(tm // sub):
            rows = slice(r * sub, (r + 1) * sub)
            z = jnp.dot(xn_sc[rows, :], w_ref[...], preferred_element_type=F32)
            out_ref[rows, :] = epilogue(z).astype(out_ref.dtype)

    def layer_norm_gelu(z):
        v = jax.nn.gelu(z)
        mu = jnp.mean(v, axis=-1, keepdims=True)
        vc = v - mu
        y = vc * lax.rsqrt(jnp.mean(vc * vc, axis=-1, keepdims=True) + EPS)
        return y * lng_ref[...] + lnb_ref[...]

    @pl.when(j == 0)
    def _():
        section(u_ref, lambda z: z)

    @pl.when(j == 1)
    def _():
        section(su_ref, jax.nn.gelu)

    @pl.when(j == 2)
    def _():
        section(vn_ref, layer_norm_gelu)

    @pl.when(j >= 3)
    def _():
        section(gates_ref, jax.nn.sigmoid)


def _inproj(x, g, w_in_bf, ln_g, ln_b, d_ssm, d_sgu):
    rows, dm = x.shape
    d_in = w_in_bf.shape[1]
    tn = d_ssm
    assert d_sgu == tn and (d_in - 3 * tn) % tn == 0
    nj = d_in // tn
    tm = _pick(rows, (512, 256, 128))
    return pl.pallas_call(
        _inproj_kernel,
        grid=(rows // tm, nj),
        in_specs=[
            pl.BlockSpec((tm, dm), lambda i, j: (i, 0)),
            pl.BlockSpec((1, dm), lambda i, j: (0, 0)),
            pl.BlockSpec((dm, tn), lambda i, j: (0, j)),
            pl.BlockSpec((1, tn), lambda i, j: (0, 0)),
            pl.BlockSpec((1, tn), lambda i, j: (0, 0)),
        ],
        out_specs=[
            pl.BlockSpec((tm, tn), lambda i, j: (i, 0)),
            pl.BlockSpec((tm, tn), lambda i, j: (i, 0)),
            pl.BlockSpec((tm, tn), lambda i, j: (i, 0)),
            pl.BlockSpec((tm, tn), lambda i, j: (i, jnp.maximum(j - 3, 0))),
        ],
        out_shape=[
            jax.ShapeDtypeStruct((rows, tn), F32),
            jax.ShapeDtypeStruct((rows, tn), BF16),
            jax.ShapeDtypeStruct((rows, tn), F32),
            jax.ShapeDtypeStruct((rows, d_in - 3 * tn), BF16),
        ],
        scratch_shapes=[pltpu.VMEM((tm, dm), BF16)],
        compiler_params=pltpu.CompilerParams(
            dimension_semantics=("arbitrary", "arbitrary"), vmem_limit_bytes=VMEM_LIMIT),
        name="inproj",
    )(x, g, w_in_bf, ln_g, ln_b)


def _s5_kernel(u_ref, h0r_ref, h0i_ref, bs_ref, cs_ref, d_ref, gluw_ref, glub_ref, m_ref, pw_ref,
               ya_ref, hlr_ref, hli_ref, hr_sc, hi_sc, cr_sc, ci_sc, *, seq_len):
    i = pl.program_id(0)
    tl, d_ssm = u_ref.shape
    n_state = hr_sc.shape[1]
    kin = d_ssm // SSM_SLABS
    kst = n_state // SSM_SLABS

    @pl.when(i == 0)
    def _():
        cr_sc[...] = jnp.zeros_like(cr_sc)
        ci_sc[...] = jnp.zeros_like(ci_sc)

    ub = u_ref[...].astype(BF16)
    for s in range(SSM_SLABS):
        r = jnp.dot(ub[:, kin * s:kin * (s + 1)], bs_ref[s], preferred_element_type=F32)
        hr_sc[:, kst * s:kst * (s + 1)] = r[:, :kst]
        hi_sc[:, kst * s:kst * (s + 1)] = r[:, kst:]

    n_tiles = tl // SUBLANES
    for lc in range(n_state // SCAN_LANES):
        ls = slice(lc * SCAN_LANES, (lc + 1) * SCAN_LANES)

        def tile_body(t, carry, ls=ls):
            cr, ci = carry
            row0 = i * tl + t * SUBLANES
            b = row0 // seq_len
            is_start = (row0 % seq_len) == 0
            h0r = jnp.broadcast_to(h0r_ref[pl.ds(b, 1), ls], (SUBLANES, SCAN_LANES))
            h0i = jnp.broadcast_to(h0i_ref[pl.ds(b, 1), ls], (SUBLANES, SCAN_LANES))
            cr = jnp.where(is_start, h0r, cr)
            ci = jnp.where(is_start, h0i, ci)
            r0 = pl.multiple_of(t * SUBLANES, SUBLANES)
            br = hr_sc[pl.ds(r0, SUBLANES), ls]
            bi = hi_sc[pl.ds(r0, SUBLANES), ls]
            for k, shift in enumerate((1, 2, 4)):
                mr = m_ref[2 * k, :, ls]
                mi = m_ref[2 * k + 1, :, ls]
                sr = pltpu.roll(br, shift, 0)
                si = pltpu.roll(bi, shift, 0)
                br, bi = br + (mr * sr - mi * si), bi + (mr * si + mi * sr)
            pr = pw_ref[0, :, ls]
            pi_ = pw_ref[1, :, ls]
            hr = br + (pr * cr - pi_ * ci)
            hi = bi + (pr * ci + pi_ * cr)
            hr_sc[pl.ds(r0, SUBLANES), ls] = hr
            hi_sc[pl.ds(r0, SUBLANES), ls] = hi
            last_r = hr[SUBLANES - 1:SUBLANES, :]
            last_i = hi[SUBLANES - 1:SUBLANES, :]
            hlr_ref[pl.ds(b, 1), ls] = last_r
            hli_ref[pl.ds(b, 1), ls] = last_i
            return (jnp.broadcast_to(last_r, (SUBLANES, SCAN_LANES)),
                    jnp.broadcast_to(last_i, (SUBLANES, SCAN_LANES)))

        cr, ci = lax.fori_loop(0, n_tiles, tile_body, (cr_sc[:, ls], ci_sc[:, ls]))
        cr_sc[:, ls] = cr
        ci_sc[:, ls] = ci

    ys = []
    for s in range(SSM_SLABS):
        hre = hr_sc[:, kst * s:kst * (s + 1)].astype(BF16)
        him = hi_sc[:, kst * s:kst * (s + 1)].astype(BF16)
        ys.append(jnp.dot(hre, cs_ref[s, :kst, :], preferred_element_type=F32)
                  + jnp.dot(him, cs_ref[s, kst:, :], preferred_element_type=F32))
    y = jnp.concatenate(ys, axis=1)
    y = jax.nn.gelu(y + d_ref[...] * u_ref[...])
    gt = jnp.dot(y.astype(BF16), gluw_ref[...], preferred_element_type=F32) + glub_ref[...]
    ya_ref[...] = (y * jax.nn.sigmoid(gt)).astype(BF16)


def _s5(u, h0r, h0i, tabs, seq_len):
    rows, d_ssm = u.shape
    nb, n_state = h0r.shape
    tl = _pick(rows, (256, 128, 64, 32, 16, 8)) if seq_len >= 256 else _pick(rows, (512, 256, 128, 64, 32, 16))
    assert (seq_len % tl == 0 or tl % seq_len == 0) and seq_len % SUBLANES == 0
    bs, cs, d, gluw, glub, mtab, ptab = tabs
    return pl.pallas_call(
        functools.partial(_s5_kernel, seq_len=seq_len),
        grid=(rows // tl,),
        in_specs=[
            pl.BlockSpec((tl, d_ssm), lambda i: (i, 0)),
            _const_spec((nb, n_state)),
            _const_spec((nb, n_state)),
            _const_spec(bs.shape),
            _const_spec(cs.shape),
            _const_spec(d.shape),
            _const_spec(gluw.shape),
            _const_spec(glub.shape),
            _const_spec(mtab.shape),
            _const_spec(ptab.shape),
        ],
        out_specs=[
            pl.BlockSpec((tl, d_ssm), lambda i: (i, 0)),
            pl.BlockSpec((nb, n_state), lambda i: (0, 0)),
            pl.BlockSpec((nb, n_state), lambda i: (0, 0)),
        ],
        out_shape=[
            jax.ShapeDtypeStruct((rows, d_ssm), BF16),
            jax.ShapeDtypeStruct((nb, n_state), F32),
            jax.ShapeDtypeStruct((nb, n_state), F32),
        ],
        scratch_shapes=[
            pltpu.VMEM((tl, n_state), F32),
            pltpu.VMEM((tl, n_state), F32),
            pltpu.VMEM((SUBLANES, n_state), F32),
            pltpu.VMEM((SUBLANES, n_state), F32),
        ],
        compiler_params=pltpu.CompilerParams(
            dimension_semantics=("arbitrary",), vmem_limit_bytes=VMEM_LIMIT),
        name="s5",
    )(u, h0r, h0i, bs, cs, d, gluw, glub, mtab, ptab)


def _mix_kernel(x_ref, ya_ref, su_ref, vn_ref, gates_ref, ws_ref, bias_ref, wa_ref, wb_ref, wo_ref, x1_ref):
    tm, dm = x_ref.shape
    n_heads = ws_ref.shape[0]
    hd = vn_ref.shape[1] // n_heads
    vb = vn_ref[...].astype(BF16)
    chunks = []
    for c in range(tm // SGU_CHUNK):
        heads = [jnp.dot(ws_ref[h], vb[c * SGU_CHUNK:(c + 1) * SGU_CHUNK, h * hd:(h + 1) * hd],
                         preferred_element_type=F32) for h in range(n_heads)]
        chunks.append(jnp.concatenate(heads, axis=1) + bias_ref[...])
    s_all = jnp.concatenate(chunks, axis=0)
    yb = (su_ref[...].astype(F32) * s_all).astype(BF16)
    pa = jnp.dot(ya_ref[...], wa_ref[...], preferred_element_type=F32)
    pb = jnp.dot(yb, wb_ref[...], preferred_element_type=F32)
    merged = gates_ref[:, :dm].astype(F32) * pa + gates_ref[:, dm:].astype(F32) * pb
    x1_ref[...] = x_ref[...] + jnp.dot(merged.astype(BF16), wo_ref[...], preferred_element_type=F32)


def _mix(x, ya, su, vn, gates, ws, bias, wa, wb, wo):
    rows, dm = x.shape
    d_ssm = ya.shape[1]
    d_sgu = su.shape[1]
    tm = _pick(rows, (256, 128))
    row = lambda w: pl.BlockSpec((tm, w), lambda i: (i, 0))
    return pl.pallas_call(
        _mix_kernel,
        grid=(rows // tm,),
        in_specs=[row(dm), row(d_ssm), row(d_sgu), row(d_sgu), row(2 * dm),
                  _const_spec(ws.shape), _const_spec(bias.shape),
                  _const_spec(wa.shape), _const_spec(wb.shape), _const_spec(wo.shape)],
        out_specs=row(dm),
        out_shape=jax.ShapeDtypeStruct((rows, dm), F32),
        compiler_params=pltpu.CompilerParams(
            dimension_semantics=("arbitrary",), vmem_limit_bytes=VMEM_LIMIT),
        name="mix",
    )(x, ya, su, vn, gates, ws, bias, wa, wb, wo)


def _topk_rows(s, k, payload=None):
    rows = s.shape[0]
    iota = lax.broadcasted_iota(I32, s.shape, 0).astype(F32)
    vals, outs = [], []
    for _ in range(k):
        m = jnp.max(s, axis=0, keepdims=True)
        j = jnp.min(jnp.where(s == m, iota, float(rows)), axis=0, keepdims=True)
        sel = iota == j
        vals.append(m)
        outs.append(j if payload is None else jnp.max(jnp.where(sel, payload, -1.0), axis=0, keepdims=True))
        s = jnp.where(sel, -jnp.inf, s)
    return jnp.concatenate(vals, axis=0), jnp.concatenate(outs, axis=0)


def _route_kernel(x1_ref, g_ref, wq_ref, keys_ref, xn_ref, eidx_ref, gate_ref, q_sc, e_sc, p_sc):
    tb = x1_ref.shape[0]
    tw = e_sc.shape[2]
    nsub = tb // tw
    n_heads, _, n_keys, half = keys_ref.shape
    dkey = 2 * half
    xn = _rms(x1_ref[...], g_ref[...])
    xn_ref[...] = xn
    q = jnp.dot(xn.astype(BF16), wq_ref[...], preferred_element_type=F32).astype(BF16)
    for h in range(n_heads):
        q_sc[h] = q[:, h * dkey:(h + 1) * dkey]

    nt = (((1,), (1,)), ((), ()))

    def body(n, _):
        h = n // nsub
        r0 = pl.multiple_of((n % nsub) * tw, tw)
        qh = q_sc[h, pl.ds(r0, tw), :]
        s1 = lax.dot_general(keys_ref[h, 0], qh[:, :half], nt, preferred_element_type=F32)
        s2 = lax.dot_general(keys_ref[h, 1], qh[:, half:], nt, preferred_element_type=F32)
        v1, i1 = _topk_rows(s1, PEER_TOPK)
        v2, i2 = _topk_rows(s2, PEER_TOPK)
        cv, ci = [], []
        for a in range(PEER_TOPK):
            nb = PEER_TOPK // (a + 1)
            cv.append(v1[a:a + 1, :] + v2[:nb, :])
            ci.append(i1[a:a + 1, :] * float(n_keys) + i2[:nb, :])
        ts, te = _topk_rows(jnp.concatenate(cv, axis=0), PEER_TOPK, payload=jnp.concatenate(ci, axis=0))
        ex = jnp.exp(ts - ts[0:1, :])
        e_sc[n] = te
        p_sc[n] = ex / jnp.sum(ex, axis=0, keepdims=True)
        return 0

    lax.fori_loop(0, n_heads * nsub, body, 0)
    for c in range(nsub):
        rows = slice(c * tw, (c + 1) * tw)
        e_all = jnp.concatenate([e_sc[h * nsub + c] for h in range(n_heads)], axis=0)
        p_all = jnp.concatenate([p_sc[h * nsub + c] for h in range(n_heads)], axis=0)
        eidx_ref[rows, :] = e_all.T.astype(I32)
        gate_ref[rows, :] = p_all.T


def _route(x1, g, wq, keys_bf):
    rows, dm = x1.shape
    n_heads, _, n_keys, half = keys_bf.shape
    nsel = n_heads * PEER_TOPK
    tb = _pick(rows, (256, 128))
    tw = min(tb, ROUTE_LANES)
    row = lambda w: pl.BlockSpec((tb, w), lambda i: (i, 0))
    return pl.pallas_call(
        _route_kernel,
        grid=(rows // tb,),
        in_specs=[row(dm), _const_spec(g.shape), _const_spec(wq.shape), _const_spec(keys_bf.shape)],
        out_specs=[row(dm), row(nsel), row(nsel)],
        out_shape=[
            jax.ShapeDtypeStruct((rows, dm), F32),
            jax.ShapeDtypeStruct((rows, nsel), I32),
            jax.ShapeDtypeStruct((rows, nsel), F32),
        ],
        scratch_shapes=[
            pltpu.VMEM((n_heads, tb, 2 * half), BF16),
            pltpu.VMEM((n_heads * (tb // tw), PEER_TOPK, tw), F32),
            pltpu.VMEM((n_heads * (tb // tw), PEER_TOPK, tw), F32),
        ],
        compiler_params=pltpu.CompilerParams(
            dimension_semantics=("arbitrary",), vmem_limit_bytes=VMEM_LIMIT),
        name="route",
    )(x1, g, wq, keys_bf)


PEER_BUFS = 8


def _peer_kernel(eidx_hbm, gate_ref, xn_ref, x1_ref, gf_ref, tab_hbm, y_ref,
                 idx_sm, idx_sem, *scratch, final_norm):
    i = pl.program_id(0)
    bufs = scratch[:PEER_BUFS]
    sem, peer_sc = scratch[PEER_BUFS:]
    tbk, nsel = gate_ref.shape
    half = tab_hbm.shape[1] // 2
    nlt = half // LANES

    icp = pltpu.make_async_copy(eidx_hbm.at[i], idx_sm, idx_sem)
    icp.start()
    icp.wait()

    def issue(t, s):
        base = t * nsel
        for k in range(nsel):
            e = idx_sm[base + k]
            pltpu.make_async_copy(tab_hbm.at[pl.ds(e, 1), :], bufs[s].at[pl.ds(k, 1), :],
                                  sem.at[s]).start(priority=k % 2)

    def wait(s):
        pltpu.make_async_copy(tab_hbm.at[pl.ds(0, nsel), :], bufs[s], sem.at[s]).wait()

    hi_mask = jnp.int32(-65536)
    eye = (lax.broadcasted_iota(I32, (nsel, nsel), 0) == lax.broadcasted_iota(I32, (nsel, nsel), 1))

    def unpack(w):
        return (lax.bitcast_convert_type(w & hi_mask, F32),
                lax.bitcast_convert_type(w << 16, F32))

    def compute(t, s):
        buf = bufs[s]
        xrow = xn_ref[pl.ds(t, 1), :]
        acc = jnp.zeros((nsel, LANES), F32)
        for j in range(nlt):
            uh, ul = unpack(buf[:, j * LANES:(j + 1) * LANES])
            acc = acc + uh * xrow[:, j * LANES:(j + 1) * LANES] \
                      + ul * xrow[:, half + j * LANES:half + (j + 1) * LANES]
        act = jnp.sum(acc, axis=1, keepdims=True)
        grow = jnp.broadcast_to(gate_ref[pl.ds(t, 1), :], (nsel, nsel))
        gcol = jnp.sum(jnp.where(eye, grow, 0.0), axis=1, keepdims=True)
        wgt = jnp.broadcast_to(gcol * jax.nn.gelu(act), (nsel, LANES))
        t8 = pl.multiple_of((t // SUBLANES) * SUBLANES, SUBLANES)
        rsel = lax.broadcasted_iota(I32, (SUBLANES, LANES), 0) == (t % SUBLANES)

        def put(lane0, row):
            cur = peer_sc[pl.ds(t8, SUBLANES), lane0:lane0 + LANES]
            peer_sc[pl.ds(t8, SUBLANES), lane0:lane0 + LANES] = jnp.where(
                rsel, jnp.broadcast_to(row, (SUBLANES, LANES)), cur)

        for j in range(nlt):
            vh, vl = unpack(buf[:, half + j * LANES:half + (j + 1) * LANES])
            put(j * LANES, jnp.sum(vh * wgt, axis=0, keepdims=True))
            put(half + j * LANES, jnp.sum(vl * wgt, axis=0, keepdims=True))

    def step(t, s, prefetch):
        wait(s)
        if prefetch:
            issue(t + PEER_BUFS - 1, (s + PEER_BUFS - 1) % PEER_BUFS)
        compute(t, s)

    peer_sc[...] = jnp.zeros_like(peer_sc)
    for t in range(PEER_BUFS - 1):
        issue(t, t)

    def group(gi, _):
        for s in range(PEER_BUFS):
            step(gi * PEER_BUFS + s, s, True)
        return 0

    lax.fori_loop(0, tbk // PEER_BUFS - 1, group, 0)
    for s in range(PEER_BUFS):
        t = tbk - PEER_BUFS + s
        step(t, s, t + PEER_BUFS - 1 < tbk)

    xo = x1_ref[...] + peer_sc[...]
    y_ref[...] = _rms(xo, gf_ref[...]) if final_norm else xo


def _peer(eidx, gate, xn, x1, gf, table, final_norm):
    rows, dm = x1.shape
    nsel = gate.shape[1]
    tbk = _pick(rows, (256, 128, 64, 32, 16, 8))
    assert tbk % PEER_BUFS == 0 and tbk >= 2 * PEER_BUFS
    nblk = rows // tbk
    eidx_blk = eidx.reshape(nblk, tbk * nsel)
    row = lambda w: pl.BlockSpec((tbk, w), lambda i: (i, 0))
    return pl.pallas_call(
        functools.partial(_peer_kernel, final_norm=final_norm),
        grid=(nblk,),
        in_specs=[pl.BlockSpec(memory_space=pl.ANY), row(nsel), row(dm), row(dm),
                  _const_spec(gf.shape), pl.BlockSpec(memory_space=pl.ANY)],
        out_specs=row(dm),
        out_shape=jax.ShapeDtypeStruct((rows, dm), F32),
        scratch_shapes=[
            pltpu.SMEM((tbk * nsel,), I32),
            pltpu.SemaphoreType.DMA(()),
        ] + [pltpu.VMEM((nsel, table.shape[1]), table.dtype) for _ in range(PEER_BUFS)] + [
            pltpu.SemaphoreType.DMA((PEER_BUFS,)),
            pltpu.VMEM((tbk, dm), F32),
        ],
        compiler_params=pltpu.CompilerParams(
            dimension_semantics=("arbitrary",), vmem_limit_bytes=VMEM_LIMIT),
        name="peer",
    )(eidx_blk, gate, xn, x1, gf, table)


SC_L = 16
SC_WINDOW = SC_L
SC_CUTS = (0.125, 0.53125)
SC_SPLIT_ALIGN = 512


def _sc_peer(table, eidx, gate, xn):
    ntok, nsel = eidx.shape
    d = table.shape[1]
    half = d // 2
    nch = half // SC_L
    info = plsc.get_sparse_core_info()
    n_workers = info.num_cores * info.num_subcores
    n_win = nsel // SC_WINDOW
    tpw = ntok // n_workers
    assert tpw * n_workers == ntok and SC_WINDOW == SC_L and info.num_lanes == SC_L
    mesh = plsc.VectorSubcoreMesh(core_axis_name="core", subcore_axis_name="subcore")
    c0 = math.sqrt(2.0 / math.pi)

    @functools.partial(
        pl.kernel, out_type=jax.ShapeDtypeStruct((ntok, d), F32), mesh=mesh,
        scratch_types=[pltpu.VMEM((nsel,), I32), pltpu.VMEM((nsel,), F32), pltpu.VMEM((d,), F32),
                       pltpu.VMEM((2, SC_WINDOW, d), I32), pltpu.VMEM((d,), F32),
                       pltpu.SemaphoreType.DMA((2,))],
        compiler_params=pltpu.CompilerParams(needs_layout_passes=False),
        name="sc_peer")
    def k(tab_hbm, idx_hbm, gate_hbm, x_hbm, o_hbm, idx_v, gate_v, x_v, rows_v, out_v, gsem):
        wid = lax.axis_index("subcore") * info.num_cores + lax.axis_index("core")
        lane = lax.iota(I32, SC_L)
        zero = jnp.zeros((SC_L,), F32)

        def unpack(w):
            return (lax.bitcast_convert_type(w & jnp.int32(-65536), F32),
                    lax.bitcast_convert_type(w << 16, F32))

        @pl.loop(0, tpw)
        def _(tt):
            tok = wid * tpw + tt
            pltpu.sync_copy(idx_hbm.at[pl.ds(tok * nsel, nsel)], idx_v)
            pltpu.sync_copy(gate_hbm.at[pl.ds(tok * nsel, nsel)], gate_v)
            pltpu.sync_copy(x_hbm.at[tok], x_v)

            @pl.loop(0, d // SC_L)
            def _(c):
                out_v[pl.ds(c * SC_L, SC_L)] = zero

            def gather(w):
                return pltpu.make_async_copy(tab_hbm.at[idx_v.at[pl.ds(w * SC_WINDOW, SC_WINDOW)]],
                                             rows_v.at[w % 2], gsem.at[w % 2])

            gather(0).start()
            for w in range(n_win):
                gather(w).wait()
                if w + 1 < n_win:
                    gather(w + 1).start()
                slot = w % 2

                def dot_chunk(j, accs, slot=slot):
                    xa = x_v[pl.ds(j * SC_L, SC_L)]
                    xb = x_v[pl.ds(half + j * SC_L, SC_L)]
                    out = []
                    for r in range(SC_WINDOW):
                        uh, ul = unpack(rows_v[slot, r, pl.ds(j * SC_L, SC_L)])
                        out.append(accs[r] + (uh * xa + ul * xb))
                    return tuple(out)

                accs = lax.fori_loop(0, nch, dot_chunk, (zero,) * SC_WINDOW)
                act = zero
                for r in range(SC_WINDOW):
                    act = jnp.where(lane == r, jnp.sum(accs[r]), act)
                z = c0 * (act + 0.044715 * (act * act * act))
                tanh_z = 1.0 - 2.0 / (jnp.exp(2.0 * z) + 1.0)
                wgt = gate_v[pl.ds(w * SC_WINDOW, SC_WINDOW)] * (0.5 * act * (1.0 + tanh_z))
                splat = [jnp.full((SC_L,), jnp.sum(jnp.where(lane == r, wgt, 0.0)), F32)
                         for r in range(SC_WINDOW)]

                @pl.loop(0, nch)
                def _(j, slot=slot, splat=splat):
                    oh = zero
                    ol = zero
                    for r in range(SC_WINDOW):
                        vh, vl = unpack(rows_v[slot, r, pl.ds(half + j * SC_L, SC_L)])
                        oh = oh + splat[r] * vh
                        ol = ol + splat[r] * vl
                    out_v[pl.ds(j * SC_L, SC_L)] += oh
                    out_v[pl.ds(half + j * SC_L, SC_L)] += ol

            pltpu.sync_copy(out_v, o_hbm.at[tok])

    return k(table, eidx.reshape(-1), gate.reshape(-1), xn)


def _finish_kernel(x1_ref, p_ref, gf_ref, y_ref):
    y_ref[...] = _rms(x1_ref[...] + p_ref[...], gf_ref[...])


def _finish(x1, p, gf):
    rows, dm = x1.shape
    tm = _pick(rows, (512, 256, 128))
    row = pl.BlockSpec((tm, dm), lambda i: (i, 0))
    return pl.pallas_call(
        _finish_kernel, grid=(rows // tm,), in_specs=[row, row, _const_spec(gf.shape)], out_specs=row,
        out_shape=jax.ShapeDtypeStruct((rows, dm), F32),
        compiler_params=pltpu.CompilerParams(dimension_semantics=("arbitrary",), vmem_limit_bytes=VMEM_LIMIT),
        name="finish",
    )(x1, p, gf)


def _cmul(ar, ai, br, bi):
    return ar * br - ai * bi, ar * bi + ai * br


def _s5_tables(lam_re, lam_im, log_dt, b_re, b_im, c_re, c_im, d, glu_w, glu_b):
    g, p = lam_re.shape
    hg = b_re.shape[-1]
    gs = g // SSM_SLABS
    dt = jnp.exp(log_dt.astype(F32))[:, None]
    lr = lam_re.astype(F32)
    li = lam_im.astype(F32)
    mag = jnp.exp(lr * dt)
    ab_re = mag * jnp.cos(li * dt)
    ab_im = mag * jnp.sin(li * dt)
    den = lr * lr + li * li
    nr = ab_re - 1.0
    ni = ab_im
    coef_re = (nr * lr + ni * li) / den
    coef_im = (ni * lr - nr * li) / den
    br = b_re.astype(F32)
    bi = b_im.astype(F32)
    bb_re = coef_re[..., None] * br - coef_im[..., None] * bi
    bb_im = coef_re[..., None] * bi + coef_im[..., None] * br
    eye = jnp.eye(gs, dtype=F32)

    def b_slab(bb):
        t = bb.reshape(SSM_SLABS, gs, p, hg).transpose(0, 1, 3, 2)
        return jnp.einsum('sihp,ij->sihjp', t, eye).reshape(SSM_SLABS, gs * hg, gs * p)

    bs = jnp.concatenate([b_slab(bb_re), b_slab(bb_im)], axis=2).astype(BF16)

    def c_slab(c):
        t = c.astype(F32).reshape(SSM_SLABS, gs, hg, p).transpose(0, 1, 3, 2)
        return jnp.einsum('siph,ij->sipjh', t, eye).reshape(SSM_SLABS, gs * p, gs * hg)

    cs = jnp.concatenate([c_slab(c_re), -c_slab(c_im)], axis=1).astype(BF16)

    a1 = (ab_re.reshape(1, g * p), ab_im.reshape(1, g * p))
    a2 = _cmul(*a1, *a1)
    a4 = _cmul(*a2, *a2)
    rowid = jnp.arange(SUBLANES)[:, None]
    mtab = jnp.stack([jnp.where(rowid >= s, comp, 0.0)
                      for s, a in ((1, a1), (2, a2), (4, a4)) for comp in a])
    pows = [a1]
    for _ in range(SUBLANES - 1):
        pows.append(_cmul(*pows[-1], *a1))
    ptab = jnp.stack([jnp.concatenate([q[0] for q in pows], axis=0),
                      jnp.concatenate([q[1] for q in pows], axis=0)])
    return (bs, cs, d.astype(F32).reshape(1, -1), glu_w.astype(BF16),
            glu_b.astype(F32).reshape(1, -1), mtab, ptab)


def _sgu_tables(w_s, b_s, seq_len):
    n_heads = w_s.shape[0]
    ln = min(SGU_CHUNK, seq_len)
    assert SGU_CHUNK % ln == 0 and seq_len % ln == 0
    rep = SGU_CHUNK // ln
    mask = jnp.tril(jnp.ones((ln, ln), dtype=bool))
    w = jnp.where(mask[None], w_s[:, :ln, :ln], 0.0).astype(F32)
    ws = jnp.einsum('hij,ab->haibj', w, jnp.eye(rep, dtype=F32)).reshape(n_heads, SGU_CHUNK, SGU_CHUNK)
    bias = jnp.tile(jnp.transpose(b_s[:, :ln]).astype(F32), (rep, 1))
    return ws.astype(BF16), bias


def _pack_kernel(u_ref, v_ref, o_ref):
    half = u_ref.shape[1] // 2

    def bf16_bits(x):
        return lax.bitcast_convert_type(x.astype(BF16).astype(F32), U32)

    def pack(t_ref):
        w = bf16_bits(t_ref[:, :half]) | (bf16_bits(t_ref[:, half:]) >> 16)
        return lax.bitcast_convert_type(w, I32)

    o_ref[:, :half] = pack(u_ref)
    o_ref[:, half:] = pack(v_ref)


def _pack_tables(u, v):
    n, dd = u.shape
    tr = _pick(n, (PACK_ROWS, 256, 128, 8))
    spec = pl.BlockSpec((tr, dd), lambda i: (i, 0))
    return pl.pallas_call(
        _pack_kernel,
        grid=(n // tr,),
        in_specs=[spec, spec],
        out_specs=spec,
        out_shape=jax.ShapeDtypeStruct((n, dd), I32),
        compiler_params=pltpu.CompilerParams(
            dimension_semantics=("arbitrary",), vmem_limit_bytes=VMEM_LIMIT),
        name="pack",
    )(u, v)


def _trunk_to_route(x, h0r, h0i, seq_len, lp):
    (norm_mix_g, w_in_bf, s5_tabs, ln_g, ln_b, sgu_w, sgu_b, wa, wb, wo, norm_ffn_g, wq, keys_bf, _,
     d_ssm, d_sgu) = lp
    u, su, vn, gates = _inproj(x, norm_mix_g, w_in_bf, ln_g, ln_b, d_ssm, d_sgu)
    ya, hlr, hli = _s5(u, h0r, h0i, s5_tabs, seq_len)
    ws, bias = _sgu_tables(sgu_w, sgu_b, seq_len)
    hd = d_sgu // ws.shape[0]
    bias_tile = jnp.repeat(bias, hd, axis=1)
    x1 = _mix(x, ya, su, vn, gates, ws, bias_tile, wa, wb, wo)
    xn, eidx, gate = _route(x1, norm_ffn_g, wq, keys_bf)
    return (x1, xn, eidx, gate), hlr, hli, vn


def _trunk_layer(x, h0r, h0i, seq_len, lp, final_g):
    (x1, xn, eidx, gate), hlr, hli, vn = _trunk_to_route(x, h0r, h0i, seq_len, lp)
    y = _peer(eidx, gate, xn, x1, final_g, lp[13], final_norm=True)
    return y, hlr, hli, vn


def _prompt_split(x, seq_len, lp, final_g, n_state):
    rows = x.shape[0]
    cuts = [(int(rows * c) // SC_SPLIT_ALIGN) * SC_SPLIT_ALIGN for c in SC_CUTS]
    bounds = [0] + cuts + [rows]
    hr = hi = jnp.zeros((1, n_state), F32)
    parts = []
    for a, b in zip(bounds[:-1], bounds[1:]):
        part, hr, hi, _ = _trunk_to_route(x[a:b], hr, hi, seq_len, lp)
        parts.append(part)
    x1a, xna, ea, ga = parts[-1]
    ya = _peer(ea, ga, xna, x1a, final_g, lp[13], final_norm=True)
    return parts[:-1], ya, hr, hi


def _prompt_finish(firsts, ya, lp, final_g):
    ys = [_finish(x1, _sc_peer(lp[13], e, g, xn), final_g) for (x1, xn, e, g) in firsts]
    return jnp.concatenate(ys + [ya], axis=0)


def kernel(x_prompt, x_sample, state_ssm_re, state_ssm_im, norm_mix_g, w_in, ssm_lambda_re, ssm_lambda_im, ssm_log_dt, ssm_b_re, ssm_b_im, ssm_c_re, ssm_c_im, ssm_d, ssm_glu_w, ssm_glu_b, sgu_ln_g, sgu_ln_b, sgu_w, sgu_b, w_branch_a, w_branch_b, w_out, norm_ffn_g, peer_w_q, peer_keys, peer_u, peer_v, norm_final_g):
    depth = w_in.shape[0]
    assert depth == 1, "the final norm is fused into the last layer's PEER kernel"
    bp, sp, dm = x_prompt.shape
    bs_, ss, _ = x_sample.shape
    g, p = ssm_lambda_re.shape[1:]
    d_ssm = ssm_d.shape[1]
    d_sgu = sgu_ln_g.shape[1]
    n_state = g * p
    l = 0
    lp = (norm_mix_g[l].reshape(1, dm), w_in[l].astype(BF16),
          _s5_tables(ssm_lambda_re[l], ssm_lambda_im[l], ssm_log_dt[l], ssm_b_re[l], ssm_b_im[l],
                     ssm_c_re[l], ssm_c_im[l], ssm_d[l], ssm_glu_w[l], ssm_glu_b[l]),
          sgu_ln_g[l].reshape(1, d_sgu), sgu_ln_b[l].reshape(1, d_sgu), sgu_w[l], sgu_b[l],
          w_branch_a[l].astype(BF16), w_branch_b[l].astype(BF16), w_out[l].astype(BF16),
          norm_ffn_g[l].reshape(1, dm), peer_w_q[l].astype(BF16), peer_keys[l].astype(BF16),
          _pack_tables(peer_u[l].astype(F32), peer_v[l].astype(F32)),
          d_ssm, d_sgu)
    gf = norm_final_g.reshape(1, dm)

    xp2 = x_prompt.reshape(bp * sp, dm)
    split = bp == 1 and sp >= 4 * SC_SPLIT_ALIGN
    if split:
        first, ya, hr_p, hi_p = _prompt_split(xp2, sp, lp, gf, n_state)
    else:
        zeros = jnp.zeros((bp, n_state), F32)
        yp, hr_p, hi_p, _ = _trunk_layer(xp2, zeros, zeros, sp, lp, gf)
    ys, hr_s, hi_s, v_s = _trunk_layer(x_sample.reshape(bs_ * ss, dm),
                                       state_ssm_re[l].astype(F32).reshape(bs_, n_state),
                                       state_ssm_im[l].astype(F32).reshape(bs_, n_state), ss, lp, gf)
    if split:
        yp = _prompt_finish(first, ya, lp, gf)
    return (yp.reshape(bp, sp, dm), ys.reshape(bs_, ss, dm),
            hr_p.reshape(1, bp, g, p), hi_p.reshape(1, bp, g, p),
            hr_s.reshape(1, bs_, g, p), hi_s.reshape(1, bs_, g, p),
            v_s.reshape(1, bs_, ss, d_sgu))
```

```python
import functools
import math

import jax
import jax.numpy as jnp
from jax import lax
from jax.experimental import pallas as pl
from jax.experimental.pallas import tpu as pltpu
from jax.experimental.pallas import tpu_sc as plsc

F32 = jnp.float32
BF16 = jnp.bfloat16
I32 = jnp.int32
U32 = jnp.uint32
EPS = 1e-6

LANES = 128
SUBLANES = 8
VMEM_LIMIT = 56 * 1024 * 1024

SSM_SLABS = 4
SCAN_LANES = 512
PEER_TOPK = 16
SGU_CHUNK = 128
ROUTE_LANES = 256
INPROJ_SUB = 128
PACK_ROWS = 512


def _pick(n, candidates):
    for c in candidates:
        if n % c == 0:
            return c
    raise ValueError(f"no block size for {n} in {candidates}")


def _const_spec(shape):
    nd = len(shape)
    return pl.BlockSpec(shape, lambda *_: (0,) * nd, pipeline_mode=pl.Buffered(1))


def _rms(x, g):
    return x * lax.rsqrt(jnp.mean(x * x, axis=-1, keepdims=True) + EPS) * g


def _inproj_kernel(x_ref, g_ref, w_ref, lng_ref, lnb_ref, u_ref, su_ref, vn_ref, gates_ref, xn_sc):
    j = pl.program_id(1)
    tm = x_ref.shape[0]
    sub = min(tm, INPROJ_SUB)

    @pl.when(j == 0)
    def _():
        xn_sc[...] = _rms(x_ref[...], g_ref[...]).astype(BF16)

    def section(out_ref, epilogue):
        for r in range(tm // sub):
            rows = slice(r * sub, (r + 1) * sub)
            z = jnp.dot(xn_sc[rows, :], w_ref[...], preferred_element_type=F32)
            out_ref[rows, :] = epilogue(z).astype(out_ref.dtype)

    def layer_norm_gelu(z):
        v = jax.nn.gelu(z)
        mu = jnp.mean(v, axis=-1, keepdims=True)
        vc = v - mu
        y = vc * lax.rsqrt(jnp.mean(vc * vc, axis=-1, keepdims=True) + EPS)
        return y * lng_ref[...] + lnb_ref[...]

    @pl.when(j == 0)
    def _():
        section(u_ref, lambda z: z)

    @pl.when(j == 1)
    def _():
        section(su_ref, jax.nn.gelu)

    @pl.when(j == 2)
    def _():
        section(vn_ref, layer_norm_gelu)

    @pl.when(j >= 3)
    def _():
        section(gates_ref, jax.nn.sigmoid)


def _inproj(x, g, w_in_bf, ln_g, ln_b, d_ssm, d_sgu):
    rows, dm = x.shape
    d_in = w_in_bf.shape[1]
    tn = d_ssm
    assert d_sgu == tn and (d_in - 3 * tn) % tn == 0
    nj = d_in // tn
    tm = _pick(rows, (512, 256, 128))
    return pl.pallas_call(
        _inproj_kernel,
        grid=(rows // tm, nj),
        in_specs=[
            pl.BlockSpec((tm, dm), lambda i, j: (i, 0)),
            pl.BlockSpec((1, dm), lambda i, j: (0, 0)),
            pl.BlockSpec((dm, tn), lambda i, j: (0, j)),
            pl.BlockSpec((1, tn), lambda i, j: (0, 0)),
            pl.BlockSpec((1, tn), lambda i, j: (0, 0)),
        ],
        out_specs=[
            pl.BlockSpec((tm, tn), lambda i, j: (i, 0)),
            pl.BlockSpec((tm, tn), lambda i, j: (i, 0)),
            pl.BlockSpec((tm, tn), lambda i, j: (i, 0)),
            pl.BlockSpec((tm, tn), lambda i, j: (i, jnp.maximum(j - 3, 0))),
        ],
        out_shape=[
            jax.ShapeDtypeStruct((rows, tn), F32),
            jax.ShapeDtypeStruct((rows, tn), BF16),
            jax.ShapeDtypeStruct((rows, tn), F32),
            jax.ShapeDtypeStruct((rows, d_in - 3 * tn), BF16),
        ],
        scratch_shapes=[pltpu.VMEM((tm, dm), BF16)],
        compiler_params=pltpu.CompilerParams(
            dimension_semantics=("arbitrary", "arbitrary"), vmem_limit_bytes=VMEM_LIMIT),
        name="inproj",
    )(x, g, w_in_bf, ln_g, ln_b)


def _s5_kernel(u_ref, h0r_ref, h0i_ref, bs_ref, cs_ref, d_ref, gluw_ref, glub_ref, m_ref, pw_ref,
               ya_ref, hlr_ref, hli_ref, hr_sc, hi_sc, cr_sc, ci_sc, *, seq_len):
    i = pl.program_id(0)
    tl, d_ssm = u_ref.shape
    n_state = hr_sc.shape[1]
    kin = d_ssm // SSM_SLABS
    kst = n_state // SSM_SLABS

    @pl.when(i == 0)
    def _():
        cr_sc[...] = jnp.zeros_like(cr_sc)
        ci_sc[...] = jnp.zeros_like(ci_sc)

    ub = u_ref[...].astype(BF16)
    for s in range(SSM_SLABS):
        r = jnp.dot(ub[:, kin * s:kin * (s + 1)], bs_ref[s], preferred_element_type=F32)
        hr_sc[:, kst * s:kst * (s + 1)] = r[:, :kst]
        hi_sc[:, kst * s:kst * (s + 1)] = r[:, kst:]

    n_tiles = tl // SUBLANES
    for lc in range(n_state // SCAN_LANES):
        ls = slice(lc * SCAN_LANES, (lc + 1) * SCAN_LANES)

        def tile_body(t, carry, ls=ls):
            cr, ci = carry
            row0 = i * tl + t * SUBLANES
            b = row0 // seq_len
            is_start = (row0 % seq_len) == 0
            h0r = jnp.broadcast_to(h0r_ref[pl.ds(b, 1), ls], (SUBLANES, SCAN_LANES))
            h0i = jnp.broadcast_to(h0i_ref[pl.ds(b, 1), ls], (SUBLANES, SCAN_LANES))
            cr = jnp.where(is_start, h0r, cr)
            ci = jnp.where(is_start, h0i, ci)
            r0 = pl.multiple_of(t * SUBLANES, SUBLANES)
            br = hr_sc[pl.ds(r0, SUBLANES), ls]
            bi = hi_sc[pl.ds(r0, SUBLANES), ls]
            for k, shift in enumerate((1, 2, 4)):
                mr = m_ref[2 * k, :, ls]
                mi = m_ref[2 * k + 1, :, ls]
                sr = pltpu.roll(br, shift, 0)
                si = pltpu.roll(bi, shift, 0)
                br, bi = br + (mr * sr - mi * si), bi + (mr * si + mi * sr)
            pr = pw_ref[0, :, ls]
            pi_ = pw_ref[1, :, ls]
            hr = br + (pr * cr - pi_ * ci)
            hi = bi + (pr * ci + pi_ * cr)
            hr_sc[pl.ds(r0, SUBLANES), ls] = hr
            hi_sc[pl.ds(r0, SUBLANES), ls] = hi
            last_r = hr[SUBLANES - 1:SUBLANES, :]
            last_i = hi[SUBLANES - 1:SUBLANES, :]
            hlr_ref[pl.ds(b, 1), ls] = last_r
            hli_ref[pl.ds(b, 1), ls] = last_i
            return (jnp.broadcast_to(last_r, (SUBLANES, SCAN_LANES)),
                    jnp.broadcast_to(last_i, (SUBLANES, SCAN_LANES)))

        cr, ci = lax.fori_loop(0, n_tiles, tile_body, (cr_sc[:, ls], ci_sc[:, ls]))
        cr_sc[:, ls] = cr
        ci_sc[:, ls] = ci

    ys = []
    for s in range(SSM_SLABS):
        hre = hr_sc[:, kst * s:kst * (s + 1)].astype(BF16)
        him = hi_sc[:, kst * s:kst * (s + 1)].astype(BF16)
        ys.append(jnp.dot(hre, cs_ref[s, :kst, :], preferred_element_type=F32)
                  + jnp.dot(him, cs_ref[s, kst:, :], preferred_element_type=F32))
    y = jnp.concatenate(ys, axis=1)
    y = jax.nn.gelu(y + d_ref[...] * u_ref[...])
    gt = jnp.dot(y.astype(BF16), gluw_ref[...], preferred_element_type=F32) + glub_ref[...]
    ya_ref[...] = (y * jax.nn.sigmoid(gt)).astype(BF16)


def _s5(u, h0r, h0i, tabs, seq_len):
    rows, d_ssm = u.shape
    nb, n_state = h0r.shape
    tl = _pick(rows, (256, 128, 64, 32, 16, 8)) if seq_len >= 256 else _pick(rows, (512, 256, 128, 64, 32, 16))
    assert (seq_len % tl == 0 or tl % seq_len == 0) and seq_len % SUBLANES == 0
    bs, cs, d, gluw, glub, mtab, ptab = tabs
    return pl.pallas_call(
        functools.partial(_s5_kernel, seq_len=seq_len),
        grid=(rows // tl,),
        in_specs=[
            pl.BlockSpec((tl, d_ssm), lambda i: (i, 0)),
            _const_spec((nb, n_state)),
            _const_spec((nb, n_state)),
            _const_spec(bs.shape),
            _const_spec(cs.shape),
            _const_spec(d.shape),
            _const_spec(gluw.shape),
            _const_spec(glub.shape),
            _const_spec(mtab.shape),
            _const_spec(ptab.shape),
        ],
        out_specs=[
            pl.BlockSpec((tl, d_ssm), lambda i: (i, 0)),
            pl.BlockSpec((nb, n_state), lambda i: (0, 0)),
            pl.BlockSpec((nb, n_state), lambda i: (0, 0)),
        ],
        out_shape=[
            jax.ShapeDtypeStruct((rows, d_ssm), BF16),
            jax.ShapeDtypeStruct((nb, n_state), F32),
            jax.ShapeDtypeStruct((nb, n_state), F32),
        ],
        scratch_shapes=[
            pltpu.VMEM((tl, n_state), F32),
            pltpu.VMEM((tl, n_state), F32),
            pltpu.VMEM((SUBLANES, n_state), F32),
            pltpu.VMEM((SUBLANES, n_state), F32),
        ],
        compiler_params=pltpu.CompilerParams(
            dimension_semantics=("arbitrary",), vmem_limit_bytes=VMEM_LIMIT),
        name="s5",
    )(u, h0r, h0i, bs, cs, d, gluw, glub, mtab, ptab)


def _mix_kernel(x_ref, ya_ref, su_ref, vn_ref, gates_ref, ws_ref, bias_ref, wa_ref, wb_ref, wo_ref, x1_ref):
    tm, dm = x_ref.shape
    n_heads = ws_ref.shape[0]
    hd = vn_ref.shape[1] // n_heads
    vb = vn_ref[...].astype(BF16)
    chunks = []
    for c in range(tm // SGU_CHUNK):
        heads = [jnp.dot(ws_ref[h], vb[c * SGU_CHUNK:(c + 1) * SGU_CHUNK, h * hd:(h + 1) * hd],
                         preferred_element_type=F32) for h in range(n_heads)]
        chunks.append(jnp.concatenate(heads, axis=1) + bias_ref[...])
    s_all = jnp.concatenate(chunks, axis=0)
    yb = (su_ref[...].astype(F32) * s_all).astype(BF16)
    pa = jnp.dot(ya_ref[...], wa_ref[...], preferred_element_type=F32)
    pb = jnp.dot(yb, wb_ref[...], preferred_element_type=F32)
    merged = gates_ref[:, :dm].astype(F32) * pa + gates_ref[:, dm:].astype(F32) * pb
    x1_ref[...] = x_ref[...] + jnp.dot(merged.astype(BF16), wo_ref[...], preferred_element_type=F32)


def _mix(x, ya, su, vn, gates, ws, bias, wa, wb, wo):
    rows, dm = x.shape
    d_ssm = ya.shape[1]
    d_sgu = su.shape[1]
    tm = _pick(rows, (256, 128))
    row = lambda w: pl.BlockSpec((tm, w), lambda i: (i, 0))
    return pl.pallas_call(
        _mix_kernel,
        grid=(rows // tm,),
        in_specs=[row(dm), row(d_ssm), row(d_sgu), row(d_sgu), row(2 * dm),
                  _const_spec(ws.shape), _const_spec(bias.shape),
                  _const_spec(wa.shape), _const_spec(wb.shape), _const_spec(wo.shape)],
        out_specs=row(dm),
        out_shape=jax.ShapeDtypeStruct((rows, dm), F32),
        compiler_params=pltpu.CompilerParams(
            dimension_semantics=("arbitrary",), vmem_limit_bytes=VMEM_LIMIT),
        name="mix",
    )(x, ya, su, vn, gates, ws, bias, wa, wb, wo)


def _topk_rows(s, k, payload=None):
    rows = s.shape[0]
    iota = lax.broadcasted_iota(I32, s.shape, 0).astype(F32)
    vals, outs = [], []
    for _ in range(k):
        m = jnp.max(s, axis=0, keepdims=True)
        j = jnp.min(jnp.where(s == m, iota, float(rows)), axis=0, keepdims=True)
        sel = iota == j
        vals.append(m)
        outs.append(j if payload is None else jnp.max(jnp.where(sel, payload, -1.0), axis=0, keepdims=True))
        s = jnp.where(sel, -jnp.inf, s)
    return jnp.concatenate(vals, axis=0), jnp.concatenate(outs, axis=0)


def _route_kernel(x1_ref, g_ref, wq_ref, keys_ref, xn_ref, eidx_ref, gate_ref, q_sc, e_sc, p_sc):
    tb = x1_ref.shape[0]
    tw = e_sc.shape[2]
    nsub = tb // tw
    n_heads, _, n_keys, half = keys_ref.shape
    dkey = 2 * half
    xn = _rms(x1_ref[...], g_ref[...])
    xn_ref[...] = xn
    q = jnp.dot(xn.astype(BF16), wq_ref[...], preferred_element_type=F32).astype(BF16)
    for h in range(n_heads):
        q_sc[h] = q[:, h * dkey:(h + 1) * dkey]

    nt = (((1,), (1,)), ((), ()))

    def body(n, _):
        h = n // nsub
        r0 = pl.multiple_of((n % nsub) * tw, tw)
        qh = q_sc[h, pl.ds(r0, tw), :]
        s1 = lax.dot_general(keys_ref[h, 0], qh[:, :half], nt, preferred_element_type=F32)
        s2 = lax.dot_general(keys_ref[h, 1], qh[:, half:], nt, preferred_element_type=F32)
        v1, i1 = _topk_rows(s1, PEER_TOPK)
        v2, i2 = _topk_rows(s2, PEER_TOPK)
        cv, ci = [], []
        for a in range(PEER_TOPK):
            nb = PEER_TOPK // (a + 1)
            cv.append(v1[a:a + 1, :] + v2[:nb, :])
            ci.append(i1[a:a + 1, :] * float(n_keys) + i2[:nb, :])
        ts, te = _topk_rows(jnp.concatenate(cv, axis=0), PEER_TOPK, payload=jnp.concatenate(ci, axis=0))
        ex = jnp.exp(ts - ts[0:1, :])
        e_sc[n] = te
        p_sc[n] = ex / jnp.sum(ex, axis=0, keepdims=True)
        return 0

    lax.fori_loop(0, n_heads * nsub, body, 0)
    for c in range(nsub):
        rows = slice(c * tw, (c + 1) * tw)
        e_all = jnp.concatenate([e_sc[h * nsub + c] for h in range(n_heads)], axis=0)
        p_all = jnp.concatenate([p_sc[h * nsub + c] for h in range(n_heads)], axis=0)
        eidx_ref[rows, :] = e_all.T.astype(I32)
        gate_ref[rows, :] = p_all.T


def _route(x1, g, wq, keys_bf):
    rows, dm = x1.shape
    n_heads, _, n_keys, half = keys_bf.shape
    nsel = n_heads * PEER_TOPK
    tb = _pick(rows, (256, 128))
    tw = min(tb, ROUTE_LANES)
    row = lambda w: pl.BlockSpec((tb, w), lambda i: (i, 0))
    return pl.pallas_call(
        _route_kernel,
        grid=(rows // tb,),
        in_specs=[row(dm), _const_spec(g.shape), _const_spec(wq.shape), _const_spec(keys_bf.shape)],
        out_specs=[row(dm), row(nsel), row(nsel)],
        out_shape=[
            jax.ShapeDtypeStruct((rows, dm), F32),
            jax.ShapeDtypeStruct((rows, nsel), I32),
            jax.ShapeDtypeStruct((rows, nsel), F32),
        ],
        scratch_shapes=[
            pltpu.VMEM((n_heads, tb, 2 * half), BF16),
            pltpu.VMEM((n_heads * (tb // tw), PEER_TOPK, tw), F32),
            pltpu.VMEM((n_heads * (tb // tw), PEER_TOPK, tw), F32),
        ],
        compiler_params=pltpu.CompilerParams(
            dimension_semantics=("arbitrary",), vmem_limit_bytes=VMEM_LIMIT),
        name="route",
    )(x1, g, wq, keys_bf)


PEER_BUFS = 16


def _peer_kernel(eidx_hbm, gate_ref, xn_ref, x1_ref, gf_ref, tab_hbm, y_ref,
                 idx_sm, idx_sem, *scratch, final_norm):
    i = pl.program_id(0)
    bufs = scratch[:PEER_BUFS]
    sem, peer_sc = scratch[PEER_BUFS:]
    tbk, nsel = gate_ref.shape
    half = tab_hbm.shape[1] // 2
    nlt = half // LANES

    icp = pltpu.make_async_copy(eidx_hbm.at[i], idx_sm, idx_sem)
    icp.start()
    icp.wait()

    def issue(t, s):
        base = t * nsel
        for k in range(nsel):
            e = idx_sm[base + k]
            pltpu.make_async_copy(tab_hbm.at[pl.ds(e, 1), :], bufs[s].at[pl.ds(k, 1), :],
                                  sem.at[s]).start(priority=k % 2)

    def wait(s):
        pltpu.make_async_copy(tab_hbm.at[pl.ds(0, nsel), :], bufs[s], sem.at[s]).wait()

    hi_mask = jnp.int32(-65536)
    eye = (lax.broadcasted_iota(I32, (nsel, nsel), 0) == lax.broadcasted_iota(I32, (nsel, nsel), 1))

    def unpack(w):
        return (lax.bitcast_convert_type(w & hi_mask, F32),
                lax.bitcast_convert_type(w << 16, F32))

    def compute(t, s):
        buf = bufs[s]
        xrow = xn_ref[pl.ds(t, 1), :]
        acc = jnp.zeros((nsel, LANES), F32)
        for j in range(nlt):
            uh, ul = unpack(buf[:, j * LANES:(j + 1) * LANES])
            acc = acc + uh * xrow[:, j * LANES:(j + 1) * LANES] \
                      + ul * xrow[:, half + j * LANES:half + (j + 1) * LANES]
        act = jnp.sum(acc, axis=1, keepdims=True)
        grow = jnp.broadcast_to(gate_ref[pl.ds(t, 1), :], (nsel, nsel))
        gcol = jnp.sum(jnp.where(eye, grow, 0.0), axis=1, keepdims=True)
        wgt = jnp.broadcast_to(gcol * jax.nn.gelu(act), (nsel, LANES))
        t8 = pl.multiple_of((t // SUBLANES) * SUBLANES, SUBLANES)
        rsel = lax.broadcasted_iota(I32, (SUBLANES, LANES), 0) == (t % SUBLANES)

        def put(lane0, row):
            cur = peer_sc[pl.ds(t8, SUBLANES), lane0:lane0 + LANES]
            peer_sc[pl.ds(t8, SUBLANES), lane0:lane0 + LANES] = jnp.where(
                rsel, jnp.broadcast_to(row, (SUBLANES, LANES)), cur)

        for j in range(nlt):
            vh, vl = unpack(buf[:, half + j * LANES:half + (j + 1) * LANES])
            put(j * LANES, jnp.sum(vh * wgt, axis=0, keepdims=True))
            put(half + j * LANES, jnp.sum(vl * wgt, axis=0, keepdims=True))

    def step(t, s, prefetch):
        wait(s)
        if prefetch:
            issue(t + PEER_BUFS - 1, (s + PEER_BUFS - 1) % PEER_BUFS)
        compute(t, s)

    peer_sc[...] = jnp.zeros_like(peer_sc)
    for t in range(PEER_BUFS - 1):
        issue(t, t)

    def group(gi, _):
        for s in range(PEER_BUFS):
            step(gi * PEER_BUFS + s, s, True)
        return 0

    lax.fori_loop(0, tbk // PEER_BUFS - 1, group, 0)
    for s in range(PEER_BUFS):
        t = tbk - PEER_BUFS + s
        step(t, s, t + PEER_BUFS - 1 < tbk)

    xo = x1_ref[...] + peer_sc[...]
    y_ref[...] = _rms(xo, gf_ref[...]) if final_norm else xo


def _peer(eidx, gate, xn, x1, gf, table, final_norm):
    rows, dm = x1.shape
    nsel = gate.shape[1]
    tbk = _pick(rows, (256, 128, 64, 32, 16, 8))
    assert tbk % PEER_BUFS == 0 and tbk >= 2 * PEER_BUFS
    nblk = rows // tbk
    eidx_blk = eidx.reshape(nblk, tbk * nsel)
    row = lambda w: pl.BlockSpec((tbk, w), lambda i: (i, 0))
    return pl.pallas_call(
        functools.partial(_peer_kernel, final_norm=final_norm),
        grid=(nblk,),
        in_specs=[pl.BlockSpec(memory_space=pl.ANY), row(nsel), row(dm), row(dm),
                  _const_spec(gf.shape), pl.BlockSpec(memory_space=pl.ANY)],
        out_specs=row(dm),
        out_shape=jax.ShapeDtypeStruct((rows, dm), F32),
        scratch_shapes=[
            pltpu.SMEM((tbk * nsel,), I32),
            pltpu.SemaphoreType.DMA(()),
        ] + [pltpu.VMEM((nsel, table.shape[1]), table.dtype) for _ in range(PEER_BUFS)] + [
            pltpu.SemaphoreType.DMA((PEER_BUFS,)),
            pltpu.VMEM((tbk, dm), F32),
        ],
        compiler_params=pltpu.CompilerParams(
            dimension_semantics=("arbitrary",), vmem_limit_bytes=VMEM_LIMIT),
        name="peer",
    )(eidx_blk, gate, xn, x1, gf, table)


SC_L = 16
SC_WINDOW = SC_L
SC_CUTS = (0.125, 0.46875)
SC_SPLIT_ALIGN = 512


def _sc_peer(table, eidx, gate, xn):
    ntok, nsel = eidx.shape
    d = table.shape[1]
    half = d // 2
    nch = half // SC_L
    info = plsc.get_sparse_core_info()
    n_workers = info.num_cores * info.num_subcores
    n_win = nsel // SC_WINDOW
    tpw = ntok // n_workers
    assert tpw * n_workers == ntok and SC_WINDOW == SC_L and info.num_lanes == SC_L
    mesh = plsc.VectorSubcoreMesh(core_axis_name="core", subcore_axis_name="subcore")
    c0 = math.sqrt(2.0 / math.pi)

    @functools.partial(
        pl.kernel, out_type=jax.ShapeDtypeStruct((ntok, d), F32), mesh=mesh,
        scratch_types=[pltpu.VMEM((nsel,), I32), pltpu.VMEM((nsel,), F32), pltpu.VMEM((d,), F32),
                       pltpu.VMEM((2, SC_WINDOW, d), I32), pltpu.VMEM((d,), F32),
                       pltpu.SemaphoreType.DMA((2,))],
        compiler_params=pltpu.CompilerParams(needs_layout_passes=False),
        name="sc_peer")
    def k(tab_hbm, idx_hbm, gate_hbm, x_hbm, o_hbm, idx_v, gate_v, x_v, rows_v, out_v, gsem):
        wid = lax.axis_index("subcore") * info.num_cores + lax.axis_index("core")
        lane = lax.iota(I32, SC_L)
        zero = jnp.zeros((SC_L,), F32)

        def unpack(w):
            return (lax.bitcast_convert_type(w & jnp.int32(-65536), F32),
                    lax.bitcast_convert_type(w << 16, F32))

        @pl.loop(0, tpw)
        def _(tt):
            tok = wid * tpw + tt
            pltpu.sync_copy(idx_hbm.at[pl.ds(tok * nsel, nsel)], idx_v)
            pltpu.sync_copy(gate_hbm.at[pl.ds(tok * nsel, nsel)], gate_v)
            pltpu.sync_copy(x_hbm.at[tok], x_v)

            @pl.loop(0, d // SC_L)
            def _(c):
                out_v[pl.ds(c * SC_L, SC_L)] = zero

            def gather(w):
                return pltpu.make_async_copy(tab_hbm.at[idx_v.at[pl.ds(w * SC_WINDOW, SC_WINDOW)]],
                                             rows_v.at[w % 2], gsem.at[w % 2])

            gather(0).start()
            for w in range(n_win):
                gather(w).wait()
                if w + 1 < n_win:
                    gather(w + 1).start()
                slot = w % 2

                def dot_chunk(j, accs, slot=slot):
                    xa = x_v[pl.ds(j * SC_L, SC_L)]
                    xb = x_v[pl.ds(half + j * SC_L, SC_L)]
                    out = []
                    for r in range(SC_WINDOW):
                        uh, ul = unpack(rows_v[slot, r, pl.ds(j * SC_L, SC_L)])
                        out.append(accs[r] + (uh * xa + ul * xb))
                    return tuple(out)

                accs = lax.fori_loop(0, nch, dot_chunk, (zero,) * SC_WINDOW)
                act = zero
                for r in range(SC_WINDOW):
                    act = jnp.where(lane == r, jnp.sum(accs[r]), act)
                z = c0 * (act + 0.044715 * (act * act * act))
                tanh_z = 1.0 - 2.0 / (jnp.exp(2.0 * z) + 1.0)
                wgt = gate_v[pl.ds(w * SC_WINDOW, SC_WINDOW)] * (0.5 * act * (1.0 + tanh_z))
                splat = [jnp.full((SC_L,), jnp.sum(jnp.where(lane == r, wgt, 0.0)), F32)
                         for r in range(SC_WINDOW)]

                @pl.loop(0, nch)
                def _(j, slot=slot, splat=splat):
                    oh = zero
                    ol = zero
                    for r in range(SC_WINDOW):
                        vh, vl = unpack(rows_v[slot, r, pl.ds(half + j * SC_L, SC_L)])
                        oh = oh + splat[r] * vh
                        ol = ol + splat[r] * vl
                    out_v[pl.ds(j * SC_L, SC_L)] += oh
                    out_v[pl.ds(half + j * SC_L, SC_L)] += ol

            pltpu.sync_copy(out_v, o_hbm.at[tok])

    return k(table, eidx.reshape(-1), gate.reshape(-1), xn)


def _finish_kernel(x1_ref, p_ref, gf_ref, y_ref):
    y_ref[...] = _rms(x1_ref[...] + p_ref[...], gf_ref[...])


def _finish(x1, p, gf):
    rows, dm = x1.shape
    tm = _pick(rows, (512, 256, 128))
    row = pl.BlockSpec((tm, dm), lambda i: (i, 0))
    return pl.pallas_call(
        _finish_kernel, grid=(rows // tm,), in_specs=[row, row, _const_spec(gf.shape)], out_specs=row,
        out_shape=jax.ShapeDtypeStruct((rows, dm), F32),
        compiler_params=pltpu.CompilerParams(dimension_semantics=("arbitrary",), vmem_limit_bytes=VMEM_LIMIT),
        name="finish",
    )(x1, p, gf)


def _cmul(ar, ai, br, bi):
    return ar * br - ai * bi, ar * bi + ai * br


def _s5_tables(lam_re, lam_im, log_dt, b_re, b_im, c_re, c_im, d, glu_w, glu_b):
    g, p = lam_re.shape
    hg = b_re.shape[-1]
    gs = g // SSM_SLABS
    dt = jnp.exp(log_dt.astype(F32))[:, None]
    lr = lam_re.astype(F32)
    li = lam_im.astype(F32)
    mag = jnp.exp(lr * dt)
    ab_re = mag * jnp.cos(li * dt)
    ab_im = mag * jnp.sin(li * dt)
    den = lr * lr + li * li
    nr = ab_re - 1.0
    ni = ab_im
    coef_re = (nr * lr + ni * li) / den
    coef_im = (ni * lr - nr * li) / den
    br = b_re.astype(F32)
    bi = b_im.astype(F32)
    bb_re = coef_re[..., None] * br - coef_im[..., None] * bi
    bb_im = coef_re[..., None] * bi + coef_im[..., None] * br
    eye = jnp.eye(gs, dtype=F32)

    def b_slab(bb):
        t = bb.reshape(SSM_SLABS, gs, p, hg).transpose(0, 1, 3, 2)
        return jnp.einsum('sihp,ij->sihjp', t, eye).reshape(SSM_SLABS, gs * hg, gs * p)

    bs = jnp.concatenate([b_slab(bb_re), b_slab(bb_im)], axis=2).astype(BF16)

    def c_slab(c):
        t = c.astype(F32).reshape(SSM_SLABS, gs, hg, p).transpose(0, 1, 3, 2)
        return jnp.einsum('siph,ij->sipjh', t, eye).reshape(SSM_SLABS, gs * p, gs * hg)

    cs = jnp.concatenate([c_slab(c_re), -c_slab(c_im)], axis=1).astype(BF16)

    a1 = (ab_re.reshape(1, g * p), ab_im.reshape(1, g * p))
    a2 = _cmul(*a1, *a1)
    a4 = _cmul(*a2, *a2)
    rowid = jnp.arange(SUBLANES)[:, None]
    mtab = jnp.stack([jnp.where(rowid >= s, comp, 0.0)
                      for s, a in ((1, a1), (2, a2), (4, a4)) for comp in a])
    pows = [a1]
    for _ in range(SUBLANES - 1):
        pows.append(_cmul(*pows[-1], *a1))
    ptab = jnp.stack([jnp.concatenate([q[0] for q in pows], axis=0),
                      jnp.concatenate([q[1] for q in pows], axis=0)])
    return (bs, cs, d.astype(F32).reshape(1, -1), glu_w.astype(BF16),
            glu_b.astype(F32).reshape(1, -1), mtab, ptab)


def _sgu_tables(w_s, b_s, seq_len):
    n_heads = w_s.shape[0]
    ln = min(SGU_CHUNK, seq_len)
    assert SGU_CHUNK % ln == 0 and seq_len % ln == 0
    rep = SGU_CHUNK // ln
    mask = jnp.tril(jnp.ones((ln, ln), dtype=bool))
    w = jnp.where(mask[None], w_s[:, :ln, :ln], 0.0).astype(F32)
    ws = jnp.einsum('hij,ab->haibj', w, jnp.eye(rep, dtype=F32)).reshape(n_heads, SGU_CHUNK, SGU_CHUNK)
    bias = jnp.tile(jnp.transpose(b_s[:, :ln]).astype(F32), (rep, 1))
    return ws.astype(BF16), bias


def _pack_kernel(u_ref, v_ref, o_ref):
    half = u_ref.shape[1] // 2

    def bf16_bits(x):
        return lax.bitcast_convert_type(x.astype(BF16).astype(F32), U32)

    def pack(t_ref):
        w = bf16_bits(t_ref[:, :half]) | (bf16_bits(t_ref[:, half:]) >> 16)
        return lax.bitcast_convert_type(w, I32)

    o_ref[:, :half] = pack(u_ref)
    o_ref[:, half:] = pack(v_ref)


def _pack_tables(u, v):
    n, dd = u.shape
    tr = _pick(n, (PACK_ROWS, 256, 128, 8))
    spec = pl.BlockSpec((tr, dd), lambda i: (i, 0))
    return pl.pallas_call(
        _pack_kernel,
        grid=(n // tr,),
        in_specs=[spec, spec],
        out_specs=spec,
        out_shape=jax.ShapeDtypeStruct((n, dd), I32),
        compiler_params=pltpu.CompilerParams(
            dimension_semantics=("arbitrary",), vmem_limit_bytes=VMEM_LIMIT),
        name="pack",
    )(u, v)


def _trunk_to_route(x, h0r, h0i, seq_len, lp):
    (norm_mix_g, w_in_bf, s5_tabs, ln_g, ln_b, sgu_w, sgu_b, wa, wb, wo, norm_ffn_g, wq, keys_bf, _,
     d_ssm, d_sgu) = lp
    u, su, vn, gates = _inproj(x, norm_mix_g, w_in_bf, ln_g, ln_b, d_ssm, d_sgu)
    ya, hlr, hli = _s5(u, h0r, h0i, s5_tabs, seq_len)
    ws, bias = _sgu_tables(sgu_w, sgu_b, seq_len)
    hd = d_sgu // ws.shape[0]
    bias_tile = jnp.repeat(bias, hd, axis=1)
    x1 = _mix(x, ya, su, vn, gates, ws, bias_tile, wa, wb, wo)
    xn, eidx, gate = _route(x1, norm_ffn_g, wq, keys_bf)
    return (x1, xn, eidx, gate), hlr, hli, vn


def _trunk_layer(x, h0r, h0i, seq_len, lp, final_g):
    (x1, xn, eidx, gate), hlr, hli, vn = _trunk_to_route(x, h0r, h0i, seq_len, lp)
    y = _peer(eidx, gate, xn, x1, final_g, lp[13], final_norm=True)
    return y, hlr, hli, vn


def _prompt_split(x, seq_len, lp, final_g, n_state):
    rows = x.shape[0]
    cuts = [(int(rows * c) // SC_SPLIT_ALIGN) * SC_SPLIT_ALIGN for c in SC_CUTS]
    bounds = [0] + cuts + [rows]
    hr = hi = jnp.zeros((1, n_state), F32)
    parts = []
    for a, b in zip(bounds[:-1], bounds[1:]):
        part, hr, hi, _ = _trunk_to_route(x[a:b], hr, hi, seq_len, lp)
        parts.append(part)
    x1a, xna, ea, ga = parts[-1]
    ya = _peer(ea, ga, xna, x1a, final_g, lp[13], final_norm=True)
    return parts[:-1], ya, hr, hi


def _prompt_finish(firsts, ya, lp, final_g):
    ys = [_finish(x1, _sc_peer(lp[13], e, g, xn), final_g) for (x1, xn, e, g) in firsts]
    return jnp.concatenate(ys + [ya], axis=0)


def kernel(x_prompt, x_sample, state_ssm_re, state_ssm_im, norm_mix_g, w_in, ssm_lambda_re, ssm_lambda_im, ssm_log_dt, ssm_b_re, ssm_b_im, ssm_c_re, ssm_c_im, ssm_d, ssm_glu_w, ssm_glu_b, sgu_ln_g, sgu_ln_b, sgu_w, sgu_b, w_branch_a, w_branch_b, w_out, norm_ffn_g, peer_w_q, peer_keys, peer_u, peer_v, norm_final_g):
    depth = w_in.shape[0]
    assert depth == 1, "the final norm is fused into the last layer's PEER kernel"
    bp, sp, dm = x_prompt.shape
    bs_, ss, _ = x_sample.shape
    g, p = ssm_lambda_re.shape[1:]
    d_ssm = ssm_d.shape[1]
    d_sgu = sgu_ln_g.shape[1]
    n_state = g * p
    l = 0
    lp = (norm_mix_g[l].reshape(1, dm), w_in[l].astype(BF16),
          _s5_tables(ssm_lambda_re[l], ssm_lambda_im[l], ssm_log_dt[l], ssm_b_re[l], ssm_b_im[l],
                     ssm_c_re[l], ssm_c_im[l], ssm_d[l], ssm_glu_w[l], ssm_glu_b[l]),
          sgu_ln_g[l].reshape(1, d_sgu), sgu_ln_b[l].reshape(1, d_sgu), sgu_w[l], sgu_b[l],
          w_branch_a[l].astype(BF16), w_branch_b[l].astype(BF16), w_out[l].astype(BF16),
          norm_ffn_g[l].reshape(1, dm), peer_w_q[l].astype(BF16), peer_keys[l].astype(BF16),
          _pack_tables(peer_u[l].astype(F32), peer_v[l].astype(F32)),
          d_ssm, d_sgu)
    gf = norm_final_g.reshape(1, dm)

    xp2 = x_prompt.reshape(bp * sp, dm)
    split = bp == 1 and sp >= 4 * SC_SPLIT_ALIGN
    if split:
        first, ya, hr_p, hi_p = _prompt_split(xp2, sp, lp, gf, n_state)
    else:
        zeros = jnp.zeros((bp, n_state), F32)
        yp, hr_p, hi_p, _ = _trunk_layer(xp2, zeros, zeros, sp, lp, gf)
    ys, hr_s, hi_s, v_s = _trunk_layer(x_sample.reshape(bs_ * ss, dm),
                                       state_ssm_re[l].astype(F32).reshape(bs_, n_state),
                                       state_ssm_im[l].astype(F32).reshape(bs_, n_state), ss, lp, gf)
    if split:
        yp = _prompt_finish(first, ya, lp, gf)
    return (yp.reshape(bp, sp, dm), ys.reshape(bs_, ss, dm),
            hr_p.reshape(1, bp, g, p), hi_p.reshape(1, bp, g, p),
            hr_s.reshape(1, bs_, g, p), hi_s.reshape(1, bs_, g, p),
            v_s.reshape(1, bs_, ss, d_sgu))
```

```python
import functools
import math

import jax
import jax.numpy as jnp
from jax import lax
from jax.experimental import pallas as pl
from jax.experimental.pallas import tpu as pltpu
from jax.experimental.pallas import tpu_sc as plsc

F32 = jnp.float32
BF16 = jnp.bfloat16
I32 = jnp.int32
U32 = jnp.uint32
EPS = 1e-6

LANES = 128
SUBLANES = 8
VMEM_LIMIT = 56 * 1024 * 1024

SSM_SLABS = 4
SCAN_LANES = 512
PEER_TOPK = 16
SGU_CHUNK = 128
ROUTE_LANES = 256
INPROJ_SUB = 128
PACK_ROWS = 512


def _pick(n, candidates):
    for c in candidates:
        if n % c == 0:
            return c
    raise ValueError(f"no block size for {n} in {candidates}")


def _const_spec(shape):
    nd = len(shape)
    return pl.BlockSpec(shape, lambda *_: (0,) * nd, pipeline_mode=pl.Buffered(1))


def _rms(x, g):
    return x * lax.rsqrt(jnp.mean(x * x, axis=-1, keepdims=True) + EPS) * g


def _inproj_kernel(x_ref, g_ref, w_ref, lng_ref, lnb_ref, u_ref, su_ref, vn_ref, gates_ref, xn_sc):
    j = pl.program_id(1)
    tm = x_ref.shape[0]
    sub = min(tm, INPROJ_SUB)

    @pl.when(j == 0)
    def _():
        xn_sc[...] = _rms(x_ref[...], g_ref[...]).astype(BF16)

    def section(out_ref, epilogue):
        for r in range(tm // sub):
            rows = slice(r * sub, (r + 1) * sub)
            z = jnp.dot(xn_sc[rows, :], w_ref[...], preferred_element_type=F32)
            out_ref[rows, :] = epilogue(z).astype(out_ref.dtype)

    def layer_norm_gelu(z):
        v = jax.nn.gelu(z)
        mu = jnp.mean(v, axis=-1, keepdims=True)
        vc = v - mu
        y = vc * lax.rsqrt(jnp.mean(vc * vc, axis=-1, keepdims=True) + EPS)
        return y * lng_ref[...] + lnb_ref[...]

    @pl.when(j == 0)
    def _():
        section(u_ref, lambda z: z)

    @pl.when(j == 1)
    def _():
        section(su_ref, jax.nn.gelu)

    @pl.when(j == 2)
    def _():
        section(vn_ref, layer_norm_gelu)

    @pl.when(j >= 3)
    def _():
        section(gates_ref, jax.nn.sigmoid)


def _inproj(x, g, w_in_bf, ln_g, ln_b, d_ssm, d_sgu):
    rows, dm = x.shape
    d_in = w_in_bf.shape[1]
    tn = d_ssm
    assert d_sgu == tn and (d_in - 3 * tn) % tn == 0
    nj = d_in // tn
    tm = _pick(rows, (512, 256, 128))
    return pl.pallas_call(
        _inproj_kernel,
        grid=(rows // tm, nj),
        in_specs=[
            pl.BlockSpec((tm, dm), lambda i, j: (i, 0)),
            pl.BlockSpec((1, dm), lambda i, j: (0, 0)),
            pl.BlockSpec((dm, tn), lambda i, j: (0, j)),
            pl.BlockSpec((1, tn), lambda i, j: (0, 0)),
            pl.BlockSpec((1, tn), lambda i, j: (0, 0)),
        ],
        out_specs=[
            pl.BlockSpec((tm, tn), lambda i, j: (i, 0)),
            pl.BlockSpec((tm, tn), lambda i, j: (i, 0)),
            pl.BlockSpec((tm, tn), lambda i, j: (i, 0)),
            pl.BlockSpec((tm, tn), lambda i, j: (i, jnp.maximum(j - 3, 0))),
        ],
        out_shape=[
            jax.ShapeDtypeStruct((rows, tn), F32),
            jax.ShapeDtypeStruct((rows, tn), BF16),
            jax.ShapeDtypeStruct((rows, tn), F32),
            jax.ShapeDtypeStruct((rows, d_in - 3 * tn), BF16),
        ],
        scratch_shapes=[pltpu.VMEM((tm, dm), BF16)],
        compiler_params=pltpu.CompilerParams(
            dimension_semantics=("arbitrary", "arbitrary"), vmem_limit_bytes=VMEM_LIMIT),
        name="inproj",
    )(x, g, w_in_bf, ln_g, ln_b)


def _s5_kernel(u_ref, h0r_ref, h0i_ref, bs_ref, cs_ref, d_ref, gluw_ref, glub_ref, m_ref, pw_ref,
               ya_ref, hlr_ref, hli_ref, hr_sc, hi_sc, cr_sc, ci_sc, *, seq_len):
    i = pl.program_id(0)
    tl, d_ssm = u_ref.shape
    n_state = hr_sc.shape[1]
    kin = d_ssm // SSM_SLABS
    kst = n_state // SSM_SLABS

    @pl.when(i == 0)
    def _():
        cr_sc[...] = jnp.zeros_like(cr_sc)
        ci_sc[...] = jnp.zeros_like(ci_sc)

    ub = u_ref[...].astype(BF16)
    for s in range(SSM_SLABS):
        r = jnp.dot(ub[:, kin * s:kin * (s + 1)], bs_ref[s], preferred_element_type=F32)
        hr_sc[:, kst * s:kst * (s + 1)] = r[:, :kst]
        hi_sc[:, kst * s:kst * (s + 1)] = r[:, kst:]

    n_tiles = tl // SUBLANES
    for lc in range(n_state // SCAN_LANES):
        ls = slice(lc * SCAN_LANES, (lc + 1) * SCAN_LANES)

        def tile_body(t, carry, ls=ls):
            cr, ci = carry
            row0 = i * tl + t * SUBLANES
            b = row0 // seq_len
            is_start = (row0 % seq_len) == 0
            h0r = jnp.broadcast_to(h0r_ref[pl.ds(b, 1), ls], (SUBLANES, SCAN_LANES))
            h0i = jnp.broadcast_to(h0i_ref[pl.ds(b, 1), ls], (SUBLANES, SCAN_LANES))
            cr = jnp.where(is_start, h0r, cr)
            ci = jnp.where(is_start, h0i, ci)
            r0 = pl.multiple_of(t * SUBLANES, SUBLANES)
            br = hr_sc[pl.ds(r0, SUBLANES), ls]
            bi = hi_sc[pl.ds(r0, SUBLANES), ls]
            for k, shift in enumerate((1, 2, 4)):
                mr = m_ref[2 * k, :, ls]
                mi = m_ref[2 * k + 1, :, ls]
                sr = pltpu.roll(br, shift, 0)
                si = pltpu.roll(bi, shift, 0)
                br, bi = br + (mr * sr - mi * si), bi + (mr * si + mi * sr)
            pr = pw_ref[0, :, ls]
            pi_ = pw_ref[1, :, ls]
            hr = br + (pr * cr - pi_ * ci)
            hi = bi + (pr * ci + pi_ * cr)
            hr_sc[pl.ds(r0, SUBLANES), ls] = hr
            hi_sc[pl.ds(r0, SUBLANES), ls] = hi
            last_r = hr[SUBLANES - 1:SUBLANES, :]
            last_i = hi[SUBLANES - 1:SUBLANES, :]
            hlr_ref[pl.ds(b, 1), ls] = last_r
            hli_ref[pl.ds(b, 1), ls] = last_i
            return (jnp.broadcast_to(last_r, (SUBLANES, SCAN_LANES)),
                    jnp.broadcast_to(last_i, (SUBLANES, SCAN_LANES)))

        cr, ci = lax.fori_loop(0, n_tiles, tile_body, (cr_sc[:, ls], ci_sc[:, ls]))
        cr_sc[:, ls] = cr
        ci_sc[:, ls] = ci

    ys = []
    for s in range(SSM_SLABS):
        hre = hr_sc[:, kst * s:kst * (s + 1)].astype(BF16)
        him = hi_sc[:, kst * s:kst * (s + 1)].astype(BF16)
        ys.append(jnp.dot(hre, cs_ref[s, :kst, :], preferred_element_type=F32)
                  + jnp.dot(him, cs_ref[s, kst:, :], preferred_element_type=F32))
    y = jnp.concatenate(ys, axis=1)
    y = jax.nn.gelu(y + d_ref[...] * u_ref[...])
    gt = jnp.dot(y.astype(BF16), gluw_ref[...], preferred_element_type=F32) + glub_ref[...]
    ya_ref[...] = (y * jax.nn.sigmoid(gt)).astype(BF16)


def _s5(u, h0r, h0i, tabs, seq_len):
    rows, d_ssm = u.shape
    nb, n_state = h0r.shape
    tl = _pick(rows, (256, 128, 64, 32, 16, 8)) if seq_len >= 256 else _pick(rows, (512, 256, 128, 64, 32, 16))
    assert (seq_len % tl == 0 or tl % seq_len == 0) and seq_len % SUBLANES == 0
    bs, cs, d, gluw, glub, mtab, ptab = tabs
    return pl.pallas_call(
        functools.partial(_s5_kernel, seq_len=seq_len),
        grid=(rows // tl,),
        in_specs=[
            pl.BlockSpec((tl, d_ssm), lambda i: (i, 0)),
            _const_spec((nb, n_state)),
            _const_spec((nb, n_state)),
            _const_spec(bs.shape),
            _const_spec(cs.shape),
            _const_spec(d.shape),
            _const_spec(gluw.shape),
            _const_spec(glub.shape),
            _const_spec(mtab.shape),
            _const_spec(ptab.shape),
        ],
        out_specs=[
            pl.BlockSpec((tl, d_ssm), lambda i: (i, 0)),
            pl.BlockSpec((nb, n_state), lambda i: (0, 0)),
            pl.BlockSpec((nb, n_state), lambda i: (0, 0)),
        ],
        out_shape=[
            jax.ShapeDtypeStruct((rows, d_ssm), BF16),
            jax.ShapeDtypeStruct((nb, n_state), F32),
            jax.ShapeDtypeStruct((nb, n_state), F32),
        ],
        scratch_shapes=[
            pltpu.VMEM((tl, n_state), F32),
            pltpu.VMEM((tl, n_state), F32),
            pltpu.VMEM((SUBLANES, n_state), F32),
            pltpu.VMEM((SUBLANES, n_state), F32),
        ],
        compiler_params=pltpu.CompilerParams(
            dimension_semantics=("arbitrary",), vmem_limit_bytes=VMEM_LIMIT),
        name="s5",
    )(u, h0r, h0i, bs, cs, d, gluw, glub, mtab, ptab)


def _mix_kernel(x_ref, ya_ref, su_ref, vn_ref, gates_ref, ws_ref, bias_ref, wa_ref, wb_ref, wo_ref, x1_ref):
    tm, dm = x_ref.shape
    n_heads = ws_ref.shape[0]
    hd = vn_ref.shape[1] // n_heads
    vb = vn_ref[...].astype(BF16)
    chunks = []
    for c in range(tm // SGU_CHUNK):
        heads = [jnp.dot(ws_ref[h], vb[c * SGU_CHUNK:(c + 1) * SGU_CHUNK, h * hd:(h + 1) * hd],
                         preferred_element_type=F32) for h in range(n_heads)]
        chunks.append(jnp.concatenate(heads, axis=1) + bias_ref[...])
    s_all = jnp.concatenate(chunks, axis=0)
    yb = (su_ref[...].astype(F32) * s_all).astype(BF16)
    pa = jnp.dot(ya_ref[...], wa_ref[...], preferred_element_type=F32)
    pb = jnp.dot(yb, wb_ref[...], preferred_element_type=F32)
    merged = gates_ref[:, :dm].astype(F32) * pa + gates_ref[:, dm:].astype(F32) * pb
    x1_ref[...] = x_ref[...] + jnp.dot(merged.astype(BF16), wo_ref[...], preferred_element_type=F32)


def _mix(x, ya, su, vn, gates, ws, bias, wa, wb, wo):
    rows, dm = x.shape
    d_ssm = ya.shape[1]
    d_sgu = su.shape[1]
    tm = _pick(rows, (256, 128))
    row = lambda w: pl.BlockSpec((tm, w), lambda i: (i, 0))
    return pl.pallas_call(
        _mix_kernel,
        grid=(rows // tm,),
        in_specs=[row(dm), row(d_ssm), row(d_sgu), row(d_sgu), row(2 * dm),
                  _const_spec(ws.shape), _const_spec(bias.shape),
                  _const_spec(wa.shape), _const_spec(wb.shape), _const_spec(wo.shape)],
        out_specs=row(dm),
        out_shape=jax.ShapeDtypeStruct((rows, dm), F32),
        compiler_params=pltpu.CompilerParams(
            dimension_semantics=("arbitrary",), vmem_limit_bytes=VMEM_LIMIT),
        name="mix",
    )(x, ya, su, vn, gates, ws, bias, wa, wb, wo)


def _topk_rows(s, k, payload=None):
    rows = s.shape[0]
    iota = lax.broadcasted_iota(I32, s.shape, 0).astype(F32)
    vals, outs = [], []
    for _ in range(k):
        m = jnp.max(s, axis=0, keepdims=True)
        j = jnp.min(jnp.where(s == m, iota, float(rows)), axis=0, keepdims=True)
        sel = iota == j
        vals.append(m)
        outs.append(j if payload is None else jnp.max(jnp.where(sel, payload, -1.0), axis=0, keepdims=True))
        s = jnp.where(sel, -jnp.inf, s)
    return jnp.concatenate(vals, axis=0), jnp.concatenate(outs, axis=0)


def _route_kernel(x1_ref, g_ref, wq_ref, keys_ref, xn_ref, eidx_ref, gate_ref, q_sc, e_sc, p_sc):
    tb = x1_ref.shape[0]
    tw = e_sc.shape[2]
    nsub = tb // tw
    n_heads, _, n_keys, half = keys_ref.shape
    dkey = 2 * half
    xn = _rms(x1_ref[...], g_ref[...])
    xn_ref[...] = xn
    q = jnp.dot(xn.astype(BF16), wq_ref[...], preferred_element_type=F32).astype(BF16)
    for h in range(n_heads):
        q_sc[h] = q[:, h * dkey:(h + 1) * dkey]

    nt = (((1,), (1,)), ((), ()))

    def body(n, _):
        h = n // nsub
        r0 = pl.multiple_of((n % nsub) * tw, tw)
        qh = q_sc[h, pl.ds(r0, tw), :]
        s1 = lax.dot_general(keys_ref[h, 0], qh[:, :half], nt, preferred_element_type=F32)
        s2 = lax.dot_general(keys_ref[h, 1], qh[:, half:], nt, preferred_element_type=F32)
        v1, i1 = _topk_rows(s1, PEER_TOPK)
        v2, i2 = _topk_rows(s2, PEER_TOPK)
        cv, ci = [], []
        for a in range(PEER_TOPK):
            nb = PEER_TOPK // (a + 1)
            cv.append(v1[a:a + 1, :] + v2[:nb, :])
            ci.append(i1[a:a + 1, :] * float(n_keys) + i2[:nb, :])
        ts, te = _topk_rows(jnp.concatenate(cv, axis=0), PEER_TOPK, payload=jnp.concatenate(ci, axis=0))
        ex = jnp.exp(ts - ts[0:1, :])
        e_sc[n] = te
        p_sc[n] = ex / jnp.sum(ex, axis=0, keepdims=True)
        return 0

    lax.fori_loop(0, n_heads * nsub, body, 0)
    for c in range(nsub):
        rows = slice(c * tw, (c + 1) * tw)
        e_all = jnp.concatenate([e_sc[h * nsub + c] for h in range(n_heads)], axis=0)
        p_all = jnp.concatenate([p_sc[h * nsub + c] for h in range(n_heads)], axis=0)
        eidx_ref[rows, :] = e_all.T.astype(I32)
        gate_ref[rows, :] = p_all.T


def _route(x1, g, wq, keys_bf):
    rows, dm = x1.shape
    n_heads, _, n_keys, half = keys_bf.shape
    nsel = n_heads * PEER_TOPK
    tb = _pick(rows, (256, 128))
    tw = min(tb, ROUTE_LANES)
    row = lambda w: pl.BlockSpec((tb, w), lambda i: (i, 0))
    return pl.pallas_call(
        _route_kernel,
        grid=(rows // tb,),
        in_specs=[row(dm), _const_spec(g.shape), _const_spec(wq.shape), _const_spec(keys_bf.shape)],
        out_specs=[row(dm), row(nsel), row(nsel)],
        out_shape=[
            jax.ShapeDtypeStruct((rows, dm), F32),
            jax.ShapeDtypeStruct((rows, nsel), I32),
            jax.ShapeDtypeStruct((rows, nsel), F32),
        ],
        scratch_shapes=[
            pltpu.VMEM((n_heads, tb, 2 * half), BF16),
            pltpu.VMEM((n_heads * (tb // tw), PEER_TOPK, tw), F32),
            pltpu.VMEM((n_heads * (tb // tw), PEER_TOPK, tw), F32),
        ],
        compiler_params=pltpu.CompilerParams(
            dimension_semantics=("arbitrary",), vmem_limit_bytes=VMEM_LIMIT),
        name="route",
    )(x1, g, wq, keys_bf)


PEER_BUFS = 16


def _peer_kernel(eidx_hbm, gate_ref, xn_ref, x1_ref, gf_ref, tab_hbm, y_ref,
                 idx_sm, idx_sem, *scratch, final_norm):
    i = pl.program_id(0)
    bufs = scratch[:PEER_BUFS]
    sem, peer_sc = scratch[PEER_BUFS:]
    tbk, nsel = gate_ref.shape
    half = tab_hbm.shape[1] // 2
    nlt = half // LANES

    icp = pltpu.make_async_copy(eidx_hbm.at[i], idx_sm, idx_sem)
    icp.start()
    icp.wait()

    def issue(t, s):
        base = t * nsel
        for k in range(nsel):
            e = idx_sm[base + k]
            pltpu.make_async_copy(tab_hbm.at[pl.ds(e, 1), :], bufs[s].at[pl.ds(k, 1), :],
                                  sem.at[s]).start(priority=k % 2)

    def wait(s):
        pltpu.make_async_copy(tab_hbm.at[pl.ds(0, nsel), :], bufs[s], sem.at[s]).wait()

    hi_mask = jnp.int32(-65536)
    eye = (lax.broadcasted_iota(I32, (nsel, nsel), 0) == lax.broadcasted_iota(I32, (nsel, nsel), 1))

    def unpack(w):
        return (lax.bitcast_convert_type(w & hi_mask, F32),
                lax.bitcast_convert_type(w << 16, F32))

    def compute(t, s):
        buf = bufs[s]
        xrow = xn_ref[pl.ds(t, 1), :]
        acc = jnp.zeros((nsel, LANES), F32)
        for j in range(nlt):
            uh, ul = unpack(buf[:, j * LANES:(j + 1) * LANES])
            acc = acc + uh * xrow[:, j * LANES:(j + 1) * LANES] \
                      + ul * xrow[:, half + j * LANES:half + (j + 1) * LANES]
        act = jnp.sum(acc, axis=1, keepdims=True)
        grow = jnp.broadcast_to(gate_ref[pl.ds(t, 1), :], (nsel, nsel))
        gcol = jnp.sum(jnp.where(eye, grow, 0.0), axis=1, keepdims=True)
        wgt = jnp.broadcast_to(gcol * jax.nn.gelu(act), (nsel, LANES))
        t8 = pl.multiple_of((t // SUBLANES) * SUBLANES, SUBLANES)
        rsel = lax.broadcasted_iota(I32, (SUBLANES, LANES), 0) == (t % SUBLANES)

        def put(lane0, row):
            cur = peer_sc[pl.ds(t8, SUBLANES), lane0:lane0 + LANES]
            peer_sc[pl.ds(t8, SUBLANES), lane0:lane0 + LANES] = jnp.where(
                rsel, jnp.broadcast_to(row, (SUBLANES, LANES)), cur)

        for j in range(nlt):
            vh, vl = unpack(buf[:, half + j * LANES:half + (j + 1) * LANES])
            put(j * LANES, jnp.sum(vh * wgt, axis=0, keepdims=True))
            put(half + j * LANES, jnp.sum(vl * wgt, axis=0, keepdims=True))

    def step(t, s, prefetch):
        wait(s)
        if prefetch:
            issue(t + PEER_BUFS - 1, (s + PEER_BUFS - 1) % PEER_BUFS)
        compute(t, s)

    peer_sc[...] = jnp.zeros_like(peer_sc)
    for t in range(PEER_BUFS - 1):
        issue(t, t)

    def group(gi, _):
        for s in range(PEER_BUFS):
            step(gi * PEER_BUFS + s, s, True)
        return 0

    lax.fori_loop(0, tbk // PEER_BUFS - 1, group, 0)
    for s in range(PEER_BUFS):
        t = tbk - PEER_BUFS + s
        step(t, s, t + PEER_BUFS - 1 < tbk)

    xo = x1_ref[...] + peer_sc[...]
    y_ref[...] = _rms(xo, gf_ref[...]) if final_norm else xo


def _peer(eidx, gate, xn, x1, gf, table, final_norm):
    rows, dm = x1.shape
    nsel = gate.shape[1]
    tbk = _pick(rows, (256, 128, 64, 32, 16, 8))
    assert tbk % PEER_BUFS == 0 and tbk >= 2 * PEER_BUFS
    nblk = rows // tbk
    eidx_blk = eidx.reshape(nblk, tbk * nsel)
    row = lambda w: pl.BlockSpec((tbk, w), lambda i: (i, 0))
    return pl.pallas_call(
        functools.partial(_peer_kernel, final_norm=final_norm),
        grid=(nblk,),
        in_specs=[pl.BlockSpec(memory_space=pl.ANY), row(nsel), row(dm), row(dm),
                  _const_spec(gf.shape), pl.BlockSpec(memory_space=pl.ANY)],
        out_specs=row(dm),
        out_shape=jax.ShapeDtypeStruct((rows, dm), F32),
        scratch_shapes=[
            pltpu.SMEM((tbk * nsel,), I32),
            pltpu.SemaphoreType.DMA(()),
        ] + [pltpu.VMEM((nsel, table.shape[1]), table.dtype) for _ in range(PEER_BUFS)] + [
            pltpu.SemaphoreType.DMA((PEER_BUFS,)),
            pltpu.VMEM((tbk, dm), F32),
        ],
        compiler_params=pltpu.CompilerParams(
            dimension_semantics=("arbitrary",), vmem_limit_bytes=VMEM_LIMIT),
        name="peer",
    )(eidx_blk, gate, xn, x1, gf, table)


SC_L = 16
SC_WINDOW = SC_L
SC_SUBCORES = 32
SC_CUTS = (0.125, 0.4375)
SC_SPLIT_ALIGN = 512


def _sc_peer(table, eidx, gate, xn):
    ntok, nsel = eidx.shape
    d = table.shape[1]
    half = d // 2
    nch = half // SC_L
    info = plsc.get_sparse_core_info()
    n_workers = info.num_cores * info.num_subcores
    n_win = nsel // SC_WINDOW
    tpw = ntok // n_workers
    assert tpw * n_workers == ntok and SC_WINDOW == SC_L and info.num_lanes == SC_L
    mesh = plsc.VectorSubcoreMesh(core_axis_name="core", subcore_axis_name="subcore")
    c0 = math.sqrt(2.0 / math.pi)

    @functools.partial(
        pl.kernel, out_type=jax.ShapeDtypeStruct((ntok, d), F32), mesh=mesh,
        scratch_types=[pltpu.VMEM((nsel,), I32), pltpu.VMEM((nsel,), F32), pltpu.VMEM((d,), F32),
                       pltpu.VMEM((2, SC_WINDOW, d), I32), pltpu.VMEM((d,), F32),
                       pltpu.SemaphoreType.DMA((2,))],
        compiler_params=pltpu.CompilerParams(needs_layout_passes=False),
        name="sc_peer")
    def k(tab_hbm, idx_hbm, gate_hbm, x_hbm, o_hbm, idx_v, gate_v, x_v, rows_v, out_v, gsem):
        wid = lax.axis_index("subcore") * info.num_cores + lax.axis_index("core")
        lane = lax.iota(I32, SC_L)
        zero = jnp.zeros((SC_L,), F32)

        def unpack(w):
            return (lax.bitcast_convert_type(w & jnp.int32(-65536), F32),
                    lax.bitcast_convert_type(w << 16, F32))

        @pl.loop(0, tpw)
        def _(tt):
            tok = wid * tpw + tt
            pltpu.sync_copy(idx_hbm.at[pl.ds(tok * nsel, nsel)], idx_v)
            pltpu.sync_copy(gate_hbm.at[pl.ds(tok * nsel, nsel)], gate_v)
            pltpu.sync_copy(x_hbm.at[tok], x_v)

            @pl.loop(0, d // SC_L)
            def _(c):
                out_v[pl.ds(c * SC_L, SC_L)] = zero

            def gather(w):
                return pltpu.make_async_copy(tab_hbm.at[idx_v.at[pl.ds(w * SC_WINDOW, SC_WINDOW)]],
                                             rows_v.at[w % 2], gsem.at[w % 2])

            gather(0).start()
            for w in range(n_win):
                gather(w).wait()
                if w + 1 < n_win:
                    gather(w + 1).start()
                slot = w % 2

                def dot_chunk(j, accs, slot=slot):
                    xa = x_v[pl.ds(j * SC_L, SC_L)]
                    xb = x_v[pl.ds(half + j * SC_L, SC_L)]
                    out = []
                    for r in range(SC_WINDOW):
                        uh, ul = unpack(rows_v[slot, r, pl.ds(j * SC_L, SC_L)])
                        out.append(accs[r] + (uh * xa + ul * xb))
                    return tuple(out)

                accs = lax.fori_loop(0, nch, dot_chunk, (zero,) * SC_WINDOW)
                act = zero
                for r in range(SC_WINDOW):
                    act = jnp.where(lane == r, jnp.sum(accs[r]), act)
                z = c0 * (act + 0.044715 * (act * act * act))
                tanh_z = 1.0 - 2.0 / (jnp.exp(2.0 * z) + 1.0)
                wgt = gate_v[pl.ds(w * SC_WINDOW, SC_WINDOW)] * (0.5 * act * (1.0 + tanh_z))
                splat = [jnp.full((SC_L,), jnp.sum(jnp.where(lane == r, wgt, 0.0)), F32)
                         for r in range(SC_WINDOW)]

                @pl.loop(0, nch)
                def _(j, slot=slot, splat=splat):
                    oh = zero
                    ol = zero
                    for r in range(SC_WINDOW):
                        vh, vl = unpack(rows_v[slot, r, pl.ds(half + j * SC_L, SC_L)])
                        oh = oh + splat[r] * vh
                        ol = ol + splat[r] * vl
                    out_v[pl.ds(j * SC_L, SC_L)] += oh
                    out_v[pl.ds(half + j * SC_L, SC_L)] += ol

            pltpu.sync_copy(out_v, o_hbm.at[tok])

    return k(table, eidx.reshape(-1), gate.reshape(-1), xn)


def _finish_kernel(x1_ref, p_ref, gf_ref, y_ref):
    y_ref[...] = _rms(x1_ref[...] + p_ref[...], gf_ref[...])


def _finish(x1, p, gf):
    rows, dm = x1.shape
    tm = _pick(rows, (512, 256, 128))
    row = pl.BlockSpec((tm, dm), lambda i: (i, 0))
    return pl.pallas_call(
        _finish_kernel, grid=(rows // tm,), in_specs=[row, row, _const_spec(gf.shape)], out_specs=row,
        out_shape=jax.ShapeDtypeStruct((rows, dm), F32),
        compiler_params=pltpu.CompilerParams(dimension_semantics=("arbitrary",), vmem_limit_bytes=VMEM_LIMIT),
        name="finish",
    )(x1, p, gf)


def _cmul(ar, ai, br, bi):
    return ar * br - ai * bi, ar * bi + ai * br


def _s5_tables(lam_re, lam_im, log_dt, b_re, b_im, c_re, c_im, d, glu_w, glu_b):
    g, p = lam_re.shape
    hg = b_re.shape[-1]
    gs = g // SSM_SLABS
    dt = jnp.exp(log_dt.astype(F32))[:, None]
    lr = lam_re.astype(F32)
    li = lam_im.astype(F32)
    mag = jnp.exp(lr * dt)
    ab_re = mag * jnp.cos(li * dt)
    ab_im = mag * jnp.sin(li * dt)
    den = lr * lr + li * li
    nr = ab_re - 1.0
    ni = ab_im
    coef_re = (nr * lr + ni * li) / den
    coef_im = (ni * lr - nr * li) / den
    br = b_re.astype(F32)
    bi = b_im.astype(F32)
    bb_re = coef_re[..., None] * br - coef_im[..., None] * bi
    bb_im = coef_re[..., None] * bi + coef_im[..., None] * br
    eye = jnp.eye(gs, dtype=F32)

    def b_slab(bb):
        t = bb.reshape(SSM_SLABS, gs, p, hg).transpose(0, 1, 3, 2)
        return jnp.einsum('sihp,ij->sihjp', t, eye).reshape(SSM_SLABS, gs * hg, gs * p)

    bs = jnp.concatenate([b_slab(bb_re), b_slab(bb_im)], axis=2).astype(BF16)

    def c_slab(c):
        t = c.astype(F32).reshape(SSM_SLABS, gs, hg, p).transpose(0, 1, 3, 2)
        return jnp.einsum('siph,ij->sipjh', t, eye).reshape(SSM_SLABS, gs * p, gs * hg)

    cs = jnp.concatenate([c_slab(c_re), -c_slab(c_im)], axis=1).astype(BF16)

    a1 = (ab_re.reshape(1, g * p), ab_im.reshape(1, g * p))
    a2 = _cmul(*a1, *a1)
    a4 = _cmul(*a2, *a2)
    rowid = jnp.arange(SUBLANES)[:, None]
    mtab = jnp.stack([jnp.where(rowid >= s, comp, 0.0)
                      for s, a in ((1, a1), (2, a2), (4, a4)) for comp in a])
    pows = [a1]
    for _ in range(SUBLANES - 1):
        pows.append(_cmul(*pows[-1], *a1))
    ptab = jnp.stack([jnp.concatenate([q[0] for q in pows], axis=0),
                      jnp.concatenate([q[1] for q in pows], axis=0)])
    return (bs, cs, d.astype(F32).reshape(1, -1), glu_w.astype(BF16),
            glu_b.astype(F32).reshape(1, -1), mtab, ptab)


def _sgu_tables(w_s, b_s, seq_len):
    n_heads = w_s.shape[0]
    ln = min(SGU_CHUNK, seq_len)
    assert SGU_CHUNK % ln == 0 and seq_len % ln == 0
    rep = SGU_CHUNK // ln
    mask = jnp.tril(jnp.ones((ln, ln), dtype=bool))
    w = jnp.where(mask[None], w_s[:, :ln, :ln], 0.0).astype(F32)
    ws = jnp.einsum('hij,ab->haibj', w, jnp.eye(rep, dtype=F32)).reshape(n_heads, SGU_CHUNK, SGU_CHUNK)
    bias = jnp.tile(jnp.transpose(b_s[:, :ln]).astype(F32), (rep, 1))
    return ws.astype(BF16), bias


def _pack_kernel(u_ref, v_ref, o_ref):
    half = u_ref.shape[1] // 2

    def bf16_bits(x):
        return lax.bitcast_convert_type(x.astype(BF16).astype(F32), U32)

    def pack(t_ref):
        w = bf16_bits(t_ref[:, :half]) | (bf16_bits(t_ref[:, half:]) >> 16)
        return lax.bitcast_convert_type(w, I32)

    o_ref[:, :half] = pack(u_ref)
    o_ref[:, half:] = pack(v_ref)


def _pack_tables(u, v):
    n, dd = u.shape
    tr = _pick(n, (PACK_ROWS, 256, 128, 8))
    spec = pl.BlockSpec((tr, dd), lambda i: (i, 0))
    return pl.pallas_call(
        _pack_kernel,
        grid=(n // tr,),
        in_specs=[spec, spec],
        out_specs=spec,
        out_shape=jax.ShapeDtypeStruct((n, dd), I32),
        compiler_params=pltpu.CompilerParams(
            dimension_semantics=("arbitrary",), vmem_limit_bytes=VMEM_LIMIT),
        name="pack",
    )(u, v)


def _trunk_to_route(x, h0r, h0i, seq_len, lp):
    (norm_mix_g, w_in_bf, s5_tabs, ln_g, ln_b, sgu_w, sgu_b, wa, wb, wo, norm_ffn_g, wq, keys_bf, _,
     d_ssm, d_sgu) = lp
    u, su, vn, gates = _inproj(x, norm_mix_g, w_in_bf, ln_g, ln_b, d_ssm, d_sgu)
    ya, hlr, hli = _s5(u, h0r, h0i, s5_tabs, seq_len)
    ws, bias = _sgu_tables(sgu_w, sgu_b, seq_len)
    hd = d_sgu // ws.shape[0]
    bias_tile = jnp.repeat(bias, hd, axis=1)
    x1 = _mix(x, ya, su, vn, gates, ws, bias_tile, wa, wb, wo)
    xn, eidx, gate = _route(x1, norm_ffn_g, wq, keys_bf)
    return (x1, xn, eidx, gate), hlr, hli, vn


def _trunk_layer(x, h0r, h0i, seq_len, lp, final_g):
    (x1, xn, eidx, gate), hlr, hli, vn = _trunk_to_route(x, h0r, h0i, seq_len, lp)
    y = _peer(eidx, gate, xn, x1, final_g, lp[13], final_norm=True)
    return y, hlr, hli, vn


def _prompt_split(x, seq_len, lp, final_g, n_state):
    rows = x.shape[0]
    cuts = [(int(rows * c) // SC_SPLIT_ALIGN) * SC_SPLIT_ALIGN for c in SC_CUTS]
    bounds = [0] + cuts + [rows]
    hr = hi = jnp.zeros((1, n_state), F32)
    parts = []
    for a, b in zip(bounds[:-1], bounds[1:]):
        part, hr, hi, _ = _trunk_to_route(x[a:b], hr, hi, seq_len, lp)
        parts.append(part)
    x1a, xna, ea, ga = parts[-1]
    ya = _peer(ea, ga, xna, x1a, final_g, lp[13], final_norm=True)
    return parts[:-1], ya, hr, hi


def _prompt_finish(firsts, ya, lp, final_g):
    ys = [_finish(x1, _sc_peer(lp[13], e, g, xn), final_g) for (x1, xn, e, g) in firsts]
    ys = ys if ya is None else ys + [ya]
    return ys[0] if len(ys) == 1 else jnp.concatenate(ys, axis=0)


def kernel(x_prompt, x_sample, state_ssm_re, state_ssm_im, norm_mix_g, w_in, ssm_lambda_re, ssm_lambda_im, ssm_log_dt, ssm_b_re, ssm_b_im, ssm_c_re, ssm_c_im, ssm_d, ssm_glu_w, ssm_glu_b, sgu_ln_g, sgu_ln_b, sgu_w, sgu_b, w_branch_a, w_branch_b, w_out, norm_ffn_g, peer_w_q, peer_keys, peer_u, peer_v, norm_final_g):
    depth = w_in.shape[0]
    assert depth == 1, "the final norm is fused into the last layer's PEER kernel"
    bp, sp, dm = x_prompt.shape
    bs_, ss, _ = x_sample.shape
    g, p = ssm_lambda_re.shape[1:]
    d_ssm = ssm_d.shape[1]
    d_sgu = sgu_ln_g.shape[1]
    n_state = g * p
    l = 0
    lp = (norm_mix_g[l].reshape(1, dm), w_in[l].astype(BF16),
          _s5_tables(ssm_lambda_re[l], ssm_lambda_im[l], ssm_log_dt[l], ssm_b_re[l], ssm_b_im[l],
                     ssm_c_re[l], ssm_c_im[l], ssm_d[l], ssm_glu_w[l], ssm_glu_b[l]),
          sgu_ln_g[l].reshape(1, d_sgu), sgu_ln_b[l].reshape(1, d_sgu), sgu_w[l], sgu_b[l],
          w_branch_a[l].astype(BF16), w_branch_b[l].astype(BF16), w_out[l].astype(BF16),
          norm_ffn_g[l].reshape(1, dm), peer_w_q[l].astype(BF16), peer_keys[l].astype(BF16),
          _pack_tables(peer_u[l].astype(F32), peer_v[l].astype(F32)),
          d_ssm, d_sgu)
    gf = norm_final_g.reshape(1, dm)

    xp2 = x_prompt.reshape(bp * sp, dm)
    xs2 = x_sample.reshape(bs_ * ss, dm)
    h0s = (state_ssm_re[l].astype(F32).reshape(bs_, n_state), state_ssm_im[l].astype(F32).reshape(bs_, n_state))
    split = bp == 1 and sp >= 4 * SC_SPLIT_ALIGN and (bs_ * ss) % SC_SUBCORES == 0
    if split:
        sample, hr_s, hi_s, v_s = _trunk_to_route(xs2, h0s[0], h0s[1], ss, lp)
        first, ya, hr_p, hi_p = _prompt_split(xp2, sp, lp, gf, n_state)
        ys = _prompt_finish([sample], None, lp, gf)
        yp = _prompt_finish(first, ya, lp, gf)
    else:
        zeros = jnp.zeros((bp, n_state), F32)
        yp, hr_p, hi_p, _ = _trunk_layer(xp2, zeros, zeros, sp, lp, gf)
        ys, hr_s, hi_s, v_s = _trunk_layer(xs2, h0s[0], h0s[1], ss, lp, gf)
    return (yp.reshape(bp, sp, dm), ys.reshape(bs_, ss, dm),
            hr_p.reshape(1, bp, g, p), hi_p.reshape(1, bp, g, p),
            hr_s.reshape(1, bs_, g, p), hi_s.reshape(1, bs_, g, p),
            v_s.reshape(1, bs_, ss, d_sgu))
```

```python
import functools
import math

import jax
import jax.numpy as jnp
from jax import lax
from jax.experimental import pallas as pl
from jax.experimental.pallas import tpu as pltpu
from jax.experimental.pallas import tpu_sc as plsc

F32 = jnp.float32
BF16 = jnp.bfloat16
I32 = jnp.int32
U32 = jnp.uint32
EPS = 1e-6

LANES = 128
SUBLANES = 8
VMEM_LIMIT = 56 * 1024 * 1024

SSM_SLABS = 4
SCAN_LANES = 512
PEER_TOPK = 16
SGU_CHUNK = 128
ROUTE_LANES = 256
INPROJ_SUB = 128
PACK_ROWS = 512


def _pick(n, candidates):
    for c in candidates:
        if n % c == 0:
            return c
    raise ValueError(f"no block size for {n} in {candidates}")


def _const_spec(shape):
    nd = len(shape)
    return pl.BlockSpec(shape, lambda *_: (0,) * nd, pipeline_mode=pl.Buffered(1))


def _rms(x, g):
    return x * lax.rsqrt(jnp.mean(x * x, axis=-1, keepdims=True) + EPS) * g


def _inproj_kernel(x_ref, g_ref, w_ref, lng_ref, lnb_ref, u_ref, su_ref, vn_ref, gates_ref, xn_sc):
    j = pl.program_id(1)
    tm = x_ref.shape[0]
    sub = min(tm, INPROJ_SUB)

    @pl.when(j == 0)
    def _():
        xn_sc[...] = _rms(x_ref[...], g_ref[...]).astype(BF16)

    def section(out_ref, epilogue):
        for r in range(tm // sub):
            rows = slice(r * sub, (r + 1) * sub)
            z = jnp.dot(xn_sc[rows, :], w_ref[...], preferred_element_type=F32)
            out_ref[rows, :] = epilogue(z).astype(out_ref.dtype)

    def layer_norm_gelu(z):
        v = jax.nn.gelu(z)
        mu = jnp.mean(v, axis=-1, keepdims=True)
        vc = v - mu
        y = vc * lax.rsqrt(jnp.mean(vc * vc, axis=-1, keepdims=True) + EPS)
        return y * lng_ref[...] + lnb_ref[...]

    @pl.when(j == 0)
    def _():
        section(u_ref, lambda z: z)

    @pl.when(j == 1)
    def _():
        section(su_ref, jax.nn.gelu)

    @pl.when(j == 2)
    def _():
        section(vn_ref, layer_norm_gelu)

    @pl.when(j >= 3)
    def _():
        section(gates_ref, jax.nn.sigmoid)


def _inproj(x, g, w_in_bf, ln_g, ln_b, d_ssm, d_sgu):
    rows, dm = x.shape
    d_in = w_in_bf.shape[1]
    tn = d_ssm
    assert d_sgu == tn and (d_in - 3 * tn) % tn == 0
    nj = d_in // tn
    tm = _pick(rows, (1024, 512, 256, 128))
    return pl.pallas_call(
        _inproj_kernel,
        grid=(rows // tm, nj),
        in_specs=[
            pl.BlockSpec((tm, dm), lambda i, j: (i, 0)),
            pl.BlockSpec((1, dm), lambda i, j: (0, 0)),
            pl.BlockSpec((dm, tn), lambda i, j: (0, j)),
            pl.BlockSpec((1, tn), lambda i, j: (0, 0)),
            pl.BlockSpec((1, tn), lambda i, j: (0, 0)),
        ],
        out_specs=[
            pl.BlockSpec((tm, tn), lambda i, j: (i, 0)),
            pl.BlockSpec((tm, tn), lambda i, j: (i, 0)),
            pl.BlockSpec((tm, tn), lambda i, j: (i, 0)),
            pl.BlockSpec((tm, tn), lambda i, j: (i, jnp.maximum(j - 3, 0))),
        ],
        out_shape=[
            jax.ShapeDtypeStruct((rows, tn), F32),
            jax.ShapeDtypeStruct((rows, tn), BF16),
            jax.ShapeDtypeStruct((rows, tn), F32),
            jax.ShapeDtypeStruct((rows, d_in - 3 * tn), BF16),
        ],
        scratch_shapes=[pltpu.VMEM((tm, dm), BF16)],
        compiler_params=pltpu.CompilerParams(
            dimension_semantics=("arbitrary", "arbitrary"), vmem_limit_bytes=VMEM_LIMIT),
        name="inproj",
    )(x, g, w_in_bf, ln_g, ln_b)


def _s5_kernel(u_ref, h0r_ref, h0i_ref, bs_ref, cs_ref, d_ref, gluw_ref, glub_ref, m_ref, pw_ref,
               ya_ref, hlr_ref, hli_ref, hr_sc, hi_sc, cr_sc, ci_sc, *, seq_len):
    i = pl.program_id(0)
    tl, d_ssm = u_ref.shape
    n_state = hr_sc.shape[1]
    kin = d_ssm // SSM_SLABS
    kst = n_state // SSM_SLABS

    @pl.when(i == 0)
    def _():
        cr_sc[...] = jnp.zeros_like(cr_sc)
        ci_sc[...] = jnp.zeros_like(ci_sc)

    ub = u_ref[...].astype(BF16)
    for s in range(SSM_SLABS):
        r = jnp.dot(ub[:, kin * s:kin * (s + 1)], bs_ref[s], preferred_element_type=F32)
        hr_sc[:, kst * s:kst * (s + 1)] = r[:, :kst]
        hi_sc[:, kst * s:kst * (s + 1)] = r[:, kst:]

    n_tiles = tl // SUBLANES
    for lc in range(n_state // SCAN_LANES):
        ls = slice(lc * SCAN_LANES, (lc + 1) * SCAN_LANES)

        def tile_body(t, carry, ls=ls):
            cr, ci = carry
            row0 = i * tl + t * SUBLANES
            b = row0 // seq_len
            is_start = (row0 % seq_len) == 0
            h0r = jnp.broadcast_to(h0r_ref[pl.ds(b, 1), ls], (SUBLANES, SCAN_LANES))
            h0i = jnp.broadcast_to(h0i_ref[pl.ds(b, 1), ls], (SUBLANES, SCAN_LANES))
            cr = jnp.where(is_start, h0r, cr)
            ci = jnp.where(is_start, h0i, ci)
            r0 = pl.multiple_of(t * SUBLANES, SUBLANES)
            br = hr_sc[pl.ds(r0, SUBLANES), ls]
            bi = hi_sc[pl.ds(r0, SUBLANES), ls]
            for k, shift in enumerate((1, 2, 4)):
                mr = m_ref[2 * k, :, ls]
                mi = m_ref[2 * k + 1, :, ls]
                sr = pltpu.roll(br, shift, 0)
                si = pltpu.roll(bi, shift, 0)
                br, bi = br + (mr * sr - mi * si), bi + (mr * si + mi * sr)
            pr = pw_ref[0, :, ls]
            pi_ = pw_ref[1, :, ls]
            hr = br + (pr * cr - pi_ * ci)
            hi = bi + (pr * ci + pi_ * cr)
            hr_sc[pl.ds(r0, SUBLANES), ls] = hr
            hi_sc[pl.ds(r0, SUBLANES), ls] = hi
            last_r = hr[SUBLANES - 1:SUBLANES, :]
            last_i = hi[SUBLANES - 1:SUBLANES, :]
            hlr_ref[pl.ds(b, 1), ls] = last_r
            hli_ref[pl.ds(b, 1), ls] = last_i
            return (jnp.broadcast_to(last_r, (SUBLANES, SCAN_LANES)),
                    jnp.broadcast_to(last_i, (SUBLANES, SCAN_LANES)))

        cr, ci = lax.fori_loop(0, n_tiles, tile_body, (cr_sc[:, ls], ci_sc[:, ls]))
        cr_sc[:, ls] = cr
        ci_sc[:, ls] = ci

    ys = []
    for s in range(SSM_SLABS):
        hre = hr_sc[:, kst * s:kst * (s + 1)].astype(BF16)
        him = hi_sc[:, kst * s:kst * (s + 1)].astype(BF16)
        ys.append(jnp.dot(hre, cs_ref[s, :kst, :], preferred_element_type=F32)
                  + jnp.dot(him, cs_ref[s, kst:, :], preferred_element_type=F32))
    y = jnp.concatenate(ys, axis=1)
    y = jax.nn.gelu(y + d_ref[...] * u_ref[...])
    gt = jnp.dot(y.astype(BF16), gluw_ref[...], preferred_element_type=F32) + glub_ref[...]
    ya_ref[...] = (y * jax.nn.sigmoid(gt)).astype(BF16)


def _s5(u, h0r, h0i, tabs, seq_len):
    rows, d_ssm = u.shape
    nb, n_state = h0r.shape
    tl = _pick(rows, (256, 128, 64, 32, 16, 8)) if seq_len >= 256 else _pick(rows, (512, 256, 128, 64, 32, 16))
    assert (seq_len % tl == 0 or tl % seq_len == 0) and seq_len % SUBLANES == 0
    bs, cs, d, gluw, glub, mtab, ptab = tabs
    return pl.pallas_call(
        functools.partial(_s5_kernel, seq_len=seq_len),
        grid=(rows // tl,),
        in_specs=[
            pl.BlockSpec((tl, d_ssm), lambda i: (i, 0)),
            _const_spec((nb, n_state)),
            _const_spec((nb, n_state)),
            _const_spec(bs.shape),
            _const_spec(cs.shape),
            _const_spec(d.shape),
            _const_spec(gluw.shape),
            _const_spec(glub.shape),
            _const_spec(mtab.shape),
            _const_spec(ptab.shape),
        ],
        out_specs=[
            pl.BlockSpec((tl, d_ssm), lambda i: (i, 0)),
            pl.BlockSpec((nb, n_state), lambda i: (0, 0)),
            pl.BlockSpec((nb, n_state), lambda i: (0, 0)),
        ],
        out_shape=[
            jax.ShapeDtypeStruct((rows, d_ssm), BF16),
            jax.ShapeDtypeStruct((nb, n_state), F32),
            jax.ShapeDtypeStruct((nb, n_state), F32),
        ],
        scratch_shapes=[
            pltpu.VMEM((tl, n_state), F32),
            pltpu.VMEM((tl, n_state), F32),
            pltpu.VMEM((SUBLANES, n_state), F32),
            pltpu.VMEM((SUBLANES, n_state), F32),
        ],
        compiler_params=pltpu.CompilerParams(
            dimension_semantics=("arbitrary",), vmem_limit_bytes=VMEM_LIMIT),
        name="s5",
    )(u, h0r, h0i, bs, cs, d, gluw, glub, mtab, ptab)


def _mix_kernel(x_ref, ya_ref, su_ref, vn_ref, gates_ref, ws_ref, bias_ref, wa_ref, wb_ref, wo_ref, x1_ref):
    tm, dm = x_ref.shape
    n_heads = ws_ref.shape[0]
    hd = vn_ref.shape[1] // n_heads
    vb = vn_ref[...].astype(BF16)
    chunks = []
    for c in range(tm // SGU_CHUNK):
        heads = [jnp.dot(ws_ref[h], vb[c * SGU_CHUNK:(c + 1) * SGU_CHUNK, h * hd:(h + 1) * hd],
                         preferred_element_type=F32) for h in range(n_heads)]
        chunks.append(jnp.concatenate(heads, axis=1) + bias_ref[...])
    s_all = jnp.concatenate(chunks, axis=0)
    yb = (su_ref[...].astype(F32) * s_all).astype(BF16)
    pa = jnp.dot(ya_ref[...], wa_ref[...], preferred_element_type=F32)
    pb = jnp.dot(yb, wb_ref[...], preferred_element_type=F32)
    merged = gates_ref[:, :dm].astype(F32) * pa + gates_ref[:, dm:].astype(F32) * pb
    x1_ref[...] = x_ref[...] + jnp.dot(merged.astype(BF16), wo_ref[...], preferred_element_type=F32)


def _mix(x, ya, su, vn, gates, ws, bias, wa, wb, wo):
    rows, dm = x.shape
    d_ssm = ya.shape[1]
    d_sgu = su.shape[1]
    tm = _pick(rows, (256, 128))
    row = lambda w: pl.BlockSpec((tm, w), lambda i: (i, 0))
    return pl.pallas_call(
        _mix_kernel,
        grid=(rows // tm,),
        in_specs=[row(dm), row(d_ssm), row(d_sgu), row(d_sgu), row(2 * dm),
                  _const_spec(ws.shape), _const_spec(bias.shape),
                  _const_spec(wa.shape), _const_spec(wb.shape), _const_spec(wo.shape)],
        out_specs=row(dm),
        out_shape=jax.ShapeDtypeStruct((rows, dm), F32),
        compiler_params=pltpu.CompilerParams(
            dimension_semantics=("arbitrary",), vmem_limit_bytes=VMEM_LIMIT),
        name="mix",
    )(x, ya, su, vn, gates, ws, bias, wa, wb, wo)


def _topk_rows(s, k, payload=None):
    rows = s.shape[0]
    iota = lax.broadcasted_iota(I32, s.shape, 0).astype(F32)
    vals, outs = [], []
    for _ in range(k):
        m = jnp.max(s, axis=0, keepdims=True)
        j = jnp.min(jnp.where(s == m, iota, float(rows)), axis=0, keepdims=True)
        sel = iota == j
        vals.append(m)
        outs.append(j if payload is None else jnp.max(jnp.where(sel, payload, -1.0), axis=0, keepdims=True))
        s = jnp.where(sel, -jnp.inf, s)
    return jnp.concatenate(vals, axis=0), jnp.concatenate(outs, axis=0)


def _route_kernel(x1_ref, g_ref, wq_ref, keys_ref, xn_ref, eidx_ref, gate_ref, q_sc, e_sc, p_sc):
    tb = x1_ref.shape[0]
    tw = e_sc.shape[2]
    nsub = tb // tw
    n_heads, _, n_keys, half = keys_ref.shape
    dkey = 2 * half
    xn = _rms(x1_ref[...], g_ref[...])
    xn_ref[...] = xn
    q = jnp.dot(xn.astype(BF16), wq_ref[...], preferred_element_type=F32).astype(BF16)
    for h in range(n_heads):
        q_sc[h] = q[:, h * dkey:(h + 1) * dkey]

    nt = (((1,), (1,)), ((), ()))

    def body(n, _):
        h = n // nsub
        r0 = pl.multiple_of((n % nsub) * tw, tw)
        qh = q_sc[h, pl.ds(r0, tw), :]
        s1 = lax.dot_general(keys_ref[h, 0], qh[:, :half], nt, preferred_element_type=F32)
        s2 = lax.dot_general(keys_ref[h, 1], qh[:, half:], nt, preferred_element_type=F32)
        v1, i1 = _topk_rows(s1, PEER_TOPK)
        v2, i2 = _topk_rows(s2, PEER_TOPK)
        cv, ci = [], []
        for a in range(PEER_TOPK):
            nb = PEER_TOPK // (a + 1)
            cv.append(v1[a:a + 1, :] + v2[:nb, :])
            ci.append(i1[a:a + 1, :] * float(n_keys) + i2[:nb, :])
        ts, te = _topk_rows(jnp.concatenate(cv, axis=0), PEER_TOPK, payload=jnp.concatenate(ci, axis=0))
        ex = jnp.exp(ts - ts[0:1, :])
        e_sc[n] = te
        p_sc[n] = ex / jnp.sum(ex, axis=0, keepdims=True)
        return 0

    lax.fori_loop(0, n_heads * nsub, body, 0)
    for c in range(nsub):
        rows = slice(c * tw, (c + 1) * tw)
        e_all = jnp.concatenate([e_sc[h * nsub + c] for h in range(n_heads)], axis=0)
        p_all = jnp.concatenate([p_sc[h * nsub + c] for h in range(n_heads)], axis=0)
        eidx_ref[rows, :] = e_all.T.astype(I32)
        gate_ref[rows, :] = p_all.T


def _route(x1, g, wq, keys_bf):
    rows, dm = x1.shape
    n_heads, _, n_keys, half = keys_bf.shape
    nsel = n_heads * PEER_TOPK
    tb = _pick(rows, (256, 128))
    tw = min(tb, ROUTE_LANES)
    row = lambda w: pl.BlockSpec((tb, w), lambda i: (i, 0))
    return pl.pallas_call(
        _route_kernel,
        grid=(rows // tb,),
        in_specs=[row(dm), _const_spec(g.shape), _const_spec(wq.shape), _const_spec(keys_bf.shape)],
        out_specs=[row(dm), row(nsel), row(nsel)],
        out_shape=[
            jax.ShapeDtypeStruct((rows, dm), F32),
            jax.ShapeDtypeStruct((rows, nsel), I32),
            jax.ShapeDtypeStruct((rows, nsel), F32),
        ],
        scratch_shapes=[
            pltpu.VMEM((n_heads, tb, 2 * half), BF16),
            pltpu.VMEM((n_heads * (tb // tw), PEER_TOPK, tw), F32),
            pltpu.VMEM((n_heads * (tb // tw), PEER_TOPK, tw), F32),
        ],
        compiler_params=pltpu.CompilerParams(
            dimension_semantics=("arbitrary",), vmem_limit_bytes=VMEM_LIMIT),
        name="route",
    )(x1, g, wq, keys_bf)


PEER_BUFS = 16


def _peer_kernel(eidx_hbm, gate_ref, xn_ref, x1_ref, gf_ref, tab_hbm, y_ref,
                 idx_sm, idx_sem, *scratch, final_norm):
    i = pl.program_id(0)
    bufs = scratch[:PEER_BUFS]
    sem, peer_sc = scratch[PEER_BUFS:]
    tbk, nsel = gate_ref.shape
    half = tab_hbm.shape[1] // 2
    nlt = half // LANES

    icp = pltpu.make_async_copy(eidx_hbm.at[i], idx_sm, idx_sem)
    icp.start()
    icp.wait()

    def issue(t, s):
        base = t * nsel
        for k in range(nsel):
            e = idx_sm[base + k]
            pltpu.make_async_copy(tab_hbm.at[pl.ds(e, 1), :], bufs[s].at[pl.ds(k, 1), :],
                                  sem.at[s]).start(priority=k % 2)

    def wait(s):
        pltpu.make_async_copy(tab_hbm.at[pl.ds(0, nsel), :], bufs[s], sem.at[s]).wait()

    hi_mask = jnp.int32(-65536)
    eye = (lax.broadcasted_iota(I32, (nsel, nsel), 0) == lax.broadcasted_iota(I32, (nsel, nsel), 1))

    def unpack(w):
        return (lax.bitcast_convert_type(w & hi_mask, F32),
                lax.bitcast_convert_type(w << 16, F32))

    def compute(t, s):
        buf = bufs[s]
        xrow = xn_ref[pl.ds(t, 1), :]
        acc = jnp.zeros((nsel, LANES), F32)
        for j in range(nlt):
            uh, ul = unpack(buf[:, j * LANES:(j + 1) * LANES])
            acc = acc + uh * xrow[:, j * LANES:(j + 1) * LANES] \
                      + ul * xrow[:, half + j * LANES:half + (j + 1) * LANES]
        act = jnp.sum(acc, axis=1, keepdims=True)
        grow = jnp.broadcast_to(gate_ref[pl.ds(t, 1), :], (nsel, nsel))
        gcol = jnp.sum(jnp.where(eye, grow, 0.0), axis=1, keepdims=True)
        wgt = jnp.broadcast_to(gcol * jax.nn.gelu(act), (nsel, LANES))
        t8 = pl.multiple_of((t // SUBLANES) * SUBLANES, SUBLANES)
        rsel = lax.broadcasted_iota(I32, (SUBLANES, LANES), 0) == (t % SUBLANES)

        def put(lane0, row):
            cur = peer_sc[pl.ds(t8, SUBLANES), lane0:lane0 + LANES]
            peer_sc[pl.ds(t8, SUBLANES), lane0:lane0 + LANES] = jnp.where(
                rsel, jnp.broadcast_to(row, (SUBLANES, LANES)), cur)

        for j in range(nlt):
            vh, vl = unpack(buf[:, half + j * LANES:half + (j + 1) * LANES])
            put(j * LANES, jnp.sum(vh * wgt, axis=0, keepdims=True))
            put(half + j * LANES, jnp.sum(vl * wgt, axis=0, keepdims=True))

    def step(t, s, prefetch):
        wait(s)
        if prefetch:
            issue(t + PEER_BUFS - 1, (s + PEER_BUFS - 1) % PEER_BUFS)
        compute(t, s)

    peer_sc[...] = jnp.zeros_like(peer_sc)
    for t in range(PEER_BUFS - 1):
        issue(t, t)

    def group(gi, _):
        for s in range(PEER_BUFS):
            step(gi * PEER_BUFS + s, s, True)
        return 0

    lax.fori_loop(0, tbk // PEER_BUFS - 1, group, 0)
    for s in range(PEER_BUFS):
        t = tbk - PEER_BUFS + s
        step(t, s, t + PEER_BUFS - 1 < tbk)

    xo = x1_ref[...] + peer_sc[...]
    y_ref[...] = _rms(xo, gf_ref[...]) if final_norm else xo


def _peer(eidx, gate, xn, x1, gf, table, final_norm):
    rows, dm = x1.shape
    nsel = gate.shape[1]
    tbk = _pick(rows, (256, 128, 64, 32, 16, 8))
    assert tbk % PEER_BUFS == 0 and tbk >= 2 * PEER_BUFS
    nblk = rows // tbk
    eidx_blk = eidx.reshape(nblk, tbk * nsel)
    row = lambda w: pl.BlockSpec((tbk, w), lambda i: (i, 0))
    return pl.pallas_call(
        functools.partial(_peer_kernel, final_norm=final_norm),
        grid=(nblk,),
        in_specs=[pl.BlockSpec(memory_space=pl.ANY), row(nsel), row(dm), row(dm),
                  _const_spec(gf.shape), pl.BlockSpec(memory_space=pl.ANY)],
        out_specs=row(dm),
        out_shape=jax.ShapeDtypeStruct((rows, dm), F32),
        scratch_shapes=[
            pltpu.SMEM((tbk * nsel,), I32),
            pltpu.SemaphoreType.DMA(()),
        ] + [pltpu.VMEM((nsel, table.shape[1]), table.dtype) for _ in range(PEER_BUFS)] + [
            pltpu.SemaphoreType.DMA((PEER_BUFS,)),
            pltpu.VMEM((tbk, dm), F32),
        ],
        compiler_params=pltpu.CompilerParams(
            dimension_semantics=("arbitrary",), vmem_limit_bytes=VMEM_LIMIT),
        name="peer",
    )(eidx_blk, gate, xn, x1, gf, table)


SC_L = 16
SC_WINDOW = SC_L
SC_SUBCORES = 32
SC_CUTS = (0.125, 0.4375)
SC_SPLIT_ALIGN = 512


def _sc_peer(table, eidx, gate, xn):
    ntok, nsel = eidx.shape
    d = table.shape[1]
    half = d // 2
    nch = half // SC_L
    info = plsc.get_sparse_core_info()
    n_workers = info.num_cores * info.num_subcores
    n_win = nsel // SC_WINDOW
    tpw = ntok // n_workers
    assert tpw * n_workers == ntok and SC_WINDOW == SC_L and info.num_lanes == SC_L
    mesh = plsc.VectorSubcoreMesh(core_axis_name="core", subcore_axis_name="subcore")
    c0 = math.sqrt(2.0 / math.pi)

    @functools.partial(
        pl.kernel, out_type=jax.ShapeDtypeStruct((ntok, d), F32), mesh=mesh,
        scratch_types=[pltpu.VMEM((nsel,), I32), pltpu.VMEM((nsel,), F32), pltpu.VMEM((d,), F32),
                       pltpu.VMEM((2, SC_WINDOW, d), I32), pltpu.VMEM((d,), F32),
                       pltpu.SemaphoreType.DMA((2,))],
        compiler_params=pltpu.CompilerParams(needs_layout_passes=False),
        name="sc_peer")
    def k(tab_hbm, idx_hbm, gate_hbm, x_hbm, o_hbm, idx_v, gate_v, x_v, rows_v, out_v, gsem):
        wid = lax.axis_index("subcore") * info.num_cores + lax.axis_index("core")
        lane = lax.iota(I32, SC_L)
        zero = jnp.zeros((SC_L,), F32)

        def unpack(w):
            return (lax.bitcast_convert_type(w & jnp.int32(-65536), F32),
                    lax.bitcast_convert_type(w << 16, F32))

        @pl.loop(0, tpw)
        def _(tt):
            tok = wid * tpw + tt
            pltpu.sync_copy(idx_hbm.at[pl.ds(tok * nsel, nsel)], idx_v)
            pltpu.sync_copy(gate_hbm.at[pl.ds(tok * nsel, nsel)], gate_v)
            pltpu.sync_copy(x_hbm.at[tok], x_v)

            @pl.loop(0, d // SC_L)
            def _(c):
                out_v[pl.ds(c * SC_L, SC_L)] = zero

            def gather(w):
                return pltpu.make_async_copy(tab_hbm.at[idx_v.at[pl.ds(w * SC_WINDOW, SC_WINDOW)]],
                                             rows_v.at[w % 2], gsem.at[w % 2])

            gather(0).start()
            for w in range(n_win):
                gather(w).wait()
                if w + 1 < n_win:
                    gather(w + 1).start()
                slot = w % 2

                def dot_chunk(j, accs, slot=slot):
                    xa = x_v[pl.ds(j * SC_L, SC_L)]
                    xb = x_v[pl.ds(half + j * SC_L, SC_L)]
                    out = []
                    for r in range(SC_WINDOW):
                        uh, ul = unpack(rows_v[slot, r, pl.ds(j * SC_L, SC_L)])
                        out.append(accs[r] + (uh * xa + ul * xb))
                    return tuple(out)

                accs = lax.fori_loop(0, nch, dot_chunk, (zero,) * SC_WINDOW)
                act = zero
                for r in range(SC_WINDOW):
                    act = jnp.where(lane == r, jnp.sum(accs[r]), act)
                z = c0 * (act + 0.044715 * (act * act * act))
                tanh_z = 1.0 - 2.0 / (jnp.exp(2.0 * z) + 1.0)
                wgt = gate_v[pl.ds(w * SC_WINDOW, SC_WINDOW)] * (0.5 * act * (1.0 + tanh_z))
                splat = [jnp.full((SC_L,), jnp.sum(jnp.where(lane == r, wgt, 0.0)), F32)
                         for r in range(SC_WINDOW)]

                @pl.loop(0, nch)
                def _(j, slot=slot, splat=splat):
                    oh = zero
                    ol = zero
                    for r in range(SC_WINDOW):
                        vh, vl = unpack(rows_v[slot, r, pl.ds(half + j * SC_L, SC_L)])
                        oh = oh + splat[r] * vh
                        ol = ol + splat[r] * vl
                    out_v[pl.ds(j * SC_L, SC_L)] += oh
                    out_v[pl.ds(half + j * SC_L, SC_L)] += ol

            pltpu.sync_copy(out_v, o_hbm.at[tok])

    return k(table, eidx.reshape(-1), gate.reshape(-1), xn)


def _finish_kernel(x1_ref, p_ref, gf_ref, y_ref):
    y_ref[...] = _rms(x1_ref[...] + p_ref[...], gf_ref[...])


def _finish(x1, p, gf):
    rows, dm = x1.shape
    tm = _pick(rows, (512, 256, 128))
    row = pl.BlockSpec((tm, dm), lambda i: (i, 0))
    return pl.pallas_call(
        _finish_kernel, grid=(rows // tm,), in_specs=[row, row, _const_spec(gf.shape)], out_specs=row,
        out_shape=jax.ShapeDtypeStruct((rows, dm), F32),
        compiler_params=pltpu.CompilerParams(dimension_semantics=("arbitrary",), vmem_limit_bytes=VMEM_LIMIT),
        name="finish",
    )(x1, p, gf)


def _cmul(ar, ai, br, bi):
    return ar * br - ai * bi, ar * bi + ai * br


def _s5_tables(lam_re, lam_im, log_dt, b_re, b_im, c_re, c_im, d, glu_w, glu_b):
    g, p = lam_re.shape
    hg = b_re.shape[-1]
    gs = g // SSM_SLABS
    dt = jnp.exp(log_dt.astype(F32))[:, None]
    lr = lam_re.astype(F32)
    li = lam_im.astype(F32)
    mag = jnp.exp(lr * dt)
    ab_re = mag * jnp.cos(li * dt)
    ab_im = mag * jnp.sin(li * dt)
    den = lr * lr + li * li
    nr = ab_re - 1.0
    ni = ab_im
    coef_re = (nr * lr + ni * li) / den
    coef_im = (ni * lr - nr * li) / den
    br = b_re.astype(F32)
    bi = b_im.astype(F32)
    bb_re = coef_re[..., None] * br - coef_im[..., None] * bi
    bb_im = coef_re[..., None] * bi + coef_im[..., None] * br
    eye = jnp.eye(gs, dtype=F32)

    def b_slab(bb):
        t = bb.reshape(SSM_SLABS, gs, p, hg).transpose(0, 1, 3, 2)
        return jnp.einsum('sihp,ij->sihjp', t, eye).reshape(SSM_SLABS, gs * hg, gs * p)

    bs = jnp.concatenate([b_slab(bb_re), b_slab(bb_im)], axis=2).astype(BF16)

    def c_slab(c):
        t = c.astype(F32).reshape(SSM_SLABS, gs, hg, p).transpose(0, 1, 3, 2)
        return jnp.einsum('siph,ij->sipjh', t, eye).reshape(SSM_SLABS, gs * p, gs * hg)

    cs = jnp.concatenate([c_slab(c_re), -c_slab(c_im)], axis=1).astype(BF16)

    a1 = (ab_re.reshape(1, g * p), ab_im.reshape(1, g * p))
    a2 = _cmul(*a1, *a1)
    a4 = _cmul(*a2, *a2)
    rowid = jnp.arange(SUBLANES)[:, None]
    mtab = jnp.stack([jnp.where(rowid >= s, comp, 0.0)
                      for s, a in ((1, a1), (2, a2), (4, a4)) for comp in a])
    pows = [a1]
    for _ in range(SUBLANES - 1):
        pows.append(_cmul(*pows[-1], *a1))
    ptab = jnp.stack([jnp.concatenate([q[0] for q in pows], axis=0),
                      jnp.concatenate([q[1] for q in pows], axis=0)])
    return (bs, cs, d.astype(F32).reshape(1, -1), glu_w.astype(BF16),
            glu_b.astype(F32).reshape(1, -1), mtab, ptab)


def _sgu_tables(w_s, b_s, seq_len):
    n_heads = w_s.shape[0]
    ln = min(SGU_CHUNK, seq_len)
    assert SGU_CHUNK % ln == 0 and seq_len % ln == 0
    rep = SGU_CHUNK // ln
    mask = jnp.tril(jnp.ones((ln, ln), dtype=bool))
    w = jnp.where(mask[None], w_s[:, :ln, :ln], 0.0).astype(F32)
    ws = jnp.einsum('hij,ab->haibj', w, jnp.eye(rep, dtype=F32)).reshape(n_heads, SGU_CHUNK, SGU_CHUNK)
    bias = jnp.tile(jnp.transpose(b_s[:, :ln]).astype(F32), (rep, 1))
    return ws.astype(BF16), bias


def _pack_kernel(u_ref, v_ref, o_ref):
    half = u_ref.shape[1] // 2

    def bf16_bits(x):
        return lax.bitcast_convert_type(x.astype(BF16).astype(F32), U32)

    def pack(t_ref):
        w = bf16_bits(t_ref[:, :half]) | (bf16_bits(t_ref[:, half:]) >> 16)
        return lax.bitcast_convert_type(w, I32)

    o_ref[:, :half] = pack(u_ref)
    o_ref[:, half:] = pack(v_ref)


def _pack_tables(u, v):
    n, dd = u.shape
    tr = _pick(n, (PACK_ROWS, 256, 128, 8))
    spec = pl.BlockSpec((tr, dd), lambda i: (i, 0))
    return pl.pallas_call(
        _pack_kernel,
        grid=(n // tr,),
        in_specs=[spec, spec],
        out_specs=spec,
        out_shape=jax.ShapeDtypeStruct((n, dd), I32),
        compiler_params=pltpu.CompilerParams(
            dimension_semantics=("arbitrary",), vmem_limit_bytes=VMEM_LIMIT),
        name="pack",
    )(u, v)


def _trunk_to_route(x, h0r, h0i, seq_len, lp):
    (norm_mix_g, w_in_bf, s5_tabs, ln_g, ln_b, sgu_w, sgu_b, wa, wb, wo, norm_ffn_g, wq, keys_bf, _,
     d_ssm, d_sgu) = lp
    u, su, vn, gates = _inproj(x, norm_mix_g, w_in_bf, ln_g, ln_b, d_ssm, d_sgu)
    ya, hlr, hli = _s5(u, h0r, h0i, s5_tabs, seq_len)
    ws, bias = _sgu_tables(sgu_w, sgu_b, seq_len)
    hd = d_sgu // ws.shape[0]
    bias_tile = jnp.repeat(bias, hd, axis=1)
    x1 = _mix(x, ya, su, vn, gates, ws, bias_tile, wa, wb, wo)
    xn, eidx, gate = _route(x1, norm_ffn_g, wq, keys_bf)
    return (x1, xn, eidx, gate), hlr, hli, vn


def _trunk_layer(x, h0r, h0i, seq_len, lp, final_g):
    (x1, xn, eidx, gate), hlr, hli, vn = _trunk_to_route(x, h0r, h0i, seq_len, lp)
    y = _peer(eidx, gate, xn, x1, final_g, lp[13], final_norm=True)
    return y, hlr, hli, vn


def _prompt_split(x, seq_len, lp, final_g, n_state):
    rows = x.shape[0]
    cuts = [(int(rows * c) // SC_SPLIT_ALIGN) * SC_SPLIT_ALIGN for c in SC_CUTS]
    bounds = [0] + cuts + [rows]
    hr = hi = jnp.zeros((1, n_state), F32)
    parts = []
    for a, b in zip(bounds[:-1], bounds[1:]):
        part, hr, hi, _ = _trunk_to_route(x[a:b], hr, hi, seq_len, lp)
        parts.append(part)
    x1a, xna, ea, ga = parts[-1]
    ya = _peer(ea, ga, xna, x1a, final_g, lp[13], final_norm=True)
    return parts[:-1], ya, hr, hi


def _prompt_finish(firsts, ya, lp, final_g):
    ys = [_finish(x1, _sc_peer(lp[13], e, g, xn), final_g) for (x1, xn, e, g) in firsts]
    ys = ys if ya is None else ys + [ya]
    return ys[0] if len(ys) == 1 else jnp.concatenate(ys, axis=0)


def kernel(x_prompt, x_sample, state_ssm_re, state_ssm_im, norm_mix_g, w_in, ssm_lambda_re, ssm_lambda_im, ssm_log_dt, ssm_b_re, ssm_b_im, ssm_c_re, ssm_c_im, ssm_d, ssm_glu_w, ssm_glu_b, sgu_ln_g, sgu_ln_b, sgu_w, sgu_b, w_branch_a, w_branch_b, w_out, norm_ffn_g, peer_w_q, peer_keys, peer_u, peer_v, norm_final_g):
    depth = w_in.shape[0]
    assert depth == 1, "the final norm is fused into the last layer's PEER kernel"
    bp, sp, dm = x_prompt.shape
    bs_, ss, _ = x_sample.shape
    g, p = ssm_lambda_re.shape[1:]
    d_ssm = ssm_d.shape[1]
    d_sgu = sgu_ln_g.shape[1]
    n_state = g * p
    l = 0
    lp = (norm_mix_g[l].reshape(1, dm), w_in[l].astype(BF16),
          _s5_tables(ssm_lambda_re[l], ssm_lambda_im[l], ssm_log_dt[l], ssm_b_re[l], ssm_b_im[l],
                     ssm_c_re[l], ssm_c_im[l], ssm_d[l], ssm_glu_w[l], ssm_glu_b[l]),
          sgu_ln_g[l].reshape(1, d_sgu), sgu_ln_b[l].reshape(1, d_sgu), sgu_w[l], sgu_b[l],
          w_branch_a[l].astype(BF16), w_branch_b[l].astype(BF16), w_out[l].astype(BF16),
          norm_ffn_g[l].reshape(1, dm), peer_w_q[l].astype(BF16), peer_keys[l].astype(BF16),
          _pack_tables(peer_u[l].astype(F32), peer_v[l].astype(F32)),
          d_ssm, d_sgu)
    gf = norm_final_g.reshape(1, dm)

    xp2 = x_prompt.reshape(bp * sp, dm)
    xs2 = x_sample.reshape(bs_ * ss, dm)
    h0s = (state_ssm_re[l].astype(F32).reshape(bs_, n_state), state_ssm_im[l].astype(F32).reshape(bs_, n_state))
    split = bp == 1 and sp >= 4 * SC_SPLIT_ALIGN and (bs_ * ss) % SC_SUBCORES == 0
    if split:
        sample, hr_s, hi_s, v_s = _trunk_to_route(xs2, h0s[0], h0s[1], ss, lp)
        first, ya, hr_p, hi_p = _prompt_split(xp2, sp, lp, gf, n_state)
        ys = _prompt_finish([sample], None, lp, gf)
        yp = _prompt_finish(first, ya, lp, gf)
    else:
        zeros = jnp.zeros((bp, n_state), F32)
        yp, hr_p, hi_p, _ = _trunk_layer(xp2, zeros, zeros, sp, lp, gf)
        ys, hr_s, hi_s, v_s = _trunk_layer(xs2, h0s[0], h0s[1], ss, lp, gf)
    return (yp.reshape(bp, sp, dm), ys.reshape(bs_, ss, dm),
            hr_p.reshape(1, bp, g, p), hi_p.reshape(1, bp, g, p),
            hr_s.reshape(1, bs_, g, p), hi_s.reshape(1, bs_, g, p),
            v_s.reshape(1, bs_, ss, d_sgu))
```

```python
import functools
import math

import jax
import jax.numpy as jnp
from jax import lax
from jax.experimental import pallas as pl
from jax.experimental.pallas import tpu as pltpu
from jax.experimental.pallas import tpu_sc as plsc

F32 = jnp.float32
BF16 = jnp.bfloat16
I32 = jnp.int32
U32 = jnp.uint32
EPS = 1e-6

LANES = 128
SUBLANES = 8
VMEM_LIMIT = 56 * 1024 * 1024

SSM_SLABS = 4
SCAN_LANES = 512
PEER_TOPK = 16
SGU_CHUNK = 128
ROUTE_LANES = 256
INPROJ_SUB = 128
PACK_ROWS = 512


def _pick(n, candidates):
    for c in candidates:
        if n % c == 0:
            return c
    raise ValueError(f"no block size for {n} in {candidates}")


def _const_spec(shape):
    nd = len(shape)
    return pl.BlockSpec(shape, lambda *_: (0,) * nd, pipeline_mode=pl.Buffered(1))


def _rms(x, g):
    return x * lax.rsqrt(jnp.mean(x * x, axis=-1, keepdims=True) + EPS) * g


def _inproj_kernel(x_ref, g_ref, w_ref, lng_ref, lnb_ref, u_ref, su_ref, vn_ref, gates_ref, xn_sc):
    j = pl.program_id(1)
    tm = x_ref.shape[0]
    sub = min(tm, INPROJ_SUB)

    @pl.when(j == 0)
    def _():
        xn_sc[...] = _rms(x_ref[...], g_ref[...]).astype(BF16)

    def section(out_ref, epilogue):
        for r in range(tm // sub):
            rows = slice(r * sub, (r + 1) * sub)
            z = jnp.dot(xn_sc[rows, :], w_ref[...], preferred_element_type=F32)
            out_ref[rows, :] = epilogue(z).astype(out_ref.dtype)

    def layer_norm_gelu(z):
        v = jax.nn.gelu(z)
        mu = jnp.mean(v, axis=-1, keepdims=True)
        vc = v - mu
        y = vc * lax.rsqrt(jnp.mean(vc * vc, axis=-1, keepdims=True) + EPS)
        return y * lng_ref[...] + lnb_ref[...]

    @pl.when(j == 0)
    def _():
        section(u_ref, lambda z: z)

    @pl.when(j == 1)
    def _():
        section(su_ref, jax.nn.gelu)

    @pl.when(j == 2)
    def _():
        section(vn_ref, layer_norm_gelu)

    @pl.when(j >= 3)
    def _():
        section(gates_ref, jax.nn.sigmoid)


def _inproj(x, g, w_in_bf, ln_g, ln_b, d_ssm, d_sgu):
    rows, dm = x.shape
    d_in = w_in_bf.shape[1]
    tn = d_ssm
    assert d_sgu == tn and (d_in - 3 * tn) % tn == 0
    nj = d_in // tn
    tm = _pick(rows, (1024, 512, 256, 128))
    return pl.pallas_call(
        _inproj_kernel,
        grid=(rows // tm, nj),
        in_specs=[
            pl.BlockSpec((tm, dm), lambda i, j: (i, 0)),
            pl.BlockSpec((1, dm), lambda i, j: (0, 0)),
            pl.BlockSpec((dm, tn), lambda i, j: (0, j)),
            pl.BlockSpec((1, tn), lambda i, j: (0, 0)),
            pl.BlockSpec((1, tn), lambda i, j: (0, 0)),
        ],
        out_specs=[
            pl.BlockSpec((tm, tn), lambda i, j: (i, 0)),
            pl.BlockSpec((tm, tn), lambda i, j: (i, 0)),
            pl.BlockSpec((tm, tn), lambda i, j: (i, 0)),
            pl.BlockSpec((tm, tn), lambda i, j: (i, jnp.maximum(j - 3, 0))),
        ],
        out_shape=[
            jax.ShapeDtypeStruct((rows, tn), F32),
            jax.ShapeDtypeStruct((rows, tn), BF16),
            jax.ShapeDtypeStruct((rows, tn), F32),
            jax.ShapeDtypeStruct((rows, d_in - 3 * tn), BF16),
        ],
        scratch_shapes=[pltpu.VMEM((tm, dm), BF16)],
        compiler_params=pltpu.CompilerParams(
            dimension_semantics=("arbitrary", "arbitrary"), vmem_limit_bytes=VMEM_LIMIT),
        name="inproj",
    )(x, g, w_in_bf, ln_g, ln_b)


def _s5_kernel(u_ref, h0r_ref, h0i_ref, bs_ref, cs_ref, d_ref, gluw_ref, glub_ref, m_ref, pw_ref,
               ya_ref, hlr_ref, hli_ref, hr_sc, hi_sc, cr_sc, ci_sc, *, seq_len):
    i = pl.program_id(0)
    tl, d_ssm = u_ref.shape
    n_state = hr_sc.shape[1]
    kin = d_ssm // SSM_SLABS
    kst = n_state // SSM_SLABS

    @pl.when(i == 0)
    def _():
        cr_sc[...] = jnp.zeros_like(cr_sc)
        ci_sc[...] = jnp.zeros_like(ci_sc)

    ub = u_ref[...].astype(BF16)
    for s in range(SSM_SLABS):
        r = jnp.dot(ub[:, kin * s:kin * (s + 1)], bs_ref[s], preferred_element_type=F32)
        hr_sc[:, kst * s:kst * (s + 1)] = r[:, :kst]
        hi_sc[:, kst * s:kst * (s + 1)] = r[:, kst:]

    n_tiles = tl // SUBLANES
    for lc in range(n_state // SCAN_LANES):
        ls = slice(lc * SCAN_LANES, (lc + 1) * SCAN_LANES)

        def tile_body(t, carry, ls=ls):
            cr, ci = carry
            row0 = i * tl + t * SUBLANES
            b = row0 // seq_len
            is_start = (row0 % seq_len) == 0
            h0r = jnp.broadcast_to(h0r_ref[pl.ds(b, 1), ls], (SUBLANES, SCAN_LANES))
            h0i = jnp.broadcast_to(h0i_ref[pl.ds(b, 1), ls], (SUBLANES, SCAN_LANES))
            cr = jnp.where(is_start, h0r, cr)
            ci = jnp.where(is_start, h0i, ci)
            r0 = pl.multiple_of(t * SUBLANES, SUBLANES)
            br = hr_sc[pl.ds(r0, SUBLANES), ls]
            bi = hi_sc[pl.ds(r0, SUBLANES), ls]
            for k, shift in enumerate((1, 2, 4)):
                mr = m_ref[2 * k, :, ls]
                mi = m_ref[2 * k + 1, :, ls]
                sr = pltpu.roll(br, shift, 0)
                si = pltpu.roll(bi, shift, 0)
                br, bi = br + (mr * sr - mi * si), bi + (mr * si + mi * sr)
            pr = pw_ref[0, :, ls]
            pi_ = pw_ref[1, :, ls]
            hr = br + (pr * cr - pi_ * ci)
            hi = bi + (pr * ci + pi_ * cr)
            hr_sc[pl.ds(r0, SUBLANES), ls] = hr
            hi_sc[pl.ds(r0, SUBLANES), ls] = hi
            last_r = hr[SUBLANES - 1:SUBLANES, :]
            last_i = hi[SUBLANES - 1:SUBLANES, :]
            hlr_ref[pl.ds(b, 1), ls] = last_r
            hli_ref[pl.ds(b, 1), ls] = last_i
            return (jnp.broadcast_to(last_r, (SUBLANES, SCAN_LANES)),
                    jnp.broadcast_to(last_i, (SUBLANES, SCAN_LANES)))

        cr, ci = lax.fori_loop(0, n_tiles, tile_body, (cr_sc[:, ls], ci_sc[:, ls]))
        cr_sc[:, ls] = cr
        ci_sc[:, ls] = ci

    ys = []
    for s in range(SSM_SLABS):
        hre = hr_sc[:, kst * s:kst * (s + 1)].astype(BF16)
        him = hi_sc[:, kst * s:kst * (s + 1)].astype(BF16)
        ys.append(jnp.dot(hre, cs_ref[s, :kst, :], preferred_element_type=F32)
                  + jnp.dot(him, cs_ref[s, kst:, :], preferred_element_type=F32))
    y = jnp.concatenate(ys, axis=1)
    y = jax.nn.gelu(y + d_ref[...] * u_ref[...])
    gt = jnp.dot(y.astype(BF16), gluw_ref[...], preferred_element_type=F32) + glub_ref[...]
    ya_ref[...] = (y * jax.nn.sigmoid(gt)).astype(BF16)


def _s5(u, h0r, h0i, tabs, seq_len):
    rows, d_ssm = u.shape
    nb, n_state = h0r.shape
    tl = _pick(rows, (256, 128, 64, 32, 16, 8)) if seq_len >= 256 else _pick(rows, (512, 256, 128, 64, 32, 16))
    assert (seq_len % tl == 0 or tl % seq_len == 0) and seq_len % SUBLANES == 0
    bs, cs, d, gluw, glub, mtab, ptab = tabs
    return pl.pallas_call(
        functools.partial(_s5_kernel, seq_len=seq_len),
        grid=(rows // tl,),
        in_specs=[
            pl.BlockSpec((tl, d_ssm), lambda i: (i, 0)),
            _const_spec((nb, n_state)),
            _const_spec((nb, n_state)),
            _const_spec(bs.shape),
            _const_spec(cs.shape),
            _const_spec(d.shape),
            _const_spec(gluw.shape),
            _const_spec(glub.shape),
            _const_spec(mtab.shape),
            _const_spec(ptab.shape),
        ],
        out_specs=[
            pl.BlockSpec((tl, d_ssm), lambda i: (i, 0)),
            pl.BlockSpec((nb, n_state), lambda i: (0, 0)),
            pl.BlockSpec((nb, n_state), lambda i: (0, 0)),
        ],
        out_shape=[
            jax.ShapeDtypeStruct((rows, d_ssm), BF16),
            jax.ShapeDtypeStruct((nb, n_state), F32),
            jax.ShapeDtypeStruct((nb, n_state), F32),
        ],
        scratch_shapes=[
            pltpu.VMEM((tl, n_state), F32),
            pltpu.VMEM((tl, n_state), F32),
            pltpu.VMEM((SUBLANES, n_state), F32),
            pltpu.VMEM((SUBLANES, n_state), F32),
        ],
        compiler_params=pltpu.CompilerParams(
            dimension_semantics=("arbitrary",), vmem_limit_bytes=VMEM_LIMIT),
        name="s5",
    )(u, h0r, h0i, bs, cs, d, gluw, glub, mtab, ptab)


def _mix_kernel(x_ref, ya_ref, su_ref, vn_ref, gates_ref, ws_ref, bias_ref, wa_ref, wb_ref, wo_ref, x1_ref):
    tm, dm = x_ref.shape
    n_heads = ws_ref.shape[0]
    hd = vn_ref.shape[1] // n_heads
    vb = vn_ref[...].astype(BF16)
    chunks = []
    for c in range(tm // SGU_CHUNK):
        heads = [jnp.dot(ws_ref[h], vb[c * SGU_CHUNK:(c + 1) * SGU_CHUNK, h * hd:(h + 1) * hd],
                         preferred_element_type=F32) for h in range(n_heads)]
        chunks.append(jnp.concatenate(heads, axis=1) + bias_ref[...])
    s_all = jnp.concatenate(chunks, axis=0)
    yb = (su_ref[...].astype(F32) * s_all).astype(BF16)
    pa = jnp.dot(ya_ref[...], wa_ref[...], preferred_element_type=F32)
    pb = jnp.dot(yb, wb_ref[...], preferred_element_type=F32)
    merged = gates_ref[:, :dm].astype(F32) * pa + gates_ref[:, dm:].astype(F32) * pb
    x1_ref[...] = x_ref[...] + jnp.dot(merged.astype(BF16), wo_ref[...], preferred_element_type=F32)


def _mix(x, ya, su, vn, gates, ws, bias, wa, wb, wo):
    rows, dm = x.shape
    d_ssm = ya.shape[1]
    d_sgu = su.shape[1]
    tm = _pick(rows, (256, 128))
    row = lambda w: pl.BlockSpec((tm, w), lambda i: (i, 0))
    return pl.pallas_call(
        _mix_kernel,
        grid=(rows // tm,),
        in_specs=[row(dm), row(d_ssm), row(d_sgu), row(d_sgu), row(2 * dm),
                  _const_spec(ws.shape), _const_spec(bias.shape),
                  _const_spec(wa.shape), _const_spec(wb.shape), _const_spec(wo.shape)],
        out_specs=row(dm),
        out_shape=jax.ShapeDtypeStruct((rows, dm), F32),
        compiler_params=pltpu.CompilerParams(
            dimension_semantics=("arbitrary",), vmem_limit_bytes=VMEM_LIMIT),
        name="mix",
    )(x, ya, su, vn, gates, ws, bias, wa, wb, wo)


def _topk_rows(s, k, payload=None):
    rows = s.shape[0]
    iota = lax.broadcasted_iota(I32, s.shape, 0).astype(F32)
    vals, outs = [], []
    for _ in range(k):
        m = jnp.max(s, axis=0, keepdims=True)
        j = jnp.min(jnp.where(s == m, iota, float(rows)), axis=0, keepdims=True)
        sel = iota == j
        vals.append(m)
        outs.append(j if payload is None else jnp.max(jnp.where(sel, payload, -1.0), axis=0, keepdims=True))
        s = jnp.where(sel, -jnp.inf, s)
    return jnp.concatenate(vals, axis=0), jnp.concatenate(outs, axis=0)


def _route_kernel(x1_ref, g_ref, wq_ref, keys_ref, xn_ref, eidx_ref, gate_ref, q_sc, e_sc, p_sc):
    tb = x1_ref.shape[0]
    tw = e_sc.shape[2]
    nsub = tb // tw
    n_heads, _, n_keys, half = keys_ref.shape
    dkey = 2 * half
    xn = _rms(x1_ref[...], g_ref[...])
    xn_ref[...] = xn
    q = jnp.dot(xn.astype(BF16), wq_ref[...], preferred_element_type=F32).astype(BF16)
    for h in range(n_heads):
        q_sc[h] = q[:, h * dkey:(h + 1) * dkey]

    nt = (((1,), (1,)), ((), ()))

    def body(n, _):
        h = n // nsub
        r0 = pl.multiple_of((n % nsub) * tw, tw)
        qh = q_sc[h, pl.ds(r0, tw), :]
        s1 = lax.dot_general(keys_ref[h, 0], qh[:, :half], nt, preferred_element_type=F32)
        s2 = lax.dot_general(keys_ref[h, 1], qh[:, half:], nt, preferred_element_type=F32)
        v1, i1 = _topk_rows(s1, PEER_TOPK)
        v2, i2 = _topk_rows(s2, PEER_TOPK)
        cv, ci = [], []
        for a in range(PEER_TOPK):
            nb = PEER_TOPK // (a + 1)
            cv.append(v1[a:a + 1, :] + v2[:nb, :])
            ci.append(i1[a:a + 1, :] * float(n_keys) + i2[:nb, :])
        ts, te = _topk_rows(jnp.concatenate(cv, axis=0), PEER_TOPK, payload=jnp.concatenate(ci, axis=0))
        ex = jnp.exp(ts - ts[0:1, :])
        e_sc[n] = te
        p_sc[n] = ex / jnp.sum(ex, axis=0, keepdims=True)
        return 0

    lax.fori_loop(0, n_heads * nsub, body, 0)
    for c in range(nsub):
        rows = slice(c * tw, (c + 1) * tw)
        e_all = jnp.concatenate([e_sc[h * nsub + c] for h in range(n_heads)], axis=0)
        p_all = jnp.concatenate([p_sc[h * nsub + c] for h in range(n_heads)], axis=0)
        eidx_ref[rows, :] = e_all.T.astype(I32)
        gate_ref[rows, :] = p_all.T


def _route(x1, g, wq, keys_bf):
    rows, dm = x1.shape
    n_heads, _, n_keys, half = keys_bf.shape
    nsel = n_heads * PEER_TOPK
    tb = _pick(rows, (256, 128))
    tw = min(tb, ROUTE_LANES)
    row = lambda w: pl.BlockSpec((tb, w), lambda i: (i, 0))
    return pl.pallas_call(
        _route_kernel,
        grid=(rows // tb,),
        in_specs=[row(dm), _const_spec(g.shape), _const_spec(wq.shape), _const_spec(keys_bf.shape)],
        out_specs=[row(dm), row(nsel), row(nsel)],
        out_shape=[
            jax.ShapeDtypeStruct((rows, dm), F32),
            jax.ShapeDtypeStruct((rows, nsel), I32),
            jax.ShapeDtypeStruct((rows, nsel), F32),
        ],
        scratch_shapes=[
            pltpu.VMEM((n_heads, tb, 2 * half), BF16),
            pltpu.VMEM((n_heads * (tb // tw), PEER_TOPK, tw), F32),
            pltpu.VMEM((n_heads * (tb // tw), PEER_TOPK, tw), F32),
        ],
        compiler_params=pltpu.CompilerParams(
            dimension_semantics=("arbitrary",), vmem_limit_bytes=VMEM_LIMIT),
        name="route",
    )(x1, g, wq, keys_bf)


PEER_BUFS = 16


def _peer_kernel(eidx_hbm, gate_ref, xn_ref, x1_ref, gf_ref, tab_hbm, y_ref,
                 idx_sm, idx_sem, *scratch, final_norm):
    i = pl.program_id(0)
    bufs = scratch[:PEER_BUFS]
    sem, peer_sc = scratch[PEER_BUFS:]
    tbk, nsel = gate_ref.shape
    half = tab_hbm.shape[1] // 2
    nlt = half // LANES

    icp = pltpu.make_async_copy(eidx_hbm.at[i], idx_sm, idx_sem)
    icp.start()
    icp.wait()

    def issue(t, s):
        base = t * nsel
        for k in range(nsel):
            e = idx_sm[base + k]
            pltpu.make_async_copy(tab_hbm.at[pl.ds(e, 1), :], bufs[s].at[pl.ds(k, 1), :],
                                  sem.at[s]).start(priority=k % 2)

    def wait(s):
        pltpu.make_async_copy(tab_hbm.at[pl.ds(0, nsel), :], bufs[s], sem.at[s]).wait()

    hi_mask = jnp.int32(-65536)
    eye = (lax.broadcasted_iota(I32, (nsel, nsel), 0) == lax.broadcasted_iota(I32, (nsel, nsel), 1))

    def unpack(w):
        return (lax.bitcast_convert_type(w & hi_mask, F32),
                lax.bitcast_convert_type(w << 16, F32))

    def compute(t, s):
        buf = bufs[s]
        xrow = xn_ref[pl.ds(t, 1), :]
        acc = jnp.zeros((nsel, LANES), F32)
        for j in range(nlt):
            uh, ul = unpack(buf[:, j * LANES:(j + 1) * LANES])
            acc = acc + uh * xrow[:, j * LANES:(j + 1) * LANES] \
                      + ul * xrow[:, half + j * LANES:half + (j + 1) * LANES]
        act = jnp.sum(acc, axis=1, keepdims=True)
        grow = jnp.broadcast_to(gate_ref[pl.ds(t, 1), :], (nsel, nsel))
        gcol = jnp.sum(jnp.where(eye, grow, 0.0), axis=1, keepdims=True)
        wgt = jnp.broadcast_to(gcol * jax.nn.gelu(act), (nsel, LANES))
        t8 = pl.multiple_of((t // SUBLANES) * SUBLANES, SUBLANES)
        rsel = lax.broadcasted_iota(I32, (SUBLANES, LANES), 0) == (t % SUBLANES)

        def put(lane0, row):
            cur = peer_sc[pl.ds(t8, SUBLANES), lane0:lane0 + LANES]
            peer_sc[pl.ds(t8, SUBLANES), lane0:lane0 + LANES] = jnp.where(
                rsel, jnp.broadcast_to(row, (SUBLANES, LANES)), cur)

        for j in range(nlt):
            vh, vl = unpack(buf[:, half + j * LANES:half + (j + 1) * LANES])
            put(j * LANES, jnp.sum(vh * wgt, axis=0, keepdims=True))
            put(half + j * LANES, jnp.sum(vl * wgt, axis=0, keepdims=True))

    def step(t, s, prefetch):
        wait(s)
        if prefetch:
            issue(t + PEER_BUFS - 1, (s + PEER_BUFS - 1) % PEER_BUFS)
        compute(t, s)

    peer_sc[...] = jnp.zeros_like(peer_sc)
    for t in range(PEER_BUFS - 1):
        issue(t, t)

    def group(gi, _):
        for s in range(PEER_BUFS):
            step(gi * PEER_BUFS + s, s, True)
        return 0

    lax.fori_loop(0, tbk // PEER_BUFS - 1, group, 0)
    for s in range(PEER_BUFS):
        t = tbk - PEER_BUFS + s
        step(t, s, t + PEER_BUFS - 1 < tbk)

    xo = x1_ref[...] + peer_sc[...]
    y_ref[...] = _rms(xo, gf_ref[...]) if final_norm else xo


def _peer(eidx, gate, xn, x1, gf, table, final_norm):
    rows, dm = x1.shape
    nsel = gate.shape[1]
    tbk = _pick(rows, (256, 128, 64, 32, 16, 8))
    assert tbk % PEER_BUFS == 0 and tbk >= 2 * PEER_BUFS
    nblk = rows // tbk
    eidx_blk = eidx.reshape(nblk, tbk * nsel)
    row = lambda w: pl.BlockSpec((tbk, w), lambda i: (i, 0))
    return pl.pallas_call(
        functools.partial(_peer_kernel, final_norm=final_norm),
        grid=(nblk,),
        in_specs=[pl.BlockSpec(memory_space=pl.ANY), row(nsel), row(dm), row(dm),
                  _const_spec(gf.shape), pl.BlockSpec(memory_space=pl.ANY)],
        out_specs=row(dm),
        out_shape=jax.ShapeDtypeStruct((rows, dm), F32),
        scratch_shapes=[
            pltpu.SMEM((tbk * nsel,), I32),
            pltpu.SemaphoreType.DMA(()),
        ] + [pltpu.VMEM((nsel, table.shape[1]), table.dtype) for _ in range(PEER_BUFS)] + [
            pltpu.SemaphoreType.DMA((PEER_BUFS,)),
            pltpu.VMEM((tbk, dm), F32),
        ],
        compiler_params=pltpu.CompilerParams(
            dimension_semantics=("arbitrary",), vmem_limit_bytes=VMEM_LIMIT),
        name="peer",
    )(eidx_blk, gate, xn, x1, gf, table)


SC_L = 16
SC_WINDOW = SC_L
SC_ROW_BUFS = 3
SC_SUBCORES = 32
SC_CUTS = (0.125, 0.4375)
SC_SPLIT_ALIGN = 512


def _sc_peer(table, eidx, gate, xn):
    ntok, nsel = eidx.shape
    d = table.shape[1]
    half = d // 2
    nch = half // SC_L
    info = plsc.get_sparse_core_info()
    n_workers = info.num_cores * info.num_subcores
    n_win = nsel // SC_WINDOW
    tpw = ntok // n_workers
    assert tpw * n_workers == ntok and SC_WINDOW == SC_L and info.num_lanes == SC_L
    mesh = plsc.VectorSubcoreMesh(core_axis_name="core", subcore_axis_name="subcore")
    c0 = math.sqrt(2.0 / math.pi)

    @functools.partial(
        pl.kernel, out_type=jax.ShapeDtypeStruct((ntok, d), F32), mesh=mesh,
        scratch_types=[pltpu.VMEM((nsel,), I32), pltpu.VMEM((nsel,), F32), pltpu.VMEM((d,), F32),
                       pltpu.VMEM((SC_ROW_BUFS, SC_WINDOW, d), I32), pltpu.VMEM((d,), F32),
                       pltpu.SemaphoreType.DMA((SC_ROW_BUFS,)), pltpu.SemaphoreType.DMA((4,))],
        compiler_params=pltpu.CompilerParams(needs_layout_passes=False),
        name="sc_peer")
    def k(tab_hbm, idx_hbm, gate_hbm, x_hbm, o_hbm, idx_v, gate_v, x_v, rows_v, out_v, gsem, lsem):
        wid = lax.axis_index("subcore") * info.num_cores + lax.axis_index("core")
        lane = lax.iota(I32, SC_L)
        zero = jnp.zeros((SC_L,), F32)

        def unpack(w):
            return (lax.bitcast_convert_type(w & jnp.int32(-65536), F32),
                    lax.bitcast_convert_type(w << 16, F32))

        @pl.loop(0, tpw)
        def _(tt):
            tok = wid * tpw + tt
            loads = (pltpu.make_async_copy(idx_hbm.at[pl.ds(tok * nsel, nsel)], idx_v, lsem.at[0]),
                     pltpu.make_async_copy(gate_hbm.at[pl.ds(tok * nsel, nsel)], gate_v, lsem.at[1]),
                     pltpu.make_async_copy(x_hbm.at[tok], x_v, lsem.at[2]))
            store = pltpu.make_async_copy(out_v, o_hbm.at[tok], lsem.at[3])
            for c in loads:
                c.start()

            def gather(w):
                return pltpu.make_async_copy(tab_hbm.at[idx_v.at[pl.ds(w * SC_WINDOW, SC_WINDOW)]],
                                             rows_v.at[w % SC_ROW_BUFS], gsem.at[w % SC_ROW_BUFS])

            loads[0].wait()
            for w in range(SC_ROW_BUFS - 1):
                gather(w).start()

            @pl.when(tt > 0)
            def _():
                store.wait()

            @pl.loop(0, d // SC_L)
            def _(c):
                out_v[pl.ds(c * SC_L, SC_L)] = zero

            loads[1].wait()
            loads[2].wait()
            for w in range(n_win):
                gather(w).wait()
                if w + SC_ROW_BUFS - 1 < n_win:
                    gather(w + SC_ROW_BUFS - 1).start()
                slot = w % SC_ROW_BUFS

                def dot_chunk(j, accs, slot=slot):
                    xa = x_v[pl.ds(j * SC_L, SC_L)]
                    xb = x_v[pl.ds(half + j * SC_L, SC_L)]
                    out = []
                    for r in range(SC_WINDOW):
                        uh, ul = unpack(rows_v[slot, r, pl.ds(j * SC_L, SC_L)])
                        out.append(accs[r] + (uh * xa + ul * xb))
                    return tuple(out)

                accs = lax.fori_loop(0, nch, dot_chunk, (zero,) * SC_WINDOW)
                act = zero
                for r in range(SC_WINDOW):
                    act = jnp.where(lane == r, jnp.sum(accs[r]), act)
                z = c0 * (act + 0.044715 * (act * act * act))
                tanh_z = 1.0 - 2.0 / (jnp.exp(2.0 * z) + 1.0)
                wgt = gate_v[pl.ds(w * SC_WINDOW, SC_WINDOW)] * (0.5 * act * (1.0 + tanh_z))
                splat = [jnp.full((SC_L,), jnp.sum(jnp.where(lane == r, wgt, 0.0)), F32)
                         for r in range(SC_WINDOW)]

                @pl.loop(0, nch)
                def _(j, slot=slot, splat=splat):
                    oh = zero
                    ol = zero
                    for r in range(SC_WINDOW):
                        vh, vl = unpack(rows_v[slot, r, pl.ds(half + j * SC_L, SC_L)])
                        oh = oh + splat[r] * vh
                        ol = ol + splat[r] * vl
                    out_v[pl.ds(j * SC_L, SC_L)] += oh
                    out_v[pl.ds(half + j * SC_L, SC_L)] += ol

            store.start()

        pltpu.make_async_copy(out_v, o_hbm.at[wid * tpw], lsem.at[3]).wait()

    return k(table, eidx.reshape(-1), gate.reshape(-1), xn)


def _finish_kernel(x1_ref, p_ref, gf_ref, y_ref):
    y_ref[...] = _rms(x1_ref[...] + p_ref[...], gf_ref[...])


def _finish(x1, p, gf):
    rows, dm = x1.shape
    tm = _pick(rows, (512, 256, 128))
    row = pl.BlockSpec((tm, dm), lambda i: (i, 0))
    return pl.pallas_call(
        _finish_kernel, grid=(rows // tm,), in_specs=[row, row, _const_spec(gf.shape)], out_specs=row,
        out_shape=jax.ShapeDtypeStruct((rows, dm), F32),
        compiler_params=pltpu.CompilerParams(dimension_semantics=("arbitrary",), vmem_limit_bytes=VMEM_LIMIT),
        name="finish",
    )(x1, p, gf)


def _cmul(ar, ai, br, bi):
    return ar * br - ai * bi, ar * bi + ai * br


def _s5_tables(lam_re, lam_im, log_dt, b_re, b_im, c_re, c_im, d, glu_w, glu_b):
    g, p = lam_re.shape
    hg = b_re.shape[-1]
    gs = g // SSM_SLABS
    dt = jnp.exp(log_dt.astype(F32))[:, None]
    lr = lam_re.astype(F32)
    li = lam_im.astype(F32)
    mag = jnp.exp(lr * dt)
    ab_re = mag * jnp.cos(li * dt)
    ab_im = mag * jnp.sin(li * dt)
    den = lr * lr + li * li
    nr = ab_re - 1.0
    ni = ab_im
    coef_re = (nr * lr + ni * li) / den
    coef_im = (ni * lr - nr * li) / den
    br = b_re.astype(F32)
    bi = b_im.astype(F32)
    bb_re = coef_re[..., None] * br - coef_im[..., None] * bi
    bb_im = coef_re[..., None] * bi + coef_im[..., None] * br
    eye = jnp.eye(gs, dtype=F32)

    def b_slab(bb):
        t = bb.reshape(SSM_SLABS, gs, p, hg).transpose(0, 1, 3, 2)
        return jnp.einsum('sihp,ij->sihjp', t, eye).reshape(SSM_SLABS, gs * hg, gs * p)

    bs = jnp.concatenate([b_slab(bb_re), b_slab(bb_im)], axis=2).astype(BF16)

    def c_slab(c):
        t = c.astype(F32).reshape(SSM_SLABS, gs, hg, p).transpose(0, 1, 3, 2)
        return jnp.einsum('siph,ij->sipjh', t, eye).reshape(SSM_SLABS, gs * p, gs * hg)

    cs = jnp.concatenate([c_slab(c_re), -c_slab(c_im)], axis=1).astype(BF16)

    a1 = (ab_re.reshape(1, g * p), ab_im.reshape(1, g * p))
    a2 = _cmul(*a1, *a1)
    a4 = _cmul(*a2, *a2)
    rowid = jnp.arange(SUBLANES)[:, None]
    mtab = jnp.stack([jnp.where(rowid >= s, comp, 0.0)
                      for s, a in ((1, a1), (2, a2), (4, a4)) for comp in a])
    pows = [a1]
    for _ in range(SUBLANES - 1):
        pows.append(_cmul(*pows[-1], *a1))
    ptab = jnp.stack([jnp.concatenate([q[0] for q in pows], axis=0),
                      jnp.concatenate([q[1] for q in pows], axis=0)])
    return (bs, cs, d.astype(F32).reshape(1, -1), glu_w.astype(BF16),
            glu_b.astype(F32).reshape(1, -1), mtab, ptab)


def _sgu_tables(w_s, b_s, seq_len):
    n_heads = w_s.shape[0]
    ln = min(SGU_CHUNK, seq_len)
    assert SGU_CHUNK % ln == 0 and seq_len % ln == 0
    rep = SGU_CHUNK // ln
    mask = jnp.tril(jnp.ones((ln, ln), dtype=bool))
    w = jnp.where(mask[None], w_s[:, :ln, :ln], 0.0).astype(F32)
    ws = jnp.einsum('hij,ab->haibj', w, jnp.eye(rep, dtype=F32)).reshape(n_heads, SGU_CHUNK, SGU_CHUNK)
    bias = jnp.tile(jnp.transpose(b_s[:, :ln]).astype(F32), (rep, 1))
    return ws.astype(BF16), bias


def _pack_kernel(u_ref, v_ref, o_ref):
    half = u_ref.shape[1] // 2

    def bf16_bits(x):
        return lax.bitcast_convert_type(x.astype(BF16).astype(F32), U32)

    def pack(t_ref):
        w = bf16_bits(t_ref[:, :half]) | (bf16_bits(t_ref[:, half:]) >> 16)
        return lax.bitcast_convert_type(w, I32)

    o_ref[:, :half] = pack(u_ref)
    o_ref[:, half:] = pack(v_ref)


def _pack_tables(u, v):
    n, dd = u.shape
    tr = _pick(n, (PACK_ROWS, 256, 128, 8))
    spec = pl.BlockSpec((tr, dd), lambda i: (i, 0))
    return pl.pallas_call(
        _pack_kernel,
        grid=(n // tr,),
        in_specs=[spec, spec],
        out_specs=spec,
        out_shape=jax.ShapeDtypeStruct((n, dd), I32),
        compiler_params=pltpu.CompilerParams(
            dimension_semantics=("arbitrary",), vmem_limit_bytes=VMEM_LIMIT),
        name="pack",
    )(u, v)


def _trunk_to_route(x, h0r, h0i, seq_len, lp):
    (norm_mix_g, w_in_bf, s5_tabs, ln_g, ln_b, sgu_w, sgu_b, wa, wb, wo, norm_ffn_g, wq, keys_bf, _,
     d_ssm, d_sgu) = lp
    u, su, vn, gates = _inproj(x, norm_mix_g, w_in_bf, ln_g, ln_b, d_ssm, d_sgu)
    ya, hlr, hli = _s5(u, h0r, h0i, s5_tabs, seq_len)
    ws, bias = _sgu_tables(sgu_w, sgu_b, seq_len)
    hd = d_sgu // ws.shape[0]
    bias_tile = jnp.repeat(bias, hd, axis=1)
    x1 = _mix(x, ya, su, vn, gates, ws, bias_tile, wa, wb, wo)
    xn, eidx, gate = _route(x1, norm_ffn_g, wq, keys_bf)
    return (x1, xn, eidx, gate), hlr, hli, vn


def _trunk_layer(x, h0r, h0i, seq_len, lp, final_g):
    (x1, xn, eidx, gate), hlr, hli, vn = _trunk_to_route(x, h0r, h0i, seq_len, lp)
    y = _peer(eidx, gate, xn, x1, final_g, lp[13], final_norm=True)
    return y, hlr, hli, vn


def _prompt_split(x, seq_len, lp, final_g, n_state):
    rows = x.shape[0]
    cuts = [(int(rows * c) // SC_SPLIT_ALIGN) * SC_SPLIT_ALIGN for c in SC_CUTS]
    bounds = [0] + cuts + [rows]
    hr = hi = jnp.zeros((1, n_state), F32)
    parts = []
    for a, b in zip(bounds[:-1], bounds[1:]):
        part, hr, hi, _ = _trunk_to_route(x[a:b], hr, hi, seq_len, lp)
        parts.append(part)
    x1a, xna, ea, ga = parts[-1]
    ya = _peer(ea, ga, xna, x1a, final_g, lp[13], final_norm=True)
    return parts[:-1], ya, hr, hi


def _prompt_finish(firsts, ya, lp, final_g):
    ys = [_finish(x1, _sc_peer(lp[13], e, g, xn), final_g) for (x1, xn, e, g) in firsts]
    ys = ys if ya is None else ys + [ya]
    return ys[0] if len(ys) == 1 else jnp.concatenate(ys, axis=0)


def kernel(x_prompt, x_sample, state_ssm_re, state_ssm_im, norm_mix_g, w_in, ssm_lambda_re, ssm_lambda_im, ssm_log_dt, ssm_b_re, ssm_b_im, ssm_c_re, ssm_c_im, ssm_d, ssm_glu_w, ssm_glu_b, sgu_ln_g, sgu_ln_b, sgu_w, sgu_b, w_branch_a, w_branch_b, w_out, norm_ffn_g, peer_w_q, peer_keys, peer_u, peer_v, norm_final_g):
    depth = w_in.shape[0]
    assert depth == 1, "the final norm is fused into the last layer's PEER kernel"
    bp, sp, dm = x_prompt.shape
    bs_, ss, _ = x_sample.shape
    g, p = ssm_lambda_re.shape[1:]
    d_ssm = ssm_d.shape[1]
    d_sgu = sgu_ln_g.shape[1]
    n_state = g * p
    l = 0
    lp = (norm_mix_g[l].reshape(1, dm), w_in[l].astype(BF16),
          _s5_tables(ssm_lambda_re[l], ssm_lambda_im[l], ssm_log_dt[l], ssm_b_re[l], ssm_b_im[l],
                     ssm_c_re[l], ssm_c_im[l], ssm_d[l], ssm_glu_w[l], ssm_glu_b[l]),
          sgu_ln_g[l].reshape(1, d_sgu), sgu_ln_b[l].reshape(1, d_sgu), sgu_w[l], sgu_b[l],
          w_branch_a[l].astype(BF16), w_branch_b[l].astype(BF16), w_out[l].astype(BF16),
          norm_ffn_g[l].reshape(1, dm), peer_w_q[l].astype(BF16), peer_keys[l].astype(BF16),
          _pack_tables(peer_u[l].astype(F32), peer_v[l].astype(F32)),
          d_ssm, d_sgu)
    gf = norm_final_g.reshape(1, dm)

    xp2 = x_prompt.reshape(bp * sp, dm)
    xs2 = x_sample.reshape(bs_ * ss, dm)
    h0s = (state_ssm_re[l].astype(F32).reshape(bs_, n_state), state_ssm_im[l].astype(F32).reshape(bs_, n_state))
    split = bp == 1 and sp >= 4 * SC_SPLIT_ALIGN and (bs_ * ss) % SC_SUBCORES == 0
    if split:
        sample, hr_s, hi_s, v_s = _trunk_to_route(xs2, h0s[0], h0s[1], ss, lp)
        first, ya, hr_p, hi_p = _prompt_split(xp2, sp, lp, gf, n_state)
        ys = _prompt_finish([sample], None, lp, gf)
        yp = _prompt_finish(first, ya, lp, gf)
    else:
        zeros = jnp.zeros((bp, n_state), F32)
        yp, hr_p, hi_p, _ = _trunk_layer(xp2, zeros, zeros, sp, lp, gf)
        ys, hr_s, hi_s, v_s = _trunk_layer(xs2, h0s[0], h0s[1], ss, lp, gf)
    return (yp.reshape(bp, sp, dm), ys.reshape(bs_, ss, dm),
            hr_p.reshape(1, bp, g, p), hi_p.reshape(1, bp, g, p),
            hr_s.reshape(1, bs_, g, p), hi_s.reshape(1, bs_, g, p),
            v_s.reshape(1, bs_, ss, d_sgu))
```

```python
import functools
import math

import jax
import jax.numpy as jnp
from jax import lax
from jax.experimental import pallas as pl
from jax.experimental.pallas import tpu as pltpu
from jax.experimental.pallas import tpu_sc as plsc

F32 = jnp.float32
BF16 = jnp.bfloat16
I32 = jnp.int32
U32 = jnp.uint32
EPS = 1e-6

LANES = 128
SUBLANES = 8
VMEM_LIMIT = 56 * 1024 * 1024

SSM_SLABS = 4
SCAN_LANES = 512
PEER_TOPK = 16
SGU_CHUNK = 128
ROUTE_LANES = 256
INPROJ_SUB = 128
PACK_ROWS = 512


def _pick(n, candidates):
    for c in candidates:
        if n % c == 0:
            return c
    raise ValueError(f"no block size for {n} in {candidates}")


def _const_spec(shape):
    nd = len(shape)
    return pl.BlockSpec(shape, lambda *_: (0,) * nd, pipeline_mode=pl.Buffered(1))


def _rms(x, g):
    return x * lax.rsqrt(jnp.mean(x * x, axis=-1, keepdims=True) + EPS) * g


def _inproj_kernel(x_ref, g_ref, w_ref, lng_ref, lnb_ref, u_ref, su_ref, vn_ref, gates_ref, xn_sc):
    j = pl.program_id(1)
    tm = x_ref.shape[0]
    sub = min(tm, INPROJ_SUB)

    @pl.when(j == 0)
    def _():
        xn_sc[...] = _rms(x_ref[...], g_ref[...]).astype(BF16)

    def section(out_ref, epilogue):
        for r in range(tm // sub):
            rows = slice(r * sub, (r + 1) * sub)
            z = jnp.dot(xn_sc[rows, :], w_ref[...], preferred_element_type=F32)
            out_ref[rows, :] = epilogue(z).astype(out_ref.dtype)

    def layer_norm_gelu(z):
        v = jax.nn.gelu(z)
        mu = jnp.mean(v, axis=-1, keepdims=True)
        vc = v - mu
        y = vc * lax.rsqrt(jnp.mean(vc * vc, axis=-1, keepdims=True) + EPS)
        return y * lng_ref[...] + lnb_ref[...]

    @pl.when(j == 0)
    def _():
        section(u_ref, lambda z: z)

    @pl.when(j == 1)
    def _():
        section(su_ref, jax.nn.gelu)

    @pl.when(j == 2)
    def _():
        section(vn_ref, layer_norm_gelu)

    @pl.when(j >= 3)
    def _():
        section(gates_ref, jax.nn.sigmoid)


def _inproj(x, window, g, w_in_bf, ln_g, ln_b, d_ssm, d_sgu):
    row0, rows = window
    dm = x.shape[1]
    d_in = w_in_bf.shape[1]
    tn = d_ssm
    assert d_sgu == tn and (d_in - 3 * tn) % tn == 0
    nj = d_in // tn
    tm = _pick(math.gcd(rows, row0) if row0 else rows, (1024, 512, 256, 128))
    blk0 = row0 // tm
    return pl.pallas_call(
        _inproj_kernel,
        grid=(rows // tm, nj),
        in_specs=[
            pl.BlockSpec((tm, dm), lambda i, j: (i + blk0, 0)),
            pl.BlockSpec((1, dm), lambda i, j: (0, 0)),
            pl.BlockSpec((dm, tn), lambda i, j: (0, j)),
            pl.BlockSpec((1, tn), lambda i, j: (0, 0)),
            pl.BlockSpec((1, tn), lambda i, j: (0, 0)),
        ],
        out_specs=[
            pl.BlockSpec((tm, tn), lambda i, j: (i, 0)),
            pl.BlockSpec((tm, tn), lambda i, j: (i, 0)),
            pl.BlockSpec((tm, tn), lambda i, j: (i, 0)),
            pl.BlockSpec((tm, tn), lambda i, j: (i, jnp.maximum(j - 3, 0))),
        ],
        out_shape=[
            jax.ShapeDtypeStruct((rows, tn), F32),
            jax.ShapeDtypeStruct((rows, tn), BF16),
            jax.ShapeDtypeStruct((rows, tn), F32),
            jax.ShapeDtypeStruct((rows, d_in - 3 * tn), BF16),
        ],
        scratch_shapes=[pltpu.VMEM((tm, dm), BF16)],
        compiler_params=pltpu.CompilerParams(
            dimension_semantics=("arbitrary", "arbitrary"), vmem_limit_bytes=VMEM_LIMIT),
        name="inproj",
    )(x, g, w_in_bf, ln_g, ln_b)


def _s5_kernel(u_ref, h0r_ref, h0i_ref, bs_ref, cs_ref, d_ref, gluw_ref, glub_ref, m_ref, pw_ref,
               ya_ref, hlr_ref, hli_ref, hr_sc, hi_sc, cr_sc, ci_sc, *, seq_len):
    i = pl.program_id(0)
    tl, d_ssm = u_ref.shape
    n_state = hr_sc.shape[1]
    kin = d_ssm // SSM_SLABS
    kst = n_state // SSM_SLABS

    @pl.when(i == 0)
    def _():
        cr_sc[...] = jnp.zeros_like(cr_sc)
        ci_sc[...] = jnp.zeros_like(ci_sc)

    ub = u_ref[...].astype(BF16)
    for s in range(SSM_SLABS):
        r = jnp.dot(ub[:, kin * s:kin * (s + 1)], bs_ref[s], preferred_element_type=F32)
        hr_sc[:, kst * s:kst * (s + 1)] = r[:, :kst]
        hi_sc[:, kst * s:kst * (s + 1)] = r[:, kst:]

    seg_rows = min(tl, seq_len)
    seg_tiles = seg_rows // SUBLANES
    for lc in range(n_state // SCAN_LANES):
        ls = slice(lc * SCAN_LANES, (lc + 1) * SCAN_LANES)

        def tile_body(t, carry, ls=ls):
            cr, ci = carry
            r0 = pl.multiple_of(t * SUBLANES, SUBLANES)
            br = hr_sc[pl.ds(r0, SUBLANES), ls]
            bi = hi_sc[pl.ds(r0, SUBLANES), ls]
            for k, shift in enumerate((1, 2, 4)):
                mr = m_ref[2 * k, :, ls]
                mi = m_ref[2 * k + 1, :, ls]
                sr = pltpu.roll(br, shift, 0)
                si = pltpu.roll(bi, shift, 0)
                br, bi = br + (mr * sr - mi * si), bi + (mr * si + mi * sr)
            pr = pw_ref[0, :, ls]
            pi_ = pw_ref[1, :, ls]
            hr = br + (pr * cr - pi_ * ci)
            hi = bi + (pr * ci + pi_ * cr)
            hr_sc[pl.ds(r0, SUBLANES), ls] = hr
            hi_sc[pl.ds(r0, SUBLANES), ls] = hi
            return (jnp.broadcast_to(hr[SUBLANES - 1:SUBLANES, :], (SUBLANES, SCAN_LANES)),
                    jnp.broadcast_to(hi[SUBLANES - 1:SUBLANES, :], (SUBLANES, SCAN_LANES)))

        def segment(q, _, ls=ls):
            row0 = i * tl + q * seg_rows
            b = row0 // seq_len
            is_start = (row0 % seq_len) == 0
            h0r = jnp.broadcast_to(h0r_ref[pl.ds(b, 1), ls], (SUBLANES, SCAN_LANES))
            h0i = jnp.broadcast_to(h0i_ref[pl.ds(b, 1), ls], (SUBLANES, SCAN_LANES))
            carry = (jnp.where(is_start, h0r, cr_sc[:, ls]), jnp.where(is_start, h0i, ci_sc[:, ls]))
            cr, ci = lax.fori_loop(q * seg_tiles, (q + 1) * seg_tiles, tile_body, carry)
            cr_sc[:, ls] = cr
            ci_sc[:, ls] = ci
            hlr_ref[pl.ds(b, 1), ls] = cr[0:1, :]
            hli_ref[pl.ds(b, 1), ls] = ci[0:1, :]
            return 0

        if tl == seg_rows:
            segment(0, 0)
        else:
            lax.fori_loop(0, tl // seg_rows, segment, 0)

    ys = []
    for s in range(SSM_SLABS):
        hre = hr_sc[:, kst * s:kst * (s + 1)].astype(BF16)
        him = hi_sc[:, kst * s:kst * (s + 1)].astype(BF16)
        ys.append(jnp.dot(hre, cs_ref[s, :kst, :], preferred_element_type=F32)
                  + jnp.dot(him, cs_ref[s, kst:, :], preferred_element_type=F32))
    y = jnp.concatenate(ys, axis=1)
    y = jax.nn.gelu(y + d_ref[...] * u_ref[...])
    gt = jnp.dot(y.astype(BF16), gluw_ref[...], preferred_element_type=F32) + glub_ref[...]
    ya_ref[...] = (y * jax.nn.sigmoid(gt)).astype(BF16)


def _s5(u, h0r, h0i, tabs, seq_len):
    rows, d_ssm = u.shape
    nb, n_state = h0r.shape
    tl = _pick(rows, (256, 128, 64, 32, 16, 8)) if seq_len >= 256 else _pick(rows, (512, 256, 128, 64, 32, 16))
    assert (seq_len % tl == 0 or tl % seq_len == 0) and seq_len % SUBLANES == 0
    bs, cs, d, gluw, glub, mtab, ptab = tabs
    return pl.pallas_call(
        functools.partial(_s5_kernel, seq_len=seq_len),
        grid=(rows // tl,),
        in_specs=[
            pl.BlockSpec((tl, d_ssm), lambda i: (i, 0)),
            _const_spec((nb, n_state)),
            _const_spec((nb, n_state)),
            _const_spec(bs.shape),
            _const_spec(cs.shape),
            _const_spec(d.shape),
            _const_spec(gluw.shape),
            _const_spec(glub.shape),
            _const_spec(mtab.shape),
            _const_spec(ptab.shape),
        ],
        out_specs=[
            pl.BlockSpec((tl, d_ssm), lambda i: (i, 0)),
            pl.BlockSpec((nb, n_state), lambda i: (0, 0)),
            pl.BlockSpec((nb, n_state), lambda i: (0, 0)),
        ],
        out_shape=[
            jax.ShapeDtypeStruct((rows, d_ssm), BF16),
            jax.ShapeDtypeStruct((nb, n_state), F32),
            jax.ShapeDtypeStruct((nb, n_state), F32),
        ],
        scratch_shapes=[
            pltpu.VMEM((tl, n_state), F32),
            pltpu.VMEM((tl, n_state), F32),
            pltpu.VMEM((SUBLANES, n_state), F32),
            pltpu.VMEM((SUBLANES, n_state), F32),
        ],
        compiler_params=pltpu.CompilerParams(
            dimension_semantics=("arbitrary",), vmem_limit_bytes=VMEM_LIMIT),
        name="s5",
    )(u, h0r, h0i, bs, cs, d, gluw, glub, mtab, ptab)


def _mix_kernel(x_ref, ya_ref, su_ref, vn_ref, gates_ref, ws_ref, bias_ref, wa_ref, wb_ref, wo_ref, x1_ref):
    tm, dm = x_ref.shape
    n_heads = ws_ref.shape[0]
    hd = vn_ref.shape[1] // n_heads
    vb = vn_ref[...].astype(BF16)
    chunks = []
    for c in range(tm // SGU_CHUNK):
        heads = [jnp.dot(ws_ref[h], vb[c * SGU_CHUNK:(c + 1) * SGU_CHUNK, h * hd:(h + 1) * hd],
                         preferred_element_type=F32) for h in range(n_heads)]
        chunks.append(jnp.concatenate(heads, axis=1) + bias_ref[...])
    s_all = jnp.concatenate(chunks, axis=0)
    yb = (su_ref[...].astype(F32) * s_all).astype(BF16)
    pa = jnp.dot(ya_ref[...], wa_ref[...], preferred_element_type=F32)
    pb = jnp.dot(yb, wb_ref[...], preferred_element_type=F32)
    merged = gates_ref[:, :dm].astype(F32) * pa + gates_ref[:, dm:].astype(F32) * pb
    x1_ref[...] = x_ref[...] + jnp.dot(merged.astype(BF16), wo_ref[...], preferred_element_type=F32)


def _mix(x, window, ya, su, vn, gates, ws, bias, wa, wb, wo):
    row0, rows = window
    dm = x.shape[1]
    d_ssm = ya.shape[1]
    d_sgu = su.shape[1]
    tm = _pick(math.gcd(rows, row0) if row0 else rows, (256, 128))
    blk0 = row0 // tm
    row = lambda w: pl.BlockSpec((tm, w), lambda i: (i, 0))
    return pl.pallas_call(
        _mix_kernel,
        grid=(rows // tm,),
        in_specs=[pl.BlockSpec((tm, dm), lambda i: (i + blk0, 0)), row(d_ssm), row(d_sgu), row(d_sgu), row(2 * dm),
                  _const_spec(ws.shape), _const_spec(bias.shape),
                  _const_spec(wa.shape), _const_spec(wb.shape), _const_spec(wo.shape)],
        out_specs=row(dm),
        out_shape=jax.ShapeDtypeStruct((rows, dm), F32),
        compiler_params=pltpu.CompilerParams(
            dimension_semantics=("arbitrary",), vmem_limit_bytes=VMEM_LIMIT),
        name="mix",
    )(x, ya, su, vn, gates, ws, bias, wa, wb, wo)


def _topk_rows(s, k, payload=None):
    rows = s.shape[0]
    iota = lax.broadcasted_iota(I32, s.shape, 0).astype(F32)
    vals, outs = [], []
    for _ in range(k):
        m = jnp.max(s, axis=0, keepdims=True)
        j = jnp.min(jnp.where(s == m, iota, float(rows)), axis=0, keepdims=True)
        sel = iota == j
        vals.append(m)
        outs.append(j if payload is None else jnp.max(jnp.where(sel, payload, -1.0), axis=0, keepdims=True))
        s = jnp.where(sel, -jnp.inf, s)
    return jnp.concatenate(vals, axis=0), jnp.concatenate(outs, axis=0)


def _route_kernel(x1_ref, g_ref, wq_ref, keys_ref, xn_ref, eidx_ref, gate_ref, q_sc, e_sc, p_sc):
    tb = x1_ref.shape[0]
    tw = e_sc.shape[2]
    nsub = tb // tw
    n_heads, _, n_keys, half = keys_ref.shape
    dkey = 2 * half
    xn = _rms(x1_ref[...], g_ref[...])
    xn_ref[...] = xn
    q = jnp.dot(xn.astype(BF16), wq_ref[...], preferred_element_type=F32).astype(BF16)
    for h in range(n_heads):
        q_sc[h] = q[:, h * dkey:(h + 1) * dkey]

    nt = (((1,), (1,)), ((), ()))

    def body(n, _):
        h = n // nsub
        r0 = pl.multiple_of((n % nsub) * tw, tw)
        qh = q_sc[h, pl.ds(r0, tw), :]
        s1 = lax.dot_general(keys_ref[h, 0], qh[:, :half], nt, preferred_element_type=F32)
        s2 = lax.dot_general(keys_ref[h, 1], qh[:, half:], nt, preferred_element_type=F32)
        v1, i1 = _topk_rows(s1, PEER_TOPK)
        v2, i2 = _topk_rows(s2, PEER_TOPK)
        cv, ci = [], []
        for a in range(PEER_TOPK):
            nb = PEER_TOPK // (a + 1)
            cv.append(v1[a:a + 1, :] + v2[:nb, :])
            ci.append(i1[a:a + 1, :] * float(n_keys) + i2[:nb, :])
        ts, te = _topk_rows(jnp.concatenate(cv, axis=0), PEER_TOPK, payload=jnp.concatenate(ci, axis=0))
        ex = jnp.exp(ts - ts[0:1, :])
        e_sc[n] = te
        p_sc[n] = ex / jnp.sum(ex, axis=0, keepdims=True)
        return 0

    lax.fori_loop(0, n_heads * nsub, body, 0)
    for c in range(nsub):
        rows = slice(c * tw, (c + 1) * tw)
        e_all = jnp.concatenate([e_sc[h * nsub + c] for h in range(n_heads)], axis=0)
        p_all = jnp.concatenate([p_sc[h * nsub + c] for h in range(n_heads)], axis=0)
        eidx_ref[rows, :] = e_all.T.astype(I32)
        gate_ref[rows, :] = p_all.T


def _route(x1, g, wq, keys_bf):
    rows, dm = x1.shape
    n_heads, _, n_keys, half = keys_bf.shape
    nsel = n_heads * PEER_TOPK
    tb = _pick(rows, (256, 128))
    tw = min(tb, ROUTE_LANES)
    row = lambda w: pl.BlockSpec((tb, w), lambda i: (i, 0))
    return pl.pallas_call(
        _route_kernel,
        grid=(rows // tb,),
        in_specs=[row(dm), _const_spec(g.shape), _const_spec(wq.shape), _const_spec(keys_bf.shape)],
        out_specs=[row(dm), row(nsel), row(nsel)],
        out_shape=[
            jax.ShapeDtypeStruct((rows, dm), F32),
            jax.ShapeDtypeStruct((rows, nsel), I32),
            jax.ShapeDtypeStruct((rows, nsel), F32),
        ],
        scratch_shapes=[
            pltpu.VMEM((n_heads, tb, 2 * half), BF16),
            pltpu.VMEM((n_heads * (tb // tw), PEER_TOPK, tw), F32),
            pltpu.VMEM((n_heads * (tb // tw), PEER_TOPK, tw), F32),
        ],
        compiler_params=pltpu.CompilerParams(
            dimension_semantics=("arbitrary",), vmem_limit_bytes=VMEM_LIMIT),
        name="route",
    )(x1, g, wq, keys_bf)


PEER_BUFS = 16


def _peer_kernel(eidx_hbm, gate_ref, xn_ref, x1_ref, gf_ref, tab_hbm, y_ref,
                 idx_sm, idx_sem, *scratch, final_norm):
    i = pl.program_id(0)
    bufs = scratch[:PEER_BUFS]
    sem, peer_sc = scratch[PEER_BUFS:]
    tbk, nsel = gate_ref.shape
    half = tab_hbm.shape[1] // 2
    nlt = half // LANES

    icp = pltpu.make_async_copy(eidx_hbm.at[i], idx_sm, idx_sem)
    icp.start()
    icp.wait()

    def issue(t, s):
        base = t * nsel
        for k in range(nsel):
            e = idx_sm[base + k]
            pltpu.make_async_copy(tab_hbm.at[pl.ds(e, 1), :], bufs[s].at[pl.ds(k, 1), :],
                                  sem.at[s]).start(priority=k % 2)

    def wait(s):
        pltpu.make_async_copy(tab_hbm.at[pl.ds(0, nsel), :], bufs[s], sem.at[s]).wait()

    hi_mask = jnp.int32(-65536)
    eye = (lax.broadcasted_iota(I32, (nsel, nsel), 0) == lax.broadcasted_iota(I32, (nsel, nsel), 1))

    def unpack(w):
        return (lax.bitcast_convert_type(w & hi_mask, F32),
                lax.bitcast_convert_type(w << 16, F32))

    def compute(t, s):
        buf = bufs[s]
        xrow = xn_ref[pl.ds(t, 1), :]
        acc = jnp.zeros((nsel, LANES), F32)
        for j in range(nlt):
            uh, ul = unpack(buf[:, j * LANES:(j + 1) * LANES])
            acc = acc + uh * xrow[:, j * LANES:(j + 1) * LANES] \
                      + ul * xrow[:, half + j * LANES:half + (j + 1) * LANES]
        act = jnp.sum(acc, axis=1, keepdims=True)
        grow = jnp.broadcast_to(gate_ref[pl.ds(t, 1), :], (nsel, nsel))
        gcol = jnp.sum(jnp.where(eye, grow, 0.0), axis=1, keepdims=True)
        wgt = jnp.broadcast_to(gcol * jax.nn.gelu(act), (nsel, LANES))
        t8 = pl.multiple_of((t // SUBLANES) * SUBLANES, SUBLANES)
        rsel = lax.broadcasted_iota(I32, (SUBLANES, LANES), 0) == (t % SUBLANES)

        def put(lane0, row):
            cur = peer_sc[pl.ds(t8, SUBLANES), lane0:lane0 + LANES]
            peer_sc[pl.ds(t8, SUBLANES), lane0:lane0 + LANES] = jnp.where(
                rsel, jnp.broadcast_to(row, (SUBLANES, LANES)), cur)

        for j in range(nlt):
            vh, vl = unpack(buf[:, half + j * LANES:half + (j + 1) * LANES])
            put(j * LANES, jnp.sum(vh * wgt, axis=0, keepdims=True))
            put(half + j * LANES, jnp.sum(vl * wgt, axis=0, keepdims=True))

    def step(t, s, prefetch):
        wait(s)
        if prefetch:
            issue(t + PEER_BUFS - 1, (s + PEER_BUFS - 1) % PEER_BUFS)
        compute(t, s)

    peer_sc[...] = jnp.zeros_like(peer_sc)
    for t in range(PEER_BUFS - 1):
        issue(t, t)

    def group(gi, _):
        for s in range(PEER_BUFS):
            step(gi * PEER_BUFS + s, s, True)
        return 0

    lax.fori_loop(0, tbk // PEER_BUFS - 1, group, 0)
    for s in range(PEER_BUFS):
        t = tbk - PEER_BUFS + s
        step(t, s, t + PEER_BUFS - 1 < tbk)

    xo = x1_ref[...] + peer_sc[...]
    y_ref[...] = _rms(xo, gf_ref[...]) if final_norm else xo


def _peer(eidx, gate, xn, x1, gf, table, final_norm):
    rows, dm = x1.shape
    nsel = gate.shape[1]
    tbk = _pick(rows, (256, 128, 64, 32, 16, 8))
    assert tbk % PEER_BUFS == 0 and tbk >= 2 * PEER_BUFS
    nblk = rows // tbk
    eidx_blk = eidx.reshape(nblk, tbk * nsel)
    row = lambda w: pl.BlockSpec((tbk, w), lambda i: (i, 0))
    return pl.pallas_call(
        functools.partial(_peer_kernel, final_norm=final_norm),
        grid=(nblk,),
        in_specs=[pl.BlockSpec(memory_space=pl.ANY), row(nsel), row(dm), row(dm),
                  _const_spec(gf.shape), pl.BlockSpec(memory_space=pl.ANY)],
        out_specs=row(dm),
        out_shape=jax.ShapeDtypeStruct((rows, dm), F32),
        scratch_shapes=[
            pltpu.SMEM((tbk * nsel,), I32),
            pltpu.SemaphoreType.DMA(()),
        ] + [pltpu.VMEM((nsel, table.shape[1]), table.dtype) for _ in range(PEER_BUFS)] + [
            pltpu.SemaphoreType.DMA((PEER_BUFS,)),
            pltpu.VMEM((tbk, dm), F32),
        ],
        compiler_params=pltpu.CompilerParams(
            dimension_semantics=("arbitrary",), vmem_limit_bytes=VMEM_LIMIT),
        name="peer",
    )(eidx_blk, gate, xn, x1, gf, table)


SC_L = 16
SC_WINDOW = SC_L
SC_SUBCORES = 32
SC_CUTS = (0.125, 0.4375)
SC_SPLIT_ALIGN = 512


def _sc_peer(table, eidx, gate, xn):
    ntok, nsel = eidx.shape
    d = table.shape[1]
    half = d // 2
    nch = half // SC_L
    info = plsc.get_sparse_core_info()
    n_workers = info.num_cores * info.num_subcores
    n_win = nsel // SC_WINDOW
    tpw = ntok // n_workers
    assert tpw * n_workers == ntok and SC_WINDOW == SC_L and info.num_lanes == SC_L
    mesh = plsc.VectorSubcoreMesh(core_axis_name="core", subcore_axis_name="subcore")
    c0 = math.sqrt(2.0 / math.pi)

    @functools.partial(
        pl.kernel, out_type=jax.ShapeDtypeStruct((ntok, d), F32), mesh=mesh,
        scratch_types=[pltpu.VMEM((nsel,), I32), pltpu.VMEM((nsel,), F32), pltpu.VMEM((d,), F32),
                       pltpu.VMEM((2, SC_WINDOW, d), I32), pltpu.VMEM((d,), F32),
                       pltpu.SemaphoreType.DMA((2,))],
        compiler_params=pltpu.CompilerParams(needs_layout_passes=False),
        name="sc_peer")
    def k(tab_hbm, idx_hbm, gate_hbm, x_hbm, o_hbm, idx_v, gate_v, x_v, rows_v, out_v, gsem):
        wid = lax.axis_index("subcore") * info.num_cores + lax.axis_index("core")
        lane = lax.iota(I32, SC_L)
        zero = jnp.zeros((SC_L,), F32)

        def unpack(w):
            return (lax.bitcast_convert_type(w & jnp.int32(-65536), F32),
                    lax.bitcast_convert_type(w << 16, F32))

        @pl.loop(0, tpw)
        def _(tt):
            tok = wid * tpw + tt
            pltpu.sync_copy(idx_hbm.at[pl.ds(tok * nsel, nsel)], idx_v)
            pltpu.sync_copy(gate_hbm.at[pl.ds(tok * nsel, nsel)], gate_v)
            pltpu.sync_copy(x_hbm.at[tok], x_v)

            @pl.loop(0, d // SC_L)
            def _(c):
                out_v[pl.ds(c * SC_L, SC_L)] = zero

            def gather(w):
                return pltpu.make_async_copy(tab_hbm.at[idx_v.at[pl.ds(w * SC_WINDOW, SC_WINDOW)]],
                                             rows_v.at[w % 2], gsem.at[w % 2])

            gather(0).start()
            for w in range(n_win):
                gather(w).wait()
                if w + 1 < n_win:
                    gather(w + 1).start()
                slot = w % 2

                def dot_chunk(j, accs, slot=slot):
                    xa = x_v[pl.ds(j * SC_L, SC_L)]
                    xb = x_v[pl.ds(half + j * SC_L, SC_L)]
                    out = []
                    for r in range(SC_WINDOW):
                        uh, ul = unpack(rows_v[slot, r, pl.ds(j * SC_L, SC_L)])
                        out.append(accs[r] + (uh * xa + ul * xb))
                    return tuple(out)

                accs = lax.fori_loop(0, nch, dot_chunk, (zero,) * SC_WINDOW)
                act = zero
                for r in range(SC_WINDOW):
                    act = jnp.where(lane == r, jnp.sum(accs[r]), act)
                z = c0 * (act + 0.044715 * (act * act * act))
                tanh_z = 1.0 - 2.0 / (jnp.exp(2.0 * z) + 1.0)
                wgt = gate_v[pl.ds(w * SC_WINDOW, SC_WINDOW)] * (0.5 * act * (1.0 + tanh_z))
                splat = [jnp.full((SC_L,), jnp.sum(jnp.where(lane == r, wgt, 0.0)), F32)
                         for r in range(SC_WINDOW)]

                @pl.loop(0, nch)
                def _(j, slot=slot, splat=splat):
                    oh = zero
                    ol = zero
                    for r in range(SC_WINDOW):
                        vh, vl = unpack(rows_v[slot, r, pl.ds(half + j * SC_L, SC_L)])
                        oh = oh + splat[r] * vh
                        ol = ol + splat[r] * vl
                    out_v[pl.ds(j * SC_L, SC_L)] += oh
                    out_v[pl.ds(half + j * SC_L, SC_L)] += ol

            pltpu.sync_copy(out_v, o_hbm.at[tok])

    return k(table, eidx.reshape(-1), gate.reshape(-1), xn)


def _finish_kernel(x1_ref, p_ref, gf_ref, y_ref):
    y_ref[...] = _rms(x1_ref[...] + p_ref[...], gf_ref[...])


def _finish(x1, p, gf):
    rows, dm = x1.shape
    tm = _pick(rows, (512, 256, 128))
    row = pl.BlockSpec((tm, dm), lambda i: (i, 0))
    return pl.pallas_call(
        _finish_kernel, grid=(rows // tm,), in_specs=[row, row, _const_spec(gf.shape)], out_specs=row,
        out_shape=jax.ShapeDtypeStruct((rows, dm), F32),
        compiler_params=pltpu.CompilerParams(dimension_semantics=("arbitrary",), vmem_limit_bytes=VMEM_LIMIT),
        name="finish",
    )(x1, p, gf)


def _cmul(ar, ai, br, bi):
    return ar * br - ai * bi, ar * bi + ai * br


def _s5_tables(lam_re, lam_im, log_dt, b_re, b_im, c_re, c_im, d, glu_w, glu_b):
    g, p = lam_re.shape
    hg = b_re.shape[-1]
    gs = g // SSM_SLABS
    dt = jnp.exp(log_dt.astype(F32))[:, None]
    lr = lam_re.astype(F32)
    li = lam_im.astype(F32)
    mag = jnp.exp(lr * dt)
    ab_re = mag * jnp.cos(li * dt)
    ab_im = mag * jnp.sin(li * dt)
    den = lr * lr + li * li
    nr = ab_re - 1.0
    ni = ab_im
    coef_re = (nr * lr + ni * li) / den
    coef_im = (ni * lr - nr * li) / den
    br = b_re.astype(F32)
    bi = b_im.astype(F32)
    bb_re = coef_re[..., None] * br - coef_im[..., None] * bi
    bb_im = coef_re[..., None] * bi + coef_im[..., None] * br
    eye = jnp.eye(gs, dtype=F32)

    def b_slab(bb):
        t = bb.reshape(SSM_SLABS, gs, p, hg).transpose(0, 1, 3, 2)
        return jnp.einsum('sihp,ij->sihjp', t, eye).reshape(SSM_SLABS, gs * hg, gs * p)

    bs = jnp.concatenate([b_slab(bb_re), b_slab(bb_im)], axis=2).astype(BF16)

    def c_slab(c):
        t = c.astype(F32).reshape(SSM_SLABS, gs, hg, p).transpose(0, 1, 3, 2)
        return jnp.einsum('siph,ij->sipjh', t, eye).reshape(SSM_SLABS, gs * p, gs * hg)

    cs = jnp.concatenate([c_slab(c_re), -c_slab(c_im)], axis=1).astype(BF16)

    a1 = (ab_re.reshape(1, g * p), ab_im.reshape(1, g * p))
    a2 = _cmul(*a1, *a1)
    a4 = _cmul(*a2, *a2)
    rowid = jnp.arange(SUBLANES)[:, None]
    mtab = jnp.stack([jnp.where(rowid >= s, comp, 0.0)
                      for s, a in ((1, a1), (2, a2), (4, a4)) for comp in a])
    pows = [a1]
    for _ in range(SUBLANES - 1):
        pows.append(_cmul(*pows[-1], *a1))
    ptab = jnp.stack([jnp.concatenate([q[0] for q in pows], axis=0),
                      jnp.concatenate([q[1] for q in pows], axis=0)])
    return (bs, cs, d.astype(F32).reshape(1, -1), glu_w.astype(BF16),
            glu_b.astype(F32).reshape(1, -1), mtab, ptab)


def _sgu_tables(w_s, b_s, seq_len):
    n_heads = w_s.shape[0]
    ln = min(SGU_CHUNK, seq_len)
    assert SGU_CHUNK % ln == 0 and seq_len % ln == 0
    rep = SGU_CHUNK // ln
    mask = jnp.tril(jnp.ones((ln, ln), dtype=bool))
    w = jnp.where(mask[None], w_s[:, :ln, :ln], 0.0).astype(F32)
    ws = jnp.einsum('hij,ab->haibj', w, jnp.eye(rep, dtype=F32)).reshape(n_heads, SGU_CHUNK, SGU_CHUNK)
    bias = jnp.tile(jnp.transpose(b_s[:, :ln]).astype(F32), (rep, 1))
    return ws.astype(BF16), bias


def _pack_kernel(u_ref, v_ref, o_ref):
    half = u_ref.shape[1] // 2

    def bf16_bits(x):
        return lax.bitcast_convert_type(x.astype(BF16).astype(F32), U32)

    def pack(t_ref):
        w = bf16_bits(t_ref[:, :half]) | (bf16_bits(t_ref[:, half:]) >> 16)
        return lax.bitcast_convert_type(w, I32)

    o_ref[:, :half] = pack(u_ref)
    o_ref[:, half:] = pack(v_ref)


def _pack_tables(u, v):
    n, dd = u.shape
    tr = _pick(n, (PACK_ROWS, 256, 128, 8))
    spec = pl.BlockSpec((tr, dd), lambda i: (i, 0))
    return pl.pallas_call(
        _pack_kernel,
        grid=(n // tr,),
        in_specs=[spec, spec],
        out_specs=spec,
        out_shape=jax.ShapeDtypeStruct((n, dd), I32),
        compiler_params=pltpu.CompilerParams(
            dimension_semantics=("arbitrary",), vmem_limit_bytes=VMEM_LIMIT),
        name="pack",
    )(u, v)


def _trunk_to_route(x, h0r, h0i, seq_len, lp, window=None):
    (norm_mix_g, w_in_bf, s5_tabs, ln_g, ln_b, sgu_w, sgu_b, wa, wb, wo, norm_ffn_g, wq, keys_bf, _,
     d_ssm, d_sgu) = lp
    window = (0, x.shape[0]) if window is None else window
    u, su, vn, gates = _inproj(x, window, norm_mix_g, w_in_bf, ln_g, ln_b, d_ssm, d_sgu)
    ya, hlr, hli = _s5(u, h0r, h0i, s5_tabs, seq_len)
    ws, bias = _sgu_tables(sgu_w, sgu_b, seq_len)
    hd = d_sgu // ws.shape[0]
    bias_tile = jnp.repeat(bias, hd, axis=1)
    x1 = _mix(x, window, ya, su, vn, gates, ws, bias_tile, wa, wb, wo)
    xn, eidx, gate = _route(x1, norm_ffn_g, wq, keys_bf)
    return (x1, xn, eidx, gate), hlr, hli, vn


def _trunk_layer(x, h0r, h0i, seq_len, lp, final_g):
    (x1, xn, eidx, gate), hlr, hli, vn = _trunk_to_route(x, h0r, h0i, seq_len, lp)
    y = _peer(eidx, gate, xn, x1, final_g, lp[13], final_norm=True)
    return y, hlr, hli, vn


def _prompt_split(x, seq_len, lp, final_g, n_state):
    rows = x.shape[0]
    cuts = [(int(rows * c) // SC_SPLIT_ALIGN) * SC_SPLIT_ALIGN for c in SC_CUTS]
    bounds = [0] + cuts + [rows]
    hr = hi = jnp.zeros((1, n_state), F32)
    parts = []
    for a, b in zip(bounds[:-1], bounds[1:]):
        part, hr, hi, _ = _trunk_to_route(x, hr, hi, seq_len, lp, window=(a, b - a))
        parts.append(part)
    x1a, xna, ea, ga = parts[-1]
    ya = _peer(ea, ga, xna, x1a, final_g, lp[13], final_norm=True)
    return parts[:-1], ya, hr, hi


def _prompt_finish(firsts, ya, lp, final_g):
    ys = [_finish(x1, _sc_peer(lp[13], e, g, xn), final_g) for (x1, xn, e, g) in firsts]
    ys = ys if ya is None else ys + [ya]
    return ys[0] if len(ys) == 1 else jnp.concatenate(ys, axis=0)


def kernel(x_prompt, x_sample, state_ssm_re, state_ssm_im, norm_mix_g, w_in, ssm_lambda_re, ssm_lambda_im, ssm_log_dt, ssm_b_re, ssm_b_im, ssm_c_re, ssm_c_im, ssm_d, ssm_glu_w, ssm_glu_b, sgu_ln_g, sgu_ln_b, sgu_w, sgu_b, w_branch_a, w_branch_b, w_out, norm_ffn_g, peer_w_q, peer_keys, peer_u, peer_v, norm_final_g):
    depth = w_in.shape[0]
    assert depth == 1, "the final norm is fused into the last layer's PEER kernel"
    bp, sp, dm = x_prompt.shape
    bs_, ss, _ = x_sample.shape
    g, p = ssm_lambda_re.shape[1:]
    d_ssm = ssm_d.shape[1]
    d_sgu = sgu_ln_g.shape[1]
    n_state = g * p
    l = 0
    lp = (norm_mix_g[l].reshape(1, dm), w_in[l].astype(BF16),
          _s5_tables(ssm_lambda_re[l], ssm_lambda_im[l], ssm_log_dt[l], ssm_b_re[l], ssm_b_im[l],
                     ssm_c_re[l], ssm_c_im[l], ssm_d[l], ssm_glu_w[l], ssm_glu_b[l]),
          sgu_ln_g[l].reshape(1, d_sgu), sgu_ln_b[l].reshape(1, d_sgu), sgu_w[l], sgu_b[l],
          w_branch_a[l].astype(BF16), w_branch_b[l].astype(BF16), w_out[l].astype(BF16),
          norm_ffn_g[l].reshape(1, dm), peer_w_q[l].astype(BF16), peer_keys[l].astype(BF16),
          _pack_tables(peer_u[l].astype(F32), peer_v[l].astype(F32)),
          d_ssm, d_sgu)
    gf = norm_final_g.reshape(1, dm)

    xp2 = x_prompt.reshape(bp * sp, dm)
    xs2 = x_sample.reshape(bs_ * ss, dm)
    h0s = (state_ssm_re[l].astype(F32).reshape(bs_, n_state), state_ssm_im[l].astype(F32).reshape(bs_, n_state))
    split = bp == 1 and sp >= 4 * SC_SPLIT_ALIGN and (bs_ * ss) % SC_SUBCORES == 0
    if split:
        sample, hr_s, hi_s, v_s = _trunk_to_route(xs2, h0s[0], h0s[1], ss, lp)
        first, ya, hr_p, hi_p = _prompt_split(xp2, sp, lp, gf, n_state)
        ys = _prompt_finish([sample], None, lp, gf)
        yp = _prompt_finish(first, ya, lp, gf)
    else:
        zeros = jnp.zeros((bp, n_state), F32)
        yp, hr_p, hi_p, _ = _trunk_layer(xp2, zeros, zeros, sp, lp, gf)
        ys, hr_s, hi_s, v_s = _trunk_layer(xs2, h0s[0], h0s[1], ss, lp, gf)
    return (yp.reshape(bp, sp, dm), ys.reshape(bs_, ss, dm),
            hr_p.reshape(1, bp, g, p), hi_p.reshape(1, bp, g, p),
            hr_s.reshape(1, bs_, g, p), hi_s.reshape(1, bs_, g, p),
            v_s.reshape(1, bs_, ss, d_sgu))
```

```python
import functools
import math

import jax
import jax.numpy as jnp
from jax import lax
from jax.experimental import pallas as pl
from jax.experimental.pallas import tpu as pltpu
from jax.experimental.pallas import tpu_sc as plsc

F32 = jnp.float32
BF16 = jnp.bfloat16
I32 = jnp.int32
U32 = jnp.uint32
EPS = 1e-6

LANES = 128
SUBLANES = 8
VMEM_LIMIT = 56 * 1024 * 1024

SSM_SLABS = 4
SCAN_LANES = 512
PEER_TOPK = 16
SGU_CHUNK = 128
ROUTE_LANES = 256
INPROJ_SUB = 128
PACK_ROWS = 512


def _pick(n, candidates):
    for c in candidates:
        if n % c == 0:
            return c
    raise ValueError(f"no block size for {n} in {candidates}")


def _const_spec(shape):
    nd = len(shape)
    return pl.BlockSpec(shape, lambda *_: (0,) * nd, pipeline_mode=pl.Buffered(1))


def _rms(x, g):
    return x * lax.rsqrt(jnp.mean(x * x, axis=-1, keepdims=True) + EPS) * g


def _inproj_kernel(x_ref, g_ref, w_ref, lng_ref, lnb_ref, u_ref, su_ref, vn_ref, gates_ref, xn_sc):
    j = pl.program_id(1)
    tm = x_ref.shape[0]
    sub = min(tm, INPROJ_SUB)

    @pl.when(j == 0)
    def _():
        xn_sc[...] = _rms(x_ref[...], g_ref[...]).astype(BF16)

    def section(out_ref, epilogue):
        for r in range(tm // sub):
            rows = slice(r * sub, (r + 1) * sub)
            z = jnp.dot(xn_sc[rows, :], w_ref[...], preferred_element_type=F32)
            out_ref[rows, :] = epilogue(z).astype(out_ref.dtype)

    def layer_norm_gelu(z):
        v = jax.nn.gelu(z)
        mu = jnp.mean(v, axis=-1, keepdims=True)
        vc = v - mu
        y = vc * lax.rsqrt(jnp.mean(vc * vc, axis=-1, keepdims=True) + EPS)
        return y * lng_ref[...] + lnb_ref[...]

    @pl.when(j == 0)
    def _():
        section(u_ref, lambda z: z)

    @pl.when(j == 1)
    def _():
        section(su_ref, jax.nn.gelu)

    @pl.when(j == 2)
    def _():
        section(vn_ref, layer_norm_gelu)

    @pl.when(j >= 3)
    def _():
        section(gates_ref, jax.nn.sigmoid)


def _inproj(x, window, g, w_in_bf, ln_g, ln_b, d_ssm, d_sgu):
    row0, rows = window
    dm = x.shape[1]
    d_in = w_in_bf.shape[1]
    tn = d_ssm
    assert d_sgu == tn and (d_in - 3 * tn) % tn == 0
    nj = d_in // tn
    tm = _pick(math.gcd(rows, row0) if row0 else rows, (1024, 512, 256, 128))
    blk0 = row0 // tm
    return pl.pallas_call(
        _inproj_kernel,
        grid=(rows // tm, nj),
        in_specs=[
            pl.BlockSpec((tm, dm), lambda i, j: (i + blk0, 0)),
            pl.BlockSpec((1, dm), lambda i, j: (0, 0)),
            pl.BlockSpec((dm, tn), lambda i, j: (0, j)),
            pl.BlockSpec((1, tn), lambda i, j: (0, 0)),
            pl.BlockSpec((1, tn), lambda i, j: (0, 0)),
        ],
        out_specs=[
            pl.BlockSpec((tm, tn), lambda i, j: (i, 0)),
            pl.BlockSpec((tm, tn), lambda i, j: (i, 0)),
            pl.BlockSpec((tm, tn), lambda i, j: (i, 0)),
            pl.BlockSpec((tm, tn), lambda i, j: (i, jnp.maximum(j - 3, 0))),
        ],
        out_shape=[
            jax.ShapeDtypeStruct((rows, tn), F32),
            jax.ShapeDtypeStruct((rows, tn), BF16),
            jax.ShapeDtypeStruct((rows, tn), F32),
            jax.ShapeDtypeStruct((rows, d_in - 3 * tn), BF16),
        ],
        scratch_shapes=[pltpu.VMEM((tm, dm), BF16)],
        compiler_params=pltpu.CompilerParams(
            dimension_semantics=("arbitrary", "arbitrary"), vmem_limit_bytes=VMEM_LIMIT),
        name="inproj",
    )(x, g, w_in_bf, ln_g, ln_b)


def _s5_kernel(u_ref, h0r_ref, h0i_ref, bs_ref, cs_ref, d_ref, gluw_ref, glub_ref, m_ref, pw_ref,
               ya_ref, hlr_ref, hli_ref, hr_sc, hi_sc, cr_sc, ci_sc, *, seq_len):
    i = pl.program_id(0)
    tl, d_ssm = u_ref.shape
    n_state = hr_sc.shape[1]
    kin = d_ssm // SSM_SLABS
    kst = n_state // SSM_SLABS

    @pl.when(i == 0)
    def _():
        cr_sc[...] = jnp.zeros_like(cr_sc)
        ci_sc[...] = jnp.zeros_like(ci_sc)

    ub = u_ref[...].astype(BF16)
    for s in range(SSM_SLABS):
        r = jnp.dot(ub[:, kin * s:kin * (s + 1)], bs_ref[s], preferred_element_type=F32)
        hr_sc[:, kst * s:kst * (s + 1)] = r[:, :kst]
        hi_sc[:, kst * s:kst * (s + 1)] = r[:, kst:]

    seg_rows = min(tl, seq_len)
    seg_tiles = seg_rows // SUBLANES
    for lc in range(n_state // SCAN_LANES):
        ls = slice(lc * SCAN_LANES, (lc + 1) * SCAN_LANES)

        def tile_body(t, carry, ls=ls):
            cr, ci = carry
            r0 = pl.multiple_of(t * SUBLANES, SUBLANES)
            br = hr_sc[pl.ds(r0, SUBLANES), ls]
            bi = hi_sc[pl.ds(r0, SUBLANES), ls]
            for k, shift in enumerate((1, 2, 4)):
                mr = m_ref[2 * k, :, ls]
                mi = m_ref[2 * k + 1, :, ls]
                sr = pltpu.roll(br, shift, 0)
                si = pltpu.roll(bi, shift, 0)
                br, bi = br + (mr * sr - mi * si), bi + (mr * si + mi * sr)
            pr = pw_ref[0, :, ls]
            pi_ = pw_ref[1, :, ls]
            hr = br + (pr * cr - pi_ * ci)
            hi = bi + (pr * ci + pi_ * cr)
            hr_sc[pl.ds(r0, SUBLANES), ls] = hr
            hi_sc[pl.ds(r0, SUBLANES), ls] = hi
            return (jnp.broadcast_to(hr[SUBLANES - 1:SUBLANES, :], (SUBLANES, SCAN_LANES)),
                    jnp.broadcast_to(hi[SUBLANES - 1:SUBLANES, :], (SUBLANES, SCAN_LANES)))

        def segment(q, _, ls=ls):
            row0 = i * tl + q * seg_rows
            b = row0 // seq_len
            is_start = (row0 % seq_len) == 0
            h0r = jnp.broadcast_to(h0r_ref[pl.ds(b, 1), ls], (SUBLANES, SCAN_LANES))
            h0i = jnp.broadcast_to(h0i_ref[pl.ds(b, 1), ls], (SUBLANES, SCAN_LANES))
            carry = (jnp.where(is_start, h0r, cr_sc[:, ls]), jnp.where(is_start, h0i, ci_sc[:, ls]))
            cr, ci = lax.fori_loop(q * seg_tiles, (q + 1) * seg_tiles, tile_body, carry)
            cr_sc[:, ls] = cr
            ci_sc[:, ls] = ci
            hlr_ref[pl.ds(b, 1), ls] = cr[0:1, :]
            hli_ref[pl.ds(b, 1), ls] = ci[0:1, :]
            return 0

        if tl == seg_rows:
            segment(0, 0)
        else:
            lax.fori_loop(0, tl // seg_rows, segment, 0)

    ys = []
    for s in range(SSM_SLABS):
        hre = hr_sc[:, kst * s:kst * (s + 1)].astype(BF16)
        him = hi_sc[:, kst * s:kst * (s + 1)].astype(BF16)
        ys.append(jnp.dot(hre, cs_ref[s, :kst, :], preferred_element_type=F32)
                  + jnp.dot(him, cs_ref[s, kst:, :], preferred_element_type=F32))
    y = jnp.concatenate(ys, axis=1)
    y = jax.nn.gelu(y + d_ref[...] * u_ref[...])
    gt = jnp.dot(y.astype(BF16), gluw_ref[...], preferred_element_type=F32) + glub_ref[...]
    ya_ref[...] = (y * jax.nn.sigmoid(gt)).astype(BF16)


def _s5(u, h0r, h0i, tabs, seq_len):
    rows, d_ssm = u.shape
    nb, n_state = h0r.shape
    tl = _pick(rows, (256, 128, 64, 32, 16, 8)) if seq_len >= 256 else _pick(rows, (512, 256, 128, 64, 32, 16))
    assert (seq_len % tl == 0 or tl % seq_len == 0) and seq_len % SUBLANES == 0
    bs, cs, d, gluw, glub, mtab, ptab = tabs
    return pl.pallas_call(
        functools.partial(_s5_kernel, seq_len=seq_len),
        grid=(rows // tl,),
        in_specs=[
            pl.BlockSpec((tl, d_ssm), lambda i: (i, 0)),
            _const_spec((nb, n_state)),
            _const_spec((nb, n_state)),
            _const_spec(bs.shape),
            _const_spec(cs.shape),
            _const_spec(d.shape),
            _const_spec(gluw.shape),
            _const_spec(glub.shape),
            _const_spec(mtab.shape),
            _const_spec(ptab.shape),
        ],
        out_specs=[
            pl.BlockSpec((tl, d_ssm), lambda i: (i, 0)),
            pl.BlockSpec((nb, n_state), lambda i: (0, 0)),
            pl.BlockSpec((nb, n_state), lambda i: (0, 0)),
        ],
        out_shape=[
            jax.ShapeDtypeStruct((rows, d_ssm), BF16),
            jax.ShapeDtypeStruct((nb, n_state), F32),
            jax.ShapeDtypeStruct((nb, n_state), F32),
        ],
        scratch_shapes=[
            pltpu.VMEM((tl, n_state), F32),
            pltpu.VMEM((tl, n_state), F32),
            pltpu.VMEM((SUBLANES, n_state), F32),
            pltpu.VMEM((SUBLANES, n_state), F32),
        ],
        compiler_params=pltpu.CompilerParams(
            dimension_semantics=("arbitrary",), vmem_limit_bytes=VMEM_LIMIT),
        name="s5",
    )(u, h0r, h0i, bs, cs, d, gluw, glub, mtab, ptab)


def _mix_kernel(x_ref, ya_ref, su_ref, vn_ref, gates_ref, ws_ref, bias_ref, wa_ref, wb_ref, wo_ref, x1_ref):
    tm, dm = x_ref.shape
    n_heads = ws_ref.shape[0]
    hd = vn_ref.shape[1] // n_heads
    vb = vn_ref[...].astype(BF16)
    chunks = []
    for c in range(tm // SGU_CHUNK):
        heads = [jnp.dot(ws_ref[h], vb[c * SGU_CHUNK:(c + 1) * SGU_CHUNK, h * hd:(h + 1) * hd],
                         preferred_element_type=F32) for h in range(n_heads)]
        chunks.append(jnp.concatenate(heads, axis=1) + bias_ref[...])
    s_all = jnp.concatenate(chunks, axis=0)
    yb = (su_ref[...].astype(F32) * s_all).astype(BF16)
    pa = jnp.dot(ya_ref[...], wa_ref[...], preferred_element_type=F32)
    pb = jnp.dot(yb, wb_ref[...], preferred_element_type=F32)
    merged = gates_ref[:, :dm].astype(F32) * pa + gates_ref[:, dm:].astype(F32) * pb
    x1_ref[...] = x_ref[...] + jnp.dot(merged.astype(BF16), wo_ref[...], preferred_element_type=F32)


def _mix(x, window, ya, su, vn, gates, ws, bias, wa, wb, wo):
    row0, rows = window
    dm = x.shape[1]
    d_ssm = ya.shape[1]
    d_sgu = su.shape[1]
    tm = _pick(math.gcd(rows, row0) if row0 else rows, (256, 128))
    blk0 = row0 // tm
    row = lambda w: pl.BlockSpec((tm, w), lambda i: (i, 0))
    return pl.pallas_call(
        _mix_kernel,
        grid=(rows // tm,),
        in_specs=[pl.BlockSpec((tm, dm), lambda i: (i + blk0, 0)), row(d_ssm), row(d_sgu), row(d_sgu), row(2 * dm),
                  _const_spec(ws.shape), _const_spec(bias.shape),
                  _const_spec(wa.shape), _const_spec(wb.shape), _const_spec(wo.shape)],
        out_specs=row(dm),
        out_shape=jax.ShapeDtypeStruct((rows, dm), F32),
        compiler_params=pltpu.CompilerParams(
            dimension_semantics=("arbitrary",), vmem_limit_bytes=VMEM_LIMIT),
        name="mix",
    )(x, ya, su, vn, gates, ws, bias, wa, wb, wo)


def _topk_rows(s, k, payload=None):
    rows = s.shape[0]
    iota = lax.broadcasted_iota(I32, s.shape, 0).astype(F32)
    vals, outs = [], []
    for _ in range(k):
        m = jnp.max(s, axis=0, keepdims=True)
        j = jnp.min(jnp.where(s == m, iota, float(rows)), axis=0, keepdims=True)
        sel = iota == j
        vals.append(m)
        outs.append(j if payload is None else jnp.max(jnp.where(sel, payload, -1.0), axis=0, keepdims=True))
        s = jnp.where(sel, -jnp.inf, s)
    return jnp.concatenate(vals, axis=0), jnp.concatenate(outs, axis=0)


def _route_kernel(x1_ref, g_ref, wq_ref, keys_ref, xn_ref, eidx_ref, gate_ref, q_sc, e_sc, p_sc):
    tb = x1_ref.shape[0]
    tw = e_sc.shape[2]
    nsub = tb // tw
    n_heads, _, n_keys, half = keys_ref.shape
    dkey = 2 * half
    xn = _rms(x1_ref[...], g_ref[...])
    xn_ref[...] = xn
    q = jnp.dot(xn.astype(BF16), wq_ref[...], preferred_element_type=F32).astype(BF16)
    for h in range(n_heads):
        q_sc[h] = q[:, h * dkey:(h + 1) * dkey]

    nt = (((1,), (1,)), ((), ()))

    def body(n, _):
        h = n // nsub
        r0 = pl.multiple_of((n % nsub) * tw, tw)
        qh = q_sc[h, pl.ds(r0, tw), :]
        s1 = lax.dot_general(keys_ref[h, 0], qh[:, :half], nt, preferred_element_type=F32)
        s2 = lax.dot_general(keys_ref[h, 1], qh[:, half:], nt, preferred_element_type=F32)
        v1, i1 = _topk_rows(s1, PEER_TOPK)
        v2, i2 = _topk_rows(s2, PEER_TOPK)
        cv, ci = [], []
        for a in range(PEER_TOPK):
            nb = PEER_TOPK // (a + 1)
            cv.append(v1[a:a + 1, :] + v2[:nb, :])
            ci.append(i1[a:a + 1, :] * float(n_keys) + i2[:nb, :])
        ts, te = _topk_rows(jnp.concatenate(cv, axis=0), PEER_TOPK, payload=jnp.concatenate(ci, axis=0))
        ex = jnp.exp(ts - ts[0:1, :])
        e_sc[n] = te
        p_sc[n] = ex / jnp.sum(ex, axis=0, keepdims=True)
        return 0

    lax.fori_loop(0, n_heads * nsub, body, 0)
    for c in range(nsub):
        rows = slice(c * tw, (c + 1) * tw)
        e_all = jnp.concatenate([e_sc[h * nsub + c] for h in range(n_heads)], axis=0)
        p_all = jnp.concatenate([p_sc[h * nsub + c] for h in range(n_heads)], axis=0)
        eidx_ref[rows, :] = e_all.T.astype(I32)
        gate_ref[rows, :] = p_all.T


def _route(x1, g, wq, keys_bf):
    rows, dm = x1.shape
    n_heads, _, n_keys, half = keys_bf.shape
    nsel = n_heads * PEER_TOPK
    tb = _pick(rows, (256, 128))
    tw = min(tb, ROUTE_LANES)
    row = lambda w: pl.BlockSpec((tb, w), lambda i: (i, 0))
    return pl.pallas_call(
        _route_kernel,
        grid=(rows // tb,),
        in_specs=[row(dm), _const_spec(g.shape), _const_spec(wq.shape), _const_spec(keys_bf.shape)],
        out_specs=[row(dm), row(nsel), row(nsel)],
        out_shape=[
            jax.ShapeDtypeStruct((rows, dm), F32),
            jax.ShapeDtypeStruct((rows, nsel), I32),
            jax.ShapeDtypeStruct((rows, nsel), F32),
        ],
        scratch_shapes=[
            pltpu.VMEM((n_heads, tb, 2 * half), BF16),
            pltpu.VMEM((n_heads * (tb // tw), PEER_TOPK, tw), F32),
            pltpu.VMEM((n_heads * (tb // tw), PEER_TOPK, tw), F32),
        ],
        compiler_params=pltpu.CompilerParams(
            dimension_semantics=("arbitrary",), vmem_limit_bytes=VMEM_LIMIT),
        name="route",
    )(x1, g, wq, keys_bf)


PEER_BUFS = 16


def _peer_kernel(eidx_hbm, gate_ref, xn_ref, x1_ref, gf_ref, tab_hbm, y_ref,
                 idx_sm, idx_sem, *scratch, final_norm):
    i = pl.program_id(0)
    bufs = scratch[:PEER_BUFS]
    sem, peer_sc = scratch[PEER_BUFS:]
    tbk, nsel = gate_ref.shape
    half = tab_hbm.shape[1] // 2
    nlt = half // LANES

    icp = pltpu.make_async_copy(eidx_hbm.at[i], idx_sm, idx_sem)
    icp.start()
    icp.wait()

    def issue(t, s):
        base = t * nsel
        for k in range(nsel):
            e = idx_sm[base + k]
            pltpu.make_async_copy(tab_hbm.at[pl.ds(e, 1), :], bufs[s].at[pl.ds(k, 1), :],
                                  sem.at[s]).start(priority=k % 2)

    def wait(s):
        pltpu.make_async_copy(tab_hbm.at[pl.ds(0, nsel), :], bufs[s], sem.at[s]).wait()

    hi_mask = jnp.int32(-65536)
    eye = (lax.broadcasted_iota(I32, (nsel, nsel), 0) == lax.broadcasted_iota(I32, (nsel, nsel), 1))

    def unpack(w):
        return (lax.bitcast_convert_type(w & hi_mask, F32),
                lax.bitcast_convert_type(w << 16, F32))

    def compute(t, s):
        buf = bufs[s]
        xrow = xn_ref[pl.ds(t, 1), :]
        acc = jnp.zeros((nsel, LANES), F32)
        for j in range(nlt):
            uh, ul = unpack(buf[:, j * LANES:(j + 1) * LANES])
            acc = acc + uh * xrow[:, j * LANES:(j + 1) * LANES] \
                      + ul * xrow[:, half + j * LANES:half + (j + 1) * LANES]
        act = jnp.sum(acc, axis=1, keepdims=True)
        grow = jnp.broadcast_to(gate_ref[pl.ds(t, 1), :], (nsel, nsel))
        gcol = jnp.sum(jnp.where(eye, grow, 0.0), axis=1, keepdims=True)
        wgt = jnp.broadcast_to(gcol * jax.nn.gelu(act), (nsel, LANES))
        t8 = pl.multiple_of((t // SUBLANES) * SUBLANES, SUBLANES)
        rsel = lax.broadcasted_iota(I32, (SUBLANES, LANES), 0) == (t % SUBLANES)

        def put(lane0, row):
            cur = peer_sc[pl.ds(t8, SUBLANES), lane0:lane0 + LANES]
            peer_sc[pl.ds(t8, SUBLANES), lane0:lane0 + LANES] = jnp.where(
                rsel, jnp.broadcast_to(row, (SUBLANES, LANES)), cur)

        for j in range(nlt):
            vh, vl = unpack(buf[:, half + j * LANES:half + (j + 1) * LANES])
            put(j * LANES, jnp.sum(vh * wgt, axis=0, keepdims=True))
            put(half + j * LANES, jnp.sum(vl * wgt, axis=0, keepdims=True))

    def step(t, s, prefetch):
        wait(s)
        if prefetch:
            issue(t + PEER_BUFS - 1, (s + PEER_BUFS - 1) % PEER_BUFS)
        compute(t, s)

    peer_sc[...] = jnp.zeros_like(peer_sc)
    for t in range(PEER_BUFS - 1):
        issue(t, t)

    def group(gi, _):
        for s in range(PEER_BUFS):
            step(gi * PEER_BUFS + s, s, True)
        return 0

    lax.fori_loop(0, tbk // PEER_BUFS - 1, group, 0)
    for s in range(PEER_BUFS):
        t = tbk - PEER_BUFS + s
        step(t, s, t + PEER_BUFS - 1 < tbk)

    xo = x1_ref[...] + peer_sc[...]
    y_ref[...] = _rms(xo, gf_ref[...]) if final_norm else xo


def _peer(eidx, gate, xn, x1, gf, table, final_norm):
    rows, dm = x1.shape
    nsel = gate.shape[1]
    tbk = _pick(rows, (256, 128, 64, 32, 16, 8))
    assert tbk % PEER_BUFS == 0 and tbk >= 2 * PEER_BUFS
    nblk = rows // tbk
    eidx_blk = eidx.reshape(nblk, tbk * nsel)
    row = lambda w: pl.BlockSpec((tbk, w), lambda i: (i, 0))
    return pl.pallas_call(
        functools.partial(_peer_kernel, final_norm=final_norm),
        grid=(nblk,),
        in_specs=[pl.BlockSpec(memory_space=pl.ANY), row(nsel), row(dm), row(dm),
                  _const_spec(gf.shape), pl.BlockSpec(memory_space=pl.ANY)],
        out_specs=row(dm),
        out_shape=jax.ShapeDtypeStruct((rows, dm), F32),
        scratch_shapes=[
            pltpu.SMEM((tbk * nsel,), I32),
            pltpu.SemaphoreType.DMA(()),
        ] + [pltpu.VMEM((nsel, table.shape[1]), table.dtype) for _ in range(PEER_BUFS)] + [
            pltpu.SemaphoreType.DMA((PEER_BUFS,)),
            pltpu.VMEM((tbk, dm), F32),
        ],
        compiler_params=pltpu.CompilerParams(
            dimension_semantics=("arbitrary",), vmem_limit_bytes=VMEM_LIMIT),
        name="peer",
    )(eidx_blk, gate, xn, x1, gf, table)


SC_L = 16
SC_WINDOW = SC_L
SC_SUBCORES = 32
SC_CUTS = (0.0625, 0.4375)
SC_SPLIT_ALIGN = 512


def _sc_peer(table, eidx, gate, xn):
    ntok, nsel = eidx.shape
    d = table.shape[1]
    half = d // 2
    nch = half // SC_L
    info = plsc.get_sparse_core_info()
    n_workers = info.num_cores * info.num_subcores
    n_win = nsel // SC_WINDOW
    tpw = ntok // n_workers
    assert tpw * n_workers == ntok and SC_WINDOW == SC_L and info.num_lanes == SC_L
    mesh = plsc.VectorSubcoreMesh(core_axis_name="core", subcore_axis_name="subcore")
    c0 = math.sqrt(2.0 / math.pi)

    @functools.partial(
        pl.kernel, out_type=jax.ShapeDtypeStruct((ntok, d), F32), mesh=mesh,
        scratch_types=[pltpu.VMEM((nsel,), I32), pltpu.VMEM((nsel,), F32), pltpu.VMEM((d,), F32),
                       pltpu.VMEM((2, SC_WINDOW, d), I32), pltpu.VMEM((d,), F32),
                       pltpu.SemaphoreType.DMA((2,))],
        compiler_params=pltpu.CompilerParams(needs_layout_passes=False),
        name="sc_peer")
    def k(tab_hbm, idx_hbm, gate_hbm, x_hbm, o_hbm, idx_v, gate_v, x_v, rows_v, out_v, gsem):
        wid = lax.axis_index("subcore") * info.num_cores + lax.axis_index("core")
        lane = lax.iota(I32, SC_L)
        zero = jnp.zeros((SC_L,), F32)

        def unpack(w):
            return (lax.bitcast_convert_type(w & jnp.int32(-65536), F32),
                    lax.bitcast_convert_type(w << 16, F32))

        @pl.loop(0, tpw)
        def _(tt):
            tok = wid * tpw + tt
            pltpu.sync_copy(idx_hbm.at[pl.ds(tok * nsel, nsel)], idx_v)
            pltpu.sync_copy(gate_hbm.at[pl.ds(tok * nsel, nsel)], gate_v)
            pltpu.sync_copy(x_hbm.at[tok], x_v)

            @pl.loop(0, d // SC_L)
            def _(c):
                out_v[pl.ds(c * SC_L, SC_L)] = zero

            def gather(w):
                return pltpu.make_async_copy(tab_hbm.at[idx_v.at[pl.ds(w * SC_WINDOW, SC_WINDOW)]],
                                             rows_v.at[w % 2], gsem.at[w % 2])

            gather(0).start()
            for w in range(n_win):
                gather(w).wait()
                if w + 1 < n_win:
                    gather(w + 1).start()
                slot = w % 2

                def dot_chunk(j, accs, slot=slot):
                    xa = x_v[pl.ds(j * SC_L, SC_L)]
                    xb = x_v[pl.ds(half + j * SC_L, SC_L)]
                    out = []
                    for r in range(SC_WINDOW):
                        uh, ul = unpack(rows_v[slot, r, pl.ds(j * SC_L, SC_L)])
                        out.append(accs[r] + (uh * xa + ul * xb))
                    return tuple(out)

                accs = lax.fori_loop(0, nch, dot_chunk, (zero,) * SC_WINDOW)
                act = zero
                for r in range(SC_WINDOW):
                    act = jnp.where(lane == r, jnp.sum(accs[r]), act)
                z = c0 * (act + 0.044715 * (act * act * act))
                tanh_z = 1.0 - 2.0 / (jnp.exp(2.0 * z) + 1.0)
                wgt = gate_v[pl.ds(w * SC_WINDOW, SC_WINDOW)] * (0.5 * act * (1.0 + tanh_z))
                splat = [jnp.full((SC_L,), jnp.sum(jnp.where(lane == r, wgt, 0.0)), F32)
                         for r in range(SC_WINDOW)]

                @pl.loop(0, nch)
                def _(j, slot=slot, splat=splat):
                    oh = zero
                    ol = zero
                    for r in range(SC_WINDOW):
                        vh, vl = unpack(rows_v[slot, r, pl.ds(half + j * SC_L, SC_L)])
                        oh = oh + splat[r] * vh
                        ol = ol + splat[r] * vl
                    out_v[pl.ds(j * SC_L, SC_L)] += oh
                    out_v[pl.ds(half + j * SC_L, SC_L)] += ol

            pltpu.sync_copy(out_v, o_hbm.at[tok])

    return k(table, eidx.reshape(-1), gate.reshape(-1), xn)


def _finish_kernel(x1_ref, p_ref, gf_ref, y_ref):
    y_ref[...] = _rms(x1_ref[...] + p_ref[...], gf_ref[...])


def _finish(x1, p, gf):
    rows, dm = x1.shape
    tm = _pick(rows, (512, 256, 128))
    row = pl.BlockSpec((tm, dm), lambda i: (i, 0))
    return pl.pallas_call(
        _finish_kernel, grid=(rows // tm,), in_specs=[row, row, _const_spec(gf.shape)], out_specs=row,
        out_shape=jax.ShapeDtypeStruct((rows, dm), F32),
        compiler_params=pltpu.CompilerParams(dimension_semantics=("arbitrary",), vmem_limit_bytes=VMEM_LIMIT),
        name="finish",
    )(x1, p, gf)


def _cmul(ar, ai, br, bi):
    return ar * br - ai * bi, ar * bi + ai * br


def _s5_tables(lam_re, lam_im, log_dt, b_re, b_im, c_re, c_im, d, glu_w, glu_b):
    g, p = lam_re.shape
    hg = b_re.shape[-1]
    gs = g // SSM_SLABS
    dt = jnp.exp(log_dt.astype(F32))[:, None]
    lr = lam_re.astype(F32)
    li = lam_im.astype(F32)
    mag = jnp.exp(lr * dt)
    ab_re = mag * jnp.cos(li * dt)
    ab_im = mag * jnp.sin(li * dt)
    den = lr * lr + li * li
    nr = ab_re - 1.0
    ni = ab_im
    coef_re = (nr * lr + ni * li) / den
    coef_im = (ni * lr - nr * li) / den
    br = b_re.astype(F32)
    bi = b_im.astype(F32)
    bb_re = coef_re[..., None] * br - coef_im[..., None] * bi
    bb_im = coef_re[..., None] * bi + coef_im[..., None] * br
    eye = jnp.eye(gs, dtype=F32)

    def b_slab(bb):
        t = bb.reshape(SSM_SLABS, gs, p, hg).transpose(0, 1, 3, 2)
        return jnp.einsum('sihp,ij->sihjp', t, eye).reshape(SSM_SLABS, gs * hg, gs * p)

    bs = jnp.concatenate([b_slab(bb_re), b_slab(bb_im)], axis=2).astype(BF16)

    def c_slab(c):
        t = c.astype(F32).reshape(SSM_SLABS, gs, hg, p).transpose(0, 1, 3, 2)
        return jnp.einsum('siph,ij->sipjh', t, eye).reshape(SSM_SLABS, gs * p, gs * hg)

    cs = jnp.concatenate([c_slab(c_re), -c_slab(c_im)], axis=1).astype(BF16)

    a1 = (ab_re.reshape(1, g * p), ab_im.reshape(1, g * p))
    a2 = _cmul(*a1, *a1)
    a4 = _cmul(*a2, *a2)
    rowid = jnp.arange(SUBLANES)[:, None]
    mtab = jnp.stack([jnp.where(rowid >= s, comp, 0.0)
                      for s, a in ((1, a1), (2, a2), (4, a4)) for comp in a])
    pows = [a1]
    for _ in range(SUBLANES - 1):
        pows.append(_cmul(*pows[-1], *a1))
    ptab = jnp.stack([jnp.concatenate([q[0] for q in pows], axis=0),
                      jnp.concatenate([q[1] for q in pows], axis=0)])
    return (bs, cs, d.astype(F32).reshape(1, -1), glu_w.astype(BF16),
            glu_b.astype(F32).reshape(1, -1), mtab, ptab)


def _sgu_tables(w_s, b_s, seq_len):
    n_heads = w_s.shape[0]
    ln = min(SGU_CHUNK, seq_len)
    assert SGU_CHUNK % ln == 0 and seq_len % ln == 0
    rep = SGU_CHUNK // ln
    mask = jnp.tril(jnp.ones((ln, ln), dtype=bool))
    w = jnp.where(mask[None], w_s[:, :ln, :ln], 0.0).astype(F32)
    ws = jnp.einsum('hij,ab->haibj', w, jnp.eye(rep, dtype=F32)).reshape(n_heads, SGU_CHUNK, SGU_CHUNK)
    bias = jnp.tile(jnp.transpose(b_s[:, :ln]).astype(F32), (rep, 1))
    return ws.astype(BF16), bias


def _pack_kernel(u_ref, v_ref, o_ref):
    half = u_ref.shape[1] // 2

    def bf16_bits(x):
        return lax.bitcast_convert_type(x.astype(BF16).astype(F32), U32)

    def pack(t_ref):
        w = bf16_bits(t_ref[:, :half]) | (bf16_bits(t_ref[:, half:]) >> 16)
        return lax.bitcast_convert_type(w, I32)

    o_ref[:, :half] = pack(u_ref)
    o_ref[:, half:] = pack(v_ref)


def _pack_tables(u, v):
    n, dd = u.shape
    tr = _pick(n, (PACK_ROWS, 256, 128, 8))
    spec = pl.BlockSpec((tr, dd), lambda i: (i, 0))
    return pl.pallas_call(
        _pack_kernel,
        grid=(n // tr,),
        in_specs=[spec, spec],
        out_specs=spec,
        out_shape=jax.ShapeDtypeStruct((n, dd), I32),
        compiler_params=pltpu.CompilerParams(
            dimension_semantics=("arbitrary",), vmem_limit_bytes=VMEM_LIMIT),
        name="pack",
    )(u, v)


def _trunk_to_route(x, h0r, h0i, seq_len, lp, window=None):
    (norm_mix_g, w_in_bf, s5_tabs, ln_g, ln_b, sgu_w, sgu_b, wa, wb, wo, norm_ffn_g, wq, keys_bf, _,
     d_ssm, d_sgu) = lp
    window = (0, x.shape[0]) if window is None else window
    u, su, vn, gates = _inproj(x, window, norm_mix_g, w_in_bf, ln_g, ln_b, d_ssm, d_sgu)
    ya, hlr, hli = _s5(u, h0r, h0i, s5_tabs, seq_len)
    ws, bias = _sgu_tables(sgu_w, sgu_b, seq_len)
    hd = d_sgu // ws.shape[0]
    bias_tile = jnp.repeat(bias, hd, axis=1)
    x1 = _mix(x, window, ya, su, vn, gates, ws, bias_tile, wa, wb, wo)
    xn, eidx, gate = _route(x1, norm_ffn_g, wq, keys_bf)
    return (x1, xn, eidx, gate), hlr, hli, vn


def _trunk_layer(x, h0r, h0i, seq_len, lp, final_g):
    (x1, xn, eidx, gate), hlr, hli, vn = _trunk_to_route(x, h0r, h0i, seq_len, lp)
    y = _peer(eidx, gate, xn, x1, final_g, lp[13], final_norm=True)
    return y, hlr, hli, vn


def _prompt_split(x, seq_len, lp, final_g, n_state):
    rows = x.shape[0]
    cuts = [(int(rows * c) // SC_SPLIT_ALIGN) * SC_SPLIT_ALIGN for c in SC_CUTS]
    bounds = [0] + cuts + [rows]
    hr = hi = jnp.zeros((1, n_state), F32)
    parts = []
    for a, b in zip(bounds[:-1], bounds[1:]):
        part, hr, hi, _ = _trunk_to_route(x, hr, hi, seq_len, lp, window=(a, b - a))
        parts.append(part)
    x1a, xna, ea, ga = parts[-1]
    ya = _peer(ea, ga, xna, x1a, final_g, lp[13], final_norm=True)
    return parts[:-1], ya, hr, hi


def _prompt_finish(firsts, ya, lp, final_g):
    ys = [_finish(x1, _sc_peer(lp[13], e, g, xn), final_g) for (x1, xn, e, g) in firsts]
    ys = ys if ya is None else ys + [ya]
    return ys[0] if len(ys) == 1 else jnp.concatenate(ys, axis=0)


def kernel(x_prompt, x_sample, state_ssm_re, state_ssm_im, norm_mix_g, w_in, ssm_lambda_re, ssm_lambda_im, ssm_log_dt, ssm_b_re, ssm_b_im, ssm_c_re, ssm_c_im, ssm_d, ssm_glu_w, ssm_glu_b, sgu_ln_g, sgu_ln_b, sgu_w, sgu_b, w_branch_a, w_branch_b, w_out, norm_ffn_g, peer_w_q, peer_keys, peer_u, peer_v, norm_final_g):
    depth = w_in.shape[0]
    assert depth == 1, "the final norm is fused into the last layer's PEER kernel"
    bp, sp, dm = x_prompt.shape
    bs_, ss, _ = x_sample.shape
    g, p = ssm_lambda_re.shape[1:]
    d_ssm = ssm_d.shape[1]
    d_sgu = sgu_ln_g.shape[1]
    n_state = g * p
    l = 0
    lp = (norm_mix_g[l].reshape(1, dm), w_in[l].astype(BF16),
          _s5_tables(ssm_lambda_re[l], ssm_lambda_im[l], ssm_log_dt[l], ssm_b_re[l], ssm_b_im[l],
                     ssm_c_re[l], ssm_c_im[l], ssm_d[l], ssm_glu_w[l], ssm_glu_b[l]),
          sgu_ln_g[l].reshape(1, d_sgu), sgu_ln_b[l].reshape(1, d_sgu), sgu_w[l], sgu_b[l],
          w_branch_a[l].astype(BF16), w_branch_b[l].astype(BF16), w_out[l].astype(BF16),
          norm_ffn_g[l].reshape(1, dm), peer_w_q[l].astype(BF16), peer_keys[l].astype(BF16),
          _pack_tables(peer_u[l].astype(F32), peer_v[l].astype(F32)),
          d_ssm, d_sgu)
    gf = norm_final_g.reshape(1, dm)

    xp2 = x_prompt.reshape(bp * sp, dm)
    xs2 = x_sample.reshape(bs_ * ss, dm)
    h0s = (state_ssm_re[l].astype(F32).reshape(bs_, n_state), state_ssm_im[l].astype(F32).reshape(bs_, n_state))
    split = bp == 1 and sp >= 4 * SC_SPLIT_ALIGN and (bs_ * ss) % SC_SUBCORES == 0
    if split:
        sample, hr_s, hi_s, v_s = _trunk_to_route(xs2, h0s[0], h0s[1], ss, lp)
        first, ya, hr_p, hi_p = _prompt_split(xp2, sp, lp, gf, n_state)
        ys = _prompt_finish([sample], None, lp, gf)
        yp = _prompt_finish(first, ya, lp, gf)
    else:
        zeros = jnp.zeros((bp, n_state), F32)
        yp, hr_p, hi_p, _ = _trunk_layer(xp2, zeros, zeros, sp, lp, gf)
        ys, hr_s, hi_s, v_s = _trunk_layer(xs2, h0s[0], h0s[1], ss, lp, gf)
    return (yp.reshape(bp, sp, dm), ys.reshape(bs_, ss, dm),
            hr_p.reshape(1, bp, g, p), hi_p.reshape(1, bp, g, p),
            hr_s.reshape(1, bs_, g, p), hi_s.reshape(1, bs_, g, p),
            v_s.reshape(1, bs_, ss, d_sgu))
```

```python
import functools
import math

import jax
import jax.numpy as jnp
from jax import lax
from jax.experimental import pallas as pl
from jax.experimental.pallas import tpu as pltpu
from jax.experimental.pallas import tpu_sc as plsc

F32 = jnp.float32
BF16 = jnp.bfloat16
I32 = jnp.int32
U32 = jnp.uint32
EPS = 1e-6

LANES = 128
SUBLANES = 8
VMEM_LIMIT = 56 * 1024 * 1024

SSM_SLABS = 4
SCAN_LANES = 512
PEER_TOPK = 16
SGU_CHUNK = 128
ROUTE_LANES = 256
INPROJ_SUB = 128
PACK_ROWS = 512


def _pick(n, candidates):
    for c in candidates:
        if n % c == 0:
            return c
    raise ValueError(f"no block size for {n} in {candidates}")


def _const_spec(shape):
    nd = len(shape)
    return pl.BlockSpec(shape, lambda *_: (0,) * nd, pipeline_mode=pl.Buffered(1))


def _rms(x, g):
    return x * lax.rsqrt(jnp.mean(x * x, axis=-1, keepdims=True) + EPS) * g


def _inproj_kernel(x_ref, g_ref, w_ref, lng_ref, lnb_ref, u_ref, su_ref, vn_ref, gates_ref, xn_sc):
    j = pl.program_id(1)
    tm = x_ref.shape[0]
    sub = min(tm, INPROJ_SUB)

    @pl.when(j == 0)
    def _():
        xn_sc[...] = _rms(x_ref[...], g_ref[...]).astype(BF16)

    def section(out_ref, epilogue):
        for r in range(tm // sub):
            rows = slice(r * sub, (r + 1) * sub)
            z = jnp.dot(xn_sc[rows, :], w_ref[...], preferred_element_type=F32)
            out_ref[rows, :] = epilogue(z).astype(out_ref.dtype)

    def layer_norm_gelu(z):
        v = jax.nn.gelu(z)
        mu = jnp.mean(v, axis=-1, keepdims=True)
        vc = v - mu
        y = vc * lax.rsqrt(jnp.mean(vc * vc, axis=-1, keepdims=True) + EPS)
        return y * lng_ref[...] + lnb_ref[...]

    @pl.when(j == 0)
    def _():
        section(u_ref, lambda z: z)

    @pl.when(j == 1)
    def _():
        section(su_ref, jax.nn.gelu)

    @pl.when(j == 2)
    def _():
        section(vn_ref, layer_norm_gelu)

    @pl.when(j >= 3)
    def _():
        section(gates_ref, jax.nn.sigmoid)


def _inproj(x, window, g, w_in_bf, ln_g, ln_b, d_ssm, d_sgu):
    row0, rows = window
    dm = x.shape[1]
    d_in = w_in_bf.shape[1]
    tn = d_ssm
    assert d_sgu == tn and (d_in - 3 * tn) % tn == 0
    nj = d_in // tn
    tm = _pick(math.gcd(rows, row0) if row0 else rows, (1024, 512, 256, 128))
    blk0 = row0 // tm
    return pl.pallas_call(
        _inproj_kernel,
        grid=(rows // tm, nj),
        in_specs=[
            pl.BlockSpec((tm, dm), lambda i, j: (i + blk0, 0)),
            pl.BlockSpec((1, dm), lambda i, j: (0, 0)),
            pl.BlockSpec((dm, tn), lambda i, j: (0, j)),
            pl.BlockSpec((1, tn), lambda i, j: (0, 0)),
            pl.BlockSpec((1, tn), lambda i, j: (0, 0)),
        ],
        out_specs=[
            pl.BlockSpec((tm, tn), lambda i, j: (i, 0)),
            pl.BlockSpec((tm, tn), lambda i, j: (i, 0)),
            pl.BlockSpec((tm, tn), lambda i, j: (i, 0)),
            pl.BlockSpec((tm, tn), lambda i, j: (i, jnp.maximum(j - 3, 0))),
        ],
        out_shape=[
            jax.ShapeDtypeStruct((rows, tn), F32),
            jax.ShapeDtypeStruct((rows, tn), BF16),
            jax.ShapeDtypeStruct((rows, tn), F32),
            jax.ShapeDtypeStruct((rows, d_in - 3 * tn), BF16),
        ],
        scratch_shapes=[pltpu.VMEM((tm, dm), BF16)],
        compiler_params=pltpu.CompilerParams(
            dimension_semantics=("arbitrary", "arbitrary"), vmem_limit_bytes=VMEM_LIMIT),
        name="inproj",
    )(x, g, w_in_bf, ln_g, ln_b)


def _s5_kernel(u_ref, h0r_ref, h0i_ref, bs_ref, cs_ref, d_ref, gluw_ref, glub_ref, m_ref, pw_ref,
               ya_ref, hlr_ref, hli_ref, hr_sc, hi_sc, cr_sc, ci_sc, *, seq_len):
    i = pl.program_id(0)
    tl, d_ssm = u_ref.shape
    n_state = hr_sc.shape[1]
    kin = d_ssm // SSM_SLABS
    kst = n_state // SSM_SLABS

    @pl.when(i == 0)
    def _():
        cr_sc[...] = jnp.zeros_like(cr_sc)
        ci_sc[...] = jnp.zeros_like(ci_sc)

    ub = u_ref[...].astype(BF16)
    for s in range(SSM_SLABS):
        r = jnp.dot(ub[:, kin * s:kin * (s + 1)], bs_ref[s], preferred_element_type=F32)
        hr_sc[:, kst * s:kst * (s + 1)] = r[:, :kst]
        hi_sc[:, kst * s:kst * (s + 1)] = r[:, kst:]

    seg_rows = min(tl, seq_len)
    seg_tiles = seg_rows // SUBLANES
    for lc in range(n_state // SCAN_LANES):
        ls = slice(lc * SCAN_LANES, (lc + 1) * SCAN_LANES)

        def tile_body(t, carry, ls=ls):
            cr, ci = carry
            r0 = pl.multiple_of(t * SUBLANES, SUBLANES)
            br = hr_sc[pl.ds(r0, SUBLANES), ls]
            bi = hi_sc[pl.ds(r0, SUBLANES), ls]
            for k, shift in enumerate((1, 2, 4)):
                mr = m_ref[2 * k, :, ls]
                mi = m_ref[2 * k + 1, :, ls]
                sr = pltpu.roll(br, shift, 0)
                si = pltpu.roll(bi, shift, 0)
                br, bi = br + (mr * sr - mi * si), bi + (mr * si + mi * sr)
            pr = pw_ref[0, :, ls]
            pi_ = pw_ref[1, :, ls]
            hr = br + (pr * cr - pi_ * ci)
            hi = bi + (pr * ci + pi_ * cr)
            hr_sc[pl.ds(r0, SUBLANES), ls] = hr
            hi_sc[pl.ds(r0, SUBLANES), ls] = hi
            return (jnp.broadcast_to(hr[SUBLANES - 1:SUBLANES, :], (SUBLANES, SCAN_LANES)),
                    jnp.broadcast_to(hi[SUBLANES - 1:SUBLANES, :], (SUBLANES, SCAN_LANES)))

        def segment(q, _, ls=ls):
            row0 = i * tl + q * seg_rows
            b = row0 // seq_len
            is_start = (row0 % seq_len) == 0
            h0r = jnp.broadcast_to(h0r_ref[pl.ds(b, 1), ls], (SUBLANES, SCAN_LANES))
            h0i = jnp.broadcast_to(h0i_ref[pl.ds(b, 1), ls], (SUBLANES, SCAN_LANES))
            carry = (jnp.where(is_start, h0r, cr_sc[:, ls]), jnp.where(is_start, h0i, ci_sc[:, ls]))
            cr, ci = lax.fori_loop(q * seg_tiles, (q + 1) * seg_tiles, tile_body, carry)
            cr_sc[:, ls] = cr
            ci_sc[:, ls] = ci
            hlr_ref[pl.ds(b, 1), ls] = cr[0:1, :]
            hli_ref[pl.ds(b, 1), ls] = ci[0:1, :]
            return 0

        if tl == seg_rows:
            segment(0, 0)
        else:
            lax.fori_loop(0, tl // seg_rows, segment, 0)

    ys = []
    for s in range(SSM_SLABS):
        hre = hr_sc[:, kst * s:kst * (s + 1)].astype(BF16)
        him = hi_sc[:, kst * s:kst * (s + 1)].astype(BF16)
        ys.append(jnp.dot(hre, cs_ref[s, :kst, :], preferred_element_type=F32)
                  + jnp.dot(him, cs_ref[s, kst:, :], preferred_element_type=F32))
    y = jnp.concatenate(ys, axis=1)
    y = jax.nn.gelu(y + d_ref[...] * u_ref[...])
    gt = jnp.dot(y.astype(BF16), gluw_ref[...], preferred_element_type=F32) + glub_ref[...]
    ya_ref[...] = (y * jax.nn.sigmoid(gt)).astype(BF16)


def _s5(u, h0r, h0i, tabs, seq_len):
    rows, d_ssm = u.shape
    nb, n_state = h0r.shape
    tl = _pick(rows, (256, 128, 64, 32, 16, 8)) if seq_len >= 256 else _pick(rows, (512, 256, 128, 64, 32, 16))
    assert (seq_len % tl == 0 or tl % seq_len == 0) and seq_len % SUBLANES == 0
    bs, cs, d, gluw, glub, mtab, ptab = tabs
    return pl.pallas_call(
        functools.partial(_s5_kernel, seq_len=seq_len),
        grid=(rows // tl,),
        in_specs=[
            pl.BlockSpec((tl, d_ssm), lambda i: (i, 0)),
            _const_spec((nb, n_state)),
            _const_spec((nb, n_state)),
            _const_spec(bs.shape),
            _const_spec(cs.shape),
            _const_spec(d.shape),
            _const_spec(gluw.shape),
            _const_spec(glub.shape),
            _const_spec(mtab.shape),
            _const_spec(ptab.shape),
        ],
        out_specs=[
            pl.BlockSpec((tl, d_ssm), lambda i: (i, 0)),
            pl.BlockSpec((nb, n_state), lambda i: (0, 0)),
            pl.BlockSpec((nb, n_state), lambda i: (0, 0)),
        ],
        out_shape=[
            jax.ShapeDtypeStruct((rows, d_ssm), BF16),
            jax.ShapeDtypeStruct((nb, n_state), F32),
            jax.ShapeDtypeStruct((nb, n_state), F32),
        ],
        scratch_shapes=[
            pltpu.VMEM((tl, n_state), F32),
            pltpu.VMEM((tl, n_state), F32),
            pltpu.VMEM((SUBLANES, n_state), F32),
            pltpu.VMEM((SUBLANES, n_state), F32),
        ],
        compiler_params=pltpu.CompilerParams(
            dimension_semantics=("arbitrary",), vmem_limit_bytes=VMEM_LIMIT),
        name="s5",
    )(u, h0r, h0i, bs, cs, d, gluw, glub, mtab, ptab)


def _mix_kernel(x_ref, ya_ref, su_ref, vn_ref, gates_ref, ws_ref, bias_ref, wa_ref, wb_ref, wo_ref, x1_ref):
    tm, dm = x_ref.shape
    n_heads = ws_ref.shape[0]
    hd = vn_ref.shape[1] // n_heads
    vb = vn_ref[...].astype(BF16)
    chunks = []
    for c in range(tm // SGU_CHUNK):
        heads = [jnp.dot(ws_ref[h], vb[c * SGU_CHUNK:(c + 1) * SGU_CHUNK, h * hd:(h + 1) * hd],
                         preferred_element_type=F32) for h in range(n_heads)]
        chunks.append(jnp.concatenate(heads, axis=1) + bias_ref[...])
    s_all = jnp.concatenate(chunks, axis=0)
    yb = (su_ref[...].astype(F32) * s_all).astype(BF16)
    pa = jnp.dot(ya_ref[...], wa_ref[...], preferred_element_type=F32)
    pb = jnp.dot(yb, wb_ref[...], preferred_element_type=F32)
    merged = gates_ref[:, :dm].astype(F32) * pa + gates_ref[:, dm:].astype(F32) * pb
    x1_ref[...] = x_ref[...] + jnp.dot(merged.astype(BF16), wo_ref[...], preferred_element_type=F32)


def _mix(x, window, ya, su, vn, gates, ws, bias, wa, wb, wo):
    row0, rows = window
    dm = x.shape[1]
    d_ssm = ya.shape[1]
    d_sgu = su.shape[1]
    tm = _pick(math.gcd(rows, row0) if row0 else rows, (256, 128))
    blk0 = row0 // tm
    row = lambda w: pl.BlockSpec((tm, w), lambda i: (i, 0))
    return pl.pallas_call(
        _mix_kernel,
        grid=(rows // tm,),
        in_specs=[pl.BlockSpec((tm, dm), lambda i: (i + blk0, 0)), row(d_ssm), row(d_sgu), row(d_sgu), row(2 * dm),
                  _const_spec(ws.shape), _const_spec(bias.shape),
                  _const_spec(wa.shape), _const_spec(wb.shape), _const_spec(wo.shape)],
        out_specs=row(dm),
        out_shape=jax.ShapeDtypeStruct((rows, dm), F32),
        compiler_params=pltpu.CompilerParams(
            dimension_semantics=("arbitrary",), vmem_limit_bytes=VMEM_LIMIT),
        name="mix",
    )(x, ya, su, vn, gates, ws, bias, wa, wb, wo)


def _topk_rows(s, k, payload=None):
    rows = s.shape[0]
    iota = lax.broadcasted_iota(I32, s.shape, 0).astype(F32)
    vals, outs = [], []
    for _ in range(k):
        m = jnp.max(s, axis=0, keepdims=True)
        j = jnp.min(jnp.where(s == m, iota, float(rows)), axis=0, keepdims=True)
        sel = iota == j
        vals.append(m)
        outs.append(j if payload is None else jnp.max(jnp.where(sel, payload, -1.0), axis=0, keepdims=True))
        s = jnp.where(sel, -jnp.inf, s)
    return jnp.concatenate(vals, axis=0), jnp.concatenate(outs, axis=0)


def _route_kernel(x1_ref, g_ref, wq_ref, keys_ref, xn_ref, eidx_ref, gate_ref, q_sc, e_sc, p_sc):
    tb = x1_ref.shape[0]
    tw = e_sc.shape[2]
    nsub = tb // tw
    n_heads, _, n_keys, half = keys_ref.shape
    dkey = 2 * half
    xn = _rms(x1_ref[...], g_ref[...])
    xn_ref[...] = xn
    q = jnp.dot(xn.astype(BF16), wq_ref[...], preferred_element_type=F32).astype(BF16)
    for h in range(n_heads):
        q_sc[h] = q[:, h * dkey:(h + 1) * dkey]

    nt = (((1,), (1,)), ((), ()))

    def body(n, _):
        h = n // nsub
        r0 = pl.multiple_of((n % nsub) * tw, tw)
        qh = q_sc[h, pl.ds(r0, tw), :]
        s1 = lax.dot_general(keys_ref[h, 0], qh[:, :half], nt, preferred_element_type=F32)
        s2 = lax.dot_general(keys_ref[h, 1], qh[:, half:], nt, preferred_element_type=F32)
        v1, i1 = _topk_rows(s1, PEER_TOPK)
        v2, i2 = _topk_rows(s2, PEER_TOPK)
        cv, ci = [], []
        for a in range(PEER_TOPK):
            nb = PEER_TOPK // (a + 1)
            cv.append(v1[a:a + 1, :] + v2[:nb, :])
            ci.append(i1[a:a + 1, :] * float(n_keys) + i2[:nb, :])
        ts, te = _topk_rows(jnp.concatenate(cv, axis=0), PEER_TOPK, payload=jnp.concatenate(ci, axis=0))
        ex = jnp.exp(ts - ts[0:1, :])
        e_sc[n] = te
        p_sc[n] = ex / jnp.sum(ex, axis=0, keepdims=True)
        return 0

    lax.fori_loop(0, n_heads * nsub, body, 0)
    for c in range(nsub):
        rows = slice(c * tw, (c + 1) * tw)
        e_all = jnp.concatenate([e_sc[h * nsub + c] for h in range(n_heads)], axis=0)
        p_all = jnp.concatenate([p_sc[h * nsub + c] for h in range(n_heads)], axis=0)
        eidx_ref[rows, :] = e_all.T.astype(I32)
        gate_ref[rows, :] = p_all.T


def _route(x1, g, wq, keys_bf):
    rows, dm = x1.shape
    n_heads, _, n_keys, half = keys_bf.shape
    nsel = n_heads * PEER_TOPK
    tb = _pick(rows, (256, 128))
    tw = min(tb, ROUTE_LANES)
    row = lambda w: pl.BlockSpec((tb, w), lambda i: (i, 0))
    return pl.pallas_call(
        _route_kernel,
        grid=(rows // tb,),
        in_specs=[row(dm), _const_spec(g.shape), _const_spec(wq.shape), _const_spec(keys_bf.shape)],
        out_specs=[row(dm), row(nsel), row(nsel)],
        out_shape=[
            jax.ShapeDtypeStruct((rows, dm), F32),
            jax.ShapeDtypeStruct((rows, nsel), I32),
            jax.ShapeDtypeStruct((rows, nsel), F32),
        ],
        scratch_shapes=[
            pltpu.VMEM((n_heads, tb, 2 * half), BF16),
            pltpu.VMEM((n_heads * (tb // tw), PEER_TOPK, tw), F32),
            pltpu.VMEM((n_heads * (tb // tw), PEER_TOPK, tw), F32),
        ],
        compiler_params=pltpu.CompilerParams(
            dimension_semantics=("arbitrary",), vmem_limit_bytes=VMEM_LIMIT),
        name="route",
    )(x1, g, wq, keys_bf)


PEER_BUFS = 16


def _peer_kernel(eidx_hbm, gate_ref, xn_ref, x1_ref, gf_ref, tab_hbm, y_ref,
                 idx_sm, idx_sem, *scratch, final_norm):
    i = pl.program_id(0)
    bufs = scratch[:PEER_BUFS]
    sem, peer_sc = scratch[PEER_BUFS:]
    tbk, nsel = gate_ref.shape
    half = tab_hbm.shape[1] // 2
    nlt = half // LANES

    icp = pltpu.make_async_copy(eidx_hbm.at[i], idx_sm, idx_sem)
    icp.start()
    icp.wait()

    def issue(t, s):
        base = t * nsel
        for k in range(nsel):
            e = idx_sm[base + k]
            pltpu.make_async_copy(tab_hbm.at[pl.ds(e, 1), :], bufs[s].at[pl.ds(k, 1), :],
                                  sem.at[s]).start(priority=k % 2)

    def wait(s):
        pltpu.make_async_copy(tab_hbm.at[pl.ds(0, nsel), :], bufs[s], sem.at[s]).wait()

    hi_mask = jnp.int32(-65536)
    eye = (lax.broadcasted_iota(I32, (nsel, nsel), 0) == lax.broadcasted_iota(I32, (nsel, nsel), 1))

    def unpack(w):
        return (lax.bitcast_convert_type(w & hi_mask, F32),
                lax.bitcast_convert_type(w << 16, F32))

    def compute(t, s):
        buf = bufs[s]
        xrow = xn_ref[pl.ds(t, 1), :]
        acc = jnp.zeros((nsel, LANES), F32)
        for j in range(nlt):
            uh, ul = unpack(buf[:, j * LANES:(j + 1) * LANES])
            acc = acc + uh * xrow[:, j * LANES:(j + 1) * LANES] \
                      + ul * xrow[:, half + j * LANES:half + (j + 1) * LANES]
        act = jnp.sum(acc, axis=1, keepdims=True)
        grow = jnp.broadcast_to(gate_ref[pl.ds(t, 1), :], (nsel, nsel))
        gcol = jnp.sum(jnp.where(eye, grow, 0.0), axis=1, keepdims=True)
        wgt = jnp.broadcast_to(gcol * jax.nn.gelu(act), (nsel, LANES))
        t8 = pl.multiple_of((t // SUBLANES) * SUBLANES, SUBLANES)
        rsel = lax.broadcasted_iota(I32, (SUBLANES, LANES), 0) == (t % SUBLANES)

        def put(lane0, row):
            cur = peer_sc[pl.ds(t8, SUBLANES), lane0:lane0 + LANES]
            peer_sc[pl.ds(t8, SUBLANES), lane0:lane0 + LANES] = jnp.where(
                rsel, jnp.broadcast_to(row, (SUBLANES, LANES)), cur)

        for j in range(nlt):
            vh, vl = unpack(buf[:, half + j * LANES:half + (j + 1) * LANES])
            put(j * LANES, jnp.sum(vh * wgt, axis=0, keepdims=True))
            put(half + j * LANES, jnp.sum(vl * wgt, axis=0, keepdims=True))

    def step(t, s, prefetch):
        wait(s)
        if prefetch:
            issue(t + PEER_BUFS - 1, (s + PEER_BUFS - 1) % PEER_BUFS)
        compute(t, s)

    peer_sc[...] = jnp.zeros_like(peer_sc)
    for t in range(PEER_BUFS - 1):
        issue(t, t)

    def group(gi, _):
        for s in range(PEER_BUFS):
            step(gi * PEER_BUFS + s, s, True)
        return 0

    lax.fori_loop(0, tbk // PEER_BUFS - 1, group, 0)
    for s in range(PEER_BUFS):
        t = tbk - PEER_BUFS + s
        step(t, s, t + PEER_BUFS - 1 < tbk)

    xo = x1_ref[...] + peer_sc[...]
    y_ref[...] = _rms(xo, gf_ref[...]) if final_norm else xo


def _peer(eidx, gate, xn, x1, gf, table, final_norm):
    rows, dm = x1.shape
    nsel = gate.shape[1]
    tbk = _pick(rows, (256, 128, 64, 32, 16, 8))
    assert tbk % PEER_BUFS == 0 and tbk >= 2 * PEER_BUFS
    nblk = rows // tbk
    eidx_blk = eidx.reshape(nblk, tbk * nsel)
    row = lambda w: pl.BlockSpec((tbk, w), lambda i: (i, 0))
    return pl.pallas_call(
        functools.partial(_peer_kernel, final_norm=final_norm),
        grid=(nblk,),
        in_specs=[pl.BlockSpec(memory_space=pl.ANY), row(nsel), row(dm), row(dm),
                  _const_spec(gf.shape), pl.BlockSpec(memory_space=pl.ANY)],
        out_specs=row(dm),
        out_shape=jax.ShapeDtypeStruct((rows, dm), F32),
        scratch_shapes=[
            pltpu.SMEM((tbk * nsel,), I32),
            pltpu.SemaphoreType.DMA(()),
        ] + [pltpu.VMEM((nsel, table.shape[1]), table.dtype) for _ in range(PEER_BUFS)] + [
            pltpu.SemaphoreType.DMA((PEER_BUFS,)),
            pltpu.VMEM((tbk, dm), F32),
        ],
        compiler_params=pltpu.CompilerParams(
            dimension_semantics=("arbitrary",), vmem_limit_bytes=VMEM_LIMIT),
        name="peer",
    )(eidx_blk, gate, xn, x1, gf, table)


SC_L = 16
SC_WINDOW = SC_L
SC_ROW_BUFS = 3
SC_SUBCORES = 32
SC_CUTS = (0.0625, 0.4375)
SC_SPLIT_ALIGN = 512


def _sc_peer(table, eidx, gate, xn):
    ntok, nsel = eidx.shape
    d = table.shape[1]
    half = d // 2
    nch = half // SC_L
    info = plsc.get_sparse_core_info()
    n_workers = info.num_cores * info.num_subcores
    n_win = nsel // SC_WINDOW
    tpw = ntok // n_workers
    assert tpw * n_workers == ntok and SC_WINDOW == SC_L and info.num_lanes == SC_L
    mesh = plsc.VectorSubcoreMesh(core_axis_name="core", subcore_axis_name="subcore")
    c0 = math.sqrt(2.0 / math.pi)

    @functools.partial(
        pl.kernel, out_type=jax.ShapeDtypeStruct((ntok, d), F32), mesh=mesh,
        scratch_types=[pltpu.VMEM((nsel,), I32), pltpu.VMEM((nsel,), F32), pltpu.VMEM((d,), F32),
                       pltpu.VMEM((SC_ROW_BUFS, SC_WINDOW, d), I32), pltpu.VMEM((d,), F32),
                       pltpu.SemaphoreType.DMA((SC_ROW_BUFS,)), pltpu.SemaphoreType.DMA((4,))],
        compiler_params=pltpu.CompilerParams(needs_layout_passes=False),
        name="sc_peer")
    def k(tab_hbm, idx_hbm, gate_hbm, x_hbm, o_hbm, idx_v, gate_v, x_v, rows_v, out_v, gsem, lsem):
        wid = lax.axis_index("subcore") * info.num_cores + lax.axis_index("core")
        lane = lax.iota(I32, SC_L)
        zero = jnp.zeros((SC_L,), F32)

        def unpack(w):
            return (lax.bitcast_convert_type(w & jnp.int32(-65536), F32),
                    lax.bitcast_convert_type(w << 16, F32))

        @pl.loop(0, tpw)
        def _(tt):
            tok = wid * tpw + tt
            loads = (pltpu.make_async_copy(idx_hbm.at[pl.ds(tok * nsel, nsel)], idx_v, lsem.at[0]),
                     pltpu.make_async_copy(gate_hbm.at[pl.ds(tok * nsel, nsel)], gate_v, lsem.at[1]),
                     pltpu.make_async_copy(x_hbm.at[tok], x_v, lsem.at[2]))
            store = pltpu.make_async_copy(out_v, o_hbm.at[tok], lsem.at[3])
            for c in loads:
                c.start()

            def gather(w):
                return pltpu.make_async_copy(tab_hbm.at[idx_v.at[pl.ds(w * SC_WINDOW, SC_WINDOW)]],
                                             rows_v.at[w % SC_ROW_BUFS], gsem.at[w % SC_ROW_BUFS])

            loads[0].wait()
            for w in range(SC_ROW_BUFS - 1):
                gather(w).start()

            @pl.when(tt > 0)
            def _():
                store.wait()

            @pl.loop(0, d // SC_L)
            def _(c):
                out_v[pl.ds(c * SC_L, SC_L)] = zero

            loads[1].wait()
            loads[2].wait()
            for w in range(n_win):
                gather(w).wait()
                if w + SC_ROW_BUFS - 1 < n_win:
                    gather(w + SC_ROW_BUFS - 1).start()
                slot = w % SC_ROW_BUFS

                def dot_chunk(j, accs, slot=slot):
                    xa = x_v[pl.ds(j * SC_L, SC_L)]
                    xb = x_v[pl.ds(half + j * SC_L, SC_L)]
                    out = []
                    for r in range(SC_WINDOW):
                        uh, ul = unpack(rows_v[slot, r, pl.ds(j * SC_L, SC_L)])
                        out.append(accs[r] + (uh * xa + ul * xb))
                    return tuple(out)

                accs = lax.fori_loop(0, nch, dot_chunk, (zero,) * SC_WINDOW)
                act = zero
                for r in range(SC_WINDOW):
                    act = jnp.where(lane == r, jnp.sum(accs[r]), act)
                z = c0 * (act + 0.044715 * (act * act * act))
                tanh_z = 1.0 - 2.0 / (jnp.exp(2.0 * z) + 1.0)
                wgt = gate_v[pl.ds(w * SC_WINDOW, SC_WINDOW)] * (0.5 * act * (1.0 + tanh_z))
                splat = [jnp.full((SC_L,), jnp.sum(jnp.where(lane == r, wgt, 0.0)), F32)
                         for r in range(SC_WINDOW)]

                @pl.loop(0, nch)
                def _(j, slot=slot, splat=splat):
                    oh = zero
                    ol = zero
                    for r in range(SC_WINDOW):
                        vh, vl = unpack(rows_v[slot, r, pl.ds(half + j * SC_L, SC_L)])
                        oh = oh + splat[r] * vh
                        ol = ol + splat[r] * vl
                    out_v[pl.ds(j * SC_L, SC_L)] += oh
                    out_v[pl.ds(half + j * SC_L, SC_L)] += ol

            store.start()

        pltpu.make_async_copy(out_v, o_hbm.at[wid * tpw], lsem.at[3]).wait()

    return k(table, eidx.reshape(-1), gate.reshape(-1), xn)


def _finish_kernel(x1_ref, p_ref, gf_ref, y_ref):
    y_ref[...] = _rms(x1_ref[...] + p_ref[...], gf_ref[...])


def _finish(x1, p, gf):
    rows, dm = x1.shape
    tm = _pick(rows, (512, 256, 128))
    row = pl.BlockSpec((tm, dm), lambda i: (i, 0))
    return pl.pallas_call(
        _finish_kernel, grid=(rows // tm,), in_specs=[row, row, _const_spec(gf.shape)], out_specs=row,
        out_shape=jax.ShapeDtypeStruct((rows, dm), F32),
        compiler_params=pltpu.CompilerParams(dimension_semantics=("arbitrary",), vmem_limit_bytes=VMEM_LIMIT),
        name="finish",
    )(x1, p, gf)


def _cmul(ar, ai, br, bi):
    return ar * br - ai * bi, ar * bi + ai * br


def _s5_tables(lam_re, lam_im, log_dt, b_re, b_im, c_re, c_im, d, glu_w, glu_b):
    g, p = lam_re.shape
    hg = b_re.shape[-1]
    gs = g // SSM_SLABS
    dt = jnp.exp(log_dt.astype(F32))[:, None]
    lr = lam_re.astype(F32)
    li = lam_im.astype(F32)
    mag = jnp.exp(lr * dt)
    ab_re = mag * jnp.cos(li * dt)
    ab_im = mag * jnp.sin(li * dt)
    den = lr * lr + li * li
    nr = ab_re - 1.0
    ni = ab_im
    coef_re = (nr * lr + ni * li) / den
    coef_im = (ni * lr - nr * li) / den
    br = b_re.astype(F32)
    bi = b_im.astype(F32)
    bb_re = coef_re[..., None] * br - coef_im[..., None] * bi
    bb_im = coef_re[..., None] * bi + coef_im[..., None] * br
    eye = jnp.eye(gs, dtype=F32)

    def b_slab(bb):
        t = bb.reshape(SSM_SLABS, gs, p, hg).transpose(0, 1, 3, 2)
        return jnp.einsum('sihp,ij->sihjp', t, eye).reshape(SSM_SLABS, gs * hg, gs * p)

    bs = jnp.concatenate([b_slab(bb_re), b_slab(bb_im)], axis=2).astype(BF16)

    def c_slab(c):
        t = c.astype(F32).reshape(SSM_SLABS, gs, hg, p).transpose(0, 1, 3, 2)
        return jnp.einsum('siph,ij->sipjh', t, eye).reshape(SSM_SLABS, gs * p, gs * hg)

    cs = jnp.concatenate([c_slab(c_re), -c_slab(c_im)], axis=1).astype(BF16)

    a1 = (ab_re.reshape(1, g * p), ab_im.reshape(1, g * p))
    a2 = _cmul(*a1, *a1)
    a4 = _cmul(*a2, *a2)
    rowid = jnp.arange(SUBLANES)[:, None]
    mtab = jnp.stack([jnp.where(rowid >= s, comp, 0.0)
                      for s, a in ((1, a1), (2, a2), (4, a4)) for comp in a])
    pows = [a1]
    for _ in range(SUBLANES - 1):
        pows.append(_cmul(*pows[-1], *a1))
    ptab = jnp.stack([jnp.concatenate([q[0] for q in pows], axis=0),
                      jnp.concatenate([q[1] for q in pows], axis=0)])
    return (bs, cs, d.astype(F32).reshape(1, -1), glu_w.astype(BF16),
            glu_b.astype(F32).reshape(1, -1), mtab, ptab)


def _sgu_tables(w_s, b_s, seq_len):
    n_heads = w_s.shape[0]
    ln = min(SGU_CHUNK, seq_len)
    assert SGU_CHUNK % ln == 0 and seq_len % ln == 0
    rep = SGU_CHUNK // ln
    mask = jnp.tril(jnp.ones((ln, ln), dtype=bool))
    w = jnp.where(mask[None], w_s[:, :ln, :ln], 0.0).astype(F32)
    ws = jnp.einsum('hij,ab->haibj', w, jnp.eye(rep, dtype=F32)).reshape(n_heads, SGU_CHUNK, SGU_CHUNK)
    bias = jnp.tile(jnp.transpose(b_s[:, :ln]).astype(F32), (rep, 1))
    return ws.astype(BF16), bias


def _pack_kernel(u_ref, v_ref, o_ref):
    half = u_ref.shape[1] // 2

    def bf16_bits(x):
        return lax.bitcast_convert_type(x.astype(BF16).astype(F32), U32)

    def pack(t_ref):
        w = bf16_bits(t_ref[:, :half]) | (bf16_bits(t_ref[:, half:]) >> 16)
        return lax.bitcast_convert_type(w, I32)

    o_ref[:, :half] = pack(u_ref)
    o_ref[:, half:] = pack(v_ref)


def _pack_tables(u, v):
    n, dd = u.shape
    tr = _pick(n, (PACK_ROWS, 256, 128, 8))
    spec = pl.BlockSpec((tr, dd), lambda i: (i, 0))
    return pl.pallas_call(
        _pack_kernel,
        grid=(n // tr,),
        in_specs=[spec, spec],
        out_specs=spec,
        out_shape=jax.ShapeDtypeStruct((n, dd), I32),
        compiler_params=pltpu.CompilerParams(
            dimension_semantics=("arbitrary",), vmem_limit_bytes=VMEM_LIMIT),
        name="pack",
    )(u, v)


def _trunk_to_route(x, h0r, h0i, seq_len, lp, window=None):
    (norm_mix_g, w_in_bf, s5_tabs, ln_g, ln_b, sgu_w, sgu_b, wa, wb, wo, norm_ffn_g, wq, keys_bf, _,
     d_ssm, d_sgu) = lp
    window = (0, x.shape[0]) if window is None else window
    u, su, vn, gates = _inproj(x, window, norm_mix_g, w_in_bf, ln_g, ln_b, d_ssm, d_sgu)
    ya, hlr, hli = _s5(u, h0r, h0i, s5_tabs, seq_len)
    ws, bias = _sgu_tables(sgu_w, sgu_b, seq_len)
    hd = d_sgu // ws.shape[0]
    bias_tile = jnp.repeat(bias, hd, axis=1)
    x1 = _mix(x, window, ya, su, vn, gates, ws, bias_tile, wa, wb, wo)
    xn, eidx, gate = _route(x1, norm_ffn_g, wq, keys_bf)
    return (x1, xn, eidx, gate), hlr, hli, vn


def _trunk_layer(x, h0r, h0i, seq_len, lp, final_g):
    (x1, xn, eidx, gate), hlr, hli, vn = _trunk_to_route(x, h0r, h0i, seq_len, lp)
    y = _peer(eidx, gate, xn, x1, final_g, lp[13], final_norm=True)
    return y, hlr, hli, vn


def _prompt_split(x, seq_len, lp, final_g, n_state):
    rows = x.shape[0]
    cuts = [(int(rows * c) // SC_SPLIT_ALIGN) * SC_SPLIT_ALIGN for c in SC_CUTS]
    bounds = [0] + cuts + [rows]
    hr = hi = jnp.zeros((1, n_state), F32)
    parts = []
    for a, b in zip(bounds[:-1], bounds[1:]):
        part, hr, hi, _ = _trunk_to_route(x, hr, hi, seq_len, lp, window=(a, b - a))
        parts.append(part)
    x1a, xna, ea, ga = parts[-1]
    ya = _peer(ea, ga, xna, x1a, final_g, lp[13], final_norm=True)
    return parts[:-1], ya, hr, hi


def _prompt_finish(firsts, ya, lp, final_g):
    ys = [_finish(x1, _sc_peer(lp[13], e, g, xn), final_g) for (x1, xn, e, g) in firsts]
    ys = ys if ya is None else ys + [ya]
    return ys[0] if len(ys) == 1 else jnp.concatenate(ys, axis=0)


def kernel(x_prompt, x_sample, state_ssm_re, state_ssm_im, norm_mix_g, w_in, ssm_lambda_re, ssm_lambda_im, ssm_log_dt, ssm_b_re, ssm_b_im, ssm_c_re, ssm_c_im, ssm_d, ssm_glu_w, ssm_glu_b, sgu_ln_g, sgu_ln_b, sgu_w, sgu_b, w_branch_a, w_branch_b, w_out, norm_ffn_g, peer_w_q, peer_keys, peer_u, peer_v, norm_final_g):
    depth = w_in.shape[0]
    assert depth == 1, "the final norm is fused into the last layer's PEER kernel"
    bp, sp, dm = x_prompt.shape
    bs_, ss, _ = x_sample.shape
    g, p = ssm_lambda_re.shape[1:]
    d_ssm = ssm_d.shape[1]
    d_sgu = sgu_ln_g.shape[1]
    n_state = g * p
    l = 0
    lp = (norm_mix_g[l].reshape(1, dm), w_in[l].astype(BF16),
          _s5_tables(ssm_lambda_re[l], ssm_lambda_im[l], ssm_log_dt[l], ssm_b_re[l], ssm_b_im[l],
                     ssm_c_re[l], ssm_c_im[l], ssm_d[l], ssm_glu_w[l], ssm_glu_b[l]),
          sgu_ln_g[l].reshape(1, d_sgu), sgu_ln_b[l].reshape(1, d_sgu), sgu_w[l], sgu_b[l],
          w_branch_a[l].astype(BF16), w_branch_b[l].astype(BF16), w_out[l].astype(BF16),
          norm_ffn_g[l].reshape(1, dm), peer_w_q[l].astype(BF16), peer_keys[l].astype(BF16),
          _pack_tables(peer_u[l].astype(F32), peer_v[l].astype(F32)),
          d_ssm, d_sgu)
    gf = norm_final_g.reshape(1, dm)

    xp2 = x_prompt.reshape(bp * sp, dm)
    xs2 = x_sample.reshape(bs_ * ss, dm)
    h0s = (state_ssm_re[l].astype(F32).reshape(bs_, n_state), state_ssm_im[l].astype(F32).reshape(bs_, n_state))
    split = bp == 1 and sp >= 4 * SC_SPLIT_ALIGN and (bs_ * ss) % SC_SUBCORES == 0
    if split:
        sample, hr_s, hi_s, v_s = _trunk_to_route(xs2, h0s[0], h0s[1], ss, lp)
        first, ya, hr_p, hi_p = _prompt_split(xp2, sp, lp, gf, n_state)
        ys = _prompt_finish([sample], None, lp, gf)
        yp = _prompt_finish(first, ya, lp, gf)
    else:
        zeros = jnp.zeros((bp, n_state), F32)
        yp, hr_p, hi_p, _ = _trunk_layer(xp2, zeros, zeros, sp, lp, gf)
        ys, hr_s, hi_s, v_s = _trunk_layer(xs2, h0s[0], h0s[1], ss, lp, gf)
    return (yp.reshape(bp, sp, dm), ys.reshape(bs_, ss, dm),
            hr_p.reshape(1, bp, g, p), hi_p.reshape(1, bp, g, p),
            hr_s.reshape(1, bs_, g, p), hi_s.reshape(1, bs_, g, p),
            v_s.reshape(1, bs_, ss, d_sgu))
```

```python
import functools
import math

import jax
import jax.numpy as jnp
from jax import lax
from jax.experimental import pallas as pl
from jax.experimental.pallas import tpu as pltpu
from jax.experimental.pallas import tpu_sc as plsc

F32 = jnp.float32
BF16 = jnp.bfloat16
I32 = jnp.int32
U32 = jnp.uint32
EPS = 1e-6

LANES = 128
SUBLANES = 8
VMEM_LIMIT = 56 * 1024 * 1024

SSM_SLABS = 4
SCAN_LANES = 512
PEER_TOPK = 16
SGU_CHUNK = 128
ROUTE_LANES = 256
INPROJ_SUB = 128
PACK_ROWS = 512


def _pick(n, candidates):
    for c in candidates:
        if n % c == 0:
            return c
    raise ValueError(f"no block size for {n} in {candidates}")


def _const_spec(shape):
    nd = len(shape)
    return pl.BlockSpec(shape, lambda *_: (0,) * nd, pipeline_mode=pl.Buffered(1))


def _rms(x, g):
    return x * lax.rsqrt(jnp.mean(x * x, axis=-1, keepdims=True) + EPS) * g


def _inproj_kernel(x_ref, g_ref, w_ref, lng_ref, lnb_ref, u_ref, su_ref, vn_ref, gates_ref, xn_sc):
    j = pl.program_id(1)
    tm = x_ref.shape[0]
    sub = min(tm, INPROJ_SUB)

    @pl.when(j == 0)
    def _():
        xn_sc[...] = _rms(x_ref[...], g_ref[...]).astype(BF16)

    def section(out_ref, epilogue):
        for r in range(tm // sub):
            rows = slice(r * sub, (r + 1) * sub)
            z = jnp.dot(xn_sc[rows, :], w_ref[...], preferred_element_type=F32)
            out_ref[rows, :] = epilogue(z).astype(out_ref.dtype)

    def layer_norm_gelu(z):
        v = jax.nn.gelu(z)
        mu = jnp.mean(v, axis=-1, keepdims=True)
        vc = v - mu
        y = vc * lax.rsqrt(jnp.mean(vc * vc, axis=-1, keepdims=True) + EPS)
        return y * lng_ref[...] + lnb_ref[...]

    @pl.when(j == 0)
    def _():
        section(u_ref, lambda z: z)

    @pl.when(j == 1)
    def _():
        section(su_ref, jax.nn.gelu)

    @pl.when(j == 2)
    def _():
        section(vn_ref, layer_norm_gelu)

    @pl.when(j >= 3)
    def _():
        section(gates_ref, jax.nn.sigmoid)


def _inproj(x, window, g, w_in_bf, ln_g, ln_b, d_ssm, d_sgu):
    row0, rows = window
    dm = x.shape[1]
    d_in = w_in_bf.shape[1]
    tn = d_ssm
    assert d_sgu == tn and (d_in - 3 * tn) % tn == 0
    nj = d_in // tn
    tm = _pick(math.gcd(rows, row0) if row0 else rows, (1024, 512, 256, 128))
    blk0 = row0 // tm
    return pl.pallas_call(
        _inproj_kernel,
        grid=(rows // tm, nj),
        in_specs=[
            pl.BlockSpec((tm, dm), lambda i, j: (i + blk0, 0)),
            pl.BlockSpec((1, dm), lambda i, j: (0, 0)),
            pl.BlockSpec((dm, tn), lambda i, j: (0, j)),
            pl.BlockSpec((1, tn), lambda i, j: (0, 0)),
            pl.BlockSpec((1, tn), lambda i, j: (0, 0)),
        ],
        out_specs=[
            pl.BlockSpec((tm, tn), lambda i, j: (i, 0)),
            pl.BlockSpec((tm, tn), lambda i, j: (i, 0)),
            pl.BlockSpec((tm, tn), lambda i, j: (i, 0)),
            pl.BlockSpec((tm, tn), lambda i, j: (i, jnp.maximum(j - 3, 0))),
        ],
        out_shape=[
            jax.ShapeDtypeStruct((rows, tn), F32),
            jax.ShapeDtypeStruct((rows, tn), BF16),
            jax.ShapeDtypeStruct((rows, tn), F32),
            jax.ShapeDtypeStruct((rows, d_in - 3 * tn), BF16),
        ],
        scratch_shapes=[pltpu.VMEM((tm, dm), BF16)],
        compiler_params=pltpu.CompilerParams(
            dimension_semantics=("arbitrary", "arbitrary"), vmem_limit_bytes=VMEM_LIMIT),
        name="inproj",
    )(x, g, w_in_bf, ln_g, ln_b)


def _s5_kernel(u_ref, h0r_ref, h0i_ref, bs_ref, cs_ref, d_ref, gluw_ref, glub_ref, m_ref, pw_ref,
               ya_ref, hlr_ref, hli_ref, hr_sc, hi_sc, cr_sc, ci_sc, *, seq_len):
    i = pl.program_id(0)
    tl, d_ssm = u_ref.shape
    n_state = hr_sc.shape[1]
    kin = d_ssm // SSM_SLABS
    kst = n_state // SSM_SLABS

    @pl.when(i == 0)
    def _():
        cr_sc[...] = jnp.zeros_like(cr_sc)
        ci_sc[...] = jnp.zeros_like(ci_sc)

    ub = u_ref[...].astype(BF16)
    for s in range(SSM_SLABS):
        r = jnp.dot(ub[:, kin * s:kin * (s + 1)], bs_ref[s], preferred_element_type=F32)
        hr_sc[:, kst * s:kst * (s + 1)] = r[:, :kst]
        hi_sc[:, kst * s:kst * (s + 1)] = r[:, kst:]

    seg_rows = min(tl, seq_len)
    seg_tiles = seg_rows // SUBLANES
    for lc in range(n_state // SCAN_LANES):
        ls = slice(lc * SCAN_LANES, (lc + 1) * SCAN_LANES)

        def tile_body(t, carry, ls=ls):
            cr, ci = carry
            r0 = pl.multiple_of(t * SUBLANES, SUBLANES)
            br = hr_sc[pl.ds(r0, SUBLANES), ls]
            bi = hi_sc[pl.ds(r0, SUBLANES), ls]
            for k, shift in enumerate((1, 2, 4)):
                mr = m_ref[2 * k, :, ls]
                mi = m_ref[2 * k + 1, :, ls]
                sr = pltpu.roll(br, shift, 0)
                si = pltpu.roll(bi, shift, 0)
                br, bi = br + (mr * sr - mi * si), bi + (mr * si + mi * sr)
            pr = pw_ref[0, :, ls]
            pi_ = pw_ref[1, :, ls]
            hr = br + (pr * cr - pi_ * ci)
            hi = bi + (pr * ci + pi_ * cr)
            hr_sc[pl.ds(r0, SUBLANES), ls] = hr
            hi_sc[pl.ds(r0, SUBLANES), ls] = hi
            return (jnp.broadcast_to(hr[SUBLANES - 1:SUBLANES, :], (SUBLANES, SCAN_LANES)),
                    jnp.broadcast_to(hi[SUBLANES - 1:SUBLANES, :], (SUBLANES, SCAN_LANES)))

        def segment(q, _, ls=ls):
            row0 = i * tl + q * seg_rows
            b = row0 // seq_len
            is_start = (row0 % seq_len) == 0
            h0r = jnp.broadcast_to(h0r_ref[pl.ds(b, 1), ls], (SUBLANES, SCAN_LANES))
            h0i = jnp.broadcast_to(h0i_ref[pl.ds(b, 1), ls], (SUBLANES, SCAN_LANES))
            carry = (jnp.where(is_start, h0r, cr_sc[:, ls]), jnp.where(is_start, h0i, ci_sc[:, ls]))
            cr, ci = lax.fori_loop(q * seg_tiles, (q + 1) * seg_tiles, tile_body, carry)
            cr_sc[:, ls] = cr
            ci_sc[:, ls] = ci
            hlr_ref[pl.ds(b, 1), ls] = cr[0:1, :]
            hli_ref[pl.ds(b, 1), ls] = ci[0:1, :]
            return 0

        if tl == seg_rows:
            segment(0, 0)
        else:
            lax.fori_loop(0, tl // seg_rows, segment, 0)

    ys = []
    for s in range(SSM_SLABS):
        hre = hr_sc[:, kst * s:kst * (s + 1)].astype(BF16)
        him = hi_sc[:, kst * s:kst * (s + 1)].astype(BF16)
        ys.append(jnp.dot(hre, cs_ref[s, :kst, :], preferred_element_type=F32)
                  + jnp.dot(him, cs_ref[s, kst:, :], preferred_element_type=F32))
    y = jnp.concatenate(ys, axis=1)
    y = jax.nn.gelu(y + d_ref[...] * u_ref[...])
    gt = jnp.dot(y.astype(BF16), gluw_ref[...], preferred_element_type=F32) + glub_ref[...]
    ya_ref[...] = (y * jax.nn.sigmoid(gt)).astype(BF16)


def _s5(u, h0r, h0i, tabs, seq_len):
    rows, d_ssm = u.shape
    nb, n_state = h0r.shape
    tl = _pick(rows, (256, 128, 64, 32, 16, 8)) if seq_len >= 256 else _pick(rows, (512, 256, 128, 64, 32, 16))
    assert (seq_len % tl == 0 or tl % seq_len == 0) and seq_len % SUBLANES == 0
    bs, cs, d, gluw, glub, mtab, ptab = tabs
    return pl.pallas_call(
        functools.partial(_s5_kernel, seq_len=seq_len),
        grid=(rows // tl,),
        in_specs=[
            pl.BlockSpec((tl, d_ssm), lambda i: (i, 0)),
            _const_spec((nb, n_state)),
            _const_spec((nb, n_state)),
            _const_spec(bs.shape),
            _const_spec(cs.shape),
            _const_spec(d.shape),
            _const_spec(gluw.shape),
            _const_spec(glub.shape),
            _const_spec(mtab.shape),
            _const_spec(ptab.shape),
        ],
        out_specs=[
            pl.BlockSpec((tl, d_ssm), lambda i: (i, 0)),
            pl.BlockSpec((nb, n_state), lambda i: (0, 0)),
            pl.BlockSpec((nb, n_state), lambda i: (0, 0)),
        ],
        out_shape=[
            jax.ShapeDtypeStruct((rows, d_ssm), BF16),
            jax.ShapeDtypeStruct((nb, n_state), F32),
            jax.ShapeDtypeStruct((nb, n_state), F32),
        ],
        scratch_shapes=[
            pltpu.VMEM((tl, n_state), F32),
            pltpu.VMEM((tl, n_state), F32),
            pltpu.VMEM((SUBLANES, n_state), F32),
            pltpu.VMEM((SUBLANES, n_state), F32),
        ],
        compiler_params=pltpu.CompilerParams(
            dimension_semantics=("arbitrary",), vmem_limit_bytes=VMEM_LIMIT),
        name="s5",
    )(u, h0r, h0i, bs, cs, d, gluw, glub, mtab, ptab)


def _mix_kernel(x_ref, ya_ref, su_ref, vn_ref, gates_ref, ws_ref, bias_ref, wa_ref, wb_ref, wo_ref, x1_ref):
    tm, dm = x_ref.shape
    n_heads = ws_ref.shape[0]
    hd = vn_ref.shape[1] // n_heads
    vb = vn_ref[...].astype(BF16)
    chunks = []
    for c in range(tm // SGU_CHUNK):
        heads = [jnp.dot(ws_ref[h], vb[c * SGU_CHUNK:(c + 1) * SGU_CHUNK, h * hd:(h + 1) * hd],
                         preferred_element_type=F32) for h in range(n_heads)]
        chunks.append(jnp.concatenate(heads, axis=1) + bias_ref[...])
    s_all = jnp.concatenate(chunks, axis=0)
    yb = (su_ref[...].astype(F32) * s_all).astype(BF16)
    pa = jnp.dot(ya_ref[...], wa_ref[...], preferred_element_type=F32)
    pb = jnp.dot(yb, wb_ref[...], preferred_element_type=F32)
    merged = gates_ref[:, :dm].astype(F32) * pa + gates_ref[:, dm:].astype(F32) * pb
    x1_ref[...] = x_ref[...] + jnp.dot(merged.astype(BF16), wo_ref[...], preferred_element_type=F32)


def _mix(x, window, ya, su, vn, gates, ws, bias, wa, wb, wo):
    row0, rows = window
    dm = x.shape[1]
    d_ssm = ya.shape[1]
    d_sgu = su.shape[1]
    tm = _pick(math.gcd(rows, row0) if row0 else rows, (256, 128))
    blk0 = row0 // tm
    row = lambda w: pl.BlockSpec((tm, w), lambda i: (i, 0))
    return pl.pallas_call(
        _mix_kernel,
        grid=(rows // tm,),
        in_specs=[pl.BlockSpec((tm, dm), lambda i: (i + blk0, 0)), row(d_ssm), row(d_sgu), row(d_sgu), row(2 * dm),
                  _const_spec(ws.shape), _const_spec(bias.shape),
                  _const_spec(wa.shape), _const_spec(wb.shape), _const_spec(wo.shape)],
        out_specs=row(dm),
        out_shape=jax.ShapeDtypeStruct((rows, dm), F32),
        compiler_params=pltpu.CompilerParams(
            dimension_semantics=("arbitrary",), vmem_limit_bytes=VMEM_LIMIT),
        name="mix",
    )(x, ya, su, vn, gates, ws, bias, wa, wb, wo)


def _topk_rows(s, k, payload=None):
    rows = s.shape[0]
    iota = lax.broadcasted_iota(I32, s.shape, 0).astype(F32)
    vals, outs = [], []
    for _ in range(k):
        m = jnp.max(s, axis=0, keepdims=True)
        j = jnp.min(jnp.where(s == m, iota, float(rows)), axis=0, keepdims=True)
        sel = iota == j
        vals.append(m)
        outs.append(j if payload is None else jnp.max(jnp.where(sel, payload, -1.0), axis=0, keepdims=True))
        s = jnp.where(sel, -jnp.inf, s)
    return jnp.concatenate(vals, axis=0), jnp.concatenate(outs, axis=0)


def _route_kernel(x1_ref, g_ref, wq_ref, keys_ref, xn_ref, eidx_ref, gate_ref, q_sc, e_sc, p_sc):
    tb = x1_ref.shape[0]
    tw = e_sc.shape[2]
    nsub = tb // tw
    n_heads, _, n_keys, half = keys_ref.shape
    dkey = 2 * half
    xn = _rms(x1_ref[...], g_ref[...])
    xn_ref[...] = xn
    q = jnp.dot(xn.astype(BF16), wq_ref[...], preferred_element_type=F32).astype(BF16)
    for h in range(n_heads):
        q_sc[h] = q[:, h * dkey:(h + 1) * dkey]

    nt = (((1,), (1,)), ((), ()))

    def body(n, _):
        h = n // nsub
        r0 = pl.multiple_of((n % nsub) * tw, tw)
        qh = q_sc[h, pl.ds(r0, tw), :]
        s1 = lax.dot_general(keys_ref[h, 0], qh[:, :half], nt, preferred_element_type=F32)
        s2 = lax.dot_general(keys_ref[h, 1], qh[:, half:], nt, preferred_element_type=F32)
        v1, i1 = _topk_rows(s1, PEER_TOPK)
        v2, i2 = _topk_rows(s2, PEER_TOPK)
        cv, ci = [], []
        for a in range(PEER_TOPK):
            nb = PEER_TOPK // (a + 1)
            cv.append(v1[a:a + 1, :] + v2[:nb, :])
            ci.append(i1[a:a + 1, :] * float(n_keys) + i2[:nb, :])
        ts, te = _topk_rows(jnp.concatenate(cv, axis=0), PEER_TOPK, payload=jnp.concatenate(ci, axis=0))
        ex = jnp.exp(ts - ts[0:1, :])
        e_sc[n] = te
        p_sc[n] = ex / jnp.sum(ex, axis=0, keepdims=True)
        return 0

    lax.fori_loop(0, n_heads * nsub, body, 0)
    for c in range(nsub):
        rows = slice(c * tw, (c + 1) * tw)
        e_all = jnp.concatenate([e_sc[h * nsub + c] for h in range(n_heads)], axis=0)
        p_all = jnp.concatenate([p_sc[h * nsub + c] for h in range(n_heads)], axis=0)
        eidx_ref[rows, :] = e_all.T.astype(I32)
        gate_ref[rows, :] = p_all.T


def _route(x1, g, wq, keys_bf):
    rows, dm = x1.shape
    n_heads, _, n_keys, half = keys_bf.shape
    nsel = n_heads * PEER_TOPK
    tb = _pick(rows, (256, 128))
    tw = min(tb, ROUTE_LANES)
    row = lambda w: pl.BlockSpec((tb, w), lambda i: (i, 0))
    return pl.pallas_call(
        _route_kernel,
        grid=(rows // tb,),
        in_specs=[row(dm), _const_spec(g.shape), _const_spec(wq.shape), _const_spec(keys_bf.shape)],
        out_specs=[row(dm), row(nsel), row(nsel)],
        out_shape=[
            jax.ShapeDtypeStruct((rows, dm), F32),
            jax.ShapeDtypeStruct((rows, nsel), I32),
            jax.ShapeDtypeStruct((rows, nsel), F32),
        ],
        scratch_shapes=[
            pltpu.VMEM((n_heads, tb, 2 * half), BF16),
            pltpu.VMEM((n_heads * (tb // tw), PEER_TOPK, tw), F32),
            pltpu.VMEM((n_heads * (tb // tw), PEER_TOPK, tw), F32),
        ],
        compiler_params=pltpu.CompilerParams(
            dimension_semantics=("arbitrary",), vmem_limit_bytes=VMEM_LIMIT),
        name="route",
    )(x1, g, wq, keys_bf)


PEER_BUFS = 16


def _peer_kernel(eidx_hbm, gate_ref, xn_ref, x1_ref, gf_ref, tab_hbm, y_ref,
                 idx_sm, idx_sem, *scratch, final_norm):
    i = pl.program_id(0)
    bufs = scratch[:PEER_BUFS]
    sem, peer_sc = scratch[PEER_BUFS:]
    tbk, nsel = gate_ref.shape
    half = tab_hbm.shape[1] // 2
    nlt = half // LANES

    icp = pltpu.make_async_copy(eidx_hbm.at[i], idx_sm, idx_sem)
    icp.start()
    icp.wait()

    def issue(t, s):
        base = t * nsel
        for k in range(nsel):
            e = idx_sm[base + k]
            pltpu.make_async_copy(tab_hbm.at[pl.ds(e, 1), :], bufs[s].at[pl.ds(k, 1), :],
                                  sem.at[s]).start(priority=k % 2)

    def wait(s):
        pltpu.make_async_copy(tab_hbm.at[pl.ds(0, nsel), :], bufs[s], sem.at[s]).wait()

    hi_mask = jnp.int32(-65536)
    eye = (lax.broadcasted_iota(I32, (nsel, nsel), 0) == lax.broadcasted_iota(I32, (nsel, nsel), 1))

    def unpack(w):
        return (lax.bitcast_convert_type(w & hi_mask, F32),
                lax.bitcast_convert_type(w << 16, F32))

    def compute(t, s):
        buf = bufs[s]
        xrow = xn_ref[pl.ds(t, 1), :]
        acc = jnp.zeros((nsel, LANES), F32)
        for j in range(nlt):
            uh, ul = unpack(buf[:, j * LANES:(j + 1) * LANES])
            acc = acc + uh * xrow[:, j * LANES:(j + 1) * LANES] \
                      + ul * xrow[:, half + j * LANES:half + (j + 1) * LANES]
        act = jnp.sum(acc, axis=1, keepdims=True)
        grow = jnp.broadcast_to(gate_ref[pl.ds(t, 1), :], (nsel, nsel))
        gcol = jnp.sum(jnp.where(eye, grow, 0.0), axis=1, keepdims=True)
        wgt = jnp.broadcast_to(gcol * jax.nn.gelu(act), (nsel, LANES))
        t8 = pl.multiple_of((t // SUBLANES) * SUBLANES, SUBLANES)
        rsel = lax.broadcasted_iota(I32, (SUBLANES, LANES), 0) == (t % SUBLANES)

        def put(lane0, row):
            cur = peer_sc[pl.ds(t8, SUBLANES), lane0:lane0 + LANES]
            peer_sc[pl.ds(t8, SUBLANES), lane0:lane0 + LANES] = jnp.where(
                rsel, jnp.broadcast_to(row, (SUBLANES, LANES)), cur)

        for j in range(nlt):
            vh, vl = unpack(buf[:, half + j * LANES:half + (j + 1) * LANES])
            put(j * LANES, jnp.sum(vh * wgt, axis=0, keepdims=True))
            put(half + j * LANES, jnp.sum(vl * wgt, axis=0, keepdims=True))

    def step(t, s, prefetch):
        wait(s)
        if prefetch:
            issue(t + PEER_BUFS - 1, (s + PEER_BUFS - 1) % PEER_BUFS)
        compute(t, s)

    peer_sc[...] = jnp.zeros_like(peer_sc)
    for t in range(PEER_BUFS - 1):
        issue(t, t)

    def group(gi, _):
        for s in range(PEER_BUFS):
            step(gi * PEER_BUFS + s, s, True)
        return 0

    lax.fori_loop(0, tbk // PEER_BUFS - 1, group, 0)
    for s in range(PEER_BUFS):
        t = tbk - PEER_BUFS + s
        step(t, s, t + PEER_BUFS - 1 < tbk)

    xo = x1_ref[...] + peer_sc[...]
    y_ref[...] = _rms(xo, gf_ref[...]) if final_norm else xo


def _peer(eidx, gate, xn, x1, gf, table, final_norm):
    rows, dm = x1.shape
    nsel = gate.shape[1]
    tbk = _pick(rows, (256, 128, 64, 32, 16, 8))
    assert tbk % PEER_BUFS == 0 and tbk >= 2 * PEER_BUFS
    nblk = rows // tbk
    eidx_blk = eidx.reshape(nblk, tbk * nsel)
    row = lambda w: pl.BlockSpec((tbk, w), lambda i: (i, 0))
    return pl.pallas_call(
        functools.partial(_peer_kernel, final_norm=final_norm),
        grid=(nblk,),
        in_specs=[pl.BlockSpec(memory_space=pl.ANY), row(nsel), row(dm), row(dm),
                  _const_spec(gf.shape), pl.BlockSpec(memory_space=pl.ANY)],
        out_specs=row(dm),
        out_shape=jax.ShapeDtypeStruct((rows, dm), F32),
        scratch_shapes=[
            pltpu.SMEM((tbk * nsel,), I32),
            pltpu.SemaphoreType.DMA(()),
        ] + [pltpu.VMEM((nsel, table.shape[1]), table.dtype) for _ in range(PEER_BUFS)] + [
            pltpu.SemaphoreType.DMA((PEER_BUFS,)),
            pltpu.VMEM((tbk, dm), F32),
        ],
        compiler_params=pltpu.CompilerParams(
            dimension_semantics=("arbitrary",), vmem_limit_bytes=VMEM_LIMIT),
        name="peer",
    )(eidx_blk, gate, xn, x1, gf, table)


SC_L = 16
SC_WINDOW = SC_L
SC_SUBCORES = 32
SC_CUTS = (0.0625, 0.1875, 0.4375)
SC_SPLIT_ALIGN = 512


def _sc_peer(table, eidx, gate, xn):
    ntok, nsel = eidx.shape
    d = table.shape[1]
    half = d // 2
    nch = half // SC_L
    info = plsc.get_sparse_core_info()
    n_workers = info.num_cores * info.num_subcores
    n_win = nsel // SC_WINDOW
    tpw = ntok // n_workers
    assert tpw * n_workers == ntok and SC_WINDOW == SC_L and info.num_lanes == SC_L
    mesh = plsc.VectorSubcoreMesh(core_axis_name="core", subcore_axis_name="subcore")
    c0 = math.sqrt(2.0 / math.pi)

    @functools.partial(
        pl.kernel, out_type=jax.ShapeDtypeStruct((ntok, d), F32), mesh=mesh,
        scratch_types=[pltpu.VMEM((nsel,), I32), pltpu.VMEM((nsel,), F32), pltpu.VMEM((d,), F32),
                       pltpu.VMEM((2, SC_WINDOW, d), I32), pltpu.VMEM((d,), F32),
                       pltpu.SemaphoreType.DMA((2,))],
        compiler_params=pltpu.CompilerParams(needs_layout_passes=False),
        name="sc_peer")
    def k(tab_hbm, idx_hbm, gate_hbm, x_hbm, o_hbm, idx_v, gate_v, x_v, rows_v, out_v, gsem):
        wid = lax.axis_index("subcore") * info.num_cores + lax.axis_index("core")
        lane = lax.iota(I32, SC_L)
        zero = jnp.zeros((SC_L,), F32)

        def unpack(w):
            return (lax.bitcast_convert_type(w & jnp.int32(-65536), F32),
                    lax.bitcast_convert_type(w << 16, F32))

        @pl.loop(0, tpw)
        def _(tt):
            tok = wid * tpw + tt
            pltpu.sync_copy(idx_hbm.at[pl.ds(tok * nsel, nsel)], idx_v)
            pltpu.sync_copy(gate_hbm.at[pl.ds(tok * nsel, nsel)], gate_v)
            pltpu.sync_copy(x_hbm.at[tok], x_v)

            @pl.loop(0, d // SC_L)
            def _(c):
                out_v[pl.ds(c * SC_L, SC_L)] = zero

            def gather(w):
                return pltpu.make_async_copy(tab_hbm.at[idx_v.at[pl.ds(w * SC_WINDOW, SC_WINDOW)]],
                                             rows_v.at[w % 2], gsem.at[w % 2])

            gather(0).start()
            for w in range(n_win):
                gather(w).wait()
                if w + 1 < n_win:
                    gather(w + 1).start()
                slot = w % 2

                def dot_chunk(j, accs, slot=slot):
                    xa = x_v[pl.ds(j * SC_L, SC_L)]
                    xb = x_v[pl.ds(half + j * SC_L, SC_L)]
                    out = []
                    for r in range(SC_WINDOW):
                        uh, ul = unpack(rows_v[slot, r, pl.ds(j * SC_L, SC_L)])
                        out.append(accs[r] + (uh * xa + ul * xb))
                    return tuple(out)

                accs = lax.fori_loop(0, nch, dot_chunk, (zero,) * SC_WINDOW)
                act = zero
                for r in range(SC_WINDOW):
                    act = jnp.where(lane == r, jnp.sum(accs[r]), act)
                z = c0 * (act + 0.044715 * (act * act * act))
                tanh_z = 1.0 - 2.0 / (jnp.exp(2.0 * z) + 1.0)
                wgt = gate_v[pl.ds(w * SC_WINDOW, SC_WINDOW)] * (0.5 * act * (1.0 + tanh_z))
                splat = [jnp.full((SC_L,), jnp.sum(jnp.where(lane == r, wgt, 0.0)), F32)
                         for r in range(SC_WINDOW)]

                @pl.loop(0, nch)
                def _(j, slot=slot, splat=splat):
                    oh = zero
                    ol = zero
                    for r in range(SC_WINDOW):
                        vh, vl = unpack(rows_v[slot, r, pl.ds(half + j * SC_L, SC_L)])
                        oh = oh + splat[r] * vh
                        ol = ol + splat[r] * vl
                    out_v[pl.ds(j * SC_L, SC_L)] += oh
                    out_v[pl.ds(half + j * SC_L, SC_L)] += ol

            pltpu.sync_copy(out_v, o_hbm.at[tok])

    return k(table, eidx.reshape(-1), gate.reshape(-1), xn)


def _finish_kernel(x1_ref, p_ref, gf_ref, y_ref):
    y_ref[...] = _rms(x1_ref[...] + p_ref[...], gf_ref[...])


def _finish(x1, p, gf):
    rows, dm = x1.shape
    tm = _pick(rows, (512, 256, 128))
    row = pl.BlockSpec((tm, dm), lambda i: (i, 0))
    return pl.pallas_call(
        _finish_kernel, grid=(rows // tm,), in_specs=[row, row, _const_spec(gf.shape)], out_specs=row,
        out_shape=jax.ShapeDtypeStruct((rows, dm), F32),
        compiler_params=pltpu.CompilerParams(dimension_semantics=("arbitrary",), vmem_limit_bytes=VMEM_LIMIT),
        name="finish",
    )(x1, p, gf)


def _cmul(ar, ai, br, bi):
    return ar * br - ai * bi, ar * bi + ai * br


def _s5_tables(lam_re, lam_im, log_dt, b_re, b_im, c_re, c_im, d, glu_w, glu_b):
    g, p = lam_re.shape
    hg = b_re.shape[-1]
    gs = g // SSM_SLABS
    dt = jnp.exp(log_dt.astype(F32))[:, None]
    lr = lam_re.astype(F32)
    li = lam_im.astype(F32)
    mag = jnp.exp(lr * dt)
    ab_re = mag * jnp.cos(li * dt)
    ab_im = mag * jnp.sin(li * dt)
    den = lr * lr + li * li
    nr = ab_re - 1.0
    ni = ab_im
    coef_re = (nr * lr + ni * li) / den
    coef_im = (ni * lr - nr * li) / den
    br = b_re.astype(F32)
    bi = b_im.astype(F32)
    bb_re = coef_re[..., None] * br - coef_im[..., None] * bi
    bb_im = coef_re[..., None] * bi + coef_im[..., None] * br
    eye = jnp.eye(gs, dtype=F32)

    def b_slab(bb):
        t = bb.reshape(SSM_SLABS, gs, p, hg).transpose(0, 1, 3, 2)
        return jnp.einsum('sihp,ij->sihjp', t, eye).reshape(SSM_SLABS, gs * hg, gs * p)

    bs = jnp.concatenate([b_slab(bb_re), b_slab(bb_im)], axis=2).astype(BF16)

    def c_slab(c):
        t = c.astype(F32).reshape(SSM_SLABS, gs, hg, p).transpose(0, 1, 3, 2)
        return jnp.einsum('siph,ij->sipjh', t, eye).reshape(SSM_SLABS, gs * p, gs * hg)

    cs = jnp.concatenate([c_slab(c_re), -c_slab(c_im)], axis=1).astype(BF16)

    a1 = (ab_re.reshape(1, g * p), ab_im.reshape(1, g * p))
    a2 = _cmul(*a1, *a1)
    a4 = _cmul(*a2, *a2)
    rowid = jnp.arange(SUBLANES)[:, None]
    mtab = jnp.stack([jnp.where(rowid >= s, comp, 0.0)
                      for s, a in ((1, a1), (2, a2), (4, a4)) for comp in a])
    pows = [a1]
    for _ in range(SUBLANES - 1):
        pows.append(_cmul(*pows[-1], *a1))
    ptab = jnp.stack([jnp.concatenate([q[0] for q in pows], axis=0),
                      jnp.concatenate([q[1] for q in pows], axis=0)])
    return (bs, cs, d.astype(F32).reshape(1, -1), glu_w.astype(BF16),
            glu_b.astype(F32).reshape(1, -1), mtab, ptab)


def _sgu_tables(w_s, b_s, seq_len):
    n_heads = w_s.shape[0]
    ln = min(SGU_CHUNK, seq_len)
    assert SGU_CHUNK % ln == 0 and seq_len % ln == 0
    rep = SGU_CHUNK // ln
    mask = jnp.tril(jnp.ones((ln, ln), dtype=bool))
    w = jnp.where(mask[None], w_s[:, :ln, :ln], 0.0).astype(F32)
    ws = jnp.einsum('hij,ab->haibj', w, jnp.eye(rep, dtype=F32)).reshape(n_heads, SGU_CHUNK, SGU_CHUNK)
    bias = jnp.tile(jnp.transpose(b_s[:, :ln]).astype(F32), (rep, 1))
    return ws.astype(BF16), bias


def _pack_kernel(u_ref, v_ref, o_ref):
    half = u_ref.shape[1] // 2

    def bf16_bits(x):
        return lax.bitcast_convert_type(x.astype(BF16).astype(F32), U32)

    def pack(t_ref):
        w = bf16_bits(t_ref[:, :half]) | (bf16_bits(t_ref[:, half:]) >> 16)
        return lax.bitcast_convert_type(w, I32)

    o_ref[:, :half] = pack(u_ref)
    o_ref[:, half:] = pack(v_ref)


def _pack_tables(u, v):
    n, dd = u.shape
    tr = _pick(n, (PACK_ROWS, 256, 128, 8))
    spec = pl.BlockSpec((tr, dd), lambda i: (i, 0))
    return pl.pallas_call(
        _pack_kernel,
        grid=(n // tr,),
        in_specs=[spec, spec],
        out_specs=spec,
        out_shape=jax.ShapeDtypeStruct((n, dd), I32),
        compiler_params=pltpu.CompilerParams(
            dimension_semantics=("arbitrary",), vmem_limit_bytes=VMEM_LIMIT),
        name="pack",
    )(u, v)


def _trunk_to_route(x, h0r, h0i, seq_len, lp, window=None):
    (norm_mix_g, w_in_bf, s5_tabs, ln_g, ln_b, sgu_w, sgu_b, wa, wb, wo, norm_ffn_g, wq, keys_bf, _,
     d_ssm, d_sgu) = lp
    window = (0, x.shape[0]) if window is None else window
    u, su, vn, gates = _inproj(x, window, norm_mix_g, w_in_bf, ln_g, ln_b, d_ssm, d_sgu)
    ya, hlr, hli = _s5(u, h0r, h0i, s5_tabs, seq_len)
    ws, bias = _sgu_tables(sgu_w, sgu_b, seq_len)
    hd = d_sgu // ws.shape[0]
    bias_tile = jnp.repeat(bias, hd, axis=1)
    x1 = _mix(x, window, ya, su, vn, gates, ws, bias_tile, wa, wb, wo)
    xn, eidx, gate = _route(x1, norm_ffn_g, wq, keys_bf)
    return (x1, xn, eidx, gate), hlr, hli, vn


def _trunk_layer(x, h0r, h0i, seq_len, lp, final_g):
    (x1, xn, eidx, gate), hlr, hli, vn = _trunk_to_route(x, h0r, h0i, seq_len, lp)
    y = _peer(eidx, gate, xn, x1, final_g, lp[13], final_norm=True)
    return y, hlr, hli, vn


def _prompt_split(x, seq_len, lp, final_g, n_state):
    rows = x.shape[0]
    cuts = [(int(rows * c) // SC_SPLIT_ALIGN) * SC_SPLIT_ALIGN for c in SC_CUTS]
    bounds = [0] + cuts + [rows]
    hr = hi = jnp.zeros((1, n_state), F32)
    parts = []
    for a, b in zip(bounds[:-1], bounds[1:]):
        part, hr, hi, _ = _trunk_to_route(x, hr, hi, seq_len, lp, window=(a, b - a))
        parts.append(part)
    x1a, xna, ea, ga = parts[-1]
    ya = _peer(ea, ga, xna, x1a, final_g, lp[13], final_norm=True)
    return parts[:-1], ya, hr, hi


def _prompt_finish(firsts, ya, lp, final_g):
    ys = [_finish(x1, _sc_peer(lp[13], e, g, xn), final_g) for (x1, xn, e, g) in firsts]
    ys = ys if ya is None else ys + [ya]
    return ys[0] if len(ys) == 1 else jnp.concatenate(ys, axis=0)


def kernel(x_prompt, x_sample, state_ssm_re, state_ssm_im, norm_mix_g, w_in, ssm_lambda_re, ssm_lambda_im, ssm_log_dt, ssm_b_re, ssm_b_im, ssm_c_re, ssm_c_im, ssm_d, ssm_glu_w, ssm_glu_b, sgu_ln_g, sgu_ln_b, sgu_w, sgu_b, w_branch_a, w_branch_b, w_out, norm_ffn_g, peer_w_q, peer_keys, peer_u, peer_v, norm_final_g):
    depth = w_in.shape[0]
    assert depth == 1, "the final norm is fused into the last layer's PEER kernel"
    bp, sp, dm = x_prompt.shape
    bs_, ss, _ = x_sample.shape
    g, p = ssm_lambda_re.shape[1:]
    d_ssm = ssm_d.shape[1]
    d_sgu = sgu_ln_g.shape[1]
    n_state = g * p
    l = 0
    lp = (norm_mix_g[l].reshape(1, dm), w_in[l].astype(BF16),
          _s5_tables(ssm_lambda_re[l], ssm_lambda_im[l], ssm_log_dt[l], ssm_b_re[l], ssm_b_im[l],
                     ssm_c_re[l], ssm_c_im[l], ssm_d[l], ssm_glu_w[l], ssm_glu_b[l]),
          sgu_ln_g[l].reshape(1, d_sgu), sgu_ln_b[l].reshape(1, d_sgu), sgu_w[l], sgu_b[l],
          w_branch_a[l].astype(BF16), w_branch_b[l].astype(BF16), w_out[l].astype(BF16),
          norm_ffn_g[l].reshape(1, dm), peer_w_q[l].astype(BF16), peer_keys[l].astype(BF16),
          _pack_tables(peer_u[l].astype(F32), peer_v[l].astype(F32)),
          d_ssm, d_sgu)
    gf = norm_final_g.reshape(1, dm)

    xp2 = x_prompt.reshape(bp * sp, dm)
    xs2 = x_sample.reshape(bs_ * ss, dm)
    h0s = (state_ssm_re[l].astype(F32).reshape(bs_, n_state), state_ssm_im[l].astype(F32).reshape(bs_, n_state))
    split = bp == 1 and sp >= 4 * SC_SPLIT_ALIGN and (bs_ * ss) % SC_SUBCORES == 0
    if split:
        sample, hr_s, hi_s, v_s = _trunk_to_route(xs2, h0s[0], h0s[1], ss, lp)
        first, ya, hr_p, hi_p = _prompt_split(xp2, sp, lp, gf, n_state)
        ys = _prompt_finish([sample], None, lp, gf)
        yp = _prompt_finish(first, ya, lp, gf)
    else:
        zeros = jnp.zeros((bp, n_state), F32)
        yp, hr_p, hi_p, _ = _trunk_layer(xp2, zeros, zeros, sp, lp, gf)
        ys, hr_s, hi_s, v_s = _trunk_layer(xs2, h0s[0], h0s[1], ss, lp, gf)
    return (yp.reshape(bp, sp, dm), ys.reshape(bs_, ss, dm),
            hr_p.reshape(1, bp, g, p), hi_p.reshape(1, bp, g, p),
            hr_s.reshape(1, bs_, g, p), hi_s.reshape(1, bs_, g, p),
            v_s.reshape(1, bs_, ss, d_sgu))
```

```python
import functools
import math

import jax
import jax.numpy as jnp
from jax import lax
from jax.experimental import pallas as pl
from jax.experimental.pallas import tpu as pltpu
from jax.experimental.pallas import tpu_sc as plsc

F32 = jnp.float32
BF16 = jnp.bfloat16
I32 = jnp.int32
U32 = jnp.uint32
EPS = 1e-6

LANES = 128
SUBLANES = 8
VMEM_LIMIT = 56 * 1024 * 1024

SSM_SLABS = 4
SCAN_LANES = 512
PEER_TOPK = 16
SGU_CHUNK = 128
ROUTE_LANES = 256
INPROJ_SUB = 128
PACK_ROWS = 512


def _pick(n, candidates):
    for c in candidates:
        if n % c == 0:
            return c
    raise ValueError(f"no block size for {n} in {candidates}")


def _const_spec(shape):
    nd = len(shape)
    return pl.BlockSpec(shape, lambda *_: (0,) * nd, pipeline_mode=pl.Buffered(1))


def _rms(x, g):
    return x * lax.rsqrt(jnp.mean(x * x, axis=-1, keepdims=True) + EPS) * g


def _inproj_kernel(x_ref, g_ref, w_ref, lng_ref, lnb_ref, u_ref, su_ref, vn_ref, gates_ref, xn_sc):
    j = pl.program_id(1)
    tm = x_ref.shape[0]
    sub = min(tm, INPROJ_SUB)

    @pl.when(j == 0)
    def _():
        xn_sc[...] = _rms(x_ref[...], g_ref[...]).astype(BF16)

    def section(out_ref, epilogue):
        for r in range(tm // sub):
            rows = slice(r * sub, (r + 1) * sub)
            z = jnp.dot(xn_sc[rows, :], w_ref[...], preferred_element_type=F32)
            out_ref[rows, :] = epilogue(z).astype(out_ref.dtype)

    def layer_norm_gelu(z):
        v = jax.nn.gelu(z)
        mu = jnp.mean(v, axis=-1, keepdims=True)
        vc = v - mu
        y = vc * lax.rsqrt(jnp.mean(vc * vc, axis=-1, keepdims=True) + EPS)
        return y * lng_ref[...] + lnb_ref[...]

    @pl.when(j == 0)
    def _():
        section(u_ref, lambda z: z)

    @pl.when(j == 1)
    def _():
        section(su_ref, jax.nn.gelu)

    @pl.when(j == 2)
    def _():
        section(vn_ref, layer_norm_gelu)

    @pl.when(j >= 3)
    def _():
        section(gates_ref, jax.nn.sigmoid)


def _inproj(x, window, g, w_in_bf, ln_g, ln_b, d_ssm, d_sgu):
    row0, rows = window
    dm = x.shape[1]
    d_in = w_in_bf.shape[1]
    tn = d_ssm
    assert d_sgu == tn and (d_in - 3 * tn) % tn == 0
    nj = d_in // tn
    tm = _pick(math.gcd(rows, row0) if row0 else rows, (1024, 512, 256, 128))
    blk0 = row0 // tm
    return pl.pallas_call(
        _inproj_kernel,
        grid=(rows // tm, nj),
        in_specs=[
            pl.BlockSpec((tm, dm), lambda i, j: (i + blk0, 0)),
            pl.BlockSpec((1, dm), lambda i, j: (0, 0)),
            pl.BlockSpec((dm, tn), lambda i, j: (0, j)),
            pl.BlockSpec((1, tn), lambda i, j: (0, 0)),
            pl.BlockSpec((1, tn), lambda i, j: (0, 0)),
        ],
        out_specs=[
            pl.BlockSpec((tm, tn), lambda i, j: (i, 0)),
            pl.BlockSpec((tm, tn), lambda i, j: (i, 0)),
            pl.BlockSpec((tm, tn), lambda i, j: (i, 0)),
            pl.BlockSpec((tm, tn), lambda i, j: (i, jnp.maximum(j - 3, 0))),
        ],
        out_shape=[
            jax.ShapeDtypeStruct((rows, tn), F32),
            jax.ShapeDtypeStruct((rows, tn), BF16),
            jax.ShapeDtypeStruct((rows, tn), F32),
            jax.ShapeDtypeStruct((rows, d_in - 3 * tn), BF16),
        ],
        scratch_shapes=[pltpu.VMEM((tm, dm), BF16)],
        compiler_params=pltpu.CompilerParams(
            dimension_semantics=("arbitrary", "arbitrary"), vmem_limit_bytes=VMEM_LIMIT),
        name="inproj",
    )(x, g, w_in_bf, ln_g, ln_b)


def _s5_kernel(u_ref, h0r_ref, h0i_ref, bs_ref, cs_ref, d_ref, gluw_ref, glub_ref, m_ref, pw_ref,
               ya_ref, hlr_ref, hli_ref, hr_sc, hi_sc, cr_sc, ci_sc, *, seq_len):
    i = pl.program_id(0)
    tl, d_ssm = u_ref.shape
    n_state = hr_sc.shape[1]
    kin = d_ssm // SSM_SLABS
    kst = n_state // SSM_SLABS

    @pl.when(i == 0)
    def _():
        cr_sc[...] = jnp.zeros_like(cr_sc)
        ci_sc[...] = jnp.zeros_like(ci_sc)

    ub = u_ref[...].astype(BF16)
    for s in range(SSM_SLABS):
        r = jnp.dot(ub[:, kin * s:kin * (s + 1)], bs_ref[s], preferred_element_type=F32)
        hr_sc[:, kst * s:kst * (s + 1)] = r[:, :kst]
        hi_sc[:, kst * s:kst * (s + 1)] = r[:, kst:]

    seg_rows = min(tl, seq_len)
    seg_tiles = seg_rows // SUBLANES
    for lc in range(n_state // SCAN_LANES):
        ls = slice(lc * SCAN_LANES, (lc + 1) * SCAN_LANES)

        def tile_body(t, carry, ls=ls):
            cr, ci = carry
            r0 = pl.multiple_of(t * SUBLANES, SUBLANES)
            br = hr_sc[pl.ds(r0, SUBLANES), ls]
            bi = hi_sc[pl.ds(r0, SUBLANES), ls]
            for k, shift in enumerate((1, 2, 4)):
                mr = m_ref[2 * k, :, ls]
                mi = m_ref[2 * k + 1, :, ls]
                sr = pltpu.roll(br, shift, 0)
                si = pltpu.roll(bi, shift, 0)
                br, bi = br + (mr * sr - mi * si), bi + (mr * si + mi * sr)
            pr = pw_ref[0, :, ls]
            pi_ = pw_ref[1, :, ls]
            hr = br + (pr * cr - pi_ * ci)
            hi = bi + (pr * ci + pi_ * cr)
            hr_sc[pl.ds(r0, SUBLANES), ls] = hr
            hi_sc[pl.ds(r0, SUBLANES), ls] = hi
            return (jnp.broadcast_to(hr[SUBLANES - 1:SUBLANES, :], (SUBLANES, SCAN_LANES)),
                    jnp.broadcast_to(hi[SUBLANES - 1:SUBLANES, :], (SUBLANES, SCAN_LANES)))

        def segment(q, _, ls=ls):
            row0 = i * tl + q * seg_rows
            b = row0 // seq_len
            is_start = (row0 % seq_len) == 0
            h0r = jnp.broadcast_to(h0r_ref[pl.ds(b, 1), ls], (SUBLANES, SCAN_LANES))
            h0i = jnp.broadcast_to(h0i_ref[pl.ds(b, 1), ls], (SUBLANES, SCAN_LANES))
            carry = (jnp.where(is_start, h0r, cr_sc[:, ls]), jnp.where(is_start, h0i, ci_sc[:, ls]))
            cr, ci = lax.fori_loop(q * seg_tiles, (q + 1) * seg_tiles, tile_body, carry)
            cr_sc[:, ls] = cr
            ci_sc[:, ls] = ci
            hlr_ref[pl.ds(b, 1), ls] = cr[0:1, :]
            hli_ref[pl.ds(b, 1), ls] = ci[0:1, :]
            return 0

        if tl == seg_rows:
            segment(0, 0)
        else:
            lax.fori_loop(0, tl // seg_rows, segment, 0)

    ys = []
    for s in range(SSM_SLABS):
        hre = hr_sc[:, kst * s:kst * (s + 1)].astype(BF16)
        him = hi_sc[:, kst * s:kst * (s + 1)].astype(BF16)
        ys.append(jnp.dot(hre, cs_ref[s, :kst, :], preferred_element_type=F32)
                  + jnp.dot(him, cs_ref[s, kst:, :], preferred_element_type=F32))
    y = jnp.concatenate(ys, axis=1)
    y = jax.nn.gelu(y + d_ref[...] * u_ref[...])
    gt = jnp.dot(y.astype(BF16), gluw_ref[...], preferred_element_type=F32) + glub_ref[...]
    ya_ref[...] = (y * jax.nn.sigmoid(gt)).astype(BF16)


def _s5(u, h0r, h0i, tabs, seq_len):
    rows, d_ssm = u.shape
    nb, n_state = h0r.shape
    tl = _pick(rows, (256, 128, 64, 32, 16, 8)) if seq_len >= 256 else _pick(rows, (512, 256, 128, 64, 32, 16))
    assert (seq_len % tl == 0 or tl % seq_len == 0) and seq_len % SUBLANES == 0
    bs, cs, d, gluw, glub, mtab, ptab = tabs
    return pl.pallas_call(
        functools.partial(_s5_kernel, seq_len=seq_len),
        grid=(rows // tl,),
        in_specs=[
            pl.BlockSpec((tl, d_ssm), lambda i: (i, 0)),
            _const_spec((nb, n_state)),
            _const_spec((nb, n_state)),
            _const_spec(bs.shape),
            _const_spec(cs.shape),
            _const_spec(d.shape),
            _const_spec(gluw.shape),
            _const_spec(glub.shape),
            _const_spec(mtab.shape),
            _const_spec(ptab.shape),
        ],
        out_specs=[
            pl.BlockSpec((tl, d_ssm), lambda i: (i, 0)),
            pl.BlockSpec((nb, n_state), lambda i: (0, 0)),
            pl.BlockSpec((nb, n_state), lambda i: (0, 0)),
        ],
        out_shape=[
            jax.ShapeDtypeStruct((rows, d_ssm), BF16),
            jax.ShapeDtypeStruct((nb, n_state), F32),
            jax.ShapeDtypeStruct((nb, n_state), F32),
        ],
        scratch_shapes=[
            pltpu.VMEM((tl, n_state), F32),
            pltpu.VMEM((tl, n_state), F32),
            pltpu.VMEM((SUBLANES, n_state), F32),
            pltpu.VMEM((SUBLANES, n_state), F32),
        ],
        compiler_params=pltpu.CompilerParams(
            dimension_semantics=("arbitrary",), vmem_limit_bytes=VMEM_LIMIT),
        name="s5",
    )(u, h0r, h0i, bs, cs, d, gluw, glub, mtab, ptab)


def _mix_kernel(x_ref, ya_ref, su_ref, vn_ref, gates_ref, ws_ref, bias_ref, wa_ref, wb_ref, wo_ref, x1_ref):
    tm, dm = x_ref.shape
    n_heads = ws_ref.shape[0]
    hd = vn_ref.shape[1] // n_heads
    vb = vn_ref[...].astype(BF16)
    chunks = []
    for c in range(tm // SGU_CHUNK):
        heads = [jnp.dot(ws_ref[h], vb[c * SGU_CHUNK:(c + 1) * SGU_CHUNK, h * hd:(h + 1) * hd],
                         preferred_element_type=F32) for h in range(n_heads)]
        chunks.append(jnp.concatenate(heads, axis=1) + bias_ref[...])
    s_all = jnp.concatenate(chunks, axis=0)
    yb = (su_ref[...].astype(F32) * s_all).astype(BF16)
    pa = jnp.dot(ya_ref[...], wa_ref[...], preferred_element_type=F32)
    pb = jnp.dot(yb, wb_ref[...], preferred_element_type=F32)
    merged = gates_ref[:, :dm].astype(F32) * pa + gates_ref[:, dm:].astype(F32) * pb
    x1_ref[...] = x_ref[...] + jnp.dot(merged.astype(BF16), wo_ref[...], preferred_element_type=F32)


def _mix(x, window, ya, su, vn, gates, ws, bias, wa, wb, wo):
    row0, rows = window
    dm = x.shape[1]
    d_ssm = ya.shape[1]
    d_sgu = su.shape[1]
    tm = _pick(math.gcd(rows, row0) if row0 else rows, (256, 128))
    blk0 = row0 // tm
    row = lambda w: pl.BlockSpec((tm, w), lambda i: (i, 0))
    return pl.pallas_call(
        _mix_kernel,
        grid=(rows // tm,),
        in_specs=[pl.BlockSpec((tm, dm), lambda i: (i + blk0, 0)), row(d_ssm), row(d_sgu), row(d_sgu), row(2 * dm),
                  _const_spec(ws.shape), _const_spec(bias.shape),
                  _const_spec(wa.shape), _const_spec(wb.shape), _const_spec(wo.shape)],
        out_specs=row(dm),
        out_shape=jax.ShapeDtypeStruct((rows, dm), F32),
        compiler_params=pltpu.CompilerParams(
            dimension_semantics=("arbitrary",), vmem_limit_bytes=VMEM_LIMIT),
        name="mix",
    )(x, ya, su, vn, gates, ws, bias, wa, wb, wo)


def _topk_rows(s, k, payload=None):
    rows = s.shape[0]
    iota = lax.broadcasted_iota(I32, s.shape, 0).astype(F32)
    vals, outs = [], []
    for _ in range(k):
        m = jnp.max(s, axis=0, keepdims=True)
        j = jnp.min(jnp.where(s == m, iota, float(rows)), axis=0, keepdims=True)
        sel = iota == j
        vals.append(m)
        outs.append(j if payload is None else jnp.max(jnp.where(sel, payload, -1.0), axis=0, keepdims=True))
        s = jnp.where(sel, -jnp.inf, s)
    return jnp.concatenate(vals, axis=0), jnp.concatenate(outs, axis=0)


def _route_kernel(x1_ref, g_ref, wq_ref, keys_ref, xn_ref, eidx_ref, gate_ref, q_sc, e_sc, p_sc):
    tb = x1_ref.shape[0]
    tw = e_sc.shape[2]
    nsub = tb // tw
    n_heads, _, n_keys, half = keys_ref.shape
    dkey = 2 * half
    xn = _rms(x1_ref[...], g_ref[...])
    xn_ref[...] = xn
    q = jnp.dot(xn.astype(BF16), wq_ref[...], preferred_element_type=F32).astype(BF16)
    for h in range(n_heads):
        q_sc[h] = q[:, h * dkey:(h + 1) * dkey]

    nt = (((1,), (1,)), ((), ()))

    def body(n, _):
        h = n // nsub
        r0 = pl.multiple_of((n % nsub) * tw, tw)
        qh = q_sc[h, pl.ds(r0, tw), :]
        s1 = lax.dot_general(keys_ref[h, 0], qh[:, :half], nt, preferred_element_type=F32)
        s2 = lax.dot_general(keys_ref[h, 1], qh[:, half:], nt, preferred_element_type=F32)
        v1, i1 = _topk_rows(s1, PEER_TOPK)
        v2, i2 = _topk_rows(s2, PEER_TOPK)
        cv, ci = [], []
        for a in range(PEER_TOPK):
            nb = PEER_TOPK // (a + 1)
            cv.append(v1[a:a + 1, :] + v2[:nb, :])
            ci.append(i1[a:a + 1, :] * float(n_keys) + i2[:nb, :])
        ts, te = _topk_rows(jnp.concatenate(cv, axis=0), PEER_TOPK, payload=jnp.concatenate(ci, axis=0))
        ex = jnp.exp(ts - ts[0:1, :])
        e_sc[n] = te
        p_sc[n] = ex / jnp.sum(ex, axis=0, keepdims=True)
        return 0

    lax.fori_loop(0, n_heads * nsub, body, 0)
    for c in range(nsub):
        rows = slice(c * tw, (c + 1) * tw)
        e_all = jnp.concatenate([e_sc[h * nsub + c] for h in range(n_heads)], axis=0)
        p_all = jnp.concatenate([p_sc[h * nsub + c] for h in range(n_heads)], axis=0)
        eidx_ref[rows, :] = e_all.T.astype(I32)
        gate_ref[rows, :] = p_all.T


def _route(x1, g, wq, keys_bf):
    rows, dm = x1.shape
    n_heads, _, n_keys, half = keys_bf.shape
    nsel = n_heads * PEER_TOPK
    tb = _pick(rows, (256, 128))
    tw = min(tb, ROUTE_LANES)
    row = lambda w: pl.BlockSpec((tb, w), lambda i: (i, 0))
    return pl.pallas_call(
        _route_kernel,
        grid=(rows // tb,),
        in_specs=[row(dm), _const_spec(g.shape), _const_spec(wq.shape), _const_spec(keys_bf.shape)],
        out_specs=[row(dm), row(nsel), row(nsel)],
        out_shape=[
            jax.ShapeDtypeStruct((rows, dm), F32),
            jax.ShapeDtypeStruct((rows, nsel), I32),
            jax.ShapeDtypeStruct((rows, nsel), F32),
        ],
        scratch_shapes=[
            pltpu.VMEM((n_heads, tb, 2 * half), BF16),
            pltpu.VMEM((n_heads * (tb // tw), PEER_TOPK, tw), F32),
            pltpu.VMEM((n_heads * (tb // tw), PEER_TOPK, tw), F32),
        ],
        compiler_params=pltpu.CompilerParams(
            dimension_semantics=("arbitrary",), vmem_limit_bytes=VMEM_LIMIT),
        name="route",
    )(x1, g, wq, keys_bf)


PEER_BUFS = 16


def _peer_kernel(eidx_hbm, gate_ref, xn_ref, x1_ref, gf_ref, tab_hbm, y_ref,
                 idx_sm, idx_sem, *scratch, final_norm):
    i = pl.program_id(0)
    bufs = scratch[:PEER_BUFS]
    sem, peer_sc = scratch[PEER_BUFS:]
    tbk, nsel = gate_ref.shape
    half = tab_hbm.shape[1] // 2
    nlt = half // LANES

    icp = pltpu.make_async_copy(eidx_hbm.at[i], idx_sm, idx_sem)
    icp.start()
    icp.wait()

    def issue(t, s):
        base = t * nsel
        for k in range(nsel):
            e = idx_sm[base + k]
            pltpu.make_async_copy(tab_hbm.at[pl.ds(e, 1), :], bufs[s].at[pl.ds(k, 1), :],
                                  sem.at[s]).start(priority=k % 2)

    def wait(s):
        pltpu.make_async_copy(tab_hbm.at[pl.ds(0, nsel), :], bufs[s], sem.at[s]).wait()

    hi_mask = jnp.int32(-65536)
    eye = (lax.broadcasted_iota(I32, (nsel, nsel), 0) == lax.broadcasted_iota(I32, (nsel, nsel), 1))

    def unpack(w):
        return (lax.bitcast_convert_type(w & hi_mask, F32),
                lax.bitcast_convert_type(w << 16, F32))

    def compute(t, s):
        buf = bufs[s]
        xrow = xn_ref[pl.ds(t, 1), :]
        acc = jnp.zeros((nsel, LANES), F32)
        for j in range(nlt):
            uh, ul = unpack(buf[:, j * LANES:(j + 1) * LANES])
            acc = acc + uh * xrow[:, j * LANES:(j + 1) * LANES] \
                      + ul * xrow[:, half + j * LANES:half + (j + 1) * LANES]
        act = jnp.sum(acc, axis=1, keepdims=True)
        grow = jnp.broadcast_to(gate_ref[pl.ds(t, 1), :], (nsel, nsel))
        gcol = jnp.sum(jnp.where(eye, grow, 0.0), axis=1, keepdims=True)
        wgt = jnp.broadcast_to(gcol * jax.nn.gelu(act), (nsel, LANES))
        t8 = pl.multiple_of((t // SUBLANES) * SUBLANES, SUBLANES)
        rsel = lax.broadcasted_iota(I32, (SUBLANES, LANES), 0) == (t % SUBLANES)

        def put(lane0, row):
            cur = peer_sc[pl.ds(t8, SUBLANES), lane0:lane0 + LANES]
            peer_sc[pl.ds(t8, SUBLANES), lane0:lane0 + LANES] = jnp.where(
                rsel, jnp.broadcast_to(row, (SUBLANES, LANES)), cur)

        for j in range(nlt):
            vh, vl = unpack(buf[:, half + j * LANES:half + (j + 1) * LANES])
            put(j * LANES, jnp.sum(vh * wgt, axis=0, keepdims=True))
            put(half + j * LANES, jnp.sum(vl * wgt, axis=0, keepdims=True))

    def step(t, s, prefetch):
        wait(s)
        if prefetch:
            issue(t + PEER_BUFS - 1, (s + PEER_BUFS - 1) % PEER_BUFS)
        compute(t, s)

    peer_sc[...] = jnp.zeros_like(peer_sc)
    for t in range(PEER_BUFS - 1):
        issue(t, t)

    def group(gi, _):
        for s in range(PEER_BUFS):
            step(gi * PEER_BUFS + s, s, True)
        return 0

    lax.fori_loop(0, tbk // PEER_BUFS - 1, group, 0)
    for s in range(PEER_BUFS):
        t = tbk - PEER_BUFS + s
        step(t, s, t + PEER_BUFS - 1 < tbk)

    xo = x1_ref[...] + peer_sc[...]
    y_ref[...] = _rms(xo, gf_ref[...]) if final_norm else xo


def _peer(eidx, gate, xn, x1, gf, table, final_norm):
    rows, dm = x1.shape
    nsel = gate.shape[1]
    tbk = _pick(rows, (256, 128, 64, 32, 16, 8))
    assert tbk % PEER_BUFS == 0 and tbk >= 2 * PEER_BUFS
    nblk = rows // tbk
    eidx_blk = eidx.reshape(nblk, tbk * nsel)
    row = lambda w: pl.BlockSpec((tbk, w), lambda i: (i, 0))
    return pl.pallas_call(
        functools.partial(_peer_kernel, final_norm=final_norm),
        grid=(nblk,),
        in_specs=[pl.BlockSpec(memory_space=pl.ANY), row(nsel), row(dm), row(dm),
                  _const_spec(gf.shape), pl.BlockSpec(memory_space=pl.ANY)],
        out_specs=row(dm),
        out_shape=jax.ShapeDtypeStruct((rows, dm), F32),
        scratch_shapes=[
            pltpu.SMEM((tbk * nsel,), I32),
            pltpu.SemaphoreType.DMA(()),
        ] + [pltpu.VMEM((nsel, table.shape[1]), table.dtype) for _ in range(PEER_BUFS)] + [
            pltpu.SemaphoreType.DMA((PEER_BUFS,)),
            pltpu.VMEM((tbk, dm), F32),
        ],
        compiler_params=pltpu.CompilerParams(
            dimension_semantics=("arbitrary",), vmem_limit_bytes=VMEM_LIMIT),
        name="peer",
    )(eidx_blk, gate, xn, x1, gf, table)


SC_L = 16
SC_WINDOW = SC_L
SC_SUBCORES = 32
SC_CUTS = (0.0625, 0.4375)
SC_SPLIT_ALIGN = 512


def _sc_peer(table, eidx, gate, xn):
    ntok, nsel = eidx.shape
    d = table.shape[1]
    half = d // 2
    nch = half // SC_L
    info = plsc.get_sparse_core_info()
    n_workers = info.num_cores * info.num_subcores
    n_win = nsel // SC_WINDOW
    tpw = ntok // n_workers
    assert tpw * n_workers == ntok and SC_WINDOW == SC_L and info.num_lanes == SC_L
    mesh = plsc.VectorSubcoreMesh(core_axis_name="core", subcore_axis_name="subcore")
    c0 = math.sqrt(2.0 / math.pi)

    @functools.partial(
        pl.kernel, out_type=jax.ShapeDtypeStruct((ntok, d), F32), mesh=mesh,
        scratch_types=[pltpu.VMEM((nsel,), I32), pltpu.VMEM((nsel,), F32), pltpu.VMEM((d,), F32),
                       pltpu.VMEM((2, SC_WINDOW, d), I32), pltpu.VMEM((d,), F32),
                       pltpu.SemaphoreType.DMA((2,))],
        compiler_params=pltpu.CompilerParams(needs_layout_passes=False),
        name="sc_peer")
    def k(tab_hbm, idx_hbm, gate_hbm, x_hbm, o_hbm, idx_v, gate_v, x_v, rows_v, out_v, gsem):
        wid = lax.axis_index("subcore") * info.num_cores + lax.axis_index("core")
        lane = lax.iota(I32, SC_L)
        zero = jnp.zeros((SC_L,), F32)

        def unpack(w):
            return (lax.bitcast_convert_type(w & jnp.int32(-65536), F32),
                    lax.bitcast_convert_type(w << 16, F32))

        @pl.loop(0, tpw)
        def _(tt):
            tok = wid * tpw + tt
            pltpu.sync_copy(idx_hbm.at[pl.ds(tok * nsel, nsel)], idx_v)
            pltpu.sync_copy(gate_hbm.at[pl.ds(tok * nsel, nsel)], gate_v)
            pltpu.sync_copy(x_hbm.at[tok], x_v)

            @pl.loop(0, d // SC_L)
            def _(c):
                out_v[pl.ds(c * SC_L, SC_L)] = zero

            def gather(w):
                return pltpu.make_async_copy(tab_hbm.at[idx_v.at[pl.ds(w * SC_WINDOW, SC_WINDOW)]],
                                             rows_v.at[w % 2], gsem.at[w % 2])

            gather(0).start()
            for w in range(n_win):
                gather(w).wait()
                if w + 1 < n_win:
                    gather(w + 1).start()
                slot = w % 2

                def dot_chunk(j, accs, slot=slot):
                    xa = x_v[pl.ds(j * SC_L, SC_L)]
                    xb = x_v[pl.ds(half + j * SC_L, SC_L)]
                    out = []
                    for r in range(SC_WINDOW):
                        uh, ul = unpack(rows_v[slot, r, pl.ds(j * SC_L, SC_L)])
                        out.append(accs[r] + (uh * xa + ul * xb))
                    return tuple(out)

                accs = lax.fori_loop(0, nch, dot_chunk, (zero,) * SC_WINDOW)
                act = zero
                for r in range(SC_WINDOW):
                    act = jnp.where(lane == r, jnp.sum(accs[r]), act)
                z = c0 * (act + 0.044715 * (act * act * act))
                tanh_z = 1.0 - 2.0 / (jnp.exp(2.0 * z) + 1.0)
                wgt = gate_v[pl.ds(w * SC_WINDOW, SC_WINDOW)] * (0.5 * act * (1.0 + tanh_z))
                splat = [jnp.full((SC_L,), jnp.sum(jnp.where(lane == r, wgt, 0.0)), F32)
                         for r in range(SC_WINDOW)]

                @pl.loop(0, nch)
                def _(j, slot=slot, splat=splat):
                    oh = zero
                    ol = zero
                    for r in range(SC_WINDOW):
                        vh, vl = unpack(rows_v[slot, r, pl.ds(half + j * SC_L, SC_L)])
                        oh = oh + splat[r] * vh
                        ol = ol + splat[r] * vl
                    out_v[pl.ds(j * SC_L, SC_L)] += oh
                    out_v[pl.ds(half + j * SC_L, SC_L)] += ol

            pltpu.sync_copy(out_v, o_hbm.at[tok])

    return k(table, eidx.reshape(-1), gate.reshape(-1), xn)


def _finish_kernel(x1_ref, p_ref, gf_ref, y_ref):
    y_ref[...] = _rms(x1_ref[...] + p_ref[...], gf_ref[...])


def _finish(x1, p, gf):
    rows, dm = x1.shape
    tm = _pick(rows, (512, 256, 128))
    row = pl.BlockSpec((tm, dm), lambda i: (i, 0))
    return pl.pallas_call(
        _finish_kernel, grid=(rows // tm,), in_specs=[row, row, _const_spec(gf.shape)], out_specs=row,
        out_shape=jax.ShapeDtypeStruct((rows, dm), F32),
        compiler_params=pltpu.CompilerParams(dimension_semantics=("arbitrary",), vmem_limit_bytes=VMEM_LIMIT),
        name="finish",
    )(x1, p, gf)


def _cmul(ar, ai, br, bi):
    return ar * br - ai * bi, ar * bi + ai * br


def _s5_tables(lam_re, lam_im, log_dt, b_re, b_im, c_re, c_im, d, glu_w, glu_b):
    g, p = lam_re.shape
    hg = b_re.shape[-1]
    gs = g // SSM_SLABS
    dt = jnp.exp(log_dt.astype(F32))[:, None]
    lr = lam_re.astype(F32)
    li = lam_im.astype(F32)
    mag = jnp.exp(lr * dt)
    ab_re = mag * jnp.cos(li * dt)
    ab_im = mag * jnp.sin(li * dt)
    den = lr * lr + li * li
    nr = ab_re - 1.0
    ni = ab_im
    coef_re = (nr * lr + ni * li) / den
    coef_im = (ni * lr - nr * li) / den
    br = b_re.astype(F32)
    bi = b_im.astype(F32)
    bb_re = coef_re[..., None] * br - coef_im[..., None] * bi
    bb_im = coef_re[..., None] * bi + coef_im[..., None] * br
    eye = jnp.eye(gs, dtype=F32)

    def b_slab(bb):
        t = bb.reshape(SSM_SLABS, gs, p, hg).transpose(0, 1, 3, 2)
        return jnp.einsum('sihp,ij->sihjp', t, eye).reshape(SSM_SLABS, gs * hg, gs * p)

    bs = jnp.concatenate([b_slab(bb_re), b_slab(bb_im)], axis=2).astype(BF16)

    def c_slab(c):
        t = c.astype(F32).reshape(SSM_SLABS, gs, hg, p).transpose(0, 1, 3, 2)
        return jnp.einsum('siph,ij->sipjh', t, eye).reshape(SSM_SLABS, gs * p, gs * hg)

    cs = jnp.concatenate([c_slab(c_re), -c_slab(c_im)], axis=1).astype(BF16)

    a1 = (ab_re.reshape(1, g * p), ab_im.reshape(1, g * p))
    a2 = _cmul(*a1, *a1)
    a4 = _cmul(*a2, *a2)
    rowid = jnp.arange(SUBLANES)[:, None]
    mtab = jnp.stack([jnp.where(rowid >= s, comp, 0.0)
                      for s, a in ((1, a1), (2, a2), (4, a4)) for comp in a])
    pows = [a1]
    for _ in range(SUBLANES - 1):
        pows.append(_cmul(*pows[-1], *a1))
    ptab = jnp.stack([jnp.concatenate([q[0] for q in pows], axis=0),
                      jnp.concatenate([q[1] for q in pows], axis=0)])
    return (bs, cs, d.astype(F32).reshape(1, -1), glu_w.astype(BF16),
            glu_b.astype(F32).reshape(1, -1), mtab, ptab)


def _sgu_tables(w_s, b_s, seq_len):
    n_heads = w_s.shape[0]
    ln = min(SGU_CHUNK, seq_len)
    assert SGU_CHUNK % ln == 0 and seq_len % ln == 0
    rep = SGU_CHUNK // ln
    mask = jnp.tril(jnp.ones((ln, ln), dtype=bool))
    w = jnp.where(mask[None], w_s[:, :ln, :ln], 0.0).astype(F32)
    ws = jnp.einsum('hij,ab->haibj', w, jnp.eye(rep, dtype=F32)).reshape(n_heads, SGU_CHUNK, SGU_CHUNK)
    bias = jnp.tile(jnp.transpose(b_s[:, :ln]).astype(F32), (rep, 1))
    return ws.astype(BF16), bias


def _pack_kernel(u_ref, v_ref, o_ref):
    half = u_ref.shape[1] // 2

    def bf16_bits(x):
        return lax.bitcast_convert_type(x.astype(BF16).astype(F32), U32)

    def pack(t_ref):
        w = bf16_bits(t_ref[:, :half]) | (bf16_bits(t_ref[:, half:]) >> 16)
        return lax.bitcast_convert_type(w, I32)

    o_ref[:, :half] = pack(u_ref)
    o_ref[:, half:] = pack(v_ref)


def _pack_tables(u, v):
    n, dd = u.shape
    tr = _pick(n, (PACK_ROWS, 256, 128, 8))
    spec = pl.BlockSpec((tr, dd), lambda i: (i, 0))
    return pl.pallas_call(
        _pack_kernel,
        grid=(n // tr,),
        in_specs=[spec, spec],
        out_specs=spec,
        out_shape=jax.ShapeDtypeStruct((n, dd), I32),
        compiler_params=pltpu.CompilerParams(
            dimension_semantics=("arbitrary",), vmem_limit_bytes=VMEM_LIMIT),
        name="pack",
    )(u, v)


def _trunk_to_route(x, h0r, h0i, seq_len, lp, window=None):
    (norm_mix_g, w_in_bf, s5_tabs, ln_g, ln_b, sgu_w, sgu_b, wa, wb, wo, norm_ffn_g, wq, keys_bf, _,
     d_ssm, d_sgu) = lp
    window = (0, x.shape[0]) if window is None else window
    u, su, vn, gates = _inproj(x, window, norm_mix_g, w_in_bf, ln_g, ln_b, d_ssm, d_sgu)
    ya, hlr, hli = _s5(u, h0r, h0i, s5_tabs, seq_len)
    ws, bias = _sgu_tables(sgu_w, sgu_b, seq_len)
    hd = d_sgu // ws.shape[0]
    bias_tile = jnp.repeat(bias, hd, axis=1)
    x1 = _mix(x, window, ya, su, vn, gates, ws, bias_tile, wa, wb, wo)
    xn, eidx, gate = _route(x1, norm_ffn_g, wq, keys_bf)
    return (x1, xn, eidx, gate), hlr, hli, vn


def _trunk_layer(x, h0r, h0i, seq_len, lp, final_g):
    (x1, xn, eidx, gate), hlr, hli, vn = _trunk_to_route(x, h0r, h0i, seq_len, lp)
    y = _peer(eidx, gate, xn, x1, final_g, lp[13], final_norm=True)
    return y, hlr, hli, vn


def _prompt_split(x, seq_len, lp, final_g, n_state):
    rows = x.shape[0]
    cuts = [(int(rows * c) // SC_SPLIT_ALIGN) * SC_SPLIT_ALIGN for c in SC_CUTS]
    bounds = [0] + cuts + [rows]
    hr = hi = jnp.zeros((1, n_state), F32)
    parts = []
    for a, b in zip(bounds[:-1], bounds[1:]):
        part, hr, hi, _ = _trunk_to_route(x, hr, hi, seq_len, lp, window=(a, b - a))
        parts.append(part)
    x1a, xna, ea, ga = parts[-1]
    ya = _peer(ea, ga, xna, x1a, final_g, lp[13], final_norm=True)
    return parts[:-1], ya, hr, hi


def _prompt_finish(firsts, ya, lp, final_g):
    ys = [_finish(x1, _sc_peer(lp[13], e, g, xn), final_g) for (x1, xn, e, g) in firsts]
    ys = ys if ya is None else ys + [ya]
    return ys[0] if len(ys) == 1 else jnp.concatenate(ys, axis=0)


def kernel(x_prompt, x_sample, state_ssm_re, state_ssm_im, norm_mix_g, w_in, ssm_lambda_re, ssm_lambda_im, ssm_log_dt, ssm_b_re, ssm_b_im, ssm_c_re, ssm_c_im, ssm_d, ssm_glu_w, ssm_glu_b, sgu_ln_g, sgu_ln_b, sgu_w, sgu_b, w_branch_a, w_branch_b, w_out, norm_ffn_g, peer_w_q, peer_keys, peer_u, peer_v, norm_final_g):
    depth = w_in.shape[0]
    assert depth == 1, "the final norm is fused into the last layer's PEER kernel"
    bp, sp, dm = x_prompt.shape
    bs_, ss, _ = x_sample.shape
    g, p = ssm_lambda_re.shape[1:]
    d_ssm = ssm_d.shape[1]
    d_sgu = sgu_ln_g.shape[1]
    n_state = g * p
    l = 0
    lp = (norm_mix_g[l].reshape(1, dm), w_in[l].astype(BF16),
          _s5_tables(ssm_lambda_re[l], ssm_lambda_im[l], ssm_log_dt[l], ssm_b_re[l], ssm_b_im[l],
                     ssm_c_re[l], ssm_c_im[l], ssm_d[l], ssm_glu_w[l], ssm_glu_b[l]),
          sgu_ln_g[l].reshape(1, d_sgu), sgu_ln_b[l].reshape(1, d_sgu), sgu_w[l], sgu_b[l],
          w_branch_a[l].astype(BF16), w_branch_b[l].astype(BF16), w_out[l].astype(BF16),
          norm_ffn_g[l].reshape(1, dm), peer_w_q[l].astype(BF16), peer_keys[l].astype(BF16),
          _pack_tables(peer_u[l].astype(F32), peer_v[l].astype(F32)),
          d_ssm, d_sgu)
    gf = norm_final_g.reshape(1, dm)

    xp2 = x_prompt.reshape(bp * sp, dm)
    xs2 = x_sample.reshape(bs_ * ss, dm)
    h0s = (state_ssm_re[l].astype(F32).reshape(bs_, n_state), state_ssm_im[l].astype(F32).reshape(bs_, n_state))
    split = bp == 1 and sp >= 4 * SC_SPLIT_ALIGN and (bs_ * ss) % SC_SUBCORES == 0
    if split:
        sample, hr_s, hi_s, v_s = _trunk_to_route(xs2, h0s[0], h0s[1], ss, lp)
        ys = _prompt_finish([sample], None, lp, gf)
        first, ya, hr_p, hi_p = _prompt_split(xp2, sp, lp, gf, n_state)
        yp = _prompt_finish(first, ya, lp, gf)
    else:
        zeros = jnp.zeros((bp, n_state), F32)
        yp, hr_p, hi_p, _ = _trunk_layer(xp2, zeros, zeros, sp, lp, gf)
        ys, hr_s, hi_s, v_s = _trunk_layer(xs2, h0s[0], h0s[1], ss, lp, gf)
    return (yp.reshape(bp, sp, dm), ys.reshape(bs_, ss, dm),
            hr_p.reshape(1, bp, g, p), hi_p.reshape(1, bp, g, p),
            hr_s.reshape(1, bs_, g, p), hi_s.reshape(1, bs_, g, p),
            v_s.reshape(1, bs_, ss, d_sgu))
```

```python
import functools
import math

import jax
import jax.numpy as jnp
from jax import lax
from jax.experimental import pallas as pl
from jax.experimental.pallas import tpu as pltpu
from jax.experimental.pallas import tpu_sc as plsc

F32 = jnp.float32
BF16 = jnp.bfloat16
I32 = jnp.int32
U32 = jnp.uint32
EPS = 1e-6

LANES = 128
SUBLANES = 8
VMEM_LIMIT = 56 * 1024 * 1024

SSM_SLABS = 4
SCAN_LANES = 512
PEER_TOPK = 16
SGU_CHUNK = 128
ROUTE_LANES = 256
INPROJ_SUB = 128
PACK_ROWS = 512


def _pick(n, candidates):
    for c in candidates:
        if n % c == 0:
            return c
    raise ValueError(f"no block size for {n} in {candidates}")


def _const_spec(shape):
    nd = len(shape)
    return pl.BlockSpec(shape, lambda *_: (0,) * nd, pipeline_mode=pl.Buffered(1))


def _rms(x, g):
    return x * lax.rsqrt(jnp.mean(x * x, axis=-1, keepdims=True) + EPS) * g


def _inproj_kernel(x_ref, g_ref, w_ref, lng_ref, lnb_ref, u_ref, su_ref, vn_ref, gates_ref, xn_sc):
    j = pl.program_id(1)
    tm = x_ref.shape[0]
    sub = min(tm, INPROJ_SUB)

    @pl.when(j == 0)
    def _():
        xn_sc[...] = _rms(x_ref[...], g_ref[...]).astype(BF16)

    def section(out_ref, epilogue):
        for r in range(tm // sub):
            rows = slice(r * sub, (r + 1) * sub)
            z = jnp.dot(xn_sc[rows, :], w_ref[...], preferred_element_type=F32)
            out_ref[rows, :] = epilogue(z).astype(out_ref.dtype)

    def layer_norm_gelu(z):
        v = jax.nn.gelu(z)
        mu = jnp.mean(v, axis=-1, keepdims=True)
        vc = v - mu
        y = vc * lax.rsqrt(jnp.mean(vc * vc, axis=-1, keepdims=True) + EPS)
        return y * lng_ref[...] + lnb_ref[...]

    @pl.when(j == 0)
    def _():
        section(u_ref, lambda z: z)

    @pl.when(j == 1)
    def _():
        section(su_ref, jax.nn.gelu)

    @pl.when(j == 2)
    def _():
        section(vn_ref, layer_norm_gelu)

    @pl.when(j >= 3)
    def _():
        section(gates_ref, jax.nn.sigmoid)


def _inproj(x, window, g, w_in_bf, ln_g, ln_b, d_ssm, d_sgu):
    row0, rows = window
    dm = x.shape[1]
    d_in = w_in_bf.shape[1]
    tn = d_ssm
    assert d_sgu == tn and (d_in - 3 * tn) % tn == 0
    nj = d_in // tn
    tm = _pick(math.gcd(rows, row0) if row0 else rows, (1024, 512, 256, 128))
    blk0 = row0 // tm
    return pl.pallas_call(
        _inproj_kernel,
        grid=(rows // tm, nj),
        in_specs=[
            pl.BlockSpec((tm, dm), lambda i, j: (i + blk0, 0)),
            pl.BlockSpec((1, dm), lambda i, j: (0, 0)),
            pl.BlockSpec((dm, tn), lambda i, j: (0, j)),
            pl.BlockSpec((1, tn), lambda i, j: (0, 0)),
            pl.BlockSpec((1, tn), lambda i, j: (0, 0)),
        ],
        out_specs=[
            pl.BlockSpec((tm, tn), lambda i, j: (i, 0)),
            pl.BlockSpec((tm, tn), lambda i, j: (i, 0)),
            pl.BlockSpec((tm, tn), lambda i, j: (i, 0)),
            pl.BlockSpec((tm, tn), lambda i, j: (i, jnp.maximum(j - 3, 0))),
        ],
        out_shape=[
            jax.ShapeDtypeStruct((rows, tn), F32),
            jax.ShapeDtypeStruct((rows, tn), BF16),
            jax.ShapeDtypeStruct((rows, tn), F32),
            jax.ShapeDtypeStruct((rows, d_in - 3 * tn), BF16),
        ],
        scratch_shapes=[pltpu.VMEM((tm, dm), BF16)],
        compiler_params=pltpu.CompilerParams(
            dimension_semantics=("arbitrary", "arbitrary"), vmem_limit_bytes=VMEM_LIMIT),
        name="inproj",
    )(x, g, w_in_bf, ln_g, ln_b)


def _s5_kernel(u_ref, h0r_ref, h0i_ref, bs_ref, cs_ref, d_ref, gluw_ref, glub_ref, m_ref, pw_ref,
               ya_ref, hlr_ref, hli_ref, hr_sc, hi_sc, cr_sc, ci_sc, *, seq_len):
    i = pl.program_id(0)
    tl, d_ssm = u_ref.shape
    n_state = hr_sc.shape[1]
    kin = d_ssm // SSM_SLABS
    kst = n_state // SSM_SLABS

    @pl.when(i == 0)
    def _():
        cr_sc[...] = jnp.zeros_like(cr_sc)
        ci_sc[...] = jnp.zeros_like(ci_sc)

    ub = u_ref[...].astype(BF16)
    for s in range(SSM_SLABS):
        r = jnp.dot(ub[:, kin * s:kin * (s + 1)], bs_ref[s], preferred_element_type=F32)
        hr_sc[:, kst * s:kst * (s + 1)] = r[:, :kst]
        hi_sc[:, kst * s:kst * (s + 1)] = r[:, kst:]

    seg_rows = min(tl, seq_len)
    seg_tiles = seg_rows // SUBLANES
    for lc in range(n_state // SCAN_LANES):
        ls = slice(lc * SCAN_LANES, (lc + 1) * SCAN_LANES)

        def tile_body(t, carry, ls=ls):
            cr, ci = carry
            r0 = pl.multiple_of(t * SUBLANES, SUBLANES)
            br = hr_sc[pl.ds(r0, SUBLANES), ls]
            bi = hi_sc[pl.ds(r0, SUBLANES), ls]
            for k, shift in enumerate((1, 2, 4)):
                mr = m_ref[2 * k, :, ls]
                mi = m_ref[2 * k + 1, :, ls]
                sr = pltpu.roll(br, shift, 0)
                si = pltpu.roll(bi, shift, 0)
                br, bi = br + (mr * sr - mi * si), bi + (mr * si + mi * sr)
            pr = pw_ref[0, :, ls]
            pi_ = pw_ref[1, :, ls]
            hr = br + (pr * cr - pi_ * ci)
            hi = bi + (pr * ci + pi_ * cr)
            hr_sc[pl.ds(r0, SUBLANES), ls] = hr
            hi_sc[pl.ds(r0, SUBLANES), ls] = hi
            return (jnp.broadcast_to(hr[SUBLANES - 1:SUBLANES, :], (SUBLANES, SCAN_LANES)),
                    jnp.broadcast_to(hi[SUBLANES - 1:SUBLANES, :], (SUBLANES, SCAN_LANES)))

        def segment(q, _, ls=ls):
            row0 = i * tl + q * seg_rows
            b = row0 // seq_len
            is_start = (row0 % seq_len) == 0
            h0r = jnp.broadcast_to(h0r_ref[pl.ds(b, 1), ls], (SUBLANES, SCAN_LANES))
            h0i = jnp.broadcast_to(h0i_ref[pl.ds(b, 1), ls], (SUBLANES, SCAN_LANES))
            carry = (jnp.where(is_start, h0r, cr_sc[:, ls]), jnp.where(is_start, h0i, ci_sc[:, ls]))
            cr, ci = lax.fori_loop(q * seg_tiles, (q + 1) * seg_tiles, tile_body, carry)
            cr_sc[:, ls] = cr
            ci_sc[:, ls] = ci
            hlr_ref[pl.ds(b, 1), ls] = cr[0:1, :]
            hli_ref[pl.ds(b, 1), ls] = ci[0:1, :]
            return 0

        if tl == seg_rows:
            segment(0, 0)
        else:
            lax.fori_loop(0, tl // seg_rows, segment, 0)

    ys = []
    for s in range(SSM_SLABS):
        hre = hr_sc[:, kst * s:kst * (s + 1)].astype(BF16)
        him = hi_sc[:, kst * s:kst * (s + 1)].astype(BF16)
        ys.append(jnp.dot(hre, cs_ref[s, :kst, :], preferred_element_type=F32)
                  + jnp.dot(him, cs_ref[s, kst:, :], preferred_element_type=F32))
    y = jnp.concatenate(ys, axis=1)
    y = jax.nn.gelu(y + d_ref[...] * u_ref[...])
    gt = jnp.dot(y.astype(BF16), gluw_ref[...], preferred_element_type=F32) + glub_ref[...]
    ya_ref[...] = (y * jax.nn.sigmoid(gt)).astype(BF16)


def _s5(u, h0r, h0i, tabs, seq_len):
    rows, d_ssm = u.shape
    nb, n_state = h0r.shape
    tl = _pick(rows, (256, 128, 64, 32, 16, 8)) if seq_len >= 256 else _pick(rows, (512, 256, 128, 64, 32, 16))
    assert (seq_len % tl == 0 or tl % seq_len == 0) and seq_len % SUBLANES == 0
    bs, cs, d, gluw, glub, mtab, ptab = tabs
    return pl.pallas_call(
        functools.partial(_s5_kernel, seq_len=seq_len),
        grid=(rows // tl,),
        in_specs=[
            pl.BlockSpec((tl, d_ssm), lambda i: (i, 0)),
            _const_spec((nb, n_state)),
            _const_spec((nb, n_state)),
            _const_spec(bs.shape),
            _const_spec(cs.shape),
            _const_spec(d.shape),
            _const_spec(gluw.shape),
            _const_spec(glub.shape),
            _const_spec(mtab.shape),
            _const_spec(ptab.shape),
        ],
        out_specs=[
            pl.BlockSpec((tl, d_ssm), lambda i: (i, 0)),
            pl.BlockSpec((nb, n_state), lambda i: (0, 0)),
            pl.BlockSpec((nb, n_state), lambda i: (0, 0)),
        ],
        out_shape=[
            jax.ShapeDtypeStruct((rows, d_ssm), BF16),
            jax.ShapeDtypeStruct((nb, n_state), F32),
            jax.ShapeDtypeStruct((nb, n_state), F32),
        ],
        scratch_shapes=[
            pltpu.VMEM((tl, n_state), F32),
            pltpu.VMEM((tl, n_state), F32),
            pltpu.VMEM((SUBLANES, n_state), F32),
            pltpu.VMEM((SUBLANES, n_state), F32),
        ],
        compiler_params=pltpu.CompilerParams(
            dimension_semantics=("arbitrary",), vmem_limit_bytes=VMEM_LIMIT),
        name="s5",
    )(u, h0r, h0i, bs, cs, d, gluw, glub, mtab, ptab)


def _mix_kernel(x_ref, ya_ref, su_ref, vn_ref, gates_ref, ws_ref, bias_ref, wa_ref, wb_ref, wo_ref, x1_ref):
    tm, dm = x_ref.shape
    n_heads = ws_ref.shape[0]
    hd = vn_ref.shape[1] // n_heads
    vb = vn_ref[...].astype(BF16)
    chunks = []
    for c in range(tm // SGU_CHUNK):
        heads = [jnp.dot(ws_ref[h], vb[c * SGU_CHUNK:(c + 1) * SGU_CHUNK, h * hd:(h + 1) * hd],
                         preferred_element_type=F32) for h in range(n_heads)]
        chunks.append(jnp.concatenate(heads, axis=1) + bias_ref[...])
    s_all = jnp.concatenate(chunks, axis=0)
    yb = (su_ref[...].astype(F32) * s_all).astype(BF16)
    pa = jnp.dot(ya_ref[...], wa_ref[...], preferred_element_type=F32)
    pb = jnp.dot(yb, wb_ref[...], preferred_element_type=F32)
    merged = gates_ref[:, :dm].astype(F32) * pa + gates_ref[:, dm:].astype(F32) * pb
    x1_ref[...] = x_ref[...] + jnp.dot(merged.astype(BF16), wo_ref[...], preferred_element_type=F32)


def _mix(x, window, ya, su, vn, gates, ws, bias, wa, wb, wo):
    row0, rows = window
    dm = x.shape[1]
    d_ssm = ya.shape[1]
    d_sgu = su.shape[1]
    tm = _pick(math.gcd(rows, row0) if row0 else rows, (256, 128))
    blk0 = row0 // tm
    row = lambda w: pl.BlockSpec((tm, w), lambda i: (i, 0))
    return pl.pallas_call(
        _mix_kernel,
        grid=(rows // tm,),
        in_specs=[pl.BlockSpec((tm, dm), lambda i: (i + blk0, 0)), row(d_ssm), row(d_sgu), row(d_sgu), row(2 * dm),
                  _const_spec(ws.shape), _const_spec(bias.shape),
                  _const_spec(wa.shape), _const_spec(wb.shape), _const_spec(wo.shape)],
        out_specs=row(dm),
        out_shape=jax.ShapeDtypeStruct((rows, dm), F32),
        compiler_params=pltpu.CompilerParams(
            dimension_semantics=("arbitrary",), vmem_limit_bytes=VMEM_LIMIT),
        name="mix",
    )(x, ya, su, vn, gates, ws, bias, wa, wb, wo)


def _topk_rows(s, k, payload=None):
    rows = s.shape[0]
    iota = lax.broadcasted_iota(I32, s.shape, 0).astype(F32)
    vals, outs = [], []
    for _ in range(k):
        m = jnp.max(s, axis=0, keepdims=True)
        j = jnp.min(jnp.where(s == m, iota, float(rows)), axis=0, keepdims=True)
        sel = iota == j
        vals.append(m)
        outs.append(j if payload is None else jnp.max(jnp.where(sel, payload, -1.0), axis=0, keepdims=True))
        s = jnp.where(sel, -jnp.inf, s)
    return jnp.concatenate(vals, axis=0), jnp.concatenate(outs, axis=0)


def _route_kernel(x1_ref, g_ref, wq_ref, keys_ref, xn_ref, eidx_ref, gate_ref, q_sc, e_sc, p_sc):
    tb = x1_ref.shape[0]
    tw = e_sc.shape[2]
    nsub = tb // tw
    n_heads, _, n_keys, half = keys_ref.shape
    dkey = 2 * half
    xn = _rms(x1_ref[...], g_ref[...])
    xn_ref[...] = xn
    q = jnp.dot(xn.astype(BF16), wq_ref[...], preferred_element_type=F32).astype(BF16)
    for h in range(n_heads):
        q_sc[h] = q[:, h * dkey:(h + 1) * dkey]

    nt = (((1,), (1,)), ((), ()))

    def body(n, _):
        h = n // nsub
        r0 = pl.multiple_of((n % nsub) * tw, tw)
        qh = q_sc[h, pl.ds(r0, tw), :]
        s1 = lax.dot_general(keys_ref[h, 0], qh[:, :half], nt, preferred_element_type=F32)
        s2 = lax.dot_general(keys_ref[h, 1], qh[:, half:], nt, preferred_element_type=F32)
        v1, i1 = _topk_rows(s1, PEER_TOPK)
        v2, i2 = _topk_rows(s2, PEER_TOPK)
        cv, ci = [], []
        for a in range(PEER_TOPK):
            nb = PEER_TOPK // (a + 1)
            cv.append(v1[a:a + 1, :] + v2[:nb, :])
            ci.append(i1[a:a + 1, :] * float(n_keys) + i2[:nb, :])
        ts, te = _topk_rows(jnp.concatenate(cv, axis=0), PEER_TOPK, payload=jnp.concatenate(ci, axis=0))
        ex = jnp.exp(ts - ts[0:1, :])
        e_sc[n] = te
        p_sc[n] = ex / jnp.sum(ex, axis=0, keepdims=True)
        return 0

    lax.fori_loop(0, n_heads * nsub, body, 0)
    for c in range(nsub):
        rows = slice(c * tw, (c + 1) * tw)
        e_all = jnp.concatenate([e_sc[h * nsub + c] for h in range(n_heads)], axis=0)
        p_all = jnp.concatenate([p_sc[h * nsub + c] for h in range(n_heads)], axis=0)
        eidx_ref[rows, :] = e_all.T.astype(I32)
        gate_ref[rows, :] = p_all.T


def _route(x1, g, wq, keys_bf):
    rows, dm = x1.shape
    n_heads, _, n_keys, half = keys_bf.shape
    nsel = n_heads * PEER_TOPK
    tb = _pick(rows, (256, 128))
    tw = min(tb, ROUTE_LANES)
    row = lambda w: pl.BlockSpec((tb, w), lambda i: (i, 0))
    return pl.pallas_call(
        _route_kernel,
        grid=(rows // tb,),
        in_specs=[row(dm), _const_spec(g.shape), _const_spec(wq.shape), _const_spec(keys_bf.shape)],
        out_specs=[row(dm), row(nsel), row(nsel)],
        out_shape=[
            jax.ShapeDtypeStruct((rows, dm), F32),
            jax.ShapeDtypeStruct((rows, nsel), I32),
            jax.ShapeDtypeStruct((rows, nsel), F32),
        ],
        scratch_shapes=[
            pltpu.VMEM((n_heads, tb, 2 * half), BF16),
            pltpu.VMEM((n_heads * (tb // tw), PEER_TOPK, tw), F32),
            pltpu.VMEM((n_heads * (tb // tw), PEER_TOPK, tw), F32),
        ],
        compiler_params=pltpu.CompilerParams(
            dimension_semantics=("arbitrary",), vmem_limit_bytes=VMEM_LIMIT),
        name="route",
    )(x1, g, wq, keys_bf)


def _mixroute_kernel(x_ref, ya_ref, su_ref, vn_ref, gates_ref, ws_ref, bias_ref, wa_ref, wb_ref, wo_ref,
                     g_ref, wq_ref, keys_ref, x1_ref, xn_ref, eidx_ref, gate_ref, q_sc, e_sc, p_sc):
    _mix_kernel(x_ref, ya_ref, su_ref, vn_ref, gates_ref, ws_ref, bias_ref, wa_ref, wb_ref, wo_ref, x1_ref)
    _route_kernel(x1_ref, g_ref, wq_ref, keys_ref, xn_ref, eidx_ref, gate_ref, q_sc, e_sc, p_sc)


def _mixroute(x, window, ya, su, vn, gates, ws, bias, wa, wb, wo, g, wq, keys_bf):
    row0, rows = window
    dm = x.shape[1]
    d_ssm = ya.shape[1]
    d_sgu = su.shape[1]
    n_heads, _, n_keys, half = keys_bf.shape
    nsel = n_heads * PEER_TOPK
    tm = _pick(math.gcd(rows, row0) if row0 else rows, (256, 128))
    tw = min(tm, ROUTE_LANES)
    blk0 = row0 // tm
    row = lambda w: pl.BlockSpec((tm, w), lambda i: (i, 0))
    consts = [ws, bias, wa, wb, wo, g, wq, keys_bf]
    return pl.pallas_call(
        _mixroute_kernel,
        grid=(rows // tm,),
        in_specs=[pl.BlockSpec((tm, dm), lambda i: (i + blk0, 0)), row(d_ssm), row(d_sgu), row(d_sgu), row(2 * dm)]
                 + [_const_spec(c.shape) for c in consts],
        out_specs=[row(dm), row(dm), row(nsel), row(nsel)],
        out_shape=[
            jax.ShapeDtypeStruct((rows, dm), F32),
            jax.ShapeDtypeStruct((rows, dm), F32),
            jax.ShapeDtypeStruct((rows, nsel), I32),
            jax.ShapeDtypeStruct((rows, nsel), F32),
        ],
        scratch_shapes=[
            pltpu.VMEM((n_heads, tm, 2 * half), BF16),
            pltpu.VMEM((n_heads * (tm // tw), PEER_TOPK, tw), F32),
            pltpu.VMEM((n_heads * (tm // tw), PEER_TOPK, tw), F32),
        ],
        compiler_params=pltpu.CompilerParams(
            dimension_semantics=("arbitrary",), vmem_limit_bytes=VMEM_LIMIT),
        name="mixroute",
    )(x, ya, su, vn, gates, *consts)


PEER_BUFS = 16


def _peer_kernel(eidx_hbm, gate_ref, xn_ref, x1_ref, gf_ref, tab_hbm, y_ref,
                 idx_sm, idx_sem, *scratch, final_norm):
    i = pl.program_id(0)
    bufs = scratch[:PEER_BUFS]
    sem, peer_sc = scratch[PEER_BUFS:]
    tbk, nsel = gate_ref.shape
    half = tab_hbm.shape[1] // 2
    nlt = half // LANES

    icp = pltpu.make_async_copy(eidx_hbm.at[i], idx_sm, idx_sem)
    icp.start()
    icp.wait()

    def issue(t, s):
        base = t * nsel
        for k in range(nsel):
            e = idx_sm[base + k]
            pltpu.make_async_copy(tab_hbm.at[pl.ds(e, 1), :], bufs[s].at[pl.ds(k, 1), :],
                                  sem.at[s]).start(priority=k % 2)

    def wait(s):
        pltpu.make_async_copy(tab_hbm.at[pl.ds(0, nsel), :], bufs[s], sem.at[s]).wait()

    hi_mask = jnp.int32(-65536)
    eye = (lax.broadcasted_iota(I32, (nsel, nsel), 0) == lax.broadcasted_iota(I32, (nsel, nsel), 1))

    def unpack(w):
        return (lax.bitcast_convert_type(w & hi_mask, F32),
                lax.bitcast_convert_type(w << 16, F32))

    def compute(t, s):
        buf = bufs[s]
        xrow = xn_ref[pl.ds(t, 1), :]
        acc = jnp.zeros((nsel, LANES), F32)
        for j in range(nlt):
            uh, ul = unpack(buf[:, j * LANES:(j + 1) * LANES])
            acc = acc + uh * xrow[:, j * LANES:(j + 1) * LANES] \
                      + ul * xrow[:, half + j * LANES:half + (j + 1) * LANES]
        act = jnp.sum(acc, axis=1, keepdims=True)
        grow = jnp.broadcast_to(gate_ref[pl.ds(t, 1), :], (nsel, nsel))
        gcol = jnp.sum(jnp.where(eye, grow, 0.0), axis=1, keepdims=True)
        wgt = jnp.broadcast_to(gcol * jax.nn.gelu(act), (nsel, LANES))
        t8 = pl.multiple_of((t // SUBLANES) * SUBLANES, SUBLANES)
        rsel = lax.broadcasted_iota(I32, (SUBLANES, LANES), 0) == (t % SUBLANES)

        def put(lane0, row):
            cur = peer_sc[pl.ds(t8, SUBLANES), lane0:lane0 + LANES]
            peer_sc[pl.ds(t8, SUBLANES), lane0:lane0 + LANES] = jnp.where(
                rsel, jnp.broadcast_to(row, (SUBLANES, LANES)), cur)

        for j in range(nlt):
            vh, vl = unpack(buf[:, half + j * LANES:half + (j + 1) * LANES])
            put(j * LANES, jnp.sum(vh * wgt, axis=0, keepdims=True))
            put(half + j * LANES, jnp.sum(vl * wgt, axis=0, keepdims=True))

    def step(t, s, prefetch):
        wait(s)
        if prefetch:
            issue(t + PEER_BUFS - 1, (s + PEER_BUFS - 1) % PEER_BUFS)
        compute(t, s)

    peer_sc[...] = jnp.zeros_like(peer_sc)
    for t in range(PEER_BUFS - 1):
        issue(t, t)

    def group(gi, _):
        for s in range(PEER_BUFS):
            step(gi * PEER_BUFS + s, s, True)
        return 0

    lax.fori_loop(0, tbk // PEER_BUFS - 1, group, 0)
    for s in range(PEER_BUFS):
        t = tbk - PEER_BUFS + s
        step(t, s, t + PEER_BUFS - 1 < tbk)

    xo = x1_ref[...] + peer_sc[...]
    y_ref[...] = _rms(xo, gf_ref[...]) if final_norm else xo


def _peer(eidx, gate, xn, x1, gf, table, final_norm):
    rows, dm = x1.shape
    nsel = gate.shape[1]
    tbk = _pick(rows, (256, 128, 64, 32, 16, 8))
    assert tbk % PEER_BUFS == 0 and tbk >= 2 * PEER_BUFS
    nblk = rows // tbk
    eidx_blk = eidx.reshape(nblk, tbk * nsel)
    row = lambda w: pl.BlockSpec((tbk, w), lambda i: (i, 0))
    return pl.pallas_call(
        functools.partial(_peer_kernel, final_norm=final_norm),
        grid=(nblk,),
        in_specs=[pl.BlockSpec(memory_space=pl.ANY), row(nsel), row(dm), row(dm),
                  _const_spec(gf.shape), pl.BlockSpec(memory_space=pl.ANY)],
        out_specs=row(dm),
        out_shape=jax.ShapeDtypeStruct((rows, dm), F32),
        scratch_shapes=[
            pltpu.SMEM((tbk * nsel,), I32),
            pltpu.SemaphoreType.DMA(()),
        ] + [pltpu.VMEM((nsel, table.shape[1]), table.dtype) for _ in range(PEER_BUFS)] + [
            pltpu.SemaphoreType.DMA((PEER_BUFS,)),
            pltpu.VMEM((tbk, dm), F32),
        ],
        compiler_params=pltpu.CompilerParams(
            dimension_semantics=("arbitrary",), vmem_limit_bytes=VMEM_LIMIT),
        name="peer",
    )(eidx_blk, gate, xn, x1, gf, table)


SC_L = 16
SC_WINDOW = SC_L
SC_SUBCORES = 32
SC_CUTS = (0.0625, 0.4375)
SC_SPLIT_ALIGN = 512


def _sc_peer(table, eidx, gate, xn):
    ntok, nsel = eidx.shape
    d = table.shape[1]
    half = d // 2
    nch = half // SC_L
    info = plsc.get_sparse_core_info()
    n_workers = info.num_cores * info.num_subcores
    n_win = nsel // SC_WINDOW
    tpw = ntok // n_workers
    assert tpw * n_workers == ntok and SC_WINDOW == SC_L and info.num_lanes == SC_L
    mesh = plsc.VectorSubcoreMesh(core_axis_name="core", subcore_axis_name="subcore")
    c0 = math.sqrt(2.0 / math.pi)

    @functools.partial(
        pl.kernel, out_type=jax.ShapeDtypeStruct((ntok, d), F32), mesh=mesh,
        scratch_types=[pltpu.VMEM((nsel,), I32), pltpu.VMEM((nsel,), F32), pltpu.VMEM((d,), F32),
                       pltpu.VMEM((2, SC_WINDOW, d), I32), pltpu.VMEM((d,), F32),
                       pltpu.SemaphoreType.DMA((2,))],
        compiler_params=pltpu.CompilerParams(needs_layout_passes=False),
        name="sc_peer")
    def k(tab_hbm, idx_hbm, gate_hbm, x_hbm, o_hbm, idx_v, gate_v, x_v, rows_v, out_v, gsem):
        wid = lax.axis_index("subcore") * info.num_cores + lax.axis_index("core")
        lane = lax.iota(I32, SC_L)
        zero = jnp.zeros((SC_L,), F32)

        def unpack(w):
            return (lax.bitcast_convert_type(w & jnp.int32(-65536), F32),
                    lax.bitcast_convert_type(w << 16, F32))

        @pl.loop(0, tpw)
        def _(tt):
            tok = wid * tpw + tt
            pltpu.sync_copy(idx_hbm.at[pl.ds(tok * nsel, nsel)], idx_v)
            pltpu.sync_copy(gate_hbm.at[pl.ds(tok * nsel, nsel)], gate_v)
            pltpu.sync_copy(x_hbm.at[tok], x_v)

            @pl.loop(0, d // SC_L)
            def _(c):
                out_v[pl.ds(c * SC_L, SC_L)] = zero

            def gather(w):
                return pltpu.make_async_copy(tab_hbm.at[idx_v.at[pl.ds(w * SC_WINDOW, SC_WINDOW)]],
                                             rows_v.at[w % 2], gsem.at[w % 2])

            gather(0).start()
            for w in range(n_win):
                gather(w).wait()
                if w + 1 < n_win:
                    gather(w + 1).start()
                slot = w % 2

                def dot_chunk(j, accs, slot=slot):
                    xa = x_v[pl.ds(j * SC_L, SC_L)]
                    xb = x_v[pl.ds(half + j * SC_L, SC_L)]
                    out = []
                    for r in range(SC_WINDOW):
                        uh, ul = unpack(rows_v[slot, r, pl.ds(j * SC_L, SC_L)])
                        out.append(accs[r] + (uh * xa + ul * xb))
                    return tuple(out)

                accs = lax.fori_loop(0, nch, dot_chunk, (zero,) * SC_WINDOW)
                act = zero
                for r in range(SC_WINDOW):
                    act = jnp.where(lane == r, jnp.sum(accs[r]), act)
                z = c0 * (act + 0.044715 * (act * act * act))
                tanh_z = 1.0 - 2.0 / (jnp.exp(2.0 * z) + 1.0)
                wgt = gate_v[pl.ds(w * SC_WINDOW, SC_WINDOW)] * (0.5 * act * (1.0 + tanh_z))
                splat = [jnp.full((SC_L,), jnp.sum(jnp.where(lane == r, wgt, 0.0)), F32)
                         for r in range(SC_WINDOW)]

                @pl.loop(0, nch)
                def _(j, slot=slot, splat=splat):
                    oh = zero
                    ol = zero
                    for r in range(SC_WINDOW):
                        vh, vl = unpack(rows_v[slot, r, pl.ds(half + j * SC_L, SC_L)])
                        oh = oh + splat[r] * vh
                        ol = ol + splat[r] * vl
                    out_v[pl.ds(j * SC_L, SC_L)] += oh
                    out_v[pl.ds(half + j * SC_L, SC_L)] += ol

            pltpu.sync_copy(out_v, o_hbm.at[tok])

    return k(table, eidx.reshape(-1), gate.reshape(-1), xn)


def _finish_kernel(x1_ref, p_ref, gf_ref, y_ref):
    y_ref[...] = _rms(x1_ref[...] + p_ref[...], gf_ref[...])


def _finish(x1, p, gf):
    rows, dm = x1.shape
    tm = _pick(rows, (512, 256, 128))
    row = pl.BlockSpec((tm, dm), lambda i: (i, 0))
    return pl.pallas_call(
        _finish_kernel, grid=(rows // tm,), in_specs=[row, row, _const_spec(gf.shape)], out_specs=row,
        out_shape=jax.ShapeDtypeStruct((rows, dm), F32),
        compiler_params=pltpu.CompilerParams(dimension_semantics=("arbitrary",), vmem_limit_bytes=VMEM_LIMIT),
        name="finish",
    )(x1, p, gf)


def _cmul(ar, ai, br, bi):
    return ar * br - ai * bi, ar * bi + ai * br


def _s5_tables(lam_re, lam_im, log_dt, b_re, b_im, c_re, c_im, d, glu_w, glu_b):
    g, p = lam_re.shape
    hg = b_re.shape[-1]
    gs = g // SSM_SLABS
    dt = jnp.exp(log_dt.astype(F32))[:, None]
    lr = lam_re.astype(F32)
    li = lam_im.astype(F32)
    mag = jnp.exp(lr * dt)
    ab_re = mag * jnp.cos(li * dt)
    ab_im = mag * jnp.sin(li * dt)
    den = lr * lr + li * li
    nr = ab_re - 1.0
    ni = ab_im
    coef_re = (nr * lr + ni * li) / den
    coef_im = (ni * lr - nr * li) / den
    br = b_re.astype(F32)
    bi = b_im.astype(F32)
    bb_re = coef_re[..., None] * br - coef_im[..., None] * bi
    bb_im = coef_re[..., None] * bi + coef_im[..., None] * br
    eye = jnp.eye(gs, dtype=F32)

    def b_slab(bb):
        t = bb.reshape(SSM_SLABS, gs, p, hg).transpose(0, 1, 3, 2)
        return jnp.einsum('sihp,ij->sihjp', t, eye).reshape(SSM_SLABS, gs * hg, gs * p)

    bs = jnp.concatenate([b_slab(bb_re), b_slab(bb_im)], axis=2).astype(BF16)

    def c_slab(c):
        t = c.astype(F32).reshape(SSM_SLABS, gs, hg, p).transpose(0, 1, 3, 2)
        return jnp.einsum('siph,ij->sipjh', t, eye).reshape(SSM_SLABS, gs * p, gs * hg)

    cs = jnp.concatenate([c_slab(c_re), -c_slab(c_im)], axis=1).astype(BF16)

    a1 = (ab_re.reshape(1, g * p), ab_im.reshape(1, g * p))
    a2 = _cmul(*a1, *a1)
    a4 = _cmul(*a2, *a2)
    rowid = jnp.arange(SUBLANES)[:, None]
    mtab = jnp.stack([jnp.where(rowid >= s, comp, 0.0)
                      for s, a in ((1, a1), (2, a2), (4, a4)) for comp in a])
    pows = [a1]
    for _ in range(SUBLANES - 1):
        pows.append(_cmul(*pows[-1], *a1))
    ptab = jnp.stack([jnp.concatenate([q[0] for q in pows], axis=0),
                      jnp.concatenate([q[1] for q in pows], axis=0)])
    return (bs, cs, d.astype(F32).reshape(1, -1), glu_w.astype(BF16),
            glu_b.astype(F32).reshape(1, -1), mtab, ptab)


def _sgu_tables(w_s, b_s, seq_len):
    n_heads = w_s.shape[0]
    ln = min(SGU_CHUNK, seq_len)
    assert SGU_CHUNK % ln == 0 and seq_len % ln == 0
    rep = SGU_CHUNK // ln
    mask = jnp.tril(jnp.ones((ln, ln), dtype=bool))
    w = jnp.where(mask[None], w_s[:, :ln, :ln], 0.0).astype(F32)
    ws = jnp.einsum('hij,ab->haibj', w, jnp.eye(rep, dtype=F32)).reshape(n_heads, SGU_CHUNK, SGU_CHUNK)
    bias = jnp.tile(jnp.transpose(b_s[:, :ln]).astype(F32), (rep, 1))
    return ws.astype(BF16), bias


def _pack_kernel(u_ref, v_ref, o_ref):
    half = u_ref.shape[1] // 2

    def bf16_bits(x):
        return lax.bitcast_convert_type(x.astype(BF16).astype(F32), U32)

    def pack(t_ref):
        w = bf16_bits(t_ref[:, :half]) | (bf16_bits(t_ref[:, half:]) >> 16)
        return lax.bitcast_convert_type(w, I32)

    o_ref[:, :half] = pack(u_ref)
    o_ref[:, half:] = pack(v_ref)


def _pack_tables(u, v):
    n, dd = u.shape
    tr = _pick(n, (PACK_ROWS, 256, 128, 8))
    spec = pl.BlockSpec((tr, dd), lambda i: (i, 0))
    return pl.pallas_call(
        _pack_kernel,
        grid=(n // tr,),
        in_specs=[spec, spec],
        out_specs=spec,
        out_shape=jax.ShapeDtypeStruct((n, dd), I32),
        compiler_params=pltpu.CompilerParams(
            dimension_semantics=("arbitrary",), vmem_limit_bytes=VMEM_LIMIT),
        name="pack",
    )(u, v)


def _trunk_to_route(x, h0r, h0i, seq_len, lp, window=None):
    (norm_mix_g, w_in_bf, s5_tabs, ln_g, ln_b, sgu_w, sgu_b, wa, wb, wo, norm_ffn_g, wq, keys_bf, _,
     d_ssm, d_sgu) = lp
    window = (0, x.shape[0]) if window is None else window
    u, su, vn, gates = _inproj(x, window, norm_mix_g, w_in_bf, ln_g, ln_b, d_ssm, d_sgu)
    ya, hlr, hli = _s5(u, h0r, h0i, s5_tabs, seq_len)
    ws, bias = _sgu_tables(sgu_w, sgu_b, seq_len)
    hd = d_sgu // ws.shape[0]
    bias_tile = jnp.repeat(bias, hd, axis=1)
    x1, xn, eidx, gate = _mixroute(x, window, ya, su, vn, gates, ws, bias_tile, wa, wb, wo,
                                   norm_ffn_g, wq, keys_bf)
    return (x1, xn, eidx, gate), hlr, hli, vn


def _trunk_layer(x, h0r, h0i, seq_len, lp, final_g):
    (x1, xn, eidx, gate), hlr, hli, vn = _trunk_to_route(x, h0r, h0i, seq_len, lp)
    y = _peer(eidx, gate, xn, x1, final_g, lp[13], final_norm=True)
    return y, hlr, hli, vn


def _prompt_split(x, seq_len, lp, final_g, n_state):
    rows = x.shape[0]
    cuts = [(int(rows * c) // SC_SPLIT_ALIGN) * SC_SPLIT_ALIGN for c in SC_CUTS]
    bounds = [0] + cuts + [rows]
    hr = hi = jnp.zeros((1, n_state), F32)
    parts = []
    for a, b in zip(bounds[:-1], bounds[1:]):
        part, hr, hi, _ = _trunk_to_route(x, hr, hi, seq_len, lp, window=(a, b - a))
        parts.append(part)
    x1a, xna, ea, ga = parts[-1]
    ya = _peer(ea, ga, xna, x1a, final_g, lp[13], final_norm=True)
    return parts[:-1], ya, hr, hi


def _prompt_finish(firsts, ya, lp, final_g):
    ys = [_finish(x1, _sc_peer(lp[13], e, g, xn), final_g) for (x1, xn, e, g) in firsts]
    ys = ys if ya is None else ys + [ya]
    return ys[0] if len(ys) == 1 else jnp.concatenate(ys, axis=0)


def kernel(x_prompt, x_sample, state_ssm_re, state_ssm_im, norm_mix_g, w_in, ssm_lambda_re, ssm_lambda_im, ssm_log_dt, ssm_b_re, ssm_b_im, ssm_c_re, ssm_c_im, ssm_d, ssm_glu_w, ssm_glu_b, sgu_ln_g, sgu_ln_b, sgu_w, sgu_b, w_branch_a, w_branch_b, w_out, norm_ffn_g, peer_w_q, peer_keys, peer_u, peer_v, norm_final_g):
    depth = w_in.shape[0]
    assert depth == 1, "the final norm is fused into the last layer's PEER kernel"
    bp, sp, dm = x_prompt.shape
    bs_, ss, _ = x_sample.shape
    g, p = ssm_lambda_re.shape[1:]
    d_ssm = ssm_d.shape[1]
    d_sgu = sgu_ln_g.shape[1]
    n_state = g * p
    l = 0
    lp = (norm_mix_g[l].reshape(1, dm), w_in[l].astype(BF16),
          _s5_tables(ssm_lambda_re[l], ssm_lambda_im[l], ssm_log_dt[l], ssm_b_re[l], ssm_b_im[l],
                     ssm_c_re[l], ssm_c_im[l], ssm_d[l], ssm_glu_w[l], ssm_glu_b[l]),
          sgu_ln_g[l].reshape(1, d_sgu), sgu_ln_b[l].reshape(1, d_sgu), sgu_w[l], sgu_b[l],
          w_branch_a[l].astype(BF16), w_branch_b[l].astype(BF16), w_out[l].astype(BF16),
          norm_ffn_g[l].reshape(1, dm), peer_w_q[l].astype(BF16), peer_keys[l].astype(BF16),
          _pack_tables(peer_u[l].astype(F32), peer_v[l].astype(F32)),
          d_ssm, d_sgu)
    gf = norm_final_g.reshape(1, dm)

    xp2 = x_prompt.reshape(bp * sp, dm)
    xs2 = x_sample.reshape(bs_ * ss, dm)
    h0s = (state_ssm_re[l].astype(F32).reshape(bs_, n_state), state_ssm_im[l].astype(F32).reshape(bs_, n_state))
    split = bp == 1 and sp >= 4 * SC_SPLIT_ALIGN and (bs_ * ss) % SC_SUBCORES == 0
    if split:
        sample, hr_s, hi_s, v_s = _trunk_to_route(xs2, h0s[0], h0s[1], ss, lp)
        first, ya, hr_p, hi_p = _prompt_split(xp2, sp, lp, gf, n_state)
        ys = _prompt_finish([sample], None, lp, gf)
        yp = _prompt_finish(first, ya, lp, gf)
    else:
        zeros = jnp.zeros((bp, n_state), F32)
        yp, hr_p, hi_p, _ = _trunk_layer(xp2, zeros, zeros, sp, lp, gf)
        ys, hr_s, hi_s, v_s = _trunk_layer(xs2, h0s[0], h0s[1], ss, lp, gf)
    return (yp.reshape(bp, sp, dm), ys.reshape(bs_, ss, dm),
            hr_p.reshape(1, bp, g, p), hi_p.reshape(1, bp, g, p),
            hr_s.reshape(1, bs_, g, p), hi_s.reshape(1, bs_, g, p),
            v_s.reshape(1, bs_, ss, d_sgu))
```
